```python
import math
import jax, jax.numpy as jnp
from jax import lax
import numpy as np

D_MODEL = 1024
BATCH = 4
SEQ = 8192
DEPTH = 1
DEC_BATCH = 128
DEC_SEQ = 4
PAST_LEN = 16384
PAGE_SIZE = 128

HEAD_DIM = 64
ATTN_HEADS = D_MODEL // HEAD_DIM
KV_HEADS = ATTN_HEADS // 4
GQA_GROUP = ATTN_HEADS // KV_HEADS
WINDOW = 128
N_BUCKETS = 32
MAX_DISTANCE = 128
RET_HEADS = 4
RET_DV = D_MODEL // RET_HEADS
RET_DK = RET_DV // 2
RET_CHUNK = 128
ROPE_BASE = 10000.0
IN_WIDTH = (ATTN_HEADS + 2 * KV_HEADS) * HEAD_DIM + RET_HEADS * (2 * RET_DK + 2 * RET_DV) + 2 * D_MODEL
N_GROUPS = 4
EXPERTS_PER_GROUP = 8
N_EXPERTS = N_GROUPS * EXPERTS_PER_GROUP
TOP_K = 2
EXPERT_FF = D_MODEL // 2
MOE_BLOCK = 128
NORM_EPS = 1e-6
NEG_INF = -1e30

kernel_name = 'hybrid_swa_sink_retention_hmoe_step'


def rms_norm(x, gain):
    x32 = x.astype(jnp.float32)
    y = x32 * lax.rsqrt(jnp.mean(x32 * x32, axis=-1, keepdims=True) + NORM_EPS)
    return (y * gain.astype(jnp.float32)).astype(x.dtype)


def t5_causal_bucket(dist):
    max_exact = N_BUCKETS // 2
    d_f = jnp.maximum(dist, 1).astype(jnp.float32)
    large = max_exact + (jnp.log(d_f / max_exact) / math.log(MAX_DISTANCE / max_exact)
                         * (N_BUCKETS - max_exact)).astype(jnp.int32)
    large = jnp.minimum(large, N_BUCKETS - 1)
    return jnp.where(dist < max_exact, dist, large)


def banded_sink_attention(q, k, v, k_hist, v_hist, hist_valid, sink, rel_bias):
    B, L = q.shape[:2]
    lq = min(WINDOW, L)
    nb = L // lq
    lk = WINDOW + lq
    qb = q.reshape(B, nb, lq, KV_HEADS, GQA_GROUP, HEAD_DIM)
    kb = k.reshape(B, nb, lq, KV_HEADS, HEAD_DIM)
    vb = v.reshape(B, nb, lq, KV_HEADS, HEAD_DIM)
    if nb == 1:
        k_prev, v_prev = k_hist[:, None], v_hist[:, None]
    else:
        k_prev = jnp.concatenate([k_hist[:, None], kb[:, :-1]], axis=1)
        v_prev = jnp.concatenate([v_hist[:, None], vb[:, :-1]], axis=1)
    k_ctx = jnp.concatenate([k_prev, kb], axis=2)
    v_ctx = jnp.concatenate([v_prev, vb], axis=2)
    prev_ok = (jnp.arange(nb) > 0) | hist_valid
    key_ok = jnp.concatenate([jnp.broadcast_to(prev_ok[:, None], (nb, WINDOW)),
                              jnp.ones((nb, lq), dtype=bool)], axis=1)
    dist = jnp.arange(lq)[:, None] + WINDOW - jnp.arange(lk)[None, :]
    band = (dist >= 0) & (dist < WINDOW)
    bias = rel_bias.astype(jnp.float32)[t5_causal_bucket(jnp.clip(dist, 0, WINDOW - 1))]
    bias = jnp.transpose(bias, (2, 0, 1)).reshape(KV_HEADS, GQA_GROUP, lq, lk)
    sink32 = sink.astype(jnp.float32).reshape(KV_HEADS, GQA_GROUP)[None, :, :, None, None]
    scale = HEAD_DIM ** -0.5

    def block(args):
        qn, kn, vn, ok = args
        s = jnp.einsum('bqhgd,bkhd->bhgqk', qn, kn).astype(jnp.float32) * scale + bias
        s = jnp.where(band & ok[None, :], s, NEG_INF)
        m = jnp.maximum(jnp.max(s, axis=-1, keepdims=True), sink32)
        p = jnp.exp(s - m)
        p = p / (jnp.sum(p, axis=-1, keepdims=True) + jnp.exp(sink32 - m))
        return jnp.einsum('bhgqk,bkhd->bqhgd', p.astype(vn.dtype), vn)

    out = lax.map(block, (jnp.moveaxis(qb, 1, 0), jnp.moveaxis(k_ctx, 1, 0),
                          jnp.moveaxis(v_ctx, 1, 0), key_ok))
    return jnp.moveaxis(out, 0, 1).reshape(B, L, ATTN_HEADS * HEAD_DIM)


def rotary(x, pos):
    half = x.shape[-1] // 2
    inv = ROPE_BASE ** (-jnp.arange(half, dtype=jnp.float32) * 2.0 / x.shape[-1])
    ang = pos.astype(jnp.float32)[:, None] * inv[None, :]
    cos = jnp.cos(ang)[None, :, None, :]
    sin = jnp.sin(ang)[None, :, None, :]
    x1, x2 = x[..., :half], x[..., half:]
    return jnp.concatenate([x1 * cos - x2 * sin, x2 * cos + x1 * sin], axis=-1)


def chunkwise_retention(q, k, v, s0):
    B, L, H, dk = q.shape
    dv = v.shape[-1]
    C = min(RET_CHUNK, L)
    nc = L // C
    log_gamma = jnp.log(1.0 - 2.0 ** (-5.0 - jnp.arange(H, dtype=jnp.float32)))
    idx = jnp.arange(C, dtype=jnp.float32)
    diff = idx[:, None] - idx[None, :]
    decay_in = jnp.where((diff >= 0)[..., None],
                         jnp.exp(jnp.maximum(diff, 0.0)[..., None] * log_gamma), 0.0)
    q_dec = jnp.exp((idx + 1.0)[:, None] * log_gamma)
    k_dec = jnp.exp((C - 1.0 - idx)[:, None] * log_gamma)
    c_dec = jnp.exp(C * log_gamma)
    qc = q.reshape(B, nc, C, H, dk)
    kc = k.reshape(B, nc, C, H, dk)
    vc = v.reshape(B, nc, C, H, dv)
    scores = jnp.einsum('bnihd,bnjhd->bnhij', qc, kc) * jnp.transpose(decay_in, (2, 0, 1))
    o_inner = jnp.einsum('bnhij,bnjhe->bnihe', scores, vc)

    def step(S, blk):
        qn, kn, vn = blk
        cross = jnp.einsum('bihd,bhde->bihe', qn * q_dec[None, :, :, None], S)
        S = S * c_dec[None, :, None, None] + jnp.einsum('bjhd,bjhe->bhde', kn * k_dec[None, :, :, None], vn)
        return S, cross

    s_final, cross = lax.scan(step, s0, (jnp.moveaxis(qc, 1, 0), jnp.moveaxis(kc, 1, 0), jnp.moveaxis(vc, 1, 0)))
    o = o_inner + jnp.moveaxis(cross, 0, 1)
    return o.reshape(B, L, H, dv), s_final


def retention_branch(q, k, v, g, s0, pos0):
    B, L = q.shape[:2]
    pos = pos0 + jnp.arange(L)
    q = rotary(q.reshape(B, L, RET_HEADS, RET_DK).astype(jnp.float32), pos)
    k = rotary(k.reshape(B, L, RET_HEADS, RET_DK).astype(jnp.float32), pos) * RET_DK ** -0.5
    v = v.reshape(B, L, RET_HEADS, RET_DV).astype(jnp.float32)
    o, s_new = chunkwise_retention(q, k, v, s0.astype(jnp.float32))
    o = o * lax.rsqrt(jnp.mean(o * o, axis=-1, keepdims=True) + NORM_EPS)
    o = o.reshape(B, L, RET_HEADS * RET_DV).astype(g.dtype) * jax.nn.silu(g)
    return o, s_new.astype(s0.dtype)


def parallel_mixer(xn, k_hist, v_hist, hist_valid, s0, pos0, w_in, attn_sink, rel_bias,
                   w_branch_attn, w_branch_ret, w_out):
    B, L, _ = xn.shape
    widths = [ATTN_HEADS * HEAD_DIM, KV_HEADS * HEAD_DIM, KV_HEADS * HEAD_DIM,
              RET_HEADS * RET_DK, RET_HEADS * RET_DK, RET_HEADS * RET_DV, RET_HEADS * RET_DV,
              D_MODEL, D_MODEL]
    cuts = np.cumsum(widths)[:-1].tolist()
    q_a, k_a, v_a, q_r, k_r, v_r, g_r, gate_a, gate_r = jnp.split(xn @ w_in, cuts, axis=-1)
    q_a = q_a.reshape(B, L, KV_HEADS, GQA_GROUP, HEAD_DIM)
    k_a = k_a.reshape(B, L, KV_HEADS, HEAD_DIM)
    v_a = v_a.reshape(B, L, KV_HEADS, HEAD_DIM)
    attn = banded_sink_attention(q_a, k_a, v_a, k_hist, v_hist, hist_valid, attn_sink, rel_bias)
    ret, s_new = retention_branch(q_r, k_r, v_r, g_r, s0, pos0)
    merged = jax.nn.sigmoid(gate_a) * (attn @ w_branch_attn) + jax.nn.sigmoid(gate_r) * (ret @ w_branch_ret)
    new_k = jnp.concatenate([k_hist, k_a], axis=1)[:, -WINDOW:]
    new_v = jnp.concatenate([v_hist, v_a], axis=1)[:, -WINDOW:]
    return merged @ w_out, new_k, new_v, s_new


def grouped_expert_mlp(xf, experts, gates, w_gate, w_up, w_down):
    T, D = xf.shape
    A = experts.size
    flat_e = experts.reshape(-1)
    flat_g = gates.reshape(-1)
    order = jnp.argsort(flat_e)
    sorted_e = flat_e[order]
    counts = jnp.bincount(flat_e, length=N_EXPERTS)
    padded = (counts + MOE_BLOCK - 1) // MOE_BLOCK * MOE_BLOCK
    pad_end = jnp.cumsum(padded)
    pad_start = pad_end - padded
    grp_start = jnp.cumsum(counts) - counts
    dest = pad_start[sorted_e] + jnp.arange(A) - grp_start[sorted_e]
    n_blocks = -(-A // MOE_BLOCK) + N_EXPERTS
    slot_tok = jnp.full((n_blocks * MOE_BLOCK,), T, jnp.int32).at[dest].set((order // TOP_K).astype(jnp.int32))
    slot_gate = jnp.zeros((n_blocks * MOE_BLOCK,), jnp.float32).at[dest].set(flat_g[order])
    block_expert = jnp.minimum(jnp.searchsorted(pad_end, jnp.arange(n_blocks) * MOE_BLOCK, side='right'),
                               N_EXPERTS - 1)
    x_pad = jnp.concatenate([xf, jnp.zeros((1, D), xf.dtype)], axis=0)

    def run_block(args):
        tok, e = args
        xb = x_pad[tok]
        hb = jax.nn.silu(xb @ w_gate[e]) * (xb @ w_up[e])
        return hb @ w_down[e]

    yb = lax.map(run_block, (slot_tok.reshape(n_blocks, MOE_BLOCK), block_expert))
    y = jnp.zeros((T + 1, D), jnp.float32).at[slot_tok].add(yb.reshape(-1, D).astype(jnp.float32) * slot_gate[:, None])
    return y[:T].astype(xf.dtype)


def hierarchical_moe(xn, w_router_group, b_router_group, w_router_expert, b_router_expert, w_gate, w_up, w_down):
    B, L, D = xn.shape
    xf = xn.reshape(B * L, D)
    x32 = xf.astype(jnp.float32)
    p_group = jax.nn.softmax(x32 @ w_router_group.astype(jnp.float32) + b_router_group.astype(jnp.float32), axis=-1)
    g_idx = jnp.argmax(p_group, axis=-1).astype(jnp.int32)
    g_w = jnp.max(p_group, axis=-1)
    fine = (x32 @ w_router_expert.astype(jnp.float32) + b_router_expert.astype(jnp.float32))
    fine = fine.reshape(B * L, N_GROUPS, EXPERTS_PER_GROUP)
    fine = jnp.einsum('tge,tg->te', fine, jax.nn.one_hot(g_idx, N_GROUPS, dtype=jnp.float32))
    top_p, top_i = lax.top_k(jax.nn.softmax(fine, axis=-1), TOP_K)
    gates = g_w[:, None] * top_p / jnp.sum(top_p, axis=-1, keepdims=True)
    experts = g_idx[:, None] * EXPERTS_PER_GROUP + top_i.astype(jnp.int32)
    y = grouped_expert_mlp(xf, experts, gates, w_gate, w_up, w_down)
    return y.reshape(B, L, D)


def decoder_layer(x, k_hist, v_hist, hist_valid, s0, pos0, norm_mix, w_in, attn_sink, rel_bias,
                  w_branch_attn, w_branch_ret, w_out, norm_ffn, w_router_group, b_router_group,
                  w_router_expert, b_router_expert, w_gate, w_up, w_down):
    h, new_k, new_v, s_new = parallel_mixer(rms_norm(x, norm_mix), k_hist, v_hist, hist_valid, s0, pos0,
                                            w_in, attn_sink, rel_bias, w_branch_attn, w_branch_ret, w_out)
    x = x + h
    x = x + hierarchical_moe(rms_norm(x, norm_ffn), w_router_group, b_router_group,
                             w_router_expert, b_router_expert, w_gate, w_up, w_down)
    return x, new_k, new_v, s_new


def setup_inputs(seed: int = 0) -> dict:
    key = jax.random.key(seed)
    ks = jax.random.split(key, 24)

    def nrm(k, shape, scale):
        return jax.random.normal(k, shape, jnp.float32) * scale

    attn_w = ATTN_HEADS * HEAD_DIM
    ret_w = RET_HEADS * RET_DV
    return {
        'x_prompt': nrm(ks[0], (BATCH, SEQ, D_MODEL), 1.0),
        'x_sample': nrm(ks[1], (DEC_BATCH, DEC_SEQ, D_MODEL), 1.0),
        'cache_k': nrm(ks[2], (DEPTH, DEC_BATCH, WINDOW, KV_HEADS, HEAD_DIM), 1.0),
        'cache_v': nrm(ks[3], (DEPTH, DEC_BATCH, WINDOW, KV_HEADS, HEAD_DIM), 1.0),
        'state_ret': nrm(ks[4], (DEPTH, DEC_BATCH, RET_HEADS, RET_DK, RET_DV), 0.3),
        'norm_mix': 1.0 + nrm(ks[5], (DEPTH, D_MODEL), 0.02),
        'w_in': nrm(ks[6], (DEPTH, D_MODEL, IN_WIDTH), D_MODEL ** -0.5),
        'attn_sink': nrm(ks[7], (DEPTH, ATTN_HEADS), 0.5),
        'rel_bias': nrm(ks[8], (N_BUCKETS, ATTN_HEADS), 0.5),
        'w_branch_attn': nrm(ks[9], (DEPTH, attn_w, D_MODEL), attn_w ** -0.5),
        'w_branch_ret': nrm(ks[10], (DEPTH, ret_w, D_MODEL), ret_w ** -0.5),
        'w_out': nrm(ks[11], (DEPTH, D_MODEL, D_MODEL), D_MODEL ** -0.5),
        'norm_ffn': 1.0 + nrm(ks[12], (DEPTH, D_MODEL), 0.02),
        'w_router_group': nrm(ks[13], (DEPTH, D_MODEL, N_GROUPS), D_MODEL ** -0.5),
        'b_router_group': nrm(ks[14], (DEPTH, N_GROUPS), 0.01),
        'w_router_expert': nrm(ks[15], (DEPTH, D_MODEL, N_EXPERTS), D_MODEL ** -0.5),
        'b_router_expert': nrm(ks[16], (DEPTH, N_EXPERTS), 0.01),
        'w_gate': nrm(ks[17], (DEPTH, N_EXPERTS, D_MODEL, EXPERT_FF), D_MODEL ** -0.5),
        'w_up': nrm(ks[18], (DEPTH, N_EXPERTS, D_MODEL, EXPERT_FF), D_MODEL ** -0.5),
        'w_down': nrm(ks[19], (DEPTH, N_EXPERTS, EXPERT_FF, D_MODEL), EXPERT_FF ** -0.5),
        'norm_final': 1.0 + nrm(ks[20], (D_MODEL,), 0.02),
    }


def reference(x_prompt, x_sample, cache_k, cache_v, state_ret, norm_mix, w_in, attn_sink, rel_bias,
              w_branch_attn, w_branch_ret, w_out, norm_ffn, w_router_group, b_router_group,
              w_router_expert, b_router_expert, w_gate, w_up, w_down, norm_final):
    yp, ys = x_prompt, x_sample
    bp = x_prompt.shape[0]
    pk, pv, ps, sk, sv, ss = [], [], [], [], [], []
    for layer in range(DEPTH):
        weights = (norm_mix[layer], w_in[layer], attn_sink[layer], rel_bias, w_branch_attn[layer],
                   w_branch_ret[layer], w_out[layer], norm_ffn[layer], w_router_group[layer],
                   b_router_group[layer], w_router_expert[layer], b_router_expert[layer],
                   w_gate[layer], w_up[layer], w_down[layer])
        hist0 = jnp.zeros((bp, WINDOW, KV_HEADS, HEAD_DIM), x_prompt.dtype)
        s_zero = jnp.zeros((bp, RET_HEADS, RET_DK, RET_DV), state_ret.dtype)
        yp, k1, v1, s1 = decoder_layer(yp, hist0, hist0, False, s_zero, 0, *weights)
        ys, k2, v2, s2 = decoder_layer(ys, cache_k[layer], cache_v[layer], True, state_ret[layer], PAST_LEN, *weights)
        pk.append(k1)
        pv.append(v1)
        ps.append(s1)
        sk.append(k2)
        sv.append(v2)
        ss.append(s2)
    return (rms_norm(yp, norm_final), rms_norm(ys, norm_final), jnp.stack(pk), jnp.stack(pv), jnp.stack(ps),
            jnp.stack(sk), jnp.stack(sv), jnp.stack(ss))
```

```python
import functools
import math

import jax
import jax.numpy as jnp
import numpy as np
from jax import lax
from jax.experimental import pallas as pl
from jax.experimental.pallas import tpu as pltpu

HEAD_DIM = 64
KV_HEADS = 4
GQA_GROUP = 4
ATTN_HEADS = KV_HEADS * GQA_GROUP
WINDOW = 128
N_BUCKETS = 32
MAX_DISTANCE = 128
RET_HEADS = 4
RET_DK = 128
RET_DV = 256
RET_CHUNK = 128
ROPE_BASE = 10000.0
N_GROUPS = 4
EXPERTS_PER_GROUP = 8
N_EXPERTS = N_GROUPS * EXPERTS_PER_GROUP
EXPERT_FF = 512
NORM_EPS = 1e-6
NEG_INF = -1e30
PAST_LEN = 16384

ATTN_W = ATTN_HEADS * HEAD_DIM
KV_W = KV_HEADS * HEAD_DIM
RQ_W = RET_HEADS * RET_DK
RV_W = RET_HEADS * RET_DV
OFF_QA = 0
OFF_KA = OFF_QA + ATTN_W
OFF_VA = OFF_KA + KV_W
OFF_QR = OFF_VA + KV_W
OFF_KR = OFF_QR + RQ_W
OFF_VR = OFF_KR + RQ_W
OFF_GR = OFF_VR + RV_W
OFF_GA = OFF_GR + RV_W
OFF_GT = OFF_GA + 1024
ROUTE_LANES = 128

LANE = 128
SUB = 8
PROMPT_TM = 256
SAMPLE_GROUP = 8
MOE_TM = 256
ROW_TM = 256
VMEM_LIMIT = 60 * 1024 * 1024

_F32 = jnp.float32
_BF16 = jnp.bfloat16


def _const_spec(shape):
    nd = len(shape)
    return pl.BlockSpec(shape, lambda *_: (0,) * nd, pipeline_mode=pl.Buffered(1))


def _rms(x, gain):
    return x * lax.rsqrt(jnp.mean(x * x, axis=-1, keepdims=True) + NORM_EPS) * gain


def _dot(a, b):
    return jnp.dot(a, b, preferred_element_type=_F32)


def _dot_nt(a, b):
    return lax.dot_general(a, b, (((1,), (1,)), ((), ())), preferred_element_type=_F32)


def _dot_tn(a, b):
    return lax.dot_general(a, b, (((0,), (0,)), ((), ())), preferred_element_type=_F32)


def _load_rows(ref, n):
    return jnp.concatenate([ref[pl.ds(s, n, stride=SUB), :] for s in range(SUB)], axis=1)


def _store_rows(ref, val):
    n = val.shape[0]
    for s in range(SUB):
        ref[pl.ds(s, n, stride=SUB), :] = val[:, s * LANE:(s + 1) * LANE]


def _rotary(x, cosf, sinf):
    return x * cosf + pltpu.roll(x, RET_DK // 2, 1) * sinf


def _post(x, attn, ret, gate_a, gate_r, wba, wbr, wout, nffn, wrh, wrl, br):
    merged = (jax.nn.sigmoid(gate_a) * _dot(attn.astype(_BF16), wba)
              + jax.nn.sigmoid(gate_r) * _dot(ret.astype(_BF16), wbr))
    x1 = x + _dot(merged.astype(_BF16), wout)
    xn2 = _rms(x1, nffn)
    hi = xn2.astype(_BF16)
    lo = (xn2 - hi.astype(_F32)).astype(_BF16)
    logits = _dot(hi, wrh) + (_dot(hi, wrl) + _dot(lo, wrh)) + br
    n = logits.shape[0]
    lane = lax.broadcasted_iota(jnp.int32, (n, ROUTE_LANES), 1)
    big = jnp.int32(1 << 20)
    neg = jnp.float32(-jnp.inf)
    gl = jnp.where(lane < N_GROUPS, logits, neg)
    gmax = jnp.max(gl, axis=-1, keepdims=True)
    gexp = jnp.exp(gl - gmax)
    gsum = jnp.sum(gexp, axis=-1, keepdims=True)
    pg = gexp / gsum
    g_w = jnp.max(pg, axis=-1, keepdims=True)
    g_idx = jnp.min(jnp.where(pg == g_w, lane, big), axis=-1, keepdims=True)
    e_lane = lane - N_GROUPS
    emask = (e_lane >= 0) & (e_lane < N_EXPERTS) & ((e_lane >> 3) == g_idx)
    fl = jnp.where(emask, logits, neg)
    fmax = jnp.max(fl, axis=-1, keepdims=True)
    fexp = jnp.exp(fl - fmax)
    fsum = jnp.sum(fexp, axis=-1, keepdims=True)
    pe = jnp.where(emask, fexp / fsum, -1.0)
    p1 = jnp.max(pe, axis=-1, keepdims=True)
    i1 = jnp.min(jnp.where(pe == p1, lane, big), axis=-1, keepdims=True)
    pe2 = jnp.where(lane == i1, -1.0, pe)
    p2 = jnp.max(pe2, axis=-1, keepdims=True)
    i2 = jnp.min(jnp.where(pe2 == p2, lane, big), axis=-1, keepdims=True)
    psum = p1 + p2
    gate1 = g_w * p1 / psum
    gate2 = g_w * p2 / psum
    route = jnp.where(lane == 0, (i1 - N_GROUPS).astype(_F32),
                      jnp.where(lane == 1, (i2 - N_GROUPS).astype(_F32),
                                jnp.where(lane == 2, gate1,
                                          jnp.where(lane == 3, gate2, 0.0))))
    return x1, route


def _prompt_mixer_kernel(sink_ref, cdec_ref,
                         x_ref, nmix_ref, win_ref, bias_ref, cos_ref, sin_ref, dec_ref,
                         qd_ref, kd_ref, wba_ref, wbr_ref, wout_ref, nffn_ref,
                         wrh_ref, wrl_ref, br_ref,
                         x1_ref, route_ref, knew_ref, vnew_ref, s_ref,
                         proj, kctx, vctx, attn, ret):
    i = pl.program_id(1)
    last = pl.num_programs(1) - 1
    tm = x_ref.shape[1]
    n_sub = tm // WINDOW
    scale = HEAD_DIM ** -0.5

    @pl.when(i == 0)
    def _():
        s_ref[...] = jnp.zeros_like(s_ref)
        kctx[0:WINDOW, :] = jnp.zeros((WINDOW, KV_W), _BF16)
        vctx[0:WINDOW, :] = jnp.zeros((WINDOW, KV_W), _BF16)

    x = x_ref[0]
    xn = _rms(x, nmix_ref[...]).astype(_BF16)
    n_in = win_ref.shape[1]
    panel = 512
    for c0 in range(0, n_in, panel):
        proj[:, c0:c0 + panel] = _dot(xn, win_ref[:, c0:c0 + panel])

    @pl.when(i == last)
    def _():
        knew_ref[0] = proj[tm - WINDOW:tm, OFF_KA:OFF_KA + KV_W]
        vnew_ref[0] = proj[tm - WINDOW:tm, OFF_VA:OFF_VA + KV_W]

    col = lax.broadcasted_iota(jnp.int32, (1, 2 * WINDOW), 1)
    for c in range(n_sub):
        r0 = c * WINDOW
        kctx[WINDOW:2 * WINDOW, :] = proj[r0:r0 + WINDOW, OFF_KA:OFF_KA + KV_W].astype(_BF16)
        vctx[WINDOW:2 * WINDOW, :] = proj[r0:r0 + WINDOW, OFF_VA:OFF_VA + KV_W].astype(_BF16)
        if c == 0:
            pen = jnp.where((col < WINDOW) & (i == 0), NEG_INF, 0.0).astype(_F32)
        for h in range(KV_HEADS):
            k_h = kctx[:, h * HEAD_DIM:(h + 1) * HEAD_DIM]
            v_h = vctx[:, h * HEAD_DIM:(h + 1) * HEAD_DIM]
            for g in range(GQA_GROUP):
                hq = h * GQA_GROUP + g
                q = proj[r0:r0 + WINDOW, hq * HEAD_DIM:(hq + 1) * HEAD_DIM].astype(_BF16)
                s = _dot_nt(q, k_h) * scale + bias_ref[hq]
                if c == 0:
                    s = s + pen
                snk = sink_ref[hq]
                m = jnp.maximum(jnp.max(s, axis=-1, keepdims=True), snk)
                p = jnp.exp(s - m)
                den = jnp.sum(p, axis=-1, keepdims=True) + jnp.exp(snk - m)
                p = p * (1.0 / den)
                attn[r0:r0 + WINDOW, hq * HEAD_DIM:(hq + 1) * HEAD_DIM] = _dot(p.astype(_BF16), v_h)
        kctx[0:WINDOW, :] = kctx[WINDOW:2 * WINDOW, :]
        vctx[0:WINDOW, :] = vctx[WINDOW:2 * WINDOW, :]

    cosf = cos_ref[...]
    sinf = sin_ref[...]
    for h in range(RET_HEADS):
        qrot = _rotary(proj[:, OFF_QR + h * RET_DK:OFF_QR + (h + 1) * RET_DK], cosf, sinf)
        krot = _rotary(proj[:, OFF_KR + h * RET_DK:OFF_KR + (h + 1) * RET_DK], cosf, sinf) * (RET_DK ** -0.5)
        qd = qd_ref[:, h * RET_DK:(h + 1) * RET_DK]
        kd = kd_ref[:, h * RET_DK:(h + 1) * RET_DK]
        for c in range(n_sub):
            r0 = c * RET_CHUNK
            qc = qrot[r0:r0 + RET_CHUNK]
            kc = krot[r0:r0 + RET_CHUNK]
            vc = proj[r0:r0 + RET_CHUNK, OFF_VR + h * RET_DV:OFF_VR + (h + 1) * RET_DV].astype(_BF16)
            sc = _dot_nt(qc.astype(_BF16), kc.astype(_BF16)) * dec_ref[h]
            s_old = s_ref[0, h]
            o = _dot(sc.astype(_BF16), vc) + _dot((qc * qd).astype(_BF16), s_old.astype(_BF16))
            s_ref[0, h] = s_old * cdec_ref[h] + _dot_tn((kc * kd).astype(_BF16), vc)
            o = o * lax.rsqrt(jnp.mean(o * o, axis=-1, keepdims=True) + NORM_EPS)
            gr = proj[r0:r0 + RET_CHUNK, OFF_GR + h * RET_DV:OFF_GR + (h + 1) * RET_DV]
            ret[r0:r0 + RET_CHUNK, h * RET_DV:(h + 1) * RET_DV] = o * (gr * jax.nn.sigmoid(gr))

    x1, route = _post(x, attn[...], ret[...],
                      proj[:, OFF_GA:OFF_GA + 1024], proj[:, OFF_GT:OFF_GT + 1024],
                      wba_ref[...], wbr_ref[...], wout_ref[...], nffn_ref[...],
                      wrh_ref[...], wrl_ref[...], br_ref[...])
    _store_rows(x1_ref, x1)
    route_ref[0] = route


def _prompt_mixer(x, cst, w):
    B, L, D = x.shape
    tm = min(PROMPT_TM, L)
    nb = L // tm
    n_in = w['win'].shape[1]
    step = lambda b, i, *_: (b, i, 0)
    per_b = lambda b, i, *_: (b, 0, 0)
    grid_spec = pltpu.PrefetchScalarGridSpec(
        num_scalar_prefetch=2,
        grid=(B, nb),
        in_specs=[
            pl.BlockSpec((1, tm, D), step),
            _const_spec((1, D)),
            _const_spec((D, n_in)),
            _const_spec((ATTN_HEADS, WINDOW, 2 * WINDOW)),
            pl.BlockSpec((tm, RET_DK), lambda b, i, *_: (i, 0)),
            pl.BlockSpec((tm, RET_DK), lambda b, i, *_: (i, 0)),
            _const_spec((RET_HEADS, RET_CHUNK, RET_CHUNK)),
            _const_spec((RET_CHUNK, RQ_W)),
            _const_spec((RET_CHUNK, RQ_W)),
            _const_spec((ATTN_W, D)),
            _const_spec((RV_W, D)),
            _const_spec((D, D)),
            _const_spec((1, D)),
            _const_spec((D, ROUTE_LANES)),
            _const_spec((D, ROUTE_LANES)),
            _const_spec((1, ROUTE_LANES)),
        ],
        out_specs=[
            pl.BlockSpec((tm * SUB, LANE), lambda b, i, *_: (b * nb + i, 0)),
            pl.BlockSpec((1, tm, ROUTE_LANES), step),
            pl.BlockSpec((1, WINDOW, KV_W), per_b),
            pl.BlockSpec((1, WINDOW, KV_W), per_b),
            pl.BlockSpec((1, RET_HEADS, RET_DK, RET_DV), lambda b, i, *_: (b, 0, 0, 0)),
        ],
        scratch_shapes=[
            pltpu.VMEM((tm, n_in), _F32),
            pltpu.VMEM((2 * WINDOW, KV_W), _BF16),
            pltpu.VMEM((2 * WINDOW, KV_W), _BF16),
            pltpu.VMEM((tm, ATTN_W), _F32),
            pltpu.VMEM((tm, RV_W), _F32),
        ],
    )
    assert D == SUB * LANE
    out_shape = [
        jax.ShapeDtypeStruct((B * L * SUB, LANE), _F32),
        jax.ShapeDtypeStruct((B, L, ROUTE_LANES), _F32),
        jax.ShapeDtypeStruct((B, WINDOW, KV_W), _F32),
        jax.ShapeDtypeStruct((B, WINDOW, KV_W), _F32),
        jax.ShapeDtypeStruct((B, RET_HEADS, RET_DK, RET_DV), _F32),
    ]
    return pl.pallas_call(
        _prompt_mixer_kernel,
        grid_spec=grid_spec,
        out_shape=out_shape,
        compiler_params=pltpu.CompilerParams(
            dimension_semantics=("arbitrary", "arbitrary"), vmem_limit_bytes=VMEM_LIMIT),
        name="prompt_mixer",
    )(cst['sink'], cst['cdec_p'],
      x, w['nmix'], w['win'], cst['bias_p'], cst['cos_p'], cst['sin_p'], cst['dec_p'],
      cst['qd_p'], cst['kd_p'], w['wba'], w['wbr'], w['wout'], w['nffn'],
      w['wrh'], w['wrl'], w['br'])


def _inproj_kernel(x_ref, nmix_ref, win_ref, o_ref):
    xn = _rms(x_ref[...], nmix_ref[...]).astype(_BF16)
    o_ref[...] = _dot(xn, win_ref[...])


def _sample_inproj(x2d, w):
    T, D = x2d.shape
    n_in = w['win'].shape[1]
    panel = 512
    return pl.pallas_call(
        _inproj_kernel,
        grid=(n_in // panel,),
        in_specs=[pl.BlockSpec((T, D), lambda j: (0, 0)),
                  pl.BlockSpec((1, D), lambda j: (0, 0)),
                  pl.BlockSpec((D, panel), lambda j: (0, j))],
        out_specs=pl.BlockSpec((T, panel), lambda j: (0, j)),
        out_shape=jax.ShapeDtypeStruct((T, n_in), _F32),
        compiler_params=pltpu.CompilerParams(
            dimension_semantics=("arbitrary",), vmem_limit_bytes=VMEM_LIMIT),
        name="sample_inproj",
    )(x2d, w['nmix'], w['win'])


def _sample_core_kernel(cdec_ref, proj_ref, ck_ref, cv_ref, st_ref, bh_ref, bn_ref, snk_ref,
                        cos_ref, sin_ref, dec_ref, qd_ref, kd_ref,
                        attn_ref, ret_ref, nk_ref, nv_ref, ns_ref):
    G = proj_ref.shape[0]
    ls = proj_ref.shape[1]
    scale = HEAD_DIM ** -0.5
    cosf = cos_ref[...]
    sinf = sin_ref[...]

    def body(b, carry):
        row = proj_ref[b]
        k_new = row[:, OFF_KA:OFF_KA + KV_W]
        v_new = row[:, OFF_VA:OFF_VA + KV_W]
        ck = ck_ref[b]
        cv = cv_ref[b]
        nk_ref[b, 0:WINDOW - ls, :] = ck[ls:WINDOW, :]
        nk_ref[b, WINDOW - ls:WINDOW, :] = k_new
        nv_ref[b, 0:WINDOW - ls, :] = cv[ls:WINDOW, :]
        nv_ref[b, WINDOW - ls:WINDOW, :] = v_new
        ckb = ck.astype(_BF16)
        cvb = cv.astype(_BF16)
        knb = k_new.astype(_BF16)
        vnb = v_new.astype(_BF16)
        for h in range(KV_HEADS):
            hs = slice(h * HEAD_DIM, (h + 1) * HEAD_DIM)
            q4 = jnp.concatenate(
                [row[:, (h * GQA_GROUP + g) * HEAD_DIM:(h * GQA_GROUP + g + 1) * HEAD_DIM]
                 for g in range(GQA_GROUP)], axis=0).astype(_BF16)
            s1 = _dot_nt(q4, ckb[:, hs]) * scale + bh_ref[h]
            s2 = _dot_nt(q4, knb[:, hs]) * scale + bn_ref[h]
            snk = snk_ref[h]
            m = jnp.maximum(jnp.maximum(jnp.max(s1, axis=-1, keepdims=True),
                                        jnp.max(s2, axis=-1, keepdims=True)), snk)
            p1 = jnp.exp(s1 - m)
            p2 = jnp.exp(s2 - m)
            den = (jnp.sum(p1, axis=-1, keepdims=True) + jnp.sum(p2, axis=-1, keepdims=True)
                   + jnp.exp(snk - m))
            r = 1.0 / den
            o = _dot((p1 * r).astype(_BF16), cvb[:, hs]) + _dot((p2 * r).astype(_BF16), vnb[:, hs])
            for g in range(GQA_GROUP):
                hq = h * GQA_GROUP + g
                attn_ref[b, :, hq * HEAD_DIM:(hq + 1) * HEAD_DIM] = o[g * ls:(g + 1) * ls]
        for h in range(RET_HEADS):
            qrot = _rotary(row[:, OFF_QR + h * RET_DK:OFF_QR + (h + 1) * RET_DK], cosf, sinf)
            krot = _rotary(row[:, OFF_KR + h * RET_DK:OFF_KR + (h + 1) * RET_DK], cosf, sinf) * (RET_DK ** -0.5)
            vc = row[:, OFF_VR + h * RET_DV:OFF_VR + (h + 1) * RET_DV].astype(_BF16)
            sc = _dot_nt(qrot.astype(_BF16), krot.astype(_BF16)) * dec_ref[h]
            s_old = st_ref[b, h]
            qd = qd_ref[:, h * RET_DK:(h + 1) * RET_DK]
            kd = kd_ref[:, h * RET_DK:(h + 1) * RET_DK]
            o = _dot(sc.astype(_BF16), vc) + _dot((qrot * qd).astype(_BF16), s_old.astype(_BF16))
            ns_ref[b, h] = s_old * cdec_ref[h] + _dot_tn((krot * kd).astype(_BF16), vc)
            o = o * lax.rsqrt(jnp.mean(o * o, axis=-1, keepdims=True) + NORM_EPS)
            gr = row[:, OFF_GR + h * RET_DV:OFF_GR + (h + 1) * RET_DV]
            ret_ref[b, :, h * RET_DV:(h + 1) * RET_DV] = o * (gr * jax.nn.sigmoid(gr))
        return carry

    lax.fori_loop(0, G, body, 0)


def _sample_core(proj3, ck, cv, st, cst):
    NB, ls, n_in = proj3.shape
    G = min(SAMPLE_GROUP, NB)
    blk3 = lambda i, *_: (i, 0, 0)
    blk4 = lambda i, *_: (i, 0, 0, 0)
    c2 = lambda i, *_: (0, 0)
    c3 = lambda i, *_: (0, 0, 0)
    ql = GQA_GROUP * ls
    grid_spec = pltpu.PrefetchScalarGridSpec(
        num_scalar_prefetch=1,
        grid=(NB // G,),
        in_specs=[
            pl.BlockSpec((G, ls, n_in), blk3),
            pl.BlockSpec((G, WINDOW, KV_W), blk3),
            pl.BlockSpec((G, WINDOW, KV_W), blk3),
            pl.BlockSpec((G, RET_HEADS, RET_DK, RET_DV), blk4),
            pl.BlockSpec((KV_HEADS, ql, WINDOW), c3),
            pl.BlockSpec((KV_HEADS, ql, ls), c3),
            pl.BlockSpec((KV_HEADS, ql, 1), c3),
            pl.BlockSpec((ls, RET_DK), c2),
            pl.BlockSpec((ls, RET_DK), c2),
            pl.BlockSpec((RET_HEADS, ls, ls), c3),
            pl.BlockSpec((ls, RQ_W), c2),
            pl.BlockSpec((ls, RQ_W), c2),
        ],
        out_specs=[
            pl.BlockSpec((G, ls, ATTN_W), blk3),
            pl.BlockSpec((G, ls, RV_W), blk3),
            pl.BlockSpec((G, WINDOW, KV_W), blk3),
            pl.BlockSpec((G, WINDOW, KV_W), blk3),
            pl.BlockSpec((G, RET_HEADS, RET_DK, RET_DV), blk4),
        ],
    )
    out_shape = [
        jax.ShapeDtypeStruct((NB, ls, ATTN_W), _F32),
        jax.ShapeDtypeStruct((NB, ls, RV_W), _F32),
        jax.ShapeDtypeStruct((NB, WINDOW, KV_W), _F32),
        jax.ShapeDtypeStruct((NB, WINDOW, KV_W), _F32),
        jax.ShapeDtypeStruct((NB, RET_HEADS, RET_DK, RET_DV), _F32),
    ]
    return pl.pallas_call(
        _sample_core_kernel,
        grid_spec=grid_spec,
        out_shape=out_shape,
        compiler_params=pltpu.CompilerParams(
            dimension_semantics=("arbitrary",), vmem_limit_bytes=VMEM_LIMIT),
        name="sample_core",
    )(cst['cdec_s'], proj3, ck, cv, st, cst['bias_hist'], cst['bias_new'], cst['sink_col'],
      cst['cos_s'], cst['sin_s'], cst['dec_s'], cst['qd_s'], cst['kd_s'])


def _sample_post_kernel(x_ref, attn_ref, ret_ref, ga_ref, gt_ref, wba_ref, wbr_ref, wout_ref,
                        nffn_ref, wrh_ref, wrl_ref, br_ref, x1_ref, route_ref):
    x1, route = _post(x_ref[...], attn_ref[...], ret_ref[...], ga_ref[...], gt_ref[...],
                      wba_ref[...], wbr_ref[...], wout_ref[...], nffn_ref[...],
                      wrh_ref[...], wrl_ref[...], br_ref[...])
    _store_rows(x1_ref, x1)
    route_ref[...] = route


def _sample_post(x2d, attn, ret, ga, gt, w):
    T, D = x2d.shape
    full = lambda s: pl.BlockSpec(s, lambda i: (0,) * len(s))
    return pl.pallas_call(
        _sample_post_kernel,
        grid=(1,),
        in_specs=[full((T, D)), full((T, ATTN_W)), full((T, RV_W)), full((T, D)), full((T, D)),
                  full((ATTN_W, D)), full((RV_W, D)), full((D, D)), full((1, D)),
                  full((D, ROUTE_LANES)), full((D, ROUTE_LANES)), full((1, ROUTE_LANES))],
        out_specs=[full((T * SUB, LANE)), full((T, ROUTE_LANES))],
        out_shape=[jax.ShapeDtypeStruct((T * SUB, LANE), _F32),
                   jax.ShapeDtypeStruct((T, ROUTE_LANES), _F32)],
        compiler_params=pltpu.CompilerParams(
            dimension_semantics=("arbitrary",), vmem_limit_bytes=VMEM_LIMIT),
        name="sample_post",
    )(x2d, attn, ret, ga, gt, w['wba'], w['wbr'], w['wout'], w['nffn'], w['wrh'], w['wrl'], w['br'])


def _row_copy(src, dst, s, d, sem):
    return pltpu.make_async_copy(src.at[s], dst.at[d], sem)


def _dispatch_kernel(dest_ref, xp_ref, xq_ref, xs_ref, sem, *, p_steps):
    i = pl.program_id(0)

    def copy_tile(src, src0, tok0):
        def start(r, c):
            t = tok0 + r
            _row_copy(src, xs_ref, src0 + r, dest_ref[2 * t], sem).start()
            _row_copy(src, xs_ref, src0 + r, dest_ref[2 * t + 1], sem).start()
            return c

        lax.fori_loop(0, ROW_TM, start, 0)

        def wait(r, c):
            _row_copy(src, xs_ref, 0, 0, sem).wait()
            _row_copy(src, xs_ref, 0, 0, sem).wait()
            return c

        lax.fori_loop(0, ROW_TM, wait, 0)

    @pl.when(i < p_steps)
    def _():
        copy_tile(xp_ref, i * ROW_TM, i * ROW_TM)

    @pl.when(i >= p_steps)
    def _():
        copy_tile(xq_ref, (i - p_steps) * ROW_TM, i * ROW_TM)


def _dispatch(dest, xp3, xq3):
    Tp, Tq = xp3.shape[0], xq3.shape[0]
    assert Tp % ROW_TM == 0 and Tq % ROW_TM == 0
    grid_spec = pltpu.PrefetchScalarGridSpec(
        num_scalar_prefetch=1,
        grid=((Tp + Tq) // ROW_TM,),
        in_specs=[pl.BlockSpec(memory_space=pl.ANY), pl.BlockSpec(memory_space=pl.ANY)],
        out_specs=pl.BlockSpec(memory_space=pl.ANY),
        scratch_shapes=[pltpu.SemaphoreType.DMA],
    )
    return pl.pallas_call(
        functools.partial(_dispatch_kernel, p_steps=Tp // ROW_TM),
        grid_spec=grid_spec,
        out_shape=jax.ShapeDtypeStruct((2 * (Tp + Tq), SUB, LANE), _F32),
        compiler_params=pltpu.CompilerParams(dimension_semantics=("arbitrary",)),
        name="moe_dispatch",
    )(dest, xp3, xq3)


def _gmm_kernel(tile_ref, exp_ref, lo_ref, hi_ref, first_ref, chg_ref,
                x_ref, nffn_ref, wg_ref, wu_ref, wd_ref, y_ref, wg_s, wu_s, wd_s):
    m = pl.program_id(0)
    tm = x_ref.shape[0] // SUB

    @pl.when(chg_ref[m] == 1)
    def _():
        wg_s[...] = wg_ref[0].astype(_BF16)
        wu_s[...] = wu_ref[0].astype(_BF16)
        wd_s[...] = wd_ref[0].astype(_BF16)

    lo = lo_ref[m]
    hi = hi_ref[m]

    @pl.when(hi > lo)
    def _():
        rows = tile_ref[m] * tm + lax.broadcasted_iota(jnp.int32, (tm, 1), 0)
        mine = (rows >= lo) & (rows < hi)
        xn = _rms(_load_rows(x_ref, tm), nffn_ref[...])
        x = jnp.where(mine, xn, 0.0).astype(_BF16)
        a = _dot(x, wg_s[...])
        hmid = (a * jax.nn.sigmoid(a)) * _dot(x, wu_s[...])
        y = _dot(hmid.astype(_BF16), wd_s[...])

        @pl.when(first_ref[m] == 1)
        def _():
            _store_rows(y_ref, y)

        @pl.when(first_ref[m] == 0)
        def _():
            _store_rows(y_ref, _load_rows(y_ref, tm) + y)


def _gmm(work, xs2, nffn, wg, wu, wd):
    A = xs2.shape[0] // SUB
    E, D, F = wg.shape
    n_work = work[0].shape[0]
    grid_spec = pltpu.PrefetchScalarGridSpec(
        num_scalar_prefetch=6,
        grid=(n_work,),
        in_specs=[
            pl.BlockSpec((MOE_TM * SUB, LANE), lambda m, t, e, *_: (t[m], 0)),
            pl.BlockSpec((1, D), lambda m, t, e, *_: (0, 0)),
            pl.BlockSpec((1, D, F), lambda m, t, e, *_: (e[m], 0, 0)),
            pl.BlockSpec((1, D, F), lambda m, t, e, *_: (e[m], 0, 0)),
            pl.BlockSpec((1, F, D), lambda m, t, e, *_: (e[m], 0, 0)),
        ],
        out_specs=pl.BlockSpec((MOE_TM * SUB, LANE), lambda m, t, e, *_: (t[m], 0)),
        scratch_shapes=[pltpu.VMEM((D, F), _BF16), pltpu.VMEM((D, F), _BF16),
                        pltpu.VMEM((F, D), _BF16)],
    )
    return pl.pallas_call(
        _gmm_kernel,
        grid_spec=grid_spec,
        out_shape=jax.ShapeDtypeStruct((A * SUB, LANE), _F32),
        compiler_params=pltpu.CompilerParams(
            dimension_semantics=("arbitrary",), vmem_limit_bytes=VMEM_LIMIT),
        name="moe_gmm",
    )(*work, xs2, nffn, wg, wu, wd)


def _combine_kernel(dest_ref, x1_ref, route_ref, nfin_ref, yb_ref, o_ref, buf, sem, *, tok0):
    i = pl.program_id(0)
    tm = o_ref.shape[0]
    base = tok0 + i * tm

    def tile_copy(d, k, r):
        return pltpu.make_async_copy(yb_ref.at[d], buf.at[k, r], sem)

    def start(r, c):
        t = base + r
        tile_copy(dest_ref[2 * t], 0, r).start()
        tile_copy(dest_ref[2 * t + 1], 1, r).start()
        return c

    lax.fori_loop(0, tm, start, 0)

    def wait(r, c):
        tile_copy(0, 0, 0).wait()
        tile_copy(0, 1, 0).wait()
        return c

    lax.fori_loop(0, tm, wait, 0)
    route = route_ref[...]
    g0 = route[:, 2:3]
    g1 = route[:, 3:4]
    y0 = jnp.concatenate([buf[0, :, s, :] for s in range(SUB)], axis=1)
    y1 = jnp.concatenate([buf[1, :, s, :] for s in range(SUB)], axis=1)
    y = _load_rows(x1_ref, tm) + (y0 * g0 + y1 * g1)
    o_ref[...] = _rms(y, nfin_ref[...])


def _combine(dest, x1_2, route, nfin, yb3, tok0):
    T = x1_2.shape[0] // SUB
    D = SUB * LANE
    tm = min(ROW_TM, T)
    grid_spec = pltpu.PrefetchScalarGridSpec(
        num_scalar_prefetch=1,
        grid=(T // tm,),
        in_specs=[
            pl.BlockSpec((tm * SUB, LANE), lambda i, *_: (i, 0)),
            pl.BlockSpec((tm, ROUTE_LANES), lambda i, *_: (i, 0)),
            pl.BlockSpec((1, D), lambda i, *_: (0, 0)),
            pl.BlockSpec(memory_space=pl.ANY),
        ],
        out_specs=pl.BlockSpec((tm, D), lambda i, *_: (i, 0)),
        scratch_shapes=[pltpu.VMEM((2, tm, SUB, LANE), _F32), pltpu.SemaphoreType.DMA],
    )
    return pl.pallas_call(
        functools.partial(_combine_kernel, tok0=tok0),
        grid_spec=grid_spec,
        out_shape=jax.ShapeDtypeStruct((T, D), _F32),
        compiler_params=pltpu.CompilerParams(
            dimension_semantics=("arbitrary",), vmem_limit_bytes=VMEM_LIMIT),
        name="moe_combine",
    )(dest, x1_2, route, nfin, yb3)


def _routing_tables(experts, n_tiles):
    flat_e = experts.reshape(-1)
    A = flat_e.shape[0]
    onehot = (flat_e[:, None] == jnp.arange(N_EXPERTS, dtype=jnp.int32)[None, :]).astype(jnp.int32)
    csum = jnp.cumsum(onehot, axis=0)
    counts = csum[-1]
    ends = jnp.cumsum(counts)
    starts = ends - counts
    dest = jnp.sum(onehot * (starts[None, :] + csum - 1), axis=1).astype(jnp.int32)
    tile_starts = jnp.arange(n_tiles, dtype=jnp.int32) * MOE_TM
    pts = jnp.sort(jnp.concatenate([tile_starts, starts.astype(jnp.int32)]))
    lo = pts
    hi = jnp.concatenate([pts[1:], jnp.array([A], jnp.int32)])
    tile = jnp.minimum(lo // MOE_TM, n_tiles - 1).astype(jnp.int32)
    expert = jnp.minimum(jnp.searchsorted(ends, lo, side='right'), N_EXPERTS - 1).astype(jnp.int32)
    nonempty = hi > lo
    first = (nonempty & (lo % MOE_TM == 0)).astype(jnp.int32)
    chg = jnp.concatenate([jnp.ones((1,), jnp.int32), (expert[1:] != expert[:-1]).astype(jnp.int32)])
    return dest, (tile, expert, lo, hi, first, chg)


def _bucket_table(lq, lk):
    dist = np.arange(lq)[:, None] + WINDOW - np.arange(lk)[None, :]
    band = (dist >= 0) & (dist < WINDOW)
    d = np.clip(dist, 0, WINDOW - 1)
    max_exact = N_BUCKETS // 2
    d_f = np.maximum(d, 1).astype(np.float32)
    large = max_exact + (np.log(d_f / max_exact) / math.log(MAX_DISTANCE / max_exact)
                         * (N_BUCKETS - max_exact)).astype(np.int32)
    large = np.minimum(large, N_BUCKETS - 1)
    return np.where(d < max_exact, d, large).astype(np.int32), band


def _decay_tables(C):
    log_gamma = jnp.log(1.0 - 2.0 ** (-5.0 - jnp.arange(RET_HEADS, dtype=_F32)))
    idx = jnp.arange(C, dtype=_F32)
    diff = idx[:, None] - idx[None, :]
    decay_in = jnp.where((diff >= 0)[..., None],
                         jnp.exp(jnp.maximum(diff, 0.0)[..., None] * log_gamma), 0.0)
    q_dec = jnp.exp((idx + 1.0)[:, None] * log_gamma)
    k_dec = jnp.exp((C - 1.0 - idx)[:, None] * log_gamma)
    c_dec = jnp.exp(C * log_gamma)
    dec = jnp.transpose(decay_in, (2, 0, 1))
    qd = jnp.repeat(q_dec, RET_DK, axis=1)
    kd = jnp.repeat(k_dec, RET_DK, axis=1)
    return dec, qd, kd, c_dec


def _rope_tables(pos):
    half = RET_DK // 2
    inv = ROPE_BASE ** (-jnp.arange(half, dtype=_F32) * 2.0 / RET_DK)
    ang = pos.astype(_F32)[:, None] * inv[None, :]
    cos = jnp.cos(ang)
    sin = jnp.sin(ang)
    return jnp.concatenate([cos, cos], axis=1), jnp.concatenate([-sin, sin], axis=1)


def _constants(rel_bias, attn_sink, L, ls):
    cst = {}
    rb = rel_bias.astype(_F32)
    bkt, band = _bucket_table(WINDOW, 2 * WINDOW)
    bias = jnp.transpose(rb[bkt], (2, 0, 1))
    cst['bias_p'] = jnp.where(band[None], bias, NEG_INF)
    cst['sink'] = attn_sink.astype(_F32)
    cst['cos_p'], cst['sin_p'] = _rope_tables(jnp.arange(L))
    cst['dec_p'], cst['qd_p'], cst['kd_p'], cst['cdec_p'] = _decay_tables(min(RET_CHUNK, L))
    bkt, band = _bucket_table(ls, WINDOW + ls)
    bias = jnp.where(band[None], jnp.transpose(rb[bkt], (2, 0, 1)), NEG_INF)
    bias = bias.reshape(KV_HEADS, GQA_GROUP * ls, WINDOW + ls)
    cst['bias_hist'] = bias[:, :, :WINDOW]
    cst['bias_new'] = bias[:, :, WINDOW:]
    cst['sink_col'] = jnp.repeat(attn_sink.astype(_F32).reshape(KV_HEADS, GQA_GROUP), ls,
                                 axis=1)[..., None]
    cst['cos_s'], cst['sin_s'] = _rope_tables(PAST_LEN + jnp.arange(ls))
    cst['dec_s'], cst['qd_s'], cst['kd_s'], cst['cdec_s'] = _decay_tables(min(RET_CHUNK, ls))
    return cst


def _layer_weights(layer, norm_mix, w_in, w_branch_attn, w_branch_ret, w_out, norm_ffn,
                   w_router_group, b_router_group, w_router_expert, b_router_expert):
    D = w_in.shape[1]
    wr = jnp.concatenate([w_router_group[layer].astype(_F32), w_router_expert[layer].astype(_F32)], axis=1)
    wr = jnp.pad(wr, ((0, 0), (0, ROUTE_LANES - wr.shape[1])))
    wrh = wr.astype(_BF16)
    wrl = (wr - wrh.astype(_F32)).astype(_BF16)
    br = jnp.concatenate([b_router_group[layer].astype(_F32), b_router_expert[layer].astype(_F32)])
    br = jnp.pad(br, (0, ROUTE_LANES - br.shape[0]))[None, :]
    return {
        'nmix': norm_mix[layer].astype(_F32)[None, :],
        'win': w_in[layer].astype(_BF16),
        'wba': w_branch_attn[layer].astype(_BF16),
        'wbr': w_branch_ret[layer].astype(_BF16),
        'wout': w_out[layer].astype(_BF16),
        'nffn': norm_ffn[layer].astype(_F32)[None, :],
        'wrh': wrh, 'wrl': wrl, 'br': br,
    }


def kernel(x_prompt, x_sample, cache_k, cache_v, state_ret, norm_mix, w_in, attn_sink, rel_bias,
           w_branch_attn, w_branch_ret, w_out, norm_ffn, w_router_group, b_router_group,
           w_router_expert, b_router_expert, w_gate, w_up, w_down, norm_final):
    depth = w_in.shape[0]
    assert depth == 1, "the final norm is fused into the MoE combine of the only layer"
    B, L, D = x_prompt.shape
    NB, ls, _ = x_sample.shape
    Tp, Ts = B * L, NB * ls
    nfin = norm_final.astype(_F32)[None, :]
    yp, ys = x_prompt, x_sample
    pk, pv, ps, sk, sv, ss = [], [], [], [], [], []
    for layer in range(depth):
        w = _layer_weights(layer, norm_mix, w_in, w_branch_attn, w_branch_ret, w_out, norm_ffn,
                           w_router_group, b_router_group, w_router_expert, b_router_expert)
        cst = _constants(rel_bias, attn_sink[layer], L, ls)
        x1p, routep, k1, v1, s1 = _prompt_mixer(yp, cst, w)
        ys2 = ys.reshape(Ts, D)
        proj = _sample_inproj(ys2, w)
        attn_s, ret_s, k2, v2, s2 = _sample_core(
            proj.reshape(NB, ls, -1),
            cache_k[layer].reshape(NB, WINDOW, KV_W), cache_v[layer].reshape(NB, WINDOW, KV_W),
            state_ret[layer], cst)
        x1s, routes = _sample_post(ys2, attn_s.reshape(Ts, ATTN_W), ret_s.reshape(Ts, RV_W),
                                   proj[:, OFF_GA:OFF_GA + D], proj[:, OFF_GT:OFF_GT + D], w)
        routep2 = routep.reshape(Tp, ROUTE_LANES)
        experts = jnp.concatenate([routep2[:, 0:2], routes[:, 0:2]], axis=0).astype(jnp.int32)
        n_rows = 2 * (Tp + Ts)
        assert n_rows % MOE_TM == 0
        dest, work = _routing_tables(experts, n_rows // MOE_TM)
        xs3 = _dispatch(dest, x1p.reshape(Tp, SUB, LANE), x1s.reshape(Ts, SUB, LANE))
        yb2 = _gmm(work, xs3.reshape(n_rows * SUB, LANE), w['nffn'],
                   w_gate[layer], w_up[layer], w_down[layer])
        yb3 = yb2.reshape(n_rows, SUB, LANE)
        yp = _combine(dest, x1p, routep2, nfin, yb3, 0).reshape(B, L, D)
        ys = _combine(dest, x1s, routes, nfin, yb3, Tp).reshape(NB, ls, D)
        pk.append(k1.reshape(B, WINDOW, KV_HEADS, HEAD_DIM))
        pv.append(v1.reshape(B, WINDOW, KV_HEADS, HEAD_DIM))
        ps.append(s1)
        sk.append(k2.reshape(NB, WINDOW, KV_HEADS, HEAD_DIM))
        sv.append(v2.reshape(NB, WINDOW, KV_HEADS, HEAD_DIM))
        ss.append(s2)
    return (yp, ys, jnp.stack(pk), jnp.stack(pv), jnp.stack(ps),
            jnp.stack(sk), jnp.stack(sv), jnp.stack(ss))
```

```python
import functools
import math

import jax
import jax.numpy as jnp
import numpy as np
from jax import lax
from jax.experimental import pallas as pl
from jax.experimental.pallas import tpu as pltpu

HEAD_DIM = 64
KV_HEADS = 4
GQA_GROUP = 4
ATTN_HEADS = KV_HEADS * GQA_GROUP
WINDOW = 128
N_BUCKETS = 32
MAX_DISTANCE = 128
RET_HEADS = 4
RET_DK = 128
RET_DV = 256
RET_CHUNK = 128
ROPE_BASE = 10000.0
N_GROUPS = 4
EXPERTS_PER_GROUP = 8
N_EXPERTS = N_GROUPS * EXPERTS_PER_GROUP
EXPERT_FF = 512
NORM_EPS = 1e-6
NEG_INF = -1e30
PAST_LEN = 16384

ATTN_W = ATTN_HEADS * HEAD_DIM
KV_W = KV_HEADS * HEAD_DIM
RQ_W = RET_HEADS * RET_DK
RV_W = RET_HEADS * RET_DV
OFF_QA = 0
OFF_KA = OFF_QA + ATTN_W
OFF_VA = OFF_KA + KV_W
OFF_QR = OFF_VA + KV_W
OFF_KR = OFF_QR + RQ_W
OFF_VR = OFF_KR + RQ_W
OFF_GR = OFF_VR + RV_W
OFF_GA = OFF_GR + RV_W
OFF_GT = OFF_GA + 1024
ROUTE_LANES = 128

LANE = 128
SUB = 8
PROMPT_TM = 256
SAMPLE_GROUP = 8
MOE_TM = 256
ROW_TM = 256
DMA_UNROLL = 8
VMEM_LIMIT = 60 * 1024 * 1024

_F32 = jnp.float32
_BF16 = jnp.bfloat16


def _const_spec(shape):
    nd = len(shape)
    return pl.BlockSpec(shape, lambda *_: (0,) * nd, pipeline_mode=pl.Buffered(1))


def _rms(x, gain):
    return x * lax.rsqrt(jnp.mean(x * x, axis=-1, keepdims=True) + NORM_EPS) * gain


def _dot(a, b):
    return jnp.dot(a, b, preferred_element_type=_F32)


def _dot_nt(a, b):
    return lax.dot_general(a, b, (((1,), (1,)), ((), ())), preferred_element_type=_F32)


def _dot_tn(a, b):
    return lax.dot_general(a, b, (((0,), (0,)), ((), ())), preferred_element_type=_F32)


def _load_rows(ref, n):
    return jnp.concatenate([ref[pl.ds(s, n, stride=SUB), :] for s in range(SUB)], axis=1)


def _store_rows(ref, val):
    n = val.shape[0]
    for s in range(SUB):
        ref[pl.ds(s, n, stride=SUB), :] = val[:, s * LANE:(s + 1) * LANE]


def _rotary(x, cosf, sinf):
    return x * cosf + pltpu.roll(x, RET_DK // 2, 1) * sinf


def _post(x, attn, ret, gate_a, gate_r, wba, wbr, wout, nffn, wrh, wrl, br):
    merged = (jax.nn.sigmoid(gate_a) * _dot(attn.astype(_BF16), wba)
              + jax.nn.sigmoid(gate_r) * _dot(ret.astype(_BF16), wbr))
    x1 = x + _dot(merged.astype(_BF16), wout)
    xn2 = _rms(x1, nffn)
    hi = xn2.astype(_BF16)
    lo = (xn2 - hi.astype(_F32)).astype(_BF16)
    logits = _dot(hi, wrh) + (_dot(hi, wrl) + _dot(lo, wrh)) + br
    n = logits.shape[0]
    lane = lax.broadcasted_iota(jnp.int32, (n, ROUTE_LANES), 1)
    big = jnp.int32(1 << 20)
    neg = jnp.float32(-jnp.inf)
    gl = jnp.where(lane < N_GROUPS, logits, neg)
    gmax = jnp.max(gl, axis=-1, keepdims=True)
    gexp = jnp.exp(gl - gmax)
    gsum = jnp.sum(gexp, axis=-1, keepdims=True)
    pg = gexp / gsum
    g_w = jnp.max(pg, axis=-1, keepdims=True)
    g_idx = jnp.min(jnp.where(pg == g_w, lane, big), axis=-1, keepdims=True)
    e_lane = lane - N_GROUPS
    emask = (e_lane >= 0) & (e_lane < N_EXPERTS) & ((e_lane >> 3) == g_idx)
    fl = jnp.where(emask, logits, neg)
    fmax = jnp.max(fl, axis=-1, keepdims=True)
    fexp = jnp.exp(fl - fmax)
    fsum = jnp.sum(fexp, axis=-1, keepdims=True)
    pe = jnp.where(emask, fexp / fsum, -1.0)
    p1 = jnp.max(pe, axis=-1, keepdims=True)
    i1 = jnp.min(jnp.where(pe == p1, lane, big), axis=-1, keepdims=True)
    pe2 = jnp.where(lane == i1, -1.0, pe)
    p2 = jnp.max(pe2, axis=-1, keepdims=True)
    i2 = jnp.min(jnp.where(pe2 == p2, lane, big), axis=-1, keepdims=True)
    psum = p1 + p2
    gate1 = g_w * p1 / psum
    gate2 = g_w * p2 / psum
    route = jnp.where(lane == 0, (i1 - N_GROUPS).astype(_F32),
                      jnp.where(lane == 1, (i2 - N_GROUPS).astype(_F32),
                                jnp.where(lane == 2, gate1,
                                          jnp.where(lane == 3, gate2, 0.0))))
    return x1, route


def _prompt_mixer_kernel(sink_ref, cdec_ref,
                         x_ref, nmix_ref, win_ref, bias_ref, cos_ref, sin_ref, dec_ref,
                         qd_ref, kd_ref, wba_ref, wbr_ref, wout_ref, nffn_ref,
                         wrh_ref, wrl_ref, br_ref,
                         x1_ref, route_ref, knew_ref, vnew_ref, s_ref,
                         proj, kctx, vctx, attn, ret):
    i = pl.program_id(1)
    last = pl.num_programs(1) - 1
    tm = x_ref.shape[1]
    n_sub = tm // WINDOW
    scale = HEAD_DIM ** -0.5

    @pl.when(i == 0)
    def _():
        s_ref[...] = jnp.zeros_like(s_ref)
        kctx[0:WINDOW, :] = jnp.zeros((WINDOW, KV_W), _BF16)
        vctx[0:WINDOW, :] = jnp.zeros((WINDOW, KV_W), _BF16)

    x = x_ref[0]
    xn = _rms(x, nmix_ref[...]).astype(_BF16)
    n_in = win_ref.shape[1]
    panel = 512
    for c0 in range(0, n_in, panel):
        proj[:, c0:c0 + panel] = _dot(xn, win_ref[:, c0:c0 + panel])

    @pl.when(i == last)
    def _():
        knew_ref[0] = proj[tm - WINDOW:tm, OFF_KA:OFF_KA + KV_W]
        vnew_ref[0] = proj[tm - WINDOW:tm, OFF_VA:OFF_VA + KV_W]

    col = lax.broadcasted_iota(jnp.int32, (1, 2 * WINDOW), 1)
    for c in range(n_sub):
        r0 = c * WINDOW
        kctx[WINDOW:2 * WINDOW, :] = proj[r0:r0 + WINDOW, OFF_KA:OFF_KA + KV_W].astype(_BF16)
        vctx[WINDOW:2 * WINDOW, :] = proj[r0:r0 + WINDOW, OFF_VA:OFF_VA + KV_W].astype(_BF16)
        if c == 0:
            pen = jnp.where((col < WINDOW) & (i == 0), NEG_INF, 0.0).astype(_F32)
        for h in range(KV_HEADS):
            k_h = kctx[:, h * HEAD_DIM:(h + 1) * HEAD_DIM]
            v_h = vctx[:, h * HEAD_DIM:(h + 1) * HEAD_DIM]
            for g in range(GQA_GROUP):
                hq = h * GQA_GROUP + g
                q = proj[r0:r0 + WINDOW, hq * HEAD_DIM:(hq + 1) * HEAD_DIM].astype(_BF16)
                s = _dot_nt(q, k_h) * scale + bias_ref[hq]
                if c == 0:
                    s = s + pen
                snk = sink_ref[hq]
                m = jnp.maximum(jnp.max(s, axis=-1, keepdims=True), snk)
                p = jnp.exp(s - m)
                den = jnp.sum(p, axis=-1, keepdims=True) + jnp.exp(snk - m)
                p = p * (1.0 / den)
                attn[r0:r0 + WINDOW, hq * HEAD_DIM:(hq + 1) * HEAD_DIM] = _dot(p.astype(_BF16), v_h)
        kctx[0:WINDOW, :] = kctx[WINDOW:2 * WINDOW, :]
        vctx[0:WINDOW, :] = vctx[WINDOW:2 * WINDOW, :]

    cosf = cos_ref[...]
    sinf = sin_ref[...]
    for h in range(RET_HEADS):
        qrot = _rotary(proj[:, OFF_QR + h * RET_DK:OFF_QR + (h + 1) * RET_DK], cosf, sinf)
        krot = _rotary(proj[:, OFF_KR + h * RET_DK:OFF_KR + (h + 1) * RET_DK], cosf, sinf) * (RET_DK ** -0.5)
        qd = qd_ref[:, h * RET_DK:(h + 1) * RET_DK]
        kd = kd_ref[:, h * RET_DK:(h + 1) * RET_DK]
        for c in range(n_sub):
            r0 = c * RET_CHUNK
            qc = qrot[r0:r0 + RET_CHUNK]
            kc = krot[r0:r0 + RET_CHUNK]
            vc = proj[r0:r0 + RET_CHUNK, OFF_VR + h * RET_DV:OFF_VR + (h + 1) * RET_DV].astype(_BF16)
            sc = _dot_nt(qc.astype(_BF16), kc.astype(_BF16)) * dec_ref[h]
            s_old = s_ref[0, h]
            o = _dot(sc.astype(_BF16), vc) + _dot((qc * qd).astype(_BF16), s_old.astype(_BF16))
            s_ref[0, h] = s_old * cdec_ref[h] + _dot_tn((kc * kd).astype(_BF16), vc)
            o = o * lax.rsqrt(jnp.mean(o * o, axis=-1, keepdims=True) + NORM_EPS)
            gr = proj[r0:r0 + RET_CHUNK, OFF_GR + h * RET_DV:OFF_GR + (h + 1) * RET_DV]
            ret[r0:r0 + RET_CHUNK, h * RET_DV:(h + 1) * RET_DV] = o * (gr * jax.nn.sigmoid(gr))

    x1, route = _post(x, attn[...], ret[...],
                      proj[:, OFF_GA:OFF_GA + 1024], proj[:, OFF_GT:OFF_GT + 1024],
                      wba_ref[...], wbr_ref[...], wout_ref[...], nffn_ref[...],
                      wrh_ref[...], wrl_ref[...], br_ref[...])
    _store_rows(x1_ref, x1)
    route_ref[0] = route


def _prompt_mixer(x, cst, w):
    B, L, D = x.shape
    tm = min(PROMPT_TM, L)
    nb = L // tm
    n_in = w['win'].shape[1]
    step = lambda b, i, *_: (b, i, 0)
    per_b = lambda b, i, *_: (b, 0, 0)
    grid_spec = pltpu.PrefetchScalarGridSpec(
        num_scalar_prefetch=2,
        grid=(B, nb),
        in_specs=[
            pl.BlockSpec((1, tm, D), step),
            _const_spec((1, D)),
            _const_spec((D, n_in)),
            _const_spec((ATTN_HEADS, WINDOW, 2 * WINDOW)),
            pl.BlockSpec((tm, RET_DK), lambda b, i, *_: (i, 0)),
            pl.BlockSpec((tm, RET_DK), lambda b, i, *_: (i, 0)),
            _const_spec((RET_HEADS, RET_CHUNK, RET_CHUNK)),
            _const_spec((RET_CHUNK, RQ_W)),
            _const_spec((RET_CHUNK, RQ_W)),
            _const_spec((ATTN_W, D)),
            _const_spec((RV_W, D)),
            _const_spec((D, D)),
            _const_spec((1, D)),
            _const_spec((D, ROUTE_LANES)),
            _const_spec((D, ROUTE_LANES)),
            _const_spec((1, ROUTE_LANES)),
        ],
        out_specs=[
            pl.BlockSpec((tm * SUB, LANE), lambda b, i, *_: (b * nb + i, 0)),
            pl.BlockSpec((1, tm, ROUTE_LANES), step),
            pl.BlockSpec((1, WINDOW, KV_W), per_b),
            pl.BlockSpec((1, WINDOW, KV_W), per_b),
            pl.BlockSpec((1, RET_HEADS, RET_DK, RET_DV), lambda b, i, *_: (b, 0, 0, 0)),
        ],
        scratch_shapes=[
            pltpu.VMEM((tm, n_in), _F32),
            pltpu.VMEM((2 * WINDOW, KV_W), _BF16),
            pltpu.VMEM((2 * WINDOW, KV_W), _BF16),
            pltpu.VMEM((tm, ATTN_W), _F32),
            pltpu.VMEM((tm, RV_W), _F32),
        ],
    )
    assert D == SUB * LANE
    out_shape = [
        jax.ShapeDtypeStruct((B * L * SUB, LANE), _F32),
        jax.ShapeDtypeStruct((B, L, ROUTE_LANES), _F32),
        jax.ShapeDtypeStruct((B, WINDOW, KV_W), _F32),
        jax.ShapeDtypeStruct((B, WINDOW, KV_W), _F32),
        jax.ShapeDtypeStruct((B, RET_HEADS, RET_DK, RET_DV), _F32),
    ]
    return pl.pallas_call(
        _prompt_mixer_kernel,
        grid_spec=grid_spec,
        out_shape=out_shape,
        compiler_params=pltpu.CompilerParams(
            dimension_semantics=("arbitrary", "arbitrary"), vmem_limit_bytes=VMEM_LIMIT),
        name="prompt_mixer",
    )(cst['sink'], cst['cdec_p'],
      x, w['nmix'], w['win'], cst['bias_p'], cst['cos_p'], cst['sin_p'], cst['dec_p'],
      cst['qd_p'], cst['kd_p'], w['wba'], w['wbr'], w['wout'], w['nffn'],
      w['wrh'], w['wrl'], w['br'])


def _inproj_kernel(x_ref, nmix_ref, win_ref, o_ref):
    xn = _rms(x_ref[...], nmix_ref[...]).astype(_BF16)
    o_ref[...] = _dot(xn, win_ref[...])


def _sample_inproj(x2d, w):
    T, D = x2d.shape
    n_in = w['win'].shape[1]
    panel = 512
    return pl.pallas_call(
        _inproj_kernel,
        grid=(n_in // panel,),
        in_specs=[pl.BlockSpec((T, D), lambda j: (0, 0)),
                  pl.BlockSpec((1, D), lambda j: (0, 0)),
                  pl.BlockSpec((D, panel), lambda j: (0, j))],
        out_specs=pl.BlockSpec((T, panel), lambda j: (0, j)),
        out_shape=jax.ShapeDtypeStruct((T, n_in), _F32),
        compiler_params=pltpu.CompilerParams(
            dimension_semantics=("arbitrary",), vmem_limit_bytes=VMEM_LIMIT),
        name="sample_inproj",
    )(x2d, w['nmix'], w['win'])


def _sample_core_kernel(cdec_ref, proj_ref, ck_ref, cv_ref, st_ref, bh_ref, bn_ref, snk_ref,
                        cos_ref, sin_ref, dec_ref, qd_ref, kd_ref,
                        attn_ref, ret_ref, nk_ref, nv_ref, ns_ref):
    G = proj_ref.shape[0]
    ls = proj_ref.shape[1]
    scale = HEAD_DIM ** -0.5
    cosf = cos_ref[...]
    sinf = sin_ref[...]

    def body(b, carry):
        row = proj_ref[b]
        k_new = row[:, OFF_KA:OFF_KA + KV_W]
        v_new = row[:, OFF_VA:OFF_VA + KV_W]
        ck = ck_ref[b]
        cv = cv_ref[b]
        nk_ref[b, 0:WINDOW - ls, :] = ck[ls:WINDOW, :]
        nk_ref[b, WINDOW - ls:WINDOW, :] = k_new
        nv_ref[b, 0:WINDOW - ls, :] = cv[ls:WINDOW, :]
        nv_ref[b, WINDOW - ls:WINDOW, :] = v_new
        ckb = ck.astype(_BF16)
        cvb = cv.astype(_BF16)
        knb = k_new.astype(_BF16)
        vnb = v_new.astype(_BF16)
        for h in range(KV_HEADS):
            hs = slice(h * HEAD_DIM, (h + 1) * HEAD_DIM)
            q4 = jnp.concatenate(
                [row[:, (h * GQA_GROUP + g) * HEAD_DIM:(h * GQA_GROUP + g + 1) * HEAD_DIM]
                 for g in range(GQA_GROUP)], axis=0).astype(_BF16)
            s1 = _dot_nt(q4, ckb[:, hs]) * scale + bh_ref[h]
            s2 = _dot_nt(q4, knb[:, hs]) * scale + bn_ref[h]
            snk = snk_ref[h]
            m = jnp.maximum(jnp.maximum(jnp.max(s1, axis=-1, keepdims=True),
                                        jnp.max(s2, axis=-1, keepdims=True)), snk)
            p1 = jnp.exp(s1 - m)
            p2 = jnp.exp(s2 - m)
            den = (jnp.sum(p1, axis=-1, keepdims=True) + jnp.sum(p2, axis=-1, keepdims=True)
                   + jnp.exp(snk - m))
            r = 1.0 / den
            o = _dot((p1 * r).astype(_BF16), cvb[:, hs]) + _dot((p2 * r).astype(_BF16), vnb[:, hs])
            for g in range(GQA_GROUP):
                hq = h * GQA_GROUP + g
                attn_ref[b, :, hq * HEAD_DIM:(hq + 1) * HEAD_DIM] = o[g * ls:(g + 1) * ls]
        for h in range(RET_HEADS):
            qrot = _rotary(row[:, OFF_QR + h * RET_DK:OFF_QR + (h + 1) * RET_DK], cosf, sinf)
            krot = _rotary(row[:, OFF_KR + h * RET_DK:OFF_KR + (h + 1) * RET_DK], cosf, sinf) * (RET_DK ** -0.5)
            vc = row[:, OFF_VR + h * RET_DV:OFF_VR + (h + 1) * RET_DV].astype(_BF16)
            sc = _dot_nt(qrot.astype(_BF16), krot.astype(_BF16)) * dec_ref[h]
            s_old = st_ref[b, h]
            qd = qd_ref[:, h * RET_DK:(h + 1) * RET_DK]
            kd = kd_ref[:, h * RET_DK:(h + 1) * RET_DK]
            o = _dot(sc.astype(_BF16), vc) + _dot((qrot * qd).astype(_BF16), s_old.astype(_BF16))
            ns_ref[b, h] = s_old * cdec_ref[h] + _dot_tn((krot * kd).astype(_BF16), vc)
            o = o * lax.rsqrt(jnp.mean(o * o, axis=-1, keepdims=True) + NORM_EPS)
            gr = row[:, OFF_GR + h * RET_DV:OFF_GR + (h + 1) * RET_DV]
            ret_ref[b, :, h * RET_DV:(h + 1) * RET_DV] = o * (gr * jax.nn.sigmoid(gr))
        return carry

    lax.fori_loop(0, G, body, 0)


def _sample_core(proj3, ck, cv, st, cst):
    NB, ls, n_in = proj3.shape
    G = min(SAMPLE_GROUP, NB)
    blk3 = lambda i, *_: (i, 0, 0)
    blk4 = lambda i, *_: (i, 0, 0, 0)
    c2 = lambda i, *_: (0, 0)
    c3 = lambda i, *_: (0, 0, 0)
    ql = GQA_GROUP * ls
    grid_spec = pltpu.PrefetchScalarGridSpec(
        num_scalar_prefetch=1,
        grid=(NB // G,),
        in_specs=[
            pl.BlockSpec((G, ls, n_in), blk3),
            pl.BlockSpec((G, WINDOW, KV_W), blk3),
            pl.BlockSpec((G, WINDOW, KV_W), blk3),
            pl.BlockSpec((G, RET_HEADS, RET_DK, RET_DV), blk4),
            pl.BlockSpec((KV_HEADS, ql, WINDOW), c3),
            pl.BlockSpec((KV_HEADS, ql, ls), c3),
            pl.BlockSpec((KV_HEADS, ql, 1), c3),
            pl.BlockSpec((ls, RET_DK), c2),
            pl.BlockSpec((ls, RET_DK), c2),
            pl.BlockSpec((RET_HEADS, ls, ls), c3),
            pl.BlockSpec((ls, RQ_W), c2),
            pl.BlockSpec((ls, RQ_W), c2),
        ],
        out_specs=[
            pl.BlockSpec((G, ls, ATTN_W), blk3),
            pl.BlockSpec((G, ls, RV_W), blk3),
            pl.BlockSpec((G, WINDOW, KV_W), blk3),
            pl.BlockSpec((G, WINDOW, KV_W), blk3),
            pl.BlockSpec((G, RET_HEADS, RET_DK, RET_DV), blk4),
        ],
    )
    out_shape = [
        jax.ShapeDtypeStruct((NB, ls, ATTN_W), _F32),
        jax.ShapeDtypeStruct((NB, ls, RV_W), _F32),
        jax.ShapeDtypeStruct((NB, WINDOW, KV_W), _F32),
        jax.ShapeDtypeStruct((NB, WINDOW, KV_W), _F32),
        jax.ShapeDtypeStruct((NB, RET_HEADS, RET_DK, RET_DV), _F32),
    ]
    return pl.pallas_call(
        _sample_core_kernel,
        grid_spec=grid_spec,
        out_shape=out_shape,
        compiler_params=pltpu.CompilerParams(
            dimension_semantics=("arbitrary",), vmem_limit_bytes=VMEM_LIMIT),
        name="sample_core",
    )(cst['cdec_s'], proj3, ck, cv, st, cst['bias_hist'], cst['bias_new'], cst['sink_col'],
      cst['cos_s'], cst['sin_s'], cst['dec_s'], cst['qd_s'], cst['kd_s'])


def _sample_post_kernel(x_ref, attn_ref, ret_ref, ga_ref, gt_ref, wba_ref, wbr_ref, wout_ref,
                        nffn_ref, wrh_ref, wrl_ref, br_ref, x1_ref, route_ref):
    x1, route = _post(x_ref[...], attn_ref[...], ret_ref[...], ga_ref[...], gt_ref[...],
                      wba_ref[...], wbr_ref[...], wout_ref[...], nffn_ref[...],
                      wrh_ref[...], wrl_ref[...], br_ref[...])
    _store_rows(x1_ref, x1)
    route_ref[...] = route


def _sample_post(x2d, attn, ret, ga, gt, w):
    T, D = x2d.shape
    full = lambda s: pl.BlockSpec(s, lambda i: (0,) * len(s))
    return pl.pallas_call(
        _sample_post_kernel,
        grid=(1,),
        in_specs=[full((T, D)), full((T, ATTN_W)), full((T, RV_W)), full((T, D)), full((T, D)),
                  full((ATTN_W, D)), full((RV_W, D)), full((D, D)), full((1, D)),
                  full((D, ROUTE_LANES)), full((D, ROUTE_LANES)), full((1, ROUTE_LANES))],
        out_specs=[full((T * SUB, LANE)), full((T, ROUTE_LANES))],
        out_shape=[jax.ShapeDtypeStruct((T * SUB, LANE), _F32),
                   jax.ShapeDtypeStruct((T, ROUTE_LANES), _F32)],
        compiler_params=pltpu.CompilerParams(
            dimension_semantics=("arbitrary",), vmem_limit_bytes=VMEM_LIMIT),
        name="sample_post",
    )(x2d, attn, ret, ga, gt, w['wba'], w['wbr'], w['wout'], w['nffn'], w['wrh'], w['wrl'], w['br'])


def _dispatch_kernel(dest_ref, xp_ref, xq_ref, xs_ref, sem, *, p_steps):
    i = pl.program_id(0)

    def copy_tile(src):
        def tile_copy(r, d):
            return pltpu.make_async_copy(src.at[r], xs_ref.at[d], sem)

        def start(r, c):
            t = i * ROW_TM + r
            tile_copy(r, dest_ref[2 * t]).start()
            tile_copy(r, dest_ref[2 * t + 1]).start()
            return c

        lax.fori_loop(0, ROW_TM, start, 0, unroll=DMA_UNROLL)

        def wait(r, c):
            tile_copy(0, 0).wait()
            tile_copy(0, 0).wait()
            return c

        lax.fori_loop(0, ROW_TM, wait, 0, unroll=DMA_UNROLL)

    @pl.when(i < p_steps)
    def _():
        copy_tile(xp_ref)

    @pl.when(i >= p_steps)
    def _():
        copy_tile(xq_ref)


def _dispatch(dest, xp3, xq3):
    Tp, Tq = xp3.shape[0], xq3.shape[0]
    assert Tp % ROW_TM == 0 and Tq % ROW_TM == 0
    p_steps = Tp // ROW_TM
    grid_spec = pltpu.PrefetchScalarGridSpec(
        num_scalar_prefetch=1,
        grid=((Tp + Tq) // ROW_TM,),
        in_specs=[
            pl.BlockSpec((ROW_TM, SUB, LANE), lambda i, *_: (jnp.minimum(i, p_steps - 1), 0, 0)),
            pl.BlockSpec((ROW_TM, SUB, LANE), lambda i, *_: (jnp.maximum(i - p_steps, 0), 0, 0)),
        ],
        out_specs=pl.BlockSpec(memory_space=pl.ANY),
        scratch_shapes=[pltpu.SemaphoreType.DMA],
    )
    return pl.pallas_call(
        functools.partial(_dispatch_kernel, p_steps=p_steps),
        grid_spec=grid_spec,
        out_shape=jax.ShapeDtypeStruct((2 * (Tp + Tq), SUB, LANE), _F32),
        compiler_params=pltpu.CompilerParams(dimension_semantics=("arbitrary",)),
        name="moe_dispatch",
    )(dest, xp3, xq3)


def _gmm_kernel(tile_ref, exp_ref, lo_ref, hi_ref, first_ref, chg_ref,
                x_ref, nffn_ref, wg_ref, wu_ref, wd_ref, y_ref, wg_s, wu_s, wd_s):
    m = pl.program_id(0)
    tm = x_ref.shape[0] // SUB

    @pl.when(chg_ref[m] == 1)
    def _():
        wg_s[...] = wg_ref[0].astype(_BF16)
        wu_s[...] = wu_ref[0].astype(_BF16)
        wd_s[...] = wd_ref[0].astype(_BF16)

    lo = lo_ref[m]
    hi = hi_ref[m]

    @pl.when(hi > lo)
    def _():
        rows = tile_ref[m] * tm + lax.broadcasted_iota(jnp.int32, (tm, 1), 0)
        mine = (rows >= lo) & (rows < hi)
        xn = _rms(_load_rows(x_ref, tm), nffn_ref[...])
        x = jnp.where(mine, xn, 0.0).astype(_BF16)
        a = _dot(x, wg_s[...])
        hmid = (a * jax.nn.sigmoid(a)) * _dot(x, wu_s[...])
        y = _dot(hmid.astype(_BF16), wd_s[...])

        @pl.when(first_ref[m] == 1)
        def _():
            _store_rows(y_ref, y)

        @pl.when(first_ref[m] == 0)
        def _():
            _store_rows(y_ref, _load_rows(y_ref, tm) + y)


def _gmm(work, xs2, nffn, wg, wu, wd):
    A = xs2.shape[0] // SUB
    E, D, F = wg.shape
    n_work = work[0].shape[0]
    grid_spec = pltpu.PrefetchScalarGridSpec(
        num_scalar_prefetch=6,
        grid=(n_work,),
        in_specs=[
            pl.BlockSpec((MOE_TM * SUB, LANE), lambda m, t, e, *_: (t[m], 0)),
            pl.BlockSpec((1, D), lambda m, t, e, *_: (0, 0)),
            pl.BlockSpec((1, D, F), lambda m, t, e, *_: (e[m], 0, 0)),
            pl.BlockSpec((1, D, F), lambda m, t, e, *_: (e[m], 0, 0)),
            pl.BlockSpec((1, F, D), lambda m, t, e, *_: (e[m], 0, 0)),
        ],
        out_specs=pl.BlockSpec((MOE_TM * SUB, LANE), lambda m, t, e, *_: (t[m], 0)),
        scratch_shapes=[pltpu.VMEM((D, F), _BF16), pltpu.VMEM((D, F), _BF16),
                        pltpu.VMEM((F, D), _BF16)],
    )
    return pl.pallas_call(
        _gmm_kernel,
        grid_spec=grid_spec,
        out_shape=jax.ShapeDtypeStruct((A * SUB, LANE), _F32),
        compiler_params=pltpu.CompilerParams(
            dimension_semantics=("arbitrary",), vmem_limit_bytes=VMEM_LIMIT),
        name="moe_gmm",
    )(*work, xs2, nffn, wg, wu, wd)


def _combine_kernel(dest_ref, x1_ref, route_ref, nfin_ref, yb_ref, o_ref, buf, sem, *, tok0):
    i = pl.program_id(0)
    tm = o_ref.shape[0]
    base = tok0 + i * tm

    def tile_copy(d, k, r):
        return pltpu.make_async_copy(yb_ref.at[d], buf.at[k, r], sem)

    def start(r, c):
        t = base + r
        tile_copy(dest_ref[2 * t], 0, r).start()
        tile_copy(dest_ref[2 * t + 1], 1, r).start()
        return c

    lax.fori_loop(0, tm, start, 0, unroll=DMA_UNROLL)

    def wait(r, c):
        tile_copy(0, 0, 0).wait()
        tile_copy(0, 1, 0).wait()
        return c

    lax.fori_loop(0, tm, wait, 0, unroll=DMA_UNROLL)
    route = route_ref[...]
    g0 = route[:, 2:3]
    g1 = route[:, 3:4]
    y0 = jnp.concatenate([buf[0, :, s, :] for s in range(SUB)], axis=1)
    y1 = jnp.concatenate([buf[1, :, s, :] for s in range(SUB)], axis=1)
    y = _load_rows(x1_ref, tm) + (y0 * g0 + y1 * g1)
    o_ref[...] = _rms(y, nfin_ref[...])


def _combine(dest, x1_2, route, nfin, yb3, tok0):
    T = x1_2.shape[0] // SUB
    D = SUB * LANE
    tm = min(ROW_TM, T)
    grid_spec = pltpu.PrefetchScalarGridSpec(
        num_scalar_prefetch=1,
        grid=(T // tm,),
        in_specs=[
            pl.BlockSpec((tm * SUB, LANE), lambda i, *_: (i, 0)),
            pl.BlockSpec((tm, ROUTE_LANES), lambda i, *_: (i, 0)),
            pl.BlockSpec((1, D), lambda i, *_: (0, 0)),
            pl.BlockSpec(memory_space=pl.ANY),
        ],
        out_specs=pl.BlockSpec((tm, D), lambda i, *_: (i, 0)),
        scratch_shapes=[pltpu.VMEM((2, tm, SUB, LANE), _F32), pltpu.SemaphoreType.DMA],
    )
    return pl.pallas_call(
        functools.partial(_combine_kernel, tok0=tok0),
        grid_spec=grid_spec,
        out_shape=jax.ShapeDtypeStruct((T, D), _F32),
        compiler_params=pltpu.CompilerParams(
            dimension_semantics=("arbitrary",), vmem_limit_bytes=VMEM_LIMIT),
        name="moe_combine",
    )(dest, x1_2, route, nfin, yb3)


def _routing_tables(experts, n_tiles):
    flat_e = experts.reshape(-1)
    A = flat_e.shape[0]
    onehot = (flat_e[:, None] == jnp.arange(N_EXPERTS, dtype=jnp.int32)[None, :]).astype(jnp.int32)
    csum = jnp.cumsum(onehot, axis=0)
    counts = csum[-1]
    ends = jnp.cumsum(counts)
    starts = ends - counts
    dest = jnp.sum(onehot * (starts[None, :] + csum - 1), axis=1).astype(jnp.int32)
    tile_starts = jnp.arange(n_tiles, dtype=jnp.int32) * MOE_TM
    pts = jnp.sort(jnp.concatenate([tile_starts, starts.astype(jnp.int32)]))
    lo = pts
    hi = jnp.concatenate([pts[1:], jnp.array([A], jnp.int32)])
    tile = jnp.minimum(lo // MOE_TM, n_tiles - 1).astype(jnp.int32)
    expert = jnp.minimum(jnp.searchsorted(ends, lo, side='right'), N_EXPERTS - 1).astype(jnp.int32)
    nonempty = hi > lo
    first = (nonempty & (lo % MOE_TM == 0)).astype(jnp.int32)
    chg = jnp.concatenate([jnp.ones((1,), jnp.int32), (expert[1:] != expert[:-1]).astype(jnp.int32)])
    return dest, (tile, expert, lo, hi, first, chg)


def _bucket_table(lq, lk):
    dist = np.arange(lq)[:, None] + WINDOW - np.arange(lk)[None, :]
    band = (dist >= 0) & (dist < WINDOW)
    d = np.clip(dist, 0, WINDOW - 1)
    max_exact = N_BUCKETS // 2
    d_f = np.maximum(d, 1).astype(np.float32)
    large = max_exact + (np.log(d_f / max_exact) / math.log(MAX_DISTANCE / max_exact)
                         * (N_BUCKETS - max_exact)).astype(np.int32)
    large = np.minimum(large, N_BUCKETS - 1)
    return np.where(d < max_exact, d, large).astype(np.int32), band


def _decay_tables(C):
    log_gamma = jnp.log(1.0 - 2.0 ** (-5.0 - jnp.arange(RET_HEADS, dtype=_F32)))
    idx = jnp.arange(C, dtype=_F32)
    diff = idx[:, None] - idx[None, :]
    decay_in = jnp.where((diff >= 0)[..., None],
                         jnp.exp(jnp.maximum(diff, 0.0)[..., None] * log_gamma), 0.0)
    q_dec = jnp.exp((idx + 1.0)[:, None] * log_gamma)
    k_dec = jnp.exp((C - 1.0 - idx)[:, None] * log_gamma)
    c_dec = jnp.exp(C * log_gamma)
    dec = jnp.transpose(decay_in, (2, 0, 1))
    qd = jnp.repeat(q_dec, RET_DK, axis=1)
    kd = jnp.repeat(k_dec, RET_DK, axis=1)
    return dec, qd, kd, c_dec


def _rope_tables(pos):
    half = RET_DK // 2
    inv = ROPE_BASE ** (-jnp.arange(half, dtype=_F32) * 2.0 / RET_DK)
    ang = pos.astype(_F32)[:, None] * inv[None, :]
    cos = jnp.cos(ang)
    sin = jnp.sin(ang)
    return jnp.concatenate([cos, cos], axis=1), jnp.concatenate([-sin, sin], axis=1)


def _constants(rel_bias, attn_sink, L, ls):
    cst = {}
    rb = rel_bias.astype(_F32)
    bkt, band = _bucket_table(WINDOW, 2 * WINDOW)
    bias = jnp.transpose(rb[bkt], (2, 0, 1))
    cst['bias_p'] = jnp.where(band[None], bias, NEG_INF)
    cst['sink'] = attn_sink.astype(_F32)
    cst['cos_p'], cst['sin_p'] = _rope_tables(jnp.arange(L))
    cst['dec_p'], cst['qd_p'], cst['kd_p'], cst['cdec_p'] = _decay_tables(min(RET_CHUNK, L))
    bkt, band = _bucket_table(ls, WINDOW + ls)
    bias = jnp.where(band[None], jnp.transpose(rb[bkt], (2, 0, 1)), NEG_INF)
    bias = bias.reshape(KV_HEADS, GQA_GROUP * ls, WINDOW + ls)
    cst['bias_hist'] = bias[:, :, :WINDOW]
    cst['bias_new'] = bias[:, :, WINDOW:]
    cst['sink_col'] = jnp.repeat(attn_sink.astype(_F32).reshape(KV_HEADS, GQA_GROUP), ls,
                                 axis=1)[..., None]
    cst['cos_s'], cst['sin_s'] = _rope_tables(PAST_LEN + jnp.arange(ls))
    cst['dec_s'], cst['qd_s'], cst['kd_s'], cst['cdec_s'] = _decay_tables(min(RET_CHUNK, ls))
    return cst


def _layer_weights(layer, norm_mix, w_in, w_branch_attn, w_branch_ret, w_out, norm_ffn,
                   w_router_group, b_router_group, w_router_expert, b_router_expert):
    D = w_in.shape[1]
    wr = jnp.concatenate([w_router_group[layer].astype(_F32), w_router_expert[layer].astype(_F32)], axis=1)
    wr = jnp.pad(wr, ((0, 0), (0, ROUTE_LANES - wr.shape[1])))
    wrh = wr.astype(_BF16)
    wrl = (wr - wrh.astype(_F32)).astype(_BF16)
    br = jnp.concatenate([b_router_group[layer].astype(_F32), b_router_expert[layer].astype(_F32)])
    br = jnp.pad(br, (0, ROUTE_LANES - br.shape[0]))[None, :]
    return {
        'nmix': norm_mix[layer].astype(_F32)[None, :],
        'win': w_in[layer].astype(_BF16),
        'wba': w_branch_attn[layer].astype(_BF16),
        'wbr': w_branch_ret[layer].astype(_BF16),
        'wout': w_out[layer].astype(_BF16),
        'nffn': norm_ffn[layer].astype(_F32)[None, :],
        'wrh': wrh, 'wrl': wrl, 'br': br,
    }


def kernel(x_prompt, x_sample, cache_k, cache_v, state_ret, norm_mix, w_in, attn_sink, rel_bias,
           w_branch_attn, w_branch_ret, w_out, norm_ffn, w_router_group, b_router_group,
           w_router_expert, b_router_expert, w_gate, w_up, w_down, norm_final):
    depth = w_in.shape[0]
    assert depth == 1, "the final norm is fused into the MoE combine of the only layer"
    B, L, D = x_prompt.shape
    NB, ls, _ = x_sample.shape
    Tp, Ts = B * L, NB * ls
    nfin = norm_final.astype(_F32)[None, :]
    yp, ys = x_prompt, x_sample
    pk, pv, ps, sk, sv, ss = [], [], [], [], [], []
    for layer in range(depth):
        w = _layer_weights(layer, norm_mix, w_in, w_branch_attn, w_branch_ret, w_out, norm_ffn,
                           w_router_group, b_router_group, w_router_expert, b_router_expert)
        cst = _constants(rel_bias, attn_sink[layer], L, ls)
        x1p, routep, k1, v1, s1 = _prompt_mixer(yp, cst, w)
        ys2 = ys.reshape(Ts, D)
        proj = _sample_inproj(ys2, w)
        attn_s, ret_s, k2, v2, s2 = _sample_core(
            proj.reshape(NB, ls, -1),
            cache_k[layer].reshape(NB, WINDOW, KV_W), cache_v[layer].reshape(NB, WINDOW, KV_W),
            state_ret[layer], cst)
        x1s, routes = _sample_post(ys2, attn_s.reshape(Ts, ATTN_W), ret_s.reshape(Ts, RV_W),
                                   proj[:, OFF_GA:OFF_GA + D], proj[:, OFF_GT:OFF_GT + D], w)
        routep2 = routep.reshape(Tp, ROUTE_LANES)
        experts = jnp.concatenate([routep2[:, 0:2], routes[:, 0:2]], axis=0).astype(jnp.int32)
        n_rows = 2 * (Tp + Ts)
        assert n_rows % MOE_TM == 0
        dest, work = _routing_tables(experts, n_rows // MOE_TM)
        xs3 = _dispatch(dest, x1p.reshape(Tp, SUB, LANE), x1s.reshape(Ts, SUB, LANE))
        yb2 = _gmm(work, xs3.reshape(n_rows * SUB, LANE), w['nffn'],
                   w_gate[layer], w_up[layer], w_down[layer])
        yb3 = yb2.reshape(n_rows, SUB, LANE)
        yp = _combine(dest, x1p, routep2, nfin, yb3, 0).reshape(B, L, D)
        ys = _combine(dest, x1s, routes, nfin, yb3, Tp).reshape(NB, ls, D)
        pk.append(k1.reshape(B, WINDOW, KV_HEADS, HEAD_DIM))
        pv.append(v1.reshape(B, WINDOW, KV_HEADS, HEAD_DIM))
        ps.append(s1)
        sk.append(k2.reshape(NB, WINDOW, KV_HEADS, HEAD_DIM))
        sv.append(v2.reshape(NB, WINDOW, KV_HEADS, HEAD_DIM))
        ss.append(s2)
    return (yp, ys, jnp.stack(pk), jnp.stack(pv), jnp.stack(ps),
            jnp.stack(sk), jnp.stack(sv), jnp.stack(ss))
```

```python
import functools
import math

import jax
import jax.numpy as jnp
import numpy as np
from jax import lax
from jax.experimental import pallas as pl
from jax.experimental.pallas import tpu as pltpu

HEAD_DIM = 64
KV_HEADS = 4
GQA_GROUP = 4
ATTN_HEADS = KV_HEADS * GQA_GROUP
WINDOW = 128
N_BUCKETS = 32
MAX_DISTANCE = 128
RET_HEADS = 4
RET_DK = 128
RET_DV = 256
RET_CHUNK = 128
ROPE_BASE = 10000.0
N_GROUPS = 4
EXPERTS_PER_GROUP = 8
N_EXPERTS = N_GROUPS * EXPERTS_PER_GROUP
EXPERT_FF = 512
NORM_EPS = 1e-6
NEG_INF = -1e30
PAST_LEN = 16384

ATTN_W = ATTN_HEADS * HEAD_DIM
KV_W = KV_HEADS * HEAD_DIM
RQ_W = RET_HEADS * RET_DK
RV_W = RET_HEADS * RET_DV
OFF_QA = 0
OFF_KA = OFF_QA + ATTN_W
OFF_VA = OFF_KA + KV_W
OFF_QR = OFF_VA + KV_W
OFF_KR = OFF_QR + RQ_W
OFF_VR = OFF_KR + RQ_W
OFF_GR = OFF_VR + RV_W
OFF_GA = OFF_GR + RV_W
OFF_GT = OFF_GA + 1024
ROUTE_LANES = 128
ROUTE_ROWS = 40
ROUTE_OUT = 8

LANE = 128
SUB = 8
PROMPT_TM = 256
SAMPLE_GROUP = 8
MOE_TM = 256
ROW_TM = 256
DMA_UNROLL = 8
VMEM_LIMIT = 60 * 1024 * 1024

_F32 = jnp.float32
_BF16 = jnp.bfloat16


def _const_spec(shape):
    nd = len(shape)
    return pl.BlockSpec(shape, lambda *_: (0,) * nd, pipeline_mode=pl.Buffered(1))


def _rms(x, gain):
    return x * lax.rsqrt(jnp.mean(x * x, axis=-1, keepdims=True) + NORM_EPS) * gain


def _dot(a, b):
    return jnp.dot(a, b, preferred_element_type=_F32)


def _dot_nt(a, b):
    return lax.dot_general(a, b, (((1,), (1,)), ((), ())), preferred_element_type=_F32)


def _dot_tn(a, b):
    return lax.dot_general(a, b, (((0,), (0,)), ((), ())), preferred_element_type=_F32)


def _load_rows(ref, n):
    return jnp.concatenate([ref[pl.ds(s, n, stride=SUB), :] for s in range(SUB)], axis=1)


def _store_rows(ref, val):
    n = val.shape[0]
    for s in range(SUB):
        ref[pl.ds(s, n, stride=SUB), :] = val[:, s * LANE:(s + 1) * LANE]


def _rotary(x, cosf, sinf):
    return x * cosf + pltpu.roll(x, RET_DK // 2, 1) * sinf


def _post(x, attn, ret, gate_a, gate_r, wba, wbr, wout, nffn, wrh, wrl, br, cnt):
    merged = (jax.nn.sigmoid(gate_a) * _dot(attn.astype(_BF16), wba)
              + jax.nn.sigmoid(gate_r) * _dot(ret.astype(_BF16), wbr))
    x1 = x + _dot(merged.astype(_BF16), wout)
    xn2 = _rms(x1, nffn)
    hi = xn2.astype(_BF16)
    lo = (xn2 - hi.astype(_F32)).astype(_BF16)
    logits = _dot(hi, wrh) + (_dot(hi, wrl) + _dot(lo, wrh)) + br
    n = logits.shape[0]
    lt = logits.T[0:ROUTE_ROWS, :]
    row = lax.broadcasted_iota(jnp.int32, (ROUTE_ROWS, n), 0)
    big = jnp.int32(1 << 20)
    neg = jnp.float32(-jnp.inf)
    gl = jnp.where(row < N_GROUPS, lt, neg)
    gmax = jnp.max(gl, axis=0, keepdims=True)
    gexp = jnp.exp(gl - gmax)
    gsum = jnp.sum(gexp, axis=0, keepdims=True)
    pg = gexp / gsum
    g_w = jnp.max(pg, axis=0, keepdims=True)
    g_idx = jnp.min(jnp.where(pg == g_w, row, big), axis=0, keepdims=True)
    e_row = row - N_GROUPS
    emask = (e_row >= 0) & (e_row < N_EXPERTS) & ((e_row >> 3) == g_idx)
    fl = jnp.where(emask, lt, neg)
    fmax = jnp.max(fl, axis=0, keepdims=True)
    fexp = jnp.exp(fl - fmax)
    fsum = jnp.sum(fexp, axis=0, keepdims=True)
    pe = jnp.where(emask, fexp / fsum, -1.0)
    p1 = jnp.max(pe, axis=0, keepdims=True)
    i1 = jnp.min(jnp.where(pe == p1, row, big), axis=0, keepdims=True)
    pe2 = jnp.where(row == i1, -1.0, pe)
    p2 = jnp.max(pe2, axis=0, keepdims=True)
    i2 = jnp.min(jnp.where(pe2 == p2, row, big), axis=0, keepdims=True)
    psum = p1 + p2
    gate1 = g_w * p1 / psum
    gate2 = g_w * p2 / psum
    oh1 = row == i1
    oh2 = row == i2
    c = jnp.where(oh1 | oh2, 1.0, 0.0)
    tt = lax.broadcasted_iota(jnp.int32, (n, n), 0)
    tc = lax.broadcasted_iota(jnp.int32, (n, n), 1)
    upper = jnp.where(tt < tc, 1.0, 0.0).astype(_BF16)
    before = _dot(c.astype(_BF16), upper) + cnt
    rank1 = jnp.sum(jnp.where(oh1, before, 0.0), axis=0, keepdims=True)
    rank2 = jnp.sum(jnp.where(oh2, before, 0.0), axis=0, keepdims=True)
    cnt = cnt + jnp.sum(c, axis=1, keepdims=True)
    r8 = lax.broadcasted_iota(jnp.int32, (ROUTE_OUT, n), 0)
    vals = [(i1 - N_GROUPS).astype(_F32), (i2 - N_GROUPS).astype(_F32), gate1, gate2, rank1, rank2]
    route = jnp.zeros((ROUTE_OUT, n), _F32)
    for k, v in enumerate(vals):
        route = jnp.where(r8 == k, v, route)
    return x1, route, cnt


def _prompt_mixer_kernel(sink_ref, cdec_ref,
                         x_ref, nmix_ref, win_ref, bias_ref, cos_ref, sin_ref, dec_ref,
                         qd_ref, kd_ref, wba_ref, wbr_ref, wout_ref, nffn_ref,
                         wrh_ref, wrl_ref, br_ref,
                         x1_ref, route_ref, knew_ref, vnew_ref, s_ref, cnt_ref,
                         proj, kctx, vctx, attn, ret):
    i = pl.program_id(1)
    last = pl.num_programs(1) - 1

    @pl.when((i == 0) & (pl.program_id(0) == 0))
    def _():
        cnt_ref[...] = jnp.zeros_like(cnt_ref)

    tm = x_ref.shape[1]
    n_sub = tm // WINDOW
    scale = HEAD_DIM ** -0.5

    @pl.when(i == 0)
    def _():
        s_ref[...] = jnp.zeros_like(s_ref)
        kctx[0:WINDOW, :] = jnp.zeros((WINDOW, KV_W), _BF16)
        vctx[0:WINDOW, :] = jnp.zeros((WINDOW, KV_W), _BF16)

    x = x_ref[0]
    xn = _rms(x, nmix_ref[...]).astype(_BF16)
    n_in = win_ref.shape[1]
    panel = 512
    for c0 in range(0, n_in, panel):
        proj[:, c0:c0 + panel] = _dot(xn, win_ref[:, c0:c0 + panel])

    @pl.when(i == last)
    def _():
        knew_ref[0] = proj[tm - WINDOW:tm, OFF_KA:OFF_KA + KV_W]
        vnew_ref[0] = proj[tm - WINDOW:tm, OFF_VA:OFF_VA + KV_W]

    col = lax.broadcasted_iota(jnp.int32, (1, 2 * WINDOW), 1)
    for c in range(n_sub):
        r0 = c * WINDOW
        kctx[WINDOW:2 * WINDOW, :] = proj[r0:r0 + WINDOW, OFF_KA:OFF_KA + KV_W].astype(_BF16)
        vctx[WINDOW:2 * WINDOW, :] = proj[r0:r0 + WINDOW, OFF_VA:OFF_VA + KV_W].astype(_BF16)
        if c == 0:
            pen = jnp.where((col < WINDOW) & (i == 0), NEG_INF, 0.0).astype(_F32)
        for h in range(KV_HEADS):
            k_h = kctx[:, h * HEAD_DIM:(h + 1) * HEAD_DIM]
            v_h = vctx[:, h * HEAD_DIM:(h + 1) * HEAD_DIM]
            for g in range(GQA_GROUP):
                hq = h * GQA_GROUP + g
                q = proj[r0:r0 + WINDOW, hq * HEAD_DIM:(hq + 1) * HEAD_DIM].astype(_BF16)
                s = _dot_nt(q, k_h) * scale + bias_ref[hq]
                if c == 0:
                    s = s + pen
                snk = sink_ref[hq]
                m = jnp.maximum(jnp.max(s, axis=-1, keepdims=True), snk)
                p = jnp.exp(s - m)
                den = jnp.sum(p, axis=-1, keepdims=True) + jnp.exp(snk - m)
                p = p * (1.0 / den)
                attn[r0:r0 + WINDOW, hq * HEAD_DIM:(hq + 1) * HEAD_DIM] = _dot(p.astype(_BF16), v_h)
        kctx[0:WINDOW, :] = kctx[WINDOW:2 * WINDOW, :]
        vctx[0:WINDOW, :] = vctx[WINDOW:2 * WINDOW, :]

    cosf = cos_ref[...]
    sinf = sin_ref[...]
    for h in range(RET_HEADS):
        qrot = _rotary(proj[:, OFF_QR + h * RET_DK:OFF_QR + (h + 1) * RET_DK], cosf, sinf)
        krot = _rotary(proj[:, OFF_KR + h * RET_DK:OFF_KR + (h + 1) * RET_DK], cosf, sinf) * (RET_DK ** -0.5)
        qd = qd_ref[:, h * RET_DK:(h + 1) * RET_DK]
        kd = kd_ref[:, h * RET_DK:(h + 1) * RET_DK]
        for c in range(n_sub):
            r0 = c * RET_CHUNK
            qc = qrot[r0:r0 + RET_CHUNK]
            kc = krot[r0:r0 + RET_CHUNK]
            vc = proj[r0:r0 + RET_CHUNK, OFF_VR + h * RET_DV:OFF_VR + (h + 1) * RET_DV].astype(_BF16)
            sc = _dot_nt(qc.astype(_BF16), kc.astype(_BF16)) * dec_ref[h]
            s_old = s_ref[0, h]
            o = _dot(sc.astype(_BF16), vc) + _dot((qc * qd).astype(_BF16), s_old.astype(_BF16))
            s_ref[0, h] = s_old * cdec_ref[h] + _dot_tn((kc * kd).astype(_BF16), vc)
            o = o * lax.rsqrt(jnp.mean(o * o, axis=-1, keepdims=True) + NORM_EPS)
            gr = proj[r0:r0 + RET_CHUNK, OFF_GR + h * RET_DV:OFF_GR + (h + 1) * RET_DV]
            ret[r0:r0 + RET_CHUNK, h * RET_DV:(h + 1) * RET_DV] = o * (gr * jax.nn.sigmoid(gr))

    x1, route, cnt = _post(x, attn[...], ret[...],
                           proj[:, OFF_GA:OFF_GA + 1024], proj[:, OFF_GT:OFF_GT + 1024],
                           wba_ref[...], wbr_ref[...], wout_ref[...], nffn_ref[...],
                           wrh_ref[...], wrl_ref[...], br_ref[...], cnt_ref[:, 0:1])
    _store_rows(x1_ref, x1)
    route_ref[...] = route
    cnt_ref[...] = jnp.broadcast_to(cnt, cnt_ref.shape)


def _prompt_mixer(x, cst, w):
    B, L, D = x.shape
    tm = min(PROMPT_TM, L)
    nb = L // tm
    n_in = w['win'].shape[1]
    step = lambda b, i, *_: (b, i, 0)
    per_b = lambda b, i, *_: (b, 0, 0)
    grid_spec = pltpu.PrefetchScalarGridSpec(
        num_scalar_prefetch=2,
        grid=(B, nb),
        in_specs=[
            pl.BlockSpec((1, tm, D), step),
            _const_spec((1, D)),
            _const_spec((D, n_in)),
            _const_spec((ATTN_HEADS, WINDOW, 2 * WINDOW)),
            pl.BlockSpec((tm, RET_DK), lambda b, i, *_: (i, 0)),
            pl.BlockSpec((tm, RET_DK), lambda b, i, *_: (i, 0)),
            _const_spec((RET_HEADS, RET_CHUNK, RET_CHUNK)),
            _const_spec((RET_CHUNK, RQ_W)),
            _const_spec((RET_CHUNK, RQ_W)),
            _const_spec((ATTN_W, D)),
            _const_spec((RV_W, D)),
            _const_spec((D, D)),
            _const_spec((1, D)),
            _const_spec((D, ROUTE_LANES)),
            _const_spec((D, ROUTE_LANES)),
            _const_spec((1, ROUTE_LANES)),
        ],
        out_specs=[
            pl.BlockSpec((tm * SUB, LANE), lambda b, i, *_: (b * nb + i, 0)),
            pl.BlockSpec((ROUTE_OUT, tm), lambda b, i, *_: (0, b * nb + i)),
            pl.BlockSpec((1, WINDOW, KV_W), per_b),
            pl.BlockSpec((1, WINDOW, KV_W), per_b),
            pl.BlockSpec((1, RET_HEADS, RET_DK, RET_DV), lambda b, i, *_: (b, 0, 0, 0)),
            pl.BlockSpec((ROUTE_ROWS, LANE), lambda b, i, *_: (0, 0)),
        ],
        scratch_shapes=[
            pltpu.VMEM((tm, n_in), _F32),
            pltpu.VMEM((2 * WINDOW, KV_W), _BF16),
            pltpu.VMEM((2 * WINDOW, KV_W), _BF16),
            pltpu.VMEM((tm, ATTN_W), _F32),
            pltpu.VMEM((tm, RV_W), _F32),
        ],
    )
    assert D == SUB * LANE
    out_shape = [
        jax.ShapeDtypeStruct((B * L * SUB, LANE), _F32),
        jax.ShapeDtypeStruct((ROUTE_OUT, B * L), _F32),
        jax.ShapeDtypeStruct((B, WINDOW, KV_W), _F32),
        jax.ShapeDtypeStruct((B, WINDOW, KV_W), _F32),
        jax.ShapeDtypeStruct((B, RET_HEADS, RET_DK, RET_DV), _F32),
        jax.ShapeDtypeStruct((ROUTE_ROWS, LANE), _F32),
    ]
    return pl.pallas_call(
        _prompt_mixer_kernel,
        grid_spec=grid_spec,
        out_shape=out_shape,
        compiler_params=pltpu.CompilerParams(
            dimension_semantics=("arbitrary", "arbitrary"), vmem_limit_bytes=VMEM_LIMIT),
        name="prompt_mixer",
    )(cst['sink'], cst['cdec_p'],
      x, w['nmix'], w['win'], cst['bias_p'], cst['cos_p'], cst['sin_p'], cst['dec_p'],
      cst['qd_p'], cst['kd_p'], w['wba'], w['wbr'], w['wout'], w['nffn'],
      w['wrh'], w['wrl'], w['br'])


def _inproj_kernel(x_ref, nmix_ref, win_ref, o_ref):
    xn = _rms(x_ref[...], nmix_ref[...]).astype(_BF16)
    o_ref[...] = _dot(xn, win_ref[...])


def _sample_inproj(x2d, w):
    T, D = x2d.shape
    n_in = w['win'].shape[1]
    panel = 512
    return pl.pallas_call(
        _inproj_kernel,
        grid=(n_in // panel,),
        in_specs=[pl.BlockSpec((T, D), lambda j: (0, 0)),
                  pl.BlockSpec((1, D), lambda j: (0, 0)),
                  pl.BlockSpec((D, panel), lambda j: (0, j))],
        out_specs=pl.BlockSpec((T, panel), lambda j: (0, j)),
        out_shape=jax.ShapeDtypeStruct((T, n_in), _F32),
        compiler_params=pltpu.CompilerParams(
            dimension_semantics=("arbitrary",), vmem_limit_bytes=VMEM_LIMIT),
        name="sample_inproj",
    )(x2d, w['nmix'], w['win'])


def _sample_core_kernel(cdec_ref, proj_ref, ck_ref, cv_ref, st_ref, bh_ref, bn_ref, snk_ref,
                        cos_ref, sin_ref, dec_ref, qd_ref, kd_ref,
                        attn_ref, ret_ref, nk_ref, nv_ref, ns_ref):
    G = proj_ref.shape[0]
    ls = proj_ref.shape[1]
    scale = HEAD_DIM ** -0.5
    cosf = cos_ref[...]
    sinf = sin_ref[...]

    def body(b, carry):
        row = proj_ref[b]
        k_new = row[:, OFF_KA:OFF_KA + KV_W]
        v_new = row[:, OFF_VA:OFF_VA + KV_W]
        ck = ck_ref[b]
        cv = cv_ref[b]
        nk_ref[b, 0:WINDOW - ls, :] = ck[ls:WINDOW, :]
        nk_ref[b, WINDOW - ls:WINDOW, :] = k_new
        nv_ref[b, 0:WINDOW - ls, :] = cv[ls:WINDOW, :]
        nv_ref[b, WINDOW - ls:WINDOW, :] = v_new
        ckb = ck.astype(_BF16)
        cvb = cv.astype(_BF16)
        knb = k_new.astype(_BF16)
        vnb = v_new.astype(_BF16)
        for h in range(KV_HEADS):
            hs = slice(h * HEAD_DIM, (h + 1) * HEAD_DIM)
            q4 = jnp.concatenate(
                [row[:, (h * GQA_GROUP + g) * HEAD_DIM:(h * GQA_GROUP + g + 1) * HEAD_DIM]
                 for g in range(GQA_GROUP)], axis=0).astype(_BF16)
            s1 = _dot_nt(q4, ckb[:, hs]) * scale + bh_ref[h]
            s2 = _dot_nt(q4, knb[:, hs]) * scale + bn_ref[h]
            snk = snk_ref[h]
            m = jnp.maximum(jnp.maximum(jnp.max(s1, axis=-1, keepdims=True),
                                        jnp.max(s2, axis=-1, keepdims=True)), snk)
            p1 = jnp.exp(s1 - m)
            p2 = jnp.exp(s2 - m)
            den = (jnp.sum(p1, axis=-1, keepdims=True) + jnp.sum(p2, axis=-1, keepdims=True)
                   + jnp.exp(snk - m))
            r = 1.0 / den
            o = _dot((p1 * r).astype(_BF16), cvb[:, hs]) + _dot((p2 * r).astype(_BF16), vnb[:, hs])
            for g in range(GQA_GROUP):
                hq = h * GQA_GROUP + g
                attn_ref[b, :, hq * HEAD_DIM:(hq + 1) * HEAD_DIM] = o[g * ls:(g + 1) * ls]
        for h in range(RET_HEADS):
            qrot = _rotary(row[:, OFF_QR + h * RET_DK:OFF_QR + (h + 1) * RET_DK], cosf, sinf)
            krot = _rotary(row[:, OFF_KR + h * RET_DK:OFF_KR + (h + 1) * RET_DK], cosf, sinf) * (RET_DK ** -0.5)
            vc = row[:, OFF_VR + h * RET_DV:OFF_VR + (h + 1) * RET_DV].astype(_BF16)
            sc = _dot_nt(qrot.astype(_BF16), krot.astype(_BF16)) * dec_ref[h]
            s_old = st_ref[b, h]
            qd = qd_ref[:, h * RET_DK:(h + 1) * RET_DK]
            kd = kd_ref[:, h * RET_DK:(h + 1) * RET_DK]
            o = _dot(sc.astype(_BF16), vc) + _dot((qrot * qd).astype(_BF16), s_old.astype(_BF16))
            ns_ref[b, h] = s_old * cdec_ref[h] + _dot_tn((krot * kd).astype(_BF16), vc)
            o = o * lax.rsqrt(jnp.mean(o * o, axis=-1, keepdims=True) + NORM_EPS)
            gr = row[:, OFF_GR + h * RET_DV:OFF_GR + (h + 1) * RET_DV]
            ret_ref[b, :, h * RET_DV:(h + 1) * RET_DV] = o * (gr * jax.nn.sigmoid(gr))
        return carry

    lax.fori_loop(0, G, body, 0)


def _sample_core(proj3, ck, cv, st, cst):
    NB, ls, n_in = proj3.shape
    G = min(SAMPLE_GROUP, NB)
    blk3 = lambda i, *_: (i, 0, 0)
    blk4 = lambda i, *_: (i, 0, 0, 0)
    c2 = lambda i, *_: (0, 0)
    c3 = lambda i, *_: (0, 0, 0)
    ql = GQA_GROUP * ls
    grid_spec = pltpu.PrefetchScalarGridSpec(
        num_scalar_prefetch=1,
        grid=(NB // G,),
        in_specs=[
            pl.BlockSpec((G, ls, n_in), blk3),
            pl.BlockSpec((G, WINDOW, KV_W), blk3),
            pl.BlockSpec((G, WINDOW, KV_W), blk3),
            pl.BlockSpec((G, RET_HEADS, RET_DK, RET_DV), blk4),
            pl.BlockSpec((KV_HEADS, ql, WINDOW), c3),
            pl.BlockSpec((KV_HEADS, ql, ls), c3),
            pl.BlockSpec((KV_HEADS, ql, 1), c3),
            pl.BlockSpec((ls, RET_DK), c2),
            pl.BlockSpec((ls, RET_DK), c2),
            pl.BlockSpec((RET_HEADS, ls, ls), c3),
            pl.BlockSpec((ls, RQ_W), c2),
            pl.BlockSpec((ls, RQ_W), c2),
        ],
        out_specs=[
            pl.BlockSpec((G, ls, ATTN_W), blk3),
            pl.BlockSpec((G, ls, RV_W), blk3),
            pl.BlockSpec((G, WINDOW, KV_W), blk3),
            pl.BlockSpec((G, WINDOW, KV_W), blk3),
            pl.BlockSpec((G, RET_HEADS, RET_DK, RET_DV), blk4),
        ],
    )
    out_shape = [
        jax.ShapeDtypeStruct((NB, ls, ATTN_W), _F32),
        jax.ShapeDtypeStruct((NB, ls, RV_W), _F32),
        jax.ShapeDtypeStruct((NB, WINDOW, KV_W), _F32),
        jax.ShapeDtypeStruct((NB, WINDOW, KV_W), _F32),
        jax.ShapeDtypeStruct((NB, RET_HEADS, RET_DK, RET_DV), _F32),
    ]
    return pl.pallas_call(
        _sample_core_kernel,
        grid_spec=grid_spec,
        out_shape=out_shape,
        compiler_params=pltpu.CompilerParams(
            dimension_semantics=("arbitrary",), vmem_limit_bytes=VMEM_LIMIT),
        name="sample_core",
    )(cst['cdec_s'], proj3, ck, cv, st, cst['bias_hist'], cst['bias_new'], cst['sink_col'],
      cst['cos_s'], cst['sin_s'], cst['dec_s'], cst['qd_s'], cst['kd_s'])


def _sample_post_kernel(x_ref, attn_ref, ret_ref, ga_ref, gt_ref, wba_ref, wbr_ref, wout_ref,
                        nffn_ref, wrh_ref, wrl_ref, br_ref, cnt0_ref, x1_ref, route_ref, cnt_ref):
    x1, route, cnt = _post(x_ref[...], attn_ref[...], ret_ref[...], ga_ref[...], gt_ref[...],
                           wba_ref[...], wbr_ref[...], wout_ref[...], nffn_ref[...],
                           wrh_ref[...], wrl_ref[...], br_ref[...], cnt0_ref[:, 0:1])
    _store_rows(x1_ref, x1)
    route_ref[...] = route
    cnt_ref[...] = jnp.broadcast_to(cnt, cnt_ref.shape)


def _sample_post(x2d, attn, ret, ga, gt, w, cnt0):
    T, D = x2d.shape
    full = lambda s: pl.BlockSpec(s, lambda i: (0,) * len(s))
    return pl.pallas_call(
        _sample_post_kernel,
        grid=(1,),
        in_specs=[full((T, D)), full((T, ATTN_W)), full((T, RV_W)), full((T, D)), full((T, D)),
                  full((ATTN_W, D)), full((RV_W, D)), full((D, D)), full((1, D)),
                  full((D, ROUTE_LANES)), full((D, ROUTE_LANES)), full((1, ROUTE_LANES)),
                  full((ROUTE_ROWS, LANE))],
        out_specs=[full((T * SUB, LANE)), full((ROUTE_OUT, T)), full((ROUTE_ROWS, LANE))],
        out_shape=[jax.ShapeDtypeStruct((T * SUB, LANE), _F32),
                   jax.ShapeDtypeStruct((ROUTE_OUT, T), _F32),
                   jax.ShapeDtypeStruct((ROUTE_ROWS, LANE), _F32)],
        compiler_params=pltpu.CompilerParams(
            dimension_semantics=("arbitrary",), vmem_limit_bytes=VMEM_LIMIT),
        name="sample_post",
    )(x2d, attn, ret, ga, gt, w['wba'], w['wbr'], w['wout'], w['nffn'], w['wrh'], w['wrl'], w['br'],
      cnt0)


def _dispatch_kernel(dest_ref, xp_ref, xq_ref, xs_ref, sem, *, p_steps):
    i = pl.program_id(0)

    def copy_tile(src):
        def tile_copy(r, d):
            return pltpu.make_async_copy(src.at[r], xs_ref.at[d], sem)

        def start(r, c):
            t = i * ROW_TM + r
            tile_copy(r, dest_ref[2 * t]).start(priority=0)
            tile_copy(r, dest_ref[2 * t + 1]).start(priority=1)
            return c

        lax.fori_loop(0, ROW_TM, start, 0, unroll=DMA_UNROLL)

        def wait(r, c):
            tile_copy(0, 0).wait()
            tile_copy(0, 0).wait()
            return c

        lax.fori_loop(0, ROW_TM, wait, 0, unroll=DMA_UNROLL)

    @pl.when(i < p_steps)
    def _():
        copy_tile(xp_ref)

    @pl.when(i >= p_steps)
    def _():
        copy_tile(xq_ref)


def _dispatch(dest, xp3, xq3):
    Tp, Tq = xp3.shape[0], xq3.shape[0]
    assert Tp % ROW_TM == 0 and Tq % ROW_TM == 0
    p_steps = Tp // ROW_TM
    grid_spec = pltpu.PrefetchScalarGridSpec(
        num_scalar_prefetch=1,
        grid=((Tp + Tq) // ROW_TM,),
        in_specs=[
            pl.BlockSpec((ROW_TM, SUB, LANE), lambda i, *_: (jnp.minimum(i, p_steps - 1), 0, 0)),
            pl.BlockSpec((ROW_TM, SUB, LANE), lambda i, *_: (jnp.maximum(i - p_steps, 0), 0, 0)),
        ],
        out_specs=pl.BlockSpec(memory_space=pl.ANY),
        scratch_shapes=[pltpu.SemaphoreType.DMA],
    )
    return pl.pallas_call(
        functools.partial(_dispatch_kernel, p_steps=p_steps),
        grid_spec=grid_spec,
        out_shape=jax.ShapeDtypeStruct((2 * (Tp + Tq), SUB, LANE), _F32),
        compiler_params=pltpu.CompilerParams(dimension_semantics=("arbitrary",)),
        name="moe_dispatch",
    )(dest, xp3, xq3)


def _gmm_kernel(tile_ref, exp_ref, lo_ref, hi_ref, first_ref, chg_ref,
                x_ref, nffn_ref, wg_ref, wu_ref, wd_ref, y_ref, wg_s, wu_s, wd_s):
    m = pl.program_id(0)
    tm = x_ref.shape[0] // SUB

    @pl.when(chg_ref[m] == 1)
    def _():
        wg_s[...] = wg_ref[0].astype(_BF16)
        wu_s[...] = wu_ref[0].astype(_BF16)
        wd_s[...] = wd_ref[0].astype(_BF16)

    lo = lo_ref[m]
    hi = hi_ref[m]

    @pl.when(hi > lo)
    def _():
        rows = tile_ref[m] * tm + lax.broadcasted_iota(jnp.int32, (tm, 1), 0)
        mine = (rows >= lo) & (rows < hi)
        xn = _rms(_load_rows(x_ref, tm), nffn_ref[...])
        x = jnp.where(mine, xn, 0.0).astype(_BF16)
        a = _dot(x, wg_s[...])
        hmid = (a * jax.nn.sigmoid(a)) * _dot(x, wu_s[...])
        y = _dot(hmid.astype(_BF16), wd_s[...])

        @pl.when(first_ref[m] == 1)
        def _():
            _store_rows(y_ref, y)

        @pl.when(first_ref[m] == 0)
        def _():
            _store_rows(y_ref, _load_rows(y_ref, tm) + y)


def _gmm(work, xs2, nffn, wg, wu, wd):
    A = xs2.shape[0] // SUB
    E, D, F = wg.shape
    n_work = work[0].shape[0]
    grid_spec = pltpu.PrefetchScalarGridSpec(
        num_scalar_prefetch=6,
        grid=(n_work,),
        in_specs=[
            pl.BlockSpec((MOE_TM * SUB, LANE), lambda m, t, e, *_: (t[m], 0)),
            pl.BlockSpec((1, D), lambda m, t, e, *_: (0, 0)),
            pl.BlockSpec((1, D, F), lambda m, t, e, *_: (e[m], 0, 0)),
            pl.BlockSpec((1, D, F), lambda m, t, e, *_: (e[m], 0, 0)),
            pl.BlockSpec((1, F, D), lambda m, t, e, *_: (e[m], 0, 0)),
        ],
        out_specs=pl.BlockSpec((MOE_TM * SUB, LANE), lambda m, t, e, *_: (t[m], 0)),
        scratch_shapes=[pltpu.VMEM((D, F), _BF16), pltpu.VMEM((D, F), _BF16),
                        pltpu.VMEM((F, D), _BF16)],
    )
    return pl.pallas_call(
        _gmm_kernel,
        grid_spec=grid_spec,
        out_shape=jax.ShapeDtypeStruct((A * SUB, LANE), _F32),
        compiler_params=pltpu.CompilerParams(
            dimension_semantics=("arbitrary",), vmem_limit_bytes=VMEM_LIMIT),
        name="moe_gmm",
    )(*work, xs2, nffn, wg, wu, wd)


def _combine_kernel(dest_ref, x1_ref, route_ref, nfin_ref, yb_ref, o_ref, buf, sems, *, tok0):
    i = pl.program_id(0)
    n_steps = pl.num_programs(0)
    tm = o_ref.shape[0]

    def tile_copy(d, slot, k, r):
        return pltpu.make_async_copy(yb_ref.at[d], buf.at[slot, k, r], sems.at[slot])

    def issue(step, slot):
        base = tok0 + step * tm

        def start(r, c):
            t = base + r
            tile_copy(dest_ref[2 * t], slot, 0, r).start(priority=0)
            tile_copy(dest_ref[2 * t + 1], slot, 1, r).start(priority=1)
            return c

        lax.fori_loop(0, tm, start, 0, unroll=DMA_UNROLL)

    @pl.when(i == 0)
    def _():
        issue(0, 0)

    @pl.when(i + 1 < n_steps)
    def _():
        issue(i + 1, (i + 1) % 2)

    slot = i % 2

    def wait(r, c):
        tile_copy(0, slot, 0, 0).wait()
        tile_copy(0, slot, 1, 0).wait()
        return c

    lax.fori_loop(0, tm, wait, 0, unroll=DMA_UNROLL)
    rt = jnp.concatenate([route_ref[...], jnp.zeros((LANE - ROUTE_OUT, tm), _F32)], axis=0).T
    g0 = rt[:, 2:3]
    g1 = rt[:, 3:4]
    y0 = jnp.concatenate([buf[slot, 0, :, s, :] for s in range(SUB)], axis=1)
    y1 = jnp.concatenate([buf[slot, 1, :, s, :] for s in range(SUB)], axis=1)
    y = _load_rows(x1_ref, tm) + (y0 * g0 + y1 * g1)
    o_ref[...] = _rms(y, nfin_ref[...])


def _combine(dest, x1_2, route, nfin, yb3, tok0):
    T = x1_2.shape[0] // SUB
    D = SUB * LANE
    tm = min(ROW_TM, T)
    grid_spec = pltpu.PrefetchScalarGridSpec(
        num_scalar_prefetch=1,
        grid=(T // tm,),
        in_specs=[
            pl.BlockSpec((tm * SUB, LANE), lambda i, *_: (i, 0)),
            pl.BlockSpec((ROUTE_OUT, tm), lambda i, *_: (0, i)),
            pl.BlockSpec((1, D), lambda i, *_: (0, 0)),
            pl.BlockSpec(memory_space=pl.ANY),
        ],
        out_specs=pl.BlockSpec((tm, D), lambda i, *_: (i, 0)),
        scratch_shapes=[pltpu.VMEM((2, 2, tm, SUB, LANE), _F32), pltpu.SemaphoreType.DMA((2,))],
    )
    return pl.pallas_call(
        functools.partial(_combine_kernel, tok0=tok0),
        grid_spec=grid_spec,
        out_shape=jax.ShapeDtypeStruct((T, D), _F32),
        compiler_params=pltpu.CompilerParams(
            dimension_semantics=("arbitrary",), vmem_limit_bytes=VMEM_LIMIT),
        name="moe_combine",
    )(dest, x1_2, route, nfin, yb3)


def _routing_tables(route, counts, n_tiles):
    experts = route[0:2].astype(jnp.int32).T.reshape(-1)
    ranks = route[4:6].astype(jnp.int32).T.reshape(-1)
    A = experts.shape[0]
    ids = jnp.arange(N_EXPERTS, dtype=jnp.int32)
    ends = jnp.cumsum(counts)
    starts = ends - counts
    dest = ranks + jnp.sum(jnp.where(experts[:, None] == ids[None, :], starts[None, :], 0), axis=1)
    tile_starts = jnp.arange(n_tiles, dtype=jnp.int32) * MOE_TM
    pos_t = jnp.arange(n_tiles, dtype=jnp.int32) + jnp.sum(starts[None, :] < tile_starts[:, None], axis=1)
    pos_e = ids + jnp.sum(tile_starts[None, :] <= starts[:, None], axis=1)
    slots = jnp.arange(n_tiles + N_EXPERTS, dtype=jnp.int32)
    pts = (jnp.sum(jnp.where(pos_t[None, :] == slots[:, None], tile_starts[None, :], 0), axis=1)
           + jnp.sum(jnp.where(pos_e[None, :] == slots[:, None], starts[None, :], 0), axis=1))
    lo = pts.astype(jnp.int32)
    hi = jnp.concatenate([lo[1:], jnp.array([A], jnp.int32)])
    tile = jnp.minimum(lo // MOE_TM, n_tiles - 1).astype(jnp.int32)
    expert = jnp.minimum(jnp.sum(ends[None, :] <= lo[:, None], axis=1), N_EXPERTS - 1).astype(jnp.int32)
    nonempty = hi > lo
    first = (nonempty & (lo % MOE_TM == 0)).astype(jnp.int32)
    chg = jnp.concatenate([jnp.ones((1,), jnp.int32), (expert[1:] != expert[:-1]).astype(jnp.int32)])
    return dest.astype(jnp.int32), (tile, expert, lo, hi, first, chg)


def _bucket_table(lq, lk):
    dist = np.arange(lq)[:, None] + WINDOW - np.arange(lk)[None, :]
    band = (dist >= 0) & (dist < WINDOW)
    d = np.clip(dist, 0, WINDOW - 1)
    max_exact = N_BUCKETS // 2
    d_f = np.maximum(d, 1).astype(np.float32)
    large = max_exact + (np.log(d_f / max_exact) / math.log(MAX_DISTANCE / max_exact)
                         * (N_BUCKETS - max_exact)).astype(np.int32)
    large = np.minimum(large, N_BUCKETS - 1)
    return np.where(d < max_exact, d, large).astype(np.int32), band


def _bias_table(rb, lq, lk):
    bkt, band = _bucket_table(lq, lk)
    onehot = jnp.asarray(bkt)[None, :, :] == jnp.arange(N_BUCKETS, dtype=jnp.int32)[:, None, None]
    bias = jnp.sum(jnp.where(onehot[:, None], rb[:, :, None, None], 0.0), axis=0)
    return jnp.where(jnp.asarray(band)[None], bias, NEG_INF)


def _decay_tables(C):
    log_gamma = jnp.log(1.0 - 2.0 ** (-5.0 - jnp.arange(RET_HEADS, dtype=_F32)))
    idx = jnp.arange(C, dtype=_F32)
    diff = idx[:, None] - idx[None, :]
    decay_in = jnp.where((diff >= 0)[..., None],
                         jnp.exp(jnp.maximum(diff, 0.0)[..., None] * log_gamma), 0.0)
    q_dec = jnp.exp((idx + 1.0)[:, None] * log_gamma)
    k_dec = jnp.exp((C - 1.0 - idx)[:, None] * log_gamma)
    c_dec = jnp.exp(C * log_gamma)
    dec = jnp.transpose(decay_in, (2, 0, 1))
    qd = jnp.repeat(q_dec, RET_DK, axis=1)
    kd = jnp.repeat(k_dec, RET_DK, axis=1)
    return dec, qd, kd, c_dec


def _rope_tables(pos):
    half = RET_DK // 2
    inv = ROPE_BASE ** (-jnp.arange(half, dtype=_F32) * 2.0 / RET_DK)
    ang = pos.astype(_F32)[:, None] * inv[None, :]
    cos = jnp.cos(ang)
    sin = jnp.sin(ang)
    return jnp.concatenate([cos, cos], axis=1), jnp.concatenate([-sin, sin], axis=1)


def _constants(rel_bias, attn_sink, L, ls):
    cst = {}
    rb = rel_bias.astype(_F32)
    cst['bias_p'] = _bias_table(rb, WINDOW, 2 * WINDOW)
    cst['sink'] = attn_sink.astype(_F32)
    cst['cos_p'], cst['sin_p'] = _rope_tables(jnp.arange(L))
    cst['dec_p'], cst['qd_p'], cst['kd_p'], cst['cdec_p'] = _decay_tables(min(RET_CHUNK, L))
    bias = _bias_table(rb, ls, WINDOW + ls).reshape(KV_HEADS, GQA_GROUP * ls, WINDOW + ls)
    cst['bias_hist'] = bias[:, :, :WINDOW]
    cst['bias_new'] = bias[:, :, WINDOW:]
    cst['sink_col'] = jnp.repeat(attn_sink.astype(_F32).reshape(KV_HEADS, GQA_GROUP), ls,
                                 axis=1)[..., None]
    cst['cos_s'], cst['sin_s'] = _rope_tables(PAST_LEN + jnp.arange(ls))
    cst['dec_s'], cst['qd_s'], cst['kd_s'], cst['cdec_s'] = _decay_tables(min(RET_CHUNK, ls))
    return cst


def _layer_weights(layer, norm_mix, w_in, w_branch_attn, w_branch_ret, w_out, norm_ffn,
                   w_router_group, b_router_group, w_router_expert, b_router_expert):
    D = w_in.shape[1]
    wr = jnp.concatenate([w_router_group[layer].astype(_F32), w_router_expert[layer].astype(_F32)], axis=1)
    wr = jnp.pad(wr, ((0, 0), (0, ROUTE_LANES - wr.shape[1])))
    wrh = wr.astype(_BF16)
    wrl = (wr - wrh.astype(_F32)).astype(_BF16)
    br = jnp.concatenate([b_router_group[layer].astype(_F32), b_router_expert[layer].astype(_F32)])
    br = jnp.pad(br, (0, ROUTE_LANES - br.shape[0]))[None, :]
    return {
        'nmix': norm_mix[layer].astype(_F32)[None, :],
        'win': w_in[layer].astype(_BF16),
        'wba': w_branch_attn[layer].astype(_BF16),
        'wbr': w_branch_ret[layer].astype(_BF16),
        'wout': w_out[layer].astype(_BF16),
        'nffn': norm_ffn[layer].astype(_F32)[None, :],
        'wrh': wrh, 'wrl': wrl, 'br': br,
    }


def kernel(x_prompt, x_sample, cache_k, cache_v, state_ret, norm_mix, w_in, attn_sink, rel_bias,
           w_branch_attn, w_branch_ret, w_out, norm_ffn, w_router_group, b_router_group,
           w_router_expert, b_router_expert, w_gate, w_up, w_down, norm_final):
    depth = w_in.shape[0]
    assert depth == 1, "the final norm is fused into the MoE combine of the only layer"
    B, L, D = x_prompt.shape
    NB, ls, _ = x_sample.shape
    Tp, Ts = B * L, NB * ls
    nfin = norm_final.astype(_F32)[None, :]
    yp, ys = x_prompt, x_sample
    pk, pv, ps, sk, sv, ss = [], [], [], [], [], []
    for layer in range(depth):
        w = _layer_weights(layer, norm_mix, w_in, w_branch_attn, w_branch_ret, w_out, norm_ffn,
                           w_router_group, b_router_group, w_router_expert, b_router_expert)
        cst = _constants(rel_bias, attn_sink[layer], L, ls)
        x1p, routep, k1, v1, s1, cnt_p = _prompt_mixer(yp, cst, w)
        ys2 = ys.reshape(Ts, D)
        proj = _sample_inproj(ys2, w)
        attn_s, ret_s, k2, v2, s2 = _sample_core(
            proj.reshape(NB, ls, -1),
            cache_k[layer].reshape(NB, WINDOW, KV_W), cache_v[layer].reshape(NB, WINDOW, KV_W),
            state_ret[layer], cst)
        x1s, routes, cnt_all = _sample_post(
            ys2, attn_s.reshape(Ts, ATTN_W), ret_s.reshape(Ts, RV_W),
            proj[:, OFF_GA:OFF_GA + D], proj[:, OFF_GT:OFF_GT + D], w, cnt_p)
        n_rows = 2 * (Tp + Ts)
        assert n_rows % MOE_TM == 0
        counts = cnt_all[N_GROUPS:N_GROUPS + N_EXPERTS, 0].astype(jnp.int32)
        dest, work = _routing_tables(jnp.concatenate([routep, routes], axis=1), counts,
                                     n_rows // MOE_TM)
        xs3 = _dispatch(dest, x1p.reshape(Tp, SUB, LANE), x1s.reshape(Ts, SUB, LANE))
        yb2 = _gmm(work, xs3.reshape(n_rows * SUB, LANE), w['nffn'],
                   w_gate[layer], w_up[layer], w_down[layer])
        yb3 = yb2.reshape(n_rows, SUB, LANE)
        yp = _combine(dest, x1p, routep, nfin, yb3, 0).reshape(B, L, D)
        ys = _combine(dest, x1s, routes, nfin, yb3, Tp).reshape(NB, ls, D)
        pk.append(k1.reshape(B, WINDOW, KV_HEADS, HEAD_DIM))
        pv.append(v1.reshape(B, WINDOW, KV_HEADS, HEAD_DIM))
        ps.append(s1)
        sk.append(k2.reshape(NB, WINDOW, KV_HEADS, HEAD_DIM))
        sv.append(v2.reshape(NB, WINDOW, KV_HEADS, HEAD_DIM))
        ss.append(s2)
    return (yp, ys, jnp.stack(pk), jnp.stack(pv), jnp.stack(ps),
            jnp.stack(sk), jnp.stack(sv), jnp.stack(ss))
```

```python
import functools
import math

import jax
import jax.numpy as jnp
import numpy as np
from jax import lax
from jax.experimental import pallas as pl
from jax.experimental.pallas import tpu as pltpu

HEAD_DIM = 64
KV_HEADS = 4
GQA_GROUP = 4
ATTN_HEADS = KV_HEADS * GQA_GROUP
WINDOW = 128
N_BUCKETS = 32
MAX_DISTANCE = 128
RET_HEADS = 4
RET_DK = 128
RET_DV = 256
RET_CHUNK = 128
ROPE_BASE = 10000.0
N_GROUPS = 4
EXPERTS_PER_GROUP = 8
N_EXPERTS = N_GROUPS * EXPERTS_PER_GROUP
EXPERT_FF = 512
NORM_EPS = 1e-6
NEG_INF = -1e30
PAST_LEN = 16384

ATTN_W = ATTN_HEADS * HEAD_DIM
KV_W = KV_HEADS * HEAD_DIM
RQ_W = RET_HEADS * RET_DK
RV_W = RET_HEADS * RET_DV
OFF_QA = 0
OFF_KA = OFF_QA + ATTN_W
OFF_VA = OFF_KA + KV_W
OFF_QR = OFF_VA + KV_W
OFF_KR = OFF_QR + RQ_W
OFF_VR = OFF_KR + RQ_W
OFF_GR = OFF_VR + RV_W
OFF_GA = OFF_GR + RV_W
OFF_GT = OFF_GA + 1024
ROUTE_LANES = 128
ROUTE_ROWS = 40
ROUTE_OUT = 8

LANE = 128
SUB = 8
PROMPT_TM = 256
SAMPLE_GROUP = 8
SAMPLE_UNROLL = 2
MOE_TM = 256
ROW_TM = 256
DMA_UNROLL = 8
VMEM_LIMIT = 60 * 1024 * 1024

_F32 = jnp.float32
_BF16 = jnp.bfloat16


def _const_spec(shape):
    nd = len(shape)
    return pl.BlockSpec(shape, lambda *_: (0,) * nd, pipeline_mode=pl.Buffered(1))


def _rms(x, gain):
    return x * lax.rsqrt(jnp.mean(x * x, axis=-1, keepdims=True) + NORM_EPS) * gain


def _dot(a, b):
    return jnp.dot(a, b, preferred_element_type=_F32)


def _dot_nt(a, b):
    return lax.dot_general(a, b, (((1,), (1,)), ((), ())), preferred_element_type=_F32)


def _dot_tn(a, b):
    return lax.dot_general(a, b, (((0,), (0,)), ((), ())), preferred_element_type=_F32)


def _load_rows(ref, n):
    return jnp.concatenate([ref[pl.ds(s, n, stride=SUB), :] for s in range(SUB)], axis=1)


def _store_rows(ref, val):
    n = val.shape[0]
    for s in range(SUB):
        ref[pl.ds(s, n, stride=SUB), :] = val[:, s * LANE:(s + 1) * LANE]


def _rotary(x, cosf, sinf):
    return x * cosf + pltpu.roll(x, RET_DK // 2, 1) * sinf


def _post(x, attn, ret, gate_a, gate_r, wba, wbr, wout, nffn, wrh, wrl, br, cnt):
    merged = (jax.nn.sigmoid(gate_a) * _dot(attn.astype(_BF16), wba)
              + jax.nn.sigmoid(gate_r) * _dot(ret.astype(_BF16), wbr))
    x1 = x + _dot(merged.astype(_BF16), wout)
    xn2 = _rms(x1, nffn)
    hi = xn2.astype(_BF16)
    lo = (xn2 - hi.astype(_F32)).astype(_BF16)
    logits = _dot(hi, wrh) + (_dot(hi, wrl) + _dot(lo, wrh)) + br
    n = logits.shape[0]
    lt = logits.T[0:ROUTE_ROWS, :]
    row = lax.broadcasted_iota(jnp.int32, (ROUTE_ROWS, n), 0)
    big = jnp.int32(1 << 20)
    neg = jnp.float32(-jnp.inf)
    gl = jnp.where(row < N_GROUPS, lt, neg)
    gmax = jnp.max(gl, axis=0, keepdims=True)
    gexp = jnp.exp(gl - gmax)
    gsum = jnp.sum(gexp, axis=0, keepdims=True)
    pg = gexp / gsum
    g_w = jnp.max(pg, axis=0, keepdims=True)
    g_idx = jnp.min(jnp.where(pg == g_w, row, big), axis=0, keepdims=True)
    e_row = row - N_GROUPS
    emask = (e_row >= 0) & (e_row < N_EXPERTS) & ((e_row >> 3) == g_idx)
    fl = jnp.where(emask, lt, neg)
    fmax = jnp.max(fl, axis=0, keepdims=True)
    fexp = jnp.exp(fl - fmax)
    fsum = jnp.sum(fexp, axis=0, keepdims=True)
    pe = jnp.where(emask, fexp / fsum, -1.0)
    p1 = jnp.max(pe, axis=0, keepdims=True)
    i1 = jnp.min(jnp.where(pe == p1, row, big), axis=0, keepdims=True)
    pe2 = jnp.where(row == i1, -1.0, pe)
    p2 = jnp.max(pe2, axis=0, keepdims=True)
    i2 = jnp.min(jnp.where(pe2 == p2, row, big), axis=0, keepdims=True)
    psum = p1 + p2
    gate1 = g_w * p1 / psum
    gate2 = g_w * p2 / psum
    oh1 = row == i1
    oh2 = row == i2
    c = jnp.where(oh1 | oh2, 1.0, 0.0)
    tt = lax.broadcasted_iota(jnp.int32, (n, n), 0)
    tc = lax.broadcasted_iota(jnp.int32, (n, n), 1)
    upper = jnp.where(tt < tc, 1.0, 0.0).astype(_BF16)
    before = _dot(c.astype(_BF16), upper) + cnt
    rank1 = jnp.sum(jnp.where(oh1, before, 0.0), axis=0, keepdims=True)
    rank2 = jnp.sum(jnp.where(oh2, before, 0.0), axis=0, keepdims=True)
    cnt = cnt + jnp.sum(c, axis=1, keepdims=True)
    r8 = lax.broadcasted_iota(jnp.int32, (ROUTE_OUT, n), 0)
    vals = [(i1 - N_GROUPS).astype(_F32), (i2 - N_GROUPS).astype(_F32), gate1, gate2, rank1, rank2]
    route = jnp.zeros((ROUTE_OUT, n), _F32)
    for k, v in enumerate(vals):
        route = jnp.where(r8 == k, v, route)
    return x1, route, cnt


def _prompt_mixer_kernel(sink_ref, cdec_ref,
                         x_ref, nmix_ref, win_ref, bias_ref, cos_ref, sin_ref, dec_ref,
                         qd_ref, kd_ref, wba_ref, wbr_ref, wout_ref, nffn_ref,
                         wrh_ref, wrl_ref, br_ref,
                         x1_ref, route_ref, knew_ref, vnew_ref, s_ref, cnt_ref,
                         proj, kctx, vctx, attn, ret):
    i = pl.program_id(1)
    last = pl.num_programs(1) - 1

    @pl.when((i == 0) & (pl.program_id(0) == 0))
    def _():
        cnt_ref[...] = jnp.zeros_like(cnt_ref)

    tm = x_ref.shape[1]
    n_sub = tm // WINDOW
    scale = HEAD_DIM ** -0.5

    @pl.when(i == 0)
    def _():
        s_ref[...] = jnp.zeros_like(s_ref)
        kctx[0:WINDOW, :] = jnp.zeros((WINDOW, KV_W), _BF16)
        vctx[0:WINDOW, :] = jnp.zeros((WINDOW, KV_W), _BF16)

    x = x_ref[0]
    xn = _rms(x, nmix_ref[...]).astype(_BF16)
    n_in = win_ref.shape[1]
    panel = 512
    for c0 in range(0, n_in, panel):
        proj[:, c0:c0 + panel] = _dot(xn, win_ref[:, c0:c0 + panel])

    @pl.when(i == last)
    def _():
        knew_ref[0] = proj[tm - WINDOW:tm, OFF_KA:OFF_KA + KV_W]
        vnew_ref[0] = proj[tm - WINDOW:tm, OFF_VA:OFF_VA + KV_W]

    col = lax.broadcasted_iota(jnp.int32, (1, 2 * WINDOW), 1)
    for c in range(n_sub):
        r0 = c * WINDOW
        kctx[WINDOW:2 * WINDOW, :] = proj[r0:r0 + WINDOW, OFF_KA:OFF_KA + KV_W].astype(_BF16)
        vctx[WINDOW:2 * WINDOW, :] = proj[r0:r0 + WINDOW, OFF_VA:OFF_VA + KV_W].astype(_BF16)
        if c == 0:
            pen = jnp.where((col < WINDOW) & (i == 0), NEG_INF, 0.0).astype(_F32)
        for h in range(KV_HEADS):
            k_h = kctx[:, h * HEAD_DIM:(h + 1) * HEAD_DIM]
            v_h = vctx[:, h * HEAD_DIM:(h + 1) * HEAD_DIM]
            for g in range(GQA_GROUP):
                hq = h * GQA_GROUP + g
                q = (proj[r0:r0 + WINDOW, hq * HEAD_DIM:(hq + 1) * HEAD_DIM] * scale).astype(_BF16)
                s = _dot_nt(q, k_h) + bias_ref[hq]
                if c == 0:
                    s = s + pen
                snk = sink_ref[hq]
                m = jnp.maximum(jnp.max(s, axis=-1, keepdims=True), snk)
                p = jnp.exp(s - m)
                den = jnp.sum(p, axis=-1, keepdims=True) + jnp.exp(snk - m)
                p = p * (1.0 / den)
                attn[r0:r0 + WINDOW, hq * HEAD_DIM:(hq + 1) * HEAD_DIM] = _dot(p.astype(_BF16), v_h)
        kctx[0:WINDOW, :] = kctx[WINDOW:2 * WINDOW, :]
        vctx[0:WINDOW, :] = vctx[WINDOW:2 * WINDOW, :]

    cosf = cos_ref[...]
    sinf = sin_ref[...]
    for h in range(RET_HEADS):
        qrot = _rotary(proj[:, OFF_QR + h * RET_DK:OFF_QR + (h + 1) * RET_DK], cosf, sinf)
        krot = _rotary(proj[:, OFF_KR + h * RET_DK:OFF_KR + (h + 1) * RET_DK], cosf, sinf) * (RET_DK ** -0.5)
        qd = qd_ref[:, h * RET_DK:(h + 1) * RET_DK]
        kd = kd_ref[:, h * RET_DK:(h + 1) * RET_DK]
        for c in range(n_sub):
            r0 = c * RET_CHUNK
            qc = qrot[r0:r0 + RET_CHUNK]
            kc = krot[r0:r0 + RET_CHUNK]
            vc = proj[r0:r0 + RET_CHUNK, OFF_VR + h * RET_DV:OFF_VR + (h + 1) * RET_DV].astype(_BF16)
            sc = _dot_nt(qc.astype(_BF16), kc.astype(_BF16)) * dec_ref[h]
            s_old = s_ref[0, h]
            o = _dot(sc.astype(_BF16), vc) + _dot((qc * qd).astype(_BF16), s_old.astype(_BF16))
            s_ref[0, h] = s_old * cdec_ref[h] + _dot_tn((kc * kd).astype(_BF16), vc)
            o = o * lax.rsqrt(jnp.mean(o * o, axis=-1, keepdims=True) + NORM_EPS)
            gr = proj[r0:r0 + RET_CHUNK, OFF_GR + h * RET_DV:OFF_GR + (h + 1) * RET_DV]
            ret[r0:r0 + RET_CHUNK, h * RET_DV:(h + 1) * RET_DV] = o * (gr * jax.nn.sigmoid(gr))

    x1, route, cnt = _post(x, attn[...], ret[...],
                           proj[:, OFF_GA:OFF_GA + 1024], proj[:, OFF_GT:OFF_GT + 1024],
                           wba_ref[...], wbr_ref[...], wout_ref[...], nffn_ref[...],
                           wrh_ref[...], wrl_ref[...], br_ref[...], cnt_ref[:, 0:1])
    _store_rows(x1_ref, x1)
    route_ref[...] = route
    cnt_ref[...] = jnp.broadcast_to(cnt, cnt_ref.shape)


def _prompt_mixer(x, cst, w):
    B, L, D = x.shape
    tm = min(PROMPT_TM, L)
    nb = L // tm
    n_in = w['win'].shape[1]
    step = lambda b, i, *_: (b, i, 0)
    per_b = lambda b, i, *_: (b, 0, 0)
    grid_spec = pltpu.PrefetchScalarGridSpec(
        num_scalar_prefetch=2,
        grid=(B, nb),
        in_specs=[
            pl.BlockSpec((1, tm, D), step),
            _const_spec((1, D)),
            _const_spec((D, n_in)),
            _const_spec((ATTN_HEADS, WINDOW, 2 * WINDOW)),
            pl.BlockSpec((tm, RET_DK), lambda b, i, *_: (i, 0)),
            pl.BlockSpec((tm, RET_DK), lambda b, i, *_: (i, 0)),
            _const_spec((RET_HEADS, RET_CHUNK, RET_CHUNK)),
            _const_spec((RET_CHUNK, RQ_W)),
            _const_spec((RET_CHUNK, RQ_W)),
            _const_spec((ATTN_W, D)),
            _const_spec((RV_W, D)),
            _const_spec((D, D)),
            _const_spec((1, D)),
            _const_spec((D, ROUTE_LANES)),
            _const_spec((D, ROUTE_LANES)),
            _const_spec((1, ROUTE_LANES)),
        ],
        out_specs=[
            pl.BlockSpec((tm * SUB, LANE), lambda b, i, *_: (b * nb + i, 0)),
            pl.BlockSpec((ROUTE_OUT, tm), lambda b, i, *_: (0, b * nb + i)),
            pl.BlockSpec((1, WINDOW, KV_W), per_b),
            pl.BlockSpec((1, WINDOW, KV_W), per_b),
            pl.BlockSpec((1, RET_HEADS, RET_DK, RET_DV), lambda b, i, *_: (b, 0, 0, 0)),
            pl.BlockSpec((ROUTE_ROWS, LANE), lambda b, i, *_: (0, 0)),
        ],
        scratch_shapes=[
            pltpu.VMEM((tm, n_in), _F32),
            pltpu.VMEM((2 * WINDOW, KV_W), _BF16),
            pltpu.VMEM((2 * WINDOW, KV_W), _BF16),
            pltpu.VMEM((tm, ATTN_W), _F32),
            pltpu.VMEM((tm, RV_W), _F32),
        ],
    )
    assert D == SUB * LANE
    out_shape = [
        jax.ShapeDtypeStruct((B * L * SUB, LANE), _F32),
        jax.ShapeDtypeStruct((ROUTE_OUT, B * L), _F32),
        jax.ShapeDtypeStruct((B, WINDOW, KV_W), _F32),
        jax.ShapeDtypeStruct((B, WINDOW, KV_W), _F32),
        jax.ShapeDtypeStruct((B, RET_HEADS, RET_DK, RET_DV), _F32),
        jax.ShapeDtypeStruct((ROUTE_ROWS, LANE), _F32),
    ]
    return pl.pallas_call(
        _prompt_mixer_kernel,
        grid_spec=grid_spec,
        out_shape=out_shape,
        compiler_params=pltpu.CompilerParams(
            dimension_semantics=("arbitrary", "arbitrary"), vmem_limit_bytes=VMEM_LIMIT),
        name="prompt_mixer",
    )(cst['sink'], cst['cdec_p'],
      x, w['nmix'], w['win'], cst['bias_p'], cst['cos_p'], cst['sin_p'], cst['dec_p'],
      cst['qd_p'], cst['kd_p'], w['wba'], w['wbr'], w['wout'], w['nffn'],
      w['wrh'], w['wrl'], w['br'])


def _inproj_kernel(x_ref, nmix_ref, win_ref, o_ref):
    xn = _rms(x_ref[...], nmix_ref[...]).astype(_BF16)
    o_ref[...] = _dot(xn, win_ref[...])


def _sample_inproj(x2d, w):
    T, D = x2d.shape
    n_in = w['win'].shape[1]
    panel = 512
    return pl.pallas_call(
        _inproj_kernel,
        grid=(n_in // panel,),
        in_specs=[pl.BlockSpec((T, D), lambda j: (0, 0)),
                  pl.BlockSpec((1, D), lambda j: (0, 0)),
                  pl.BlockSpec((D, panel), lambda j: (0, j))],
        out_specs=pl.BlockSpec((T, panel), lambda j: (0, j)),
        out_shape=jax.ShapeDtypeStruct((T, n_in), _F32),
        compiler_params=pltpu.CompilerParams(
            dimension_semantics=("arbitrary",), vmem_limit_bytes=VMEM_LIMIT),
        name="sample_inproj",
    )(x2d, w['nmix'], w['win'])


def _sample_core_kernel(cdec_ref, proj_ref, ck_ref, cv_ref, st_ref, bh_ref, bn_ref, snk_ref,
                        cos_ref, sin_ref, dec_ref, qd_ref, kd_ref,
                        attn_ref, ret_ref, nk_ref, nv_ref, ns_ref):
    G = proj_ref.shape[0]
    ls = proj_ref.shape[1]
    scale = HEAD_DIM ** -0.5
    cosf = cos_ref[...]
    sinf = sin_ref[...]

    def body(b, carry):
        row = proj_ref[b]
        k_new = row[:, OFF_KA:OFF_KA + KV_W]
        v_new = row[:, OFF_VA:OFF_VA + KV_W]
        ck = ck_ref[b]
        cv = cv_ref[b]
        nk_ref[b, 0:WINDOW - ls, :] = ck[ls:WINDOW, :]
        nk_ref[b, WINDOW - ls:WINDOW, :] = k_new
        nv_ref[b, 0:WINDOW - ls, :] = cv[ls:WINDOW, :]
        nv_ref[b, WINDOW - ls:WINDOW, :] = v_new
        ckb = ck.astype(_BF16)
        cvb = cv.astype(_BF16)
        knb = k_new.astype(_BF16)
        vnb = v_new.astype(_BF16)
        heads = [slice(h * HEAD_DIM, (h + 1) * HEAD_DIM) for h in range(KV_HEADS)]
        s1s, s2s = [], []
        for h in range(KV_HEADS):
            q4 = (jnp.concatenate(
                [row[:, (h * GQA_GROUP + g) * HEAD_DIM:(h * GQA_GROUP + g + 1) * HEAD_DIM]
                 for g in range(GQA_GROUP)], axis=0) * scale).astype(_BF16)
            s1s.append(_dot_nt(q4, ckb[:, heads[h]]))
            s2s.append(_dot_nt(q4, knb[:, heads[h]]))
        scs, crosses, vcs = [], [], []
        for h in range(RET_HEADS):
            qrot = _rotary(row[:, OFF_QR + h * RET_DK:OFF_QR + (h + 1) * RET_DK], cosf, sinf)
            krot = _rotary(row[:, OFF_KR + h * RET_DK:OFF_KR + (h + 1) * RET_DK], cosf, sinf) * (RET_DK ** -0.5)
            vc = row[:, OFF_VR + h * RET_DV:OFF_VR + (h + 1) * RET_DV].astype(_BF16)
            qd = qd_ref[:, h * RET_DK:(h + 1) * RET_DK]
            kd = kd_ref[:, h * RET_DK:(h + 1) * RET_DK]
            s_old = st_ref[b, h]
            scs.append(_dot_nt(qrot.astype(_BF16), krot.astype(_BF16)))
            crosses.append(_dot((qrot * qd).astype(_BF16), s_old.astype(_BF16)))
            ns_ref[b, h] = s_old * cdec_ref[h] + _dot_tn((krot * kd).astype(_BF16), vc)
            vcs.append(vc)
        p1s, p2s = [], []
        for h in range(KV_HEADS):
            s1 = s1s[h] + bh_ref[h]
            s2 = s2s[h] + bn_ref[h]
            snk = snk_ref[h]
            m = jnp.maximum(jnp.maximum(jnp.max(s1, axis=-1, keepdims=True),
                                        jnp.max(s2, axis=-1, keepdims=True)), snk)
            p1 = jnp.exp(s1 - m)
            p2 = jnp.exp(s2 - m)
            den = (jnp.sum(p1, axis=-1, keepdims=True) + jnp.sum(p2, axis=-1, keepdims=True)
                   + jnp.exp(snk - m))
            r = 1.0 / den
            p1s.append((p1 * r).astype(_BF16))
            p2s.append((p2 * r).astype(_BF16))
        scb = [(scs[h] * dec_ref[h]).astype(_BF16) for h in range(RET_HEADS)]
        outs = [_dot(p1s[h], cvb[:, heads[h]]) + _dot(p2s[h], vnb[:, heads[h]])
                for h in range(KV_HEADS)]
        rets = [_dot(scb[h], vcs[h]) + crosses[h] for h in range(RET_HEADS)]
        for h in range(KV_HEADS):
            for g in range(GQA_GROUP):
                hq = h * GQA_GROUP + g
                attn_ref[b, :, hq * HEAD_DIM:(hq + 1) * HEAD_DIM] = outs[h][g * ls:(g + 1) * ls]
        for h in range(RET_HEADS):
            o = rets[h]
            o = o * lax.rsqrt(jnp.mean(o * o, axis=-1, keepdims=True) + NORM_EPS)
            gr = row[:, OFF_GR + h * RET_DV:OFF_GR + (h + 1) * RET_DV]
            ret_ref[b, :, h * RET_DV:(h + 1) * RET_DV] = o * (gr * jax.nn.sigmoid(gr))
        return carry

    lax.fori_loop(0, G, body, 0, unroll=SAMPLE_UNROLL)


def _sample_core(proj3, ck, cv, st, cst):
    NB, ls, n_in = proj3.shape
    G = min(SAMPLE_GROUP, NB)
    blk3 = lambda i, *_: (i, 0, 0)
    blk4 = lambda i, *_: (i, 0, 0, 0)
    c2 = lambda i, *_: (0, 0)
    c3 = lambda i, *_: (0, 0, 0)
    ql = GQA_GROUP * ls
    grid_spec = pltpu.PrefetchScalarGridSpec(
        num_scalar_prefetch=1,
        grid=(NB // G,),
        in_specs=[
            pl.BlockSpec((G, ls, n_in), blk3),
            pl.BlockSpec((G, WINDOW, KV_W), blk3),
            pl.BlockSpec((G, WINDOW, KV_W), blk3),
            pl.BlockSpec((G, RET_HEADS, RET_DK, RET_DV), blk4),
            pl.BlockSpec((KV_HEADS, ql, WINDOW), c3),
            pl.BlockSpec((KV_HEADS, ql, ls), c3),
            pl.BlockSpec((KV_HEADS, ql, 1), c3),
            pl.BlockSpec((ls, RET_DK), c2),
            pl.BlockSpec((ls, RET_DK), c2),
            pl.BlockSpec((RET_HEADS, ls, ls), c3),
            pl.BlockSpec((ls, RQ_W), c2),
            pl.BlockSpec((ls, RQ_W), c2),
        ],
        out_specs=[
            pl.BlockSpec((G, ls, ATTN_W), blk3),
            pl.BlockSpec((G, ls, RV_W), blk3),
            pl.BlockSpec((G, WINDOW, KV_W), blk3),
            pl.BlockSpec((G, WINDOW, KV_W), blk3),
            pl.BlockSpec((G, RET_HEADS, RET_DK, RET_DV), blk4),
        ],
    )
    out_shape = [
        jax.ShapeDtypeStruct((NB, ls, ATTN_W), _F32),
        jax.ShapeDtypeStruct((NB, ls, RV_W), _F32),
        jax.ShapeDtypeStruct((NB, WINDOW, KV_W), _F32),
        jax.ShapeDtypeStruct((NB, WINDOW, KV_W), _F32),
        jax.ShapeDtypeStruct((NB, RET_HEADS, RET_DK, RET_DV), _F32),
    ]
    return pl.pallas_call(
        _sample_core_kernel,
        grid_spec=grid_spec,
        out_shape=out_shape,
        compiler_params=pltpu.CompilerParams(
            dimension_semantics=("arbitrary",), vmem_limit_bytes=VMEM_LIMIT),
        name="sample_core",
    )(cst['cdec_s'], proj3, ck, cv, st, cst['bias_hist'], cst['bias_new'], cst['sink_col'],
      cst['cos_s'], cst['sin_s'], cst['dec_s'], cst['qd_s'], cst['kd_s'])


def _sample_post_kernel(x_ref, attn_ref, ret_ref, ga_ref, gt_ref, wba_ref, wbr_ref, wout_ref,
                        nffn_ref, wrh_ref, wrl_ref, br_ref, cnt0_ref, x1_ref, route_ref, cnt_ref):
    x1, route, cnt = _post(x_ref[...], attn_ref[...], ret_ref[...], ga_ref[...], gt_ref[...],
                           wba_ref[...], wbr_ref[...], wout_ref[...], nffn_ref[...],
                           wrh_ref[...], wrl_ref[...], br_ref[...], cnt0_ref[:, 0:1])
    _store_rows(x1_ref, x1)
    route_ref[...] = route
    cnt_ref[...] = jnp.broadcast_to(cnt, cnt_ref.shape)


def _sample_post(x2d, attn, ret, ga, gt, w, cnt0):
    T, D = x2d.shape
    full = lambda s: pl.BlockSpec(s, lambda i: (0,) * len(s))
    return pl.pallas_call(
        _sample_post_kernel,
        grid=(1,),
        in_specs=[full((T, D)), full((T, ATTN_W)), full((T, RV_W)), full((T, D)), full((T, D)),
                  full((ATTN_W, D)), full((RV_W, D)), full((D, D)), full((1, D)),
                  full((D, ROUTE_LANES)), full((D, ROUTE_LANES)), full((1, ROUTE_LANES)),
                  full((ROUTE_ROWS, LANE))],
        out_specs=[full((T * SUB, LANE)), full((ROUTE_OUT, T)), full((ROUTE_ROWS, LANE))],
        out_shape=[jax.ShapeDtypeStruct((T * SUB, LANE), _F32),
                   jax.ShapeDtypeStruct((ROUTE_OUT, T), _F32),
                   jax.ShapeDtypeStruct((ROUTE_ROWS, LANE), _F32)],
        compiler_params=pltpu.CompilerParams(
            dimension_semantics=("arbitrary",), vmem_limit_bytes=VMEM_LIMIT),
        name="sample_post",
    )(x2d, attn, ret, ga, gt, w['wba'], w['wbr'], w['wout'], w['nffn'], w['wrh'], w['wrl'], w['br'],
      cnt0)


def _dispatch_kernel(dest_ref, xp_ref, xq_ref, xs_ref, sem, *, p_steps):
    i = pl.program_id(0)

    def copy_tile(src):
        def tile_copy(r, d):
            return pltpu.make_async_copy(src.at[r], xs_ref.at[d], sem)

        def start(r, c):
            t = i * ROW_TM + r
            tile_copy(r, dest_ref[2 * t]).start(priority=0)
            tile_copy(r, dest_ref[2 * t + 1]).start(priority=1)
            return c

        lax.fori_loop(0, ROW_TM, start, 0, unroll=DMA_UNROLL)

        def wait(r, c):
            tile_copy(0, 0).wait()
            tile_copy(0, 0).wait()
            return c

        lax.fori_loop(0, ROW_TM, wait, 0, unroll=DMA_UNROLL)

    @pl.when(i < p_steps)
    def _():
        copy_tile(xp_ref)

    @pl.when(i >= p_steps)
    def _():
        copy_tile(xq_ref)


def _dispatch(dest, xp3, xq3):
    Tp, Tq = xp3.shape[0], xq3.shape[0]
    assert Tp % ROW_TM == 0 and Tq % ROW_TM == 0
    p_steps = Tp // ROW_TM
    grid_spec = pltpu.PrefetchScalarGridSpec(
        num_scalar_prefetch=1,
        grid=((Tp + Tq) // ROW_TM,),
        in_specs=[
            pl.BlockSpec((ROW_TM, SUB, LANE), lambda i, *_: (jnp.minimum(i, p_steps - 1), 0, 0)),
            pl.BlockSpec((ROW_TM, SUB, LANE), lambda i, *_: (jnp.maximum(i - p_steps, 0), 0, 0)),
        ],
        out_specs=pl.BlockSpec(memory_space=pl.ANY),
        scratch_shapes=[pltpu.SemaphoreType.DMA],
    )
    return pl.pallas_call(
        functools.partial(_dispatch_kernel, p_steps=p_steps),
        grid_spec=grid_spec,
        out_shape=jax.ShapeDtypeStruct((2 * (Tp + Tq), SUB, LANE), _F32),
        compiler_params=pltpu.CompilerParams(dimension_semantics=("arbitrary",)),
        name="moe_dispatch",
    )(dest, xp3, xq3)


def _gmm_kernel(tile_ref, exp_ref, lo_ref, hi_ref, first_ref, chg_ref,
                x_ref, nffn_ref, wg_ref, wu_ref, wd_ref, y_ref, wg_s, wu_s, wd_s):
    m = pl.program_id(0)
    tm = x_ref.shape[0] // SUB

    @pl.when(chg_ref[m] == 1)
    def _():
        wg_s[...] = wg_ref[0].astype(_BF16)
        wu_s[...] = wu_ref[0].astype(_BF16)
        wd_s[...] = wd_ref[0].astype(_BF16)

    lo = lo_ref[m]
    hi = hi_ref[m]

    @pl.when(hi > lo)
    def _():
        rows = tile_ref[m] * tm + lax.broadcasted_iota(jnp.int32, (tm, 1), 0)
        mine = (rows >= lo) & (rows < hi)
        xn = _rms(_load_rows(x_ref, tm), nffn_ref[...])
        x = jnp.where(mine, xn, 0.0).astype(_BF16)
        a = _dot(x, wg_s[...])
        hmid = (a * jax.nn.sigmoid(a)) * _dot(x, wu_s[...])
        y = _dot(hmid.astype(_BF16), wd_s[...])

        @pl.when(first_ref[m] == 1)
        def _():
            _store_rows(y_ref, y)

        @pl.when(first_ref[m] == 0)
        def _():
            _store_rows(y_ref, _load_rows(y_ref, tm) + y)


def _gmm(work, xs2, nffn, wg, wu, wd):
    A = xs2.shape[0] // SUB
    E, D, F = wg.shape
    n_work = work[0].shape[0]
    grid_spec = pltpu.PrefetchScalarGridSpec(
        num_scalar_prefetch=6,
        grid=(n_work,),
        in_specs=[
            pl.BlockSpec((MOE_TM * SUB, LANE), lambda m, t, e, *_: (t[m], 0)),
            pl.BlockSpec((1, D), lambda m, t, e, *_: (0, 0)),
            pl.BlockSpec((1, D, F), lambda m, t, e, *_: (e[m], 0, 0)),
            pl.BlockSpec((1, D, F), lambda m, t, e, *_: (e[m], 0, 0)),
            pl.BlockSpec((1, F, D), lambda m, t, e, *_: (e[m], 0, 0)),
        ],
        out_specs=pl.BlockSpec((MOE_TM * SUB, LANE), lambda m, t, e, *_: (t[m], 0)),
        scratch_shapes=[pltpu.VMEM((D, F), _BF16), pltpu.VMEM((D, F), _BF16),
                        pltpu.VMEM((F, D), _BF16)],
    )
    return pl.pallas_call(
        _gmm_kernel,
        grid_spec=grid_spec,
        out_shape=jax.ShapeDtypeStruct((A * SUB, LANE), _F32),
        compiler_params=pltpu.CompilerParams(
            dimension_semantics=("arbitrary",), vmem_limit_bytes=VMEM_LIMIT),
        name="moe_gmm",
    )(*work, xs2, nffn, wg, wu, wd)


def _combine_kernel(dest_ref, x1_ref, route_ref, nfin_ref, yb_ref, o_ref, buf, sems, *, tok0):
    i = pl.program_id(0)
    n_steps = pl.num_programs(0)
    tm = o_ref.shape[0]

    def tile_copy(d, slot, k, r):
        rows = pl.ds(pl.multiple_of(r * SUB, SUB), SUB)
        return pltpu.make_async_copy(yb_ref.at[d], buf.at[slot, k, rows], sems.at[slot])

    def issue(step, slot):
        base = tok0 + step * tm

        def start(r, c):
            t = base + r
            tile_copy(dest_ref[2 * t], slot, 0, r).start(priority=0)
            tile_copy(dest_ref[2 * t + 1], slot, 1, r).start(priority=1)
            return c

        lax.fori_loop(0, tm, start, 0, unroll=DMA_UNROLL)

    @pl.when(i == 0)
    def _():
        issue(0, 0)

    @pl.when(i + 1 < n_steps)
    def _():
        issue(i + 1, (i + 1) % 2)

    slot = i % 2

    def wait(r, c):
        tile_copy(0, slot, 0, 0).wait()
        tile_copy(0, slot, 1, 0).wait()
        return c

    lax.fori_loop(0, tm, wait, 0, unroll=DMA_UNROLL)
    rt = jnp.concatenate([route_ref[...], jnp.zeros((LANE - ROUTE_OUT, tm), _F32)], axis=0).T
    g0 = rt[:, 2:3]
    g1 = rt[:, 3:4]
    y0 = _load_rows(buf.at[slot, 0], tm)
    y1 = _load_rows(buf.at[slot, 1], tm)
    y = _load_rows(x1_ref, tm) + (y0 * g0 + y1 * g1)
    o_ref[...] = _rms(y, nfin_ref[...])


def _combine(dest, x1_2, route, nfin, yb3, tok0):
    T = x1_2.shape[0] // SUB
    D = SUB * LANE
    tm = min(ROW_TM, T)
    grid_spec = pltpu.PrefetchScalarGridSpec(
        num_scalar_prefetch=1,
        grid=(T // tm,),
        in_specs=[
            pl.BlockSpec((tm * SUB, LANE), lambda i, *_: (i, 0)),
            pl.BlockSpec((ROUTE_OUT, tm), lambda i, *_: (0, i)),
            pl.BlockSpec((1, D), lambda i, *_: (0, 0)),
            pl.BlockSpec(memory_space=pl.ANY),
        ],
        out_specs=pl.BlockSpec((tm, D), lambda i, *_: (i, 0)),
        scratch_shapes=[pltpu.VMEM((2, 2, tm * SUB, LANE), _F32), pltpu.SemaphoreType.DMA((2,))],
    )
    return pl.pallas_call(
        functools.partial(_combine_kernel, tok0=tok0),
        grid_spec=grid_spec,
        out_shape=jax.ShapeDtypeStruct((T, D), _F32),
        compiler_params=pltpu.CompilerParams(
            dimension_semantics=("arbitrary",), vmem_limit_bytes=VMEM_LIMIT),
        name="moe_combine",
    )(dest, x1_2, route, nfin, yb3)


def _routing_tables(route, counts, n_tiles):
    experts = route[0:2].astype(jnp.int32).T.reshape(-1)
    ranks = route[4:6].astype(jnp.int32).T.reshape(-1)
    A = experts.shape[0]
    ids = jnp.arange(N_EXPERTS, dtype=jnp.int32)
    ends = jnp.cumsum(counts)
    starts = ends - counts
    dest = ranks + jnp.sum(jnp.where(experts[:, None] == ids[None, :], starts[None, :], 0), axis=1)
    tile_starts = jnp.arange(n_tiles, dtype=jnp.int32) * MOE_TM
    pos_t = jnp.arange(n_tiles, dtype=jnp.int32) + jnp.sum(starts[None, :] < tile_starts[:, None], axis=1)
    pos_e = ids + jnp.sum(tile_starts[None, :] <= starts[:, None], axis=1)
    slots = jnp.arange(n_tiles + N_EXPERTS, dtype=jnp.int32)
    pts = (jnp.sum(jnp.where(pos_t[None, :] == slots[:, None], tile_starts[None, :], 0), axis=1)
           + jnp.sum(jnp.where(pos_e[None, :] == slots[:, None], starts[None, :], 0), axis=1))
    lo = pts.astype(jnp.int32)
    hi = jnp.concatenate([lo[1:], jnp.array([A], jnp.int32)])
    tile = jnp.minimum(lo // MOE_TM, n_tiles - 1).astype(jnp.int32)
    expert = jnp.minimum(jnp.sum(ends[None, :] <= lo[:, None], axis=1), N_EXPERTS - 1).astype(jnp.int32)
    nonempty = hi > lo
    first = (nonempty & (lo % MOE_TM == 0)).astype(jnp.int32)
    chg = jnp.concatenate([jnp.ones((1,), jnp.int32), (expert[1:] != expert[:-1]).astype(jnp.int32)])
    return dest.astype(jnp.int32), (tile, expert, lo, hi, first, chg)


def _bucket_table(lq, lk):
    dist = np.arange(lq)[:, None] + WINDOW - np.arange(lk)[None, :]
    band = (dist >= 0) & (dist < WINDOW)
    d = np.clip(dist, 0, WINDOW - 1)
    max_exact = N_BUCKETS // 2
    d_f = np.maximum(d, 1).astype(np.float32)
    large = max_exact + (np.log(d_f / max_exact) / math.log(MAX_DISTANCE / max_exact)
                         * (N_BUCKETS - max_exact)).astype(np.int32)
    large = np.minimum(large, N_BUCKETS - 1)
    return np.where(d < max_exact, d, large).astype(np.int32), band


def _bias_table(rb, lq, lk):
    bkt, band = _bucket_table(lq, lk)
    onehot = jnp.asarray(bkt)[None, :, :] == jnp.arange(N_BUCKETS, dtype=jnp.int32)[:, None, None]
    bias = jnp.sum(jnp.where(onehot[:, None], rb[:, :, None, None], 0.0), axis=0)
    return jnp.where(jnp.asarray(band)[None], bias, NEG_INF)


def _decay_tables(C):
    log_gamma = jnp.log(1.0 - 2.0 ** (-5.0 - jnp.arange(RET_HEADS, dtype=_F32)))
    idx = jnp.arange(C, dtype=_F32)
    diff = idx[:, None] - idx[None, :]
    decay_in = jnp.where((diff >= 0)[..., None],
                         jnp.exp(jnp.maximum(diff, 0.0)[..., None] * log_gamma), 0.0)
    q_dec = jnp.exp((idx + 1.0)[:, None] * log_gamma)
    k_dec = jnp.exp((C - 1.0 - idx)[:, None] * log_gamma)
    c_dec = jnp.exp(C * log_gamma)
    dec = jnp.transpose(decay_in, (2, 0, 1))
    qd = jnp.repeat(q_dec, RET_DK, axis=1)
    kd = jnp.repeat(k_dec, RET_DK, axis=1)
    return dec, qd, kd, c_dec


def _rope_tables(pos):
    half = RET_DK // 2
    inv = ROPE_BASE ** (-jnp.arange(half, dtype=_F32) * 2.0 / RET_DK)
    ang = pos.astype(_F32)[:, None] * inv[None, :]
    cos = jnp.cos(ang)
    sin = jnp.sin(ang)
    return jnp.concatenate([cos, cos], axis=1), jnp.concatenate([-sin, sin], axis=1)


def _constants(rel_bias, attn_sink, L, ls):
    cst = {}
    rb = rel_bias.astype(_F32)
    cst['bias_p'] = _bias_table(rb, WINDOW, 2 * WINDOW)
    cst['sink'] = attn_sink.astype(_F32)
    cst['cos_p'], cst['sin_p'] = _rope_tables(jnp.arange(L))
    cst['dec_p'], cst['qd_p'], cst['kd_p'], cst['cdec_p'] = _decay_tables(min(RET_CHUNK, L))
    bias = _bias_table(rb, ls, WINDOW + ls).reshape(KV_HEADS, GQA_GROUP * ls, WINDOW + ls)
    cst['bias_hist'] = bias[:, :, :WINDOW]
    cst['bias_new'] = bias[:, :, WINDOW:]
    cst['sink_col'] = jnp.repeat(attn_sink.astype(_F32).reshape(KV_HEADS, GQA_GROUP), ls,
                                 axis=1)[..., None]
    cst['cos_s'], cst['sin_s'] = _rope_tables(PAST_LEN + jnp.arange(ls))
    cst['dec_s'], cst['qd_s'], cst['kd_s'], cst['cdec_s'] = _decay_tables(min(RET_CHUNK, ls))
    return cst


def _layer_weights(layer, norm_mix, w_in, w_branch_attn, w_branch_ret, w_out, norm_ffn,
                   w_router_group, b_router_group, w_router_expert, b_router_expert):
    D = w_in.shape[1]
    wr = jnp.concatenate([w_router_group[layer].astype(_F32), w_router_expert[layer].astype(_F32)], axis=1)
    wr = jnp.pad(wr, ((0, 0), (0, ROUTE_LANES - wr.shape[1])))
    wrh = wr.astype(_BF16)
    wrl = (wr - wrh.astype(_F32)).astype(_BF16)
    br = jnp.concatenate([b_router_group[layer].astype(_F32), b_router_expert[layer].astype(_F32)])
    br = jnp.pad(br, (0, ROUTE_LANES - br.shape[0]))[None, :]
    return {
        'nmix': norm_mix[layer].astype(_F32)[None, :],
        'win': w_in[layer].astype(_BF16),
        'wba': w_branch_attn[layer].astype(_BF16),
        'wbr': w_branch_ret[layer].astype(_BF16),
        'wout': w_out[layer].astype(_BF16),
        'nffn': norm_ffn[layer].astype(_F32)[None, :],
        'wrh': wrh, 'wrl': wrl, 'br': br,
    }


def kernel(x_prompt, x_sample, cache_k, cache_v, state_ret, norm_mix, w_in, attn_sink, rel_bias,
           w_branch_attn, w_branch_ret, w_out, norm_ffn, w_router_group, b_router_group,
           w_router_expert, b_router_expert, w_gate, w_up, w_down, norm_final):
    depth = w_in.shape[0]
    assert depth == 1, "the final norm is fused into the MoE combine of the only layer"
    B, L, D = x_prompt.shape
    NB, ls, _ = x_sample.shape
    Tp, Ts = B * L, NB * ls
    nfin = norm_final.astype(_F32)[None, :]
    yp, ys = x_prompt, x_sample
    pk, pv, ps, sk, sv, ss = [], [], [], [], [], []
    for layer in range(depth):
        w = _layer_weights(layer, norm_mix, w_in, w_branch_attn, w_branch_ret, w_out, norm_ffn,
                           w_router_group, b_router_group, w_router_expert, b_router_expert)
        cst = _constants(rel_bias, attn_sink[layer], L, ls)
        x1p, routep, k1, v1, s1, cnt_p = _prompt_mixer(yp, cst, w)
        ys2 = ys.reshape(Ts, D)
        proj = _sample_inproj(ys2, w)
        attn_s, ret_s, k2, v2, s2 = _sample_core(
            proj.reshape(NB, ls, -1),
            cache_k[layer].reshape(NB, WINDOW, KV_W), cache_v[layer].reshape(NB, WINDOW, KV_W),
            state_ret[layer], cst)
        x1s, routes, cnt_all = _sample_post(
            ys2, attn_s.reshape(Ts, ATTN_W), ret_s.reshape(Ts, RV_W),
            proj[:, OFF_GA:OFF_GA + D], proj[:, OFF_GT:OFF_GT + D], w, cnt_p)
        n_rows = 2 * (Tp + Ts)
        assert n_rows % MOE_TM == 0
        counts = cnt_all[N_GROUPS:N_GROUPS + N_EXPERTS, 0].astype(jnp.int32)
        dest, work = _routing_tables(jnp.concatenate([routep, routes], axis=1), counts,
                                     n_rows // MOE_TM)
        xs3 = _dispatch(dest, x1p.reshape(Tp, SUB, LANE), x1s.reshape(Ts, SUB, LANE))
        yb2 = _gmm(work, xs3.reshape(n_rows * SUB, LANE), w['nffn'],
                   w_gate[layer], w_up[layer], w_down[layer])
        yb3 = yb2.reshape(n_rows, SUB, LANE)
        yp = _combine(dest, x1p, routep, nfin, yb3, 0).reshape(B, L, D)
        ys = _combine(dest, x1s, routes, nfin, yb3, Tp).reshape(NB, ls, D)
        pk.append(k1.reshape(B, WINDOW, KV_HEADS, HEAD_DIM))
        pv.append(v1.reshape(B, WINDOW, KV_HEADS, HEAD_DIM))
        ps.append(s1)
        sk.append(k2.reshape(NB, WINDOW, KV_HEADS, HEAD_DIM))
        sv.append(v2.reshape(NB, WINDOW, KV_HEADS, HEAD_DIM))
        ss.append(s2)
    return (yp, ys, jnp.stack(pk), jnp.stack(pv), jnp.stack(ps),
            jnp.stack(sk), jnp.stack(sv), jnp.stack(ss))
```

```python
import functools
import math

import jax
import jax.numpy as jnp
import numpy as np
from jax import lax
from jax.experimental import pallas as pl
from jax.experimental.pallas import tpu as pltpu

HEAD_DIM = 64
KV_HEADS = 4
GQA_GROUP = 4
ATTN_HEADS = KV_HEADS * GQA_GROUP
WINDOW = 128
N_BUCKETS = 32
MAX_DISTANCE = 128
RET_HEADS = 4
RET_DK = 128
RET_DV = 256
RET_CHUNK = 128
ROPE_BASE = 10000.0
N_GROUPS = 4
EXPERTS_PER_GROUP = 8
N_EXPERTS = N_GROUPS * EXPERTS_PER_GROUP
EXPERT_FF = 512
NORM_EPS = 1e-6
NEG_INF = -1e30
PAST_LEN = 16384

ATTN_W = ATTN_HEADS * HEAD_DIM
KV_W = KV_HEADS * HEAD_DIM
RQ_W = RET_HEADS * RET_DK
RV_W = RET_HEADS * RET_DV
OFF_QA = 0
OFF_KA = OFF_QA + ATTN_W
OFF_VA = OFF_KA + KV_W
OFF_QR = OFF_VA + KV_W
OFF_KR = OFF_QR + RQ_W
OFF_VR = OFF_KR + RQ_W
OFF_GR = OFF_VR + RV_W
OFF_GA = OFF_GR + RV_W
OFF_GT = OFF_GA + 1024
ROUTE_LANES = 128
ROUTE_ROWS = 40
ROUTE_OUT = 8

LANE = 128
SUB = 8
PROMPT_TM = 512
SAMPLE_GROUP = 8
SAMPLE_UNROLL = 2
MOE_TM = 256
ROW_TM = 256
DMA_UNROLL = 8
VMEM_LIMIT = 60 * 1024 * 1024

_F32 = jnp.float32
_BF16 = jnp.bfloat16


def _const_spec(shape):
    nd = len(shape)
    return pl.BlockSpec(shape, lambda *_: (0,) * nd, pipeline_mode=pl.Buffered(1))


def _rms(x, gain):
    return x * lax.rsqrt(jnp.mean(x * x, axis=-1, keepdims=True) + NORM_EPS) * gain


def _dot(a, b):
    return jnp.dot(a, b, preferred_element_type=_F32)


def _dot_nt(a, b):
    return lax.dot_general(a, b, (((1,), (1,)), ((), ())), preferred_element_type=_F32)


def _dot_tn(a, b):
    return lax.dot_general(a, b, (((0,), (0,)), ((), ())), preferred_element_type=_F32)


def _load_rows(ref, n):
    return jnp.concatenate([ref[pl.ds(s, n, stride=SUB), :] for s in range(SUB)], axis=1)


def _store_rows(ref, val):
    n = val.shape[0]
    for s in range(SUB):
        ref[pl.ds(s, n, stride=SUB), :] = val[:, s * LANE:(s + 1) * LANE]


def _rotary(x, cosf, sinf):
    return x * cosf + pltpu.roll(x, RET_DK // 2, 1) * sinf


def _post(x, attn_proj, ret, gate_a, gate_r, wbr, wout, nffn, wrh, wrl, br, cnt):
    merged = (jax.nn.sigmoid(gate_a) * attn_proj
              + jax.nn.sigmoid(gate_r) * _dot(ret.astype(_BF16), wbr))
    x1 = x + _dot(merged.astype(_BF16), wout)
    xn2 = _rms(x1, nffn)
    hi = xn2.astype(_BF16)
    lo = (xn2 - hi.astype(_F32)).astype(_BF16)
    logits = _dot(hi, wrh) + (_dot(hi, wrl) + _dot(lo, wrh)) + br
    n = logits.shape[0]
    lt = logits.T[0:ROUTE_ROWS, :]
    row = lax.broadcasted_iota(jnp.int32, (ROUTE_ROWS, n), 0)
    big = jnp.int32(1 << 20)
    neg = jnp.float32(-jnp.inf)
    gl = jnp.where(row < N_GROUPS, lt, neg)
    gmax = jnp.max(gl, axis=0, keepdims=True)
    gexp = jnp.exp(gl - gmax)
    gsum = jnp.sum(gexp, axis=0, keepdims=True)
    pg = gexp / gsum
    g_w = jnp.max(pg, axis=0, keepdims=True)
    g_idx = jnp.min(jnp.where(pg == g_w, row, big), axis=0, keepdims=True)
    e_row = row - N_GROUPS
    emask = (e_row >= 0) & (e_row < N_EXPERTS) & ((e_row >> 3) == g_idx)
    fl = jnp.where(emask, lt, neg)
    fmax = jnp.max(fl, axis=0, keepdims=True)
    fexp = jnp.exp(fl - fmax)
    fsum = jnp.sum(fexp, axis=0, keepdims=True)
    pe = jnp.where(emask, fexp / fsum, -1.0)
    p1 = jnp.max(pe, axis=0, keepdims=True)
    i1 = jnp.min(jnp.where(pe == p1, row, big), axis=0, keepdims=True)
    pe2 = jnp.where(row == i1, -1.0, pe)
    p2 = jnp.max(pe2, axis=0, keepdims=True)
    i2 = jnp.min(jnp.where(pe2 == p2, row, big), axis=0, keepdims=True)
    psum = p1 + p2
    gate1 = g_w * p1 / psum
    gate2 = g_w * p2 / psum
    oh1 = row == i1
    oh2 = row == i2
    c = jnp.where(oh1 | oh2, 1.0, 0.0)
    tt = lax.broadcasted_iota(jnp.int32, (n, n), 0)
    tc = lax.broadcasted_iota(jnp.int32, (n, n), 1)
    upper = jnp.where(tt < tc, 1.0, 0.0).astype(_BF16)
    before = _dot(c.astype(_BF16), upper) + cnt
    rank1 = jnp.sum(jnp.where(oh1, before, 0.0), axis=0, keepdims=True)
    rank2 = jnp.sum(jnp.where(oh2, before, 0.0), axis=0, keepdims=True)
    cnt = cnt + jnp.sum(c, axis=1, keepdims=True)
    r8 = lax.broadcasted_iota(jnp.int32, (ROUTE_OUT, n), 0)
    vals = [(i1 - N_GROUPS).astype(_F32), (i2 - N_GROUPS).astype(_F32), gate1, gate2, rank1, rank2]
    route = jnp.zeros((ROUTE_OUT, n), _F32)
    for k, v in enumerate(vals):
        route = jnp.where(r8 == k, v, route)
    return x1, route, cnt


def _prompt_mixer_kernel(sink_ref, cdec_ref,
                         x_ref, nmix_ref, win_ref, bias_ref, cos_ref, sin_ref, dec_ref,
                         qd_ref, kd_ref, wba_ref, wbr_ref, wout_ref, nffn_ref,
                         wrh_ref, wrl_ref, br_ref,
                         x1_ref, route_ref, knew_ref, vnew_ref, s_ref, cnt_ref,
                         proj, kctx, vctx, attn_t, ret):
    i = pl.program_id(1)
    last = pl.num_programs(1) - 1

    @pl.when((i == 0) & (pl.program_id(0) == 0))
    def _():
        cnt_ref[...] = jnp.zeros_like(cnt_ref)

    tm = x_ref.shape[1]
    n_sub = tm // WINDOW
    scale = HEAD_DIM ** -0.5

    @pl.when(i == 0)
    def _():
        s_ref[...] = jnp.zeros_like(s_ref)
        kctx[0:WINDOW, :] = jnp.zeros((WINDOW, KV_W), _BF16)
        vctx[0:WINDOW, :] = jnp.zeros((WINDOW, KV_W), _BF16)

    x = x_ref[0]
    xn = _rms(x, nmix_ref[...]).astype(_BF16)
    n_in = win_ref.shape[1]
    panel = 512

    def project(c0):
        proj[:, c0:c0 + panel] = _dot(xn, win_ref[:, c0:c0 + panel])

    for c0 in range(0, OFF_QR, panel):
        project(c0)
    later_panels = list(range(OFF_QR, n_in, panel))

    @pl.when(i == last)
    def _():
        knew_ref[0] = proj[tm - WINDOW:tm, OFF_KA:OFF_KA + KV_W]
        vnew_ref[0] = proj[tm - WINDOW:tm, OFF_VA:OFF_VA + KV_W]

    krow = lax.broadcasted_iota(jnp.int32, (2 * WINDOW, 1), 0)
    for c in range(n_sub):
        r0 = c * WINDOW
        kctx[WINDOW:2 * WINDOW, :] = proj[r0:r0 + WINDOW, OFF_KA:OFF_KA + KV_W].astype(_BF16)
        vctx[WINDOW:2 * WINDOW, :] = proj[r0:r0 + WINDOW, OFF_VA:OFF_VA + KV_W].astype(_BF16)
        if c == 0:
            pen = jnp.where((krow < WINDOW) & (i == 0), NEG_INF, 0.0).astype(_F32)
        for h in range(KV_HEADS):
            if later_panels:
                project(later_panels.pop(0))
            k_h = kctx[:, h * HEAD_DIM:(h + 1) * HEAD_DIM]
            v_h = vctx[:, h * HEAD_DIM:(h + 1) * HEAD_DIM]
            probs = []
            for g in range(GQA_GROUP):
                hq = h * GQA_GROUP + g
                q = (proj[r0:r0 + WINDOW, hq * HEAD_DIM:(hq + 1) * HEAD_DIM] * scale).astype(_BF16)
                s = _dot_nt(k_h, q) + bias_ref[hq]
                if c == 0:
                    s = s + pen
                snk = sink_ref[hq]
                m = jnp.maximum(jnp.max(s, axis=0, keepdims=True), snk)
                p = jnp.exp(s - m)
                den = jnp.sum(p, axis=0, keepdims=True) + jnp.exp(snk - m)
                probs.append((p * (1.0 / den)).astype(_BF16))
            o_t = _dot_tn(v_h, jnp.concatenate(probs, axis=1))
            for g in range(GQA_GROUP):
                hq = h * GQA_GROUP + g
                attn_t[hq * HEAD_DIM:(hq + 1) * HEAD_DIM, r0:r0 + WINDOW] = (
                    o_t[:, g * WINDOW:(g + 1) * WINDOW].astype(_BF16))
        kctx[0:WINDOW, :] = kctx[WINDOW:2 * WINDOW, :]
        vctx[0:WINDOW, :] = vctx[WINDOW:2 * WINDOW, :]

    for c0 in later_panels:
        project(c0)

    cosf = cos_ref[...]
    sinf = sin_ref[...]
    for h in range(RET_HEADS):
        qrot = _rotary(proj[:, OFF_QR + h * RET_DK:OFF_QR + (h + 1) * RET_DK], cosf, sinf)
        krot = _rotary(proj[:, OFF_KR + h * RET_DK:OFF_KR + (h + 1) * RET_DK], cosf, sinf) * (RET_DK ** -0.5)
        qd = qd_ref[:, h * RET_DK:(h + 1) * RET_DK]
        kd = kd_ref[:, h * RET_DK:(h + 1) * RET_DK]
        for c in range(n_sub):
            r0 = c * RET_CHUNK
            qc = qrot[r0:r0 + RET_CHUNK]
            kc = krot[r0:r0 + RET_CHUNK]
            vc = proj[r0:r0 + RET_CHUNK, OFF_VR + h * RET_DV:OFF_VR + (h + 1) * RET_DV].astype(_BF16)
            sc = _dot_nt(qc.astype(_BF16), kc.astype(_BF16)) * dec_ref[h]
            s_old = s_ref[0, h]
            o = _dot(sc.astype(_BF16), vc) + _dot((qc * qd).astype(_BF16), s_old.astype(_BF16))
            s_ref[0, h] = s_old * cdec_ref[h] + _dot_tn((kc * kd).astype(_BF16), vc)
            o = o * lax.rsqrt(jnp.mean(o * o, axis=-1, keepdims=True) + NORM_EPS)
            gr = proj[r0:r0 + RET_CHUNK, OFF_GR + h * RET_DV:OFF_GR + (h + 1) * RET_DV]
            ret[r0:r0 + RET_CHUNK, h * RET_DV:(h + 1) * RET_DV] = o * (gr * jax.nn.sigmoid(gr))

    x1, route, cnt = _post(x, _dot_tn(attn_t[...], wba_ref[...]), ret[...],
                           proj[:, OFF_GA:OFF_GA + 1024], proj[:, OFF_GT:OFF_GT + 1024],
                           wbr_ref[...], wout_ref[...], nffn_ref[...],
                           wrh_ref[...], wrl_ref[...], br_ref[...], cnt_ref[:, 0:1])
    _store_rows(x1_ref, x1)
    route_ref[...] = route
    cnt_ref[...] = jnp.broadcast_to(cnt, cnt_ref.shape)


def _prompt_mixer(x, cst, w):
    B, L, D = x.shape
    tm = min(PROMPT_TM, L)
    nb = L // tm
    n_in = w['win'].shape[1]
    step = lambda b, i, *_: (b, i, 0)
    per_b = lambda b, i, *_: (b, 0, 0)
    grid_spec = pltpu.PrefetchScalarGridSpec(
        num_scalar_prefetch=2,
        grid=(B, nb),
        in_specs=[
            pl.BlockSpec((1, tm, D), step),
            _const_spec((1, D)),
            _const_spec((D, n_in)),
            _const_spec((ATTN_HEADS, 2 * WINDOW, WINDOW)),
            pl.BlockSpec((tm, RET_DK), lambda b, i, *_: (i, 0)),
            pl.BlockSpec((tm, RET_DK), lambda b, i, *_: (i, 0)),
            _const_spec((RET_HEADS, RET_CHUNK, RET_CHUNK)),
            _const_spec((RET_CHUNK, RQ_W)),
            _const_spec((RET_CHUNK, RQ_W)),
            _const_spec((ATTN_W, D)),
            _const_spec((RV_W, D)),
            _const_spec((D, D)),
            _const_spec((1, D)),
            _const_spec((D, ROUTE_LANES)),
            _const_spec((D, ROUTE_LANES)),
            _const_spec((1, ROUTE_LANES)),
        ],
        out_specs=[
            pl.BlockSpec((tm * SUB, LANE), lambda b, i, *_: (b * nb + i, 0)),
            pl.BlockSpec((ROUTE_OUT, tm), lambda b, i, *_: (0, b * nb + i)),
            pl.BlockSpec((1, WINDOW, KV_W), per_b),
            pl.BlockSpec((1, WINDOW, KV_W), per_b),
            pl.BlockSpec((1, RET_HEADS, RET_DK, RET_DV), lambda b, i, *_: (b, 0, 0, 0)),
            pl.BlockSpec((ROUTE_ROWS, LANE), lambda b, i, *_: (0, 0)),
        ],
        scratch_shapes=[
            pltpu.VMEM((tm, n_in), _F32),
            pltpu.VMEM((2 * WINDOW, KV_W), _BF16),
            pltpu.VMEM((2 * WINDOW, KV_W), _BF16),
            pltpu.VMEM((ATTN_W, tm), _BF16),
            pltpu.VMEM((tm, RV_W), _F32),
        ],
    )
    assert D == SUB * LANE
    out_shape = [
        jax.ShapeDtypeStruct((B * L * SUB, LANE), _F32),
        jax.ShapeDtypeStruct((ROUTE_OUT, B * L), _F32),
        jax.ShapeDtypeStruct((B, WINDOW, KV_W), _F32),
        jax.ShapeDtypeStruct((B, WINDOW, KV_W), _F32),
        jax.ShapeDtypeStruct((B, RET_HEADS, RET_DK, RET_DV), _F32),
        jax.ShapeDtypeStruct((ROUTE_ROWS, LANE), _F32),
    ]
    return pl.pallas_call(
        _prompt_mixer_kernel,
        grid_spec=grid_spec,
        out_shape=out_shape,
        compiler_params=pltpu.CompilerParams(
            dimension_semantics=("arbitrary", "arbitrary"), vmem_limit_bytes=VMEM_LIMIT),
        name="prompt_mixer",
    )(cst['sink'], cst['cdec_p'],
      x, w['nmix'], w['win'], cst['bias_p'], cst['cos_p'], cst['sin_p'], cst['dec_p'],
      cst['qd_p'], cst['kd_p'], w['wba'], w['wbr'], w['wout'], w['nffn'],
      w['wrh'], w['wrl'], w['br'])


def _inproj_kernel(x_ref, nmix_ref, win_ref, o_ref):
    xn = _rms(x_ref[...], nmix_ref[...]).astype(_BF16)
    o_ref[...] = _dot(xn, win_ref[...])


def _sample_inproj(x2d, w):
    T, D = x2d.shape
    n_in = w['win'].shape[1]
    panel = 512
    return pl.pallas_call(
        _inproj_kernel,
        grid=(n_in // panel,),
        in_specs=[pl.BlockSpec((T, D), lambda j: (0, 0)),
                  pl.BlockSpec((1, D), lambda j: (0, 0)),
                  pl.BlockSpec((D, panel), lambda j: (0, j))],
        out_specs=pl.BlockSpec((T, panel), lambda j: (0, j)),
        out_shape=jax.ShapeDtypeStruct((T, n_in), _F32),
        compiler_params=pltpu.CompilerParams(
            dimension_semantics=("arbitrary",), vmem_limit_bytes=VMEM_LIMIT),
        name="sample_inproj",
    )(x2d, w['nmix'], w['win'])


def _sample_core_kernel(cdec_ref, proj_ref, ck_ref, cv_ref, st_ref, bh_ref, bn_ref, snk_ref,
                        cos_ref, sin_ref, dec_ref, qd_ref, kd_ref,
                        attn_ref, ret_ref, nk_ref, nv_ref, ns_ref):
    G = proj_ref.shape[0]
    ls = proj_ref.shape[1]
    scale = HEAD_DIM ** -0.5
    cosf = cos_ref[...]
    sinf = sin_ref[...]

    def body(b, carry):
        row = proj_ref[b]
        k_new = row[:, OFF_KA:OFF_KA + KV_W]
        v_new = row[:, OFF_VA:OFF_VA + KV_W]
        ck = ck_ref[b]
        cv = cv_ref[b]
        nk_ref[b, 0:WINDOW - ls, :] = ck[ls:WINDOW, :]
        nk_ref[b, WINDOW - ls:WINDOW, :] = k_new
        nv_ref[b, 0:WINDOW - ls, :] = cv[ls:WINDOW, :]
        nv_ref[b, WINDOW - ls:WINDOW, :] = v_new
        ckb = ck.astype(_BF16)
        cvb = cv.astype(_BF16)
        knb = k_new.astype(_BF16)
        vnb = v_new.astype(_BF16)
        heads = [slice(h * HEAD_DIM, (h + 1) * HEAD_DIM) for h in range(KV_HEADS)]
        s1s, s2s = [], []
        for h in range(KV_HEADS):
            q4 = (jnp.concatenate(
                [row[:, (h * GQA_GROUP + g) * HEAD_DIM:(h * GQA_GROUP + g + 1) * HEAD_DIM]
                 for g in range(GQA_GROUP)], axis=0) * scale).astype(_BF16)
            s1s.append(_dot_nt(q4, ckb[:, heads[h]]))
            s2s.append(_dot_nt(q4, knb[:, heads[h]]))
        scs, crosses, vcs = [], [], []
        for h in range(RET_HEADS):
            qrot = _rotary(row[:, OFF_QR + h * RET_DK:OFF_QR + (h + 1) * RET_DK], cosf, sinf)
            krot = _rotary(row[:, OFF_KR + h * RET_DK:OFF_KR + (h + 1) * RET_DK], cosf, sinf) * (RET_DK ** -0.5)
            vc = row[:, OFF_VR + h * RET_DV:OFF_VR + (h + 1) * RET_DV].astype(_BF16)
            qd = qd_ref[:, h * RET_DK:(h + 1) * RET_DK]
            kd = kd_ref[:, h * RET_DK:(h + 1) * RET_DK]
            s_old = st_ref[b, h]
            scs.append(_dot_nt(qrot.astype(_BF16), krot.astype(_BF16)))
            crosses.append(_dot((qrot * qd).astype(_BF16), s_old.astype(_BF16)))
            ns_ref[b, h] = s_old * cdec_ref[h] + _dot_tn((krot * kd).astype(_BF16), vc)
            vcs.append(vc)
        p1s, p2s = [], []
        for h in range(KV_HEADS):
            s1 = s1s[h] + bh_ref[h]
            s2 = s2s[h] + bn_ref[h]
            snk = snk_ref[h]
            m = jnp.maximum(jnp.maximum(jnp.max(s1, axis=-1, keepdims=True),
                                        jnp.max(s2, axis=-1, keepdims=True)), snk)
            p1 = jnp.exp(s1 - m)
            p2 = jnp.exp(s2 - m)
            den = (jnp.sum(p1, axis=-1, keepdims=True) + jnp.sum(p2, axis=-1, keepdims=True)
                   + jnp.exp(snk - m))
            r = 1.0 / den
            p1s.append((p1 * r).astype(_BF16))
            p2s.append((p2 * r).astype(_BF16))
        scb = [(scs[h] * dec_ref[h]).astype(_BF16) for h in range(RET_HEADS)]
        outs = [_dot(p1s[h], cvb[:, heads[h]]) + _dot(p2s[h], vnb[:, heads[h]])
                for h in range(KV_HEADS)]
        rets = [_dot(scb[h], vcs[h]) + crosses[h] for h in range(RET_HEADS)]
        for h in range(KV_HEADS):
            for g in range(GQA_GROUP):
                hq = h * GQA_GROUP + g
                attn_ref[b, :, hq * HEAD_DIM:(hq + 1) * HEAD_DIM] = outs[h][g * ls:(g + 1) * ls]
        for h in range(RET_HEADS):
            o = rets[h]
            o = o * lax.rsqrt(jnp.mean(o * o, axis=-1, keepdims=True) + NORM_EPS)
            gr = row[:, OFF_GR + h * RET_DV:OFF_GR + (h + 1) * RET_DV]
            ret_ref[b, :, h * RET_DV:(h + 1) * RET_DV] = o * (gr * jax.nn.sigmoid(gr))
        return carry

    lax.fori_loop(0, G, body, 0, unroll=SAMPLE_UNROLL)


def _sample_core(proj3, ck, cv, st, cst):
    NB, ls, n_in = proj3.shape
    G = min(SAMPLE_GROUP, NB)
    blk3 = lambda i, *_: (i, 0, 0)
    blk4 = lambda i, *_: (i, 0, 0, 0)
    c2 = lambda i, *_: (0, 0)
    c3 = lambda i, *_: (0, 0, 0)
    ql = GQA_GROUP * ls
    grid_spec = pltpu.PrefetchScalarGridSpec(
        num_scalar_prefetch=1,
        grid=(NB // G,),
        in_specs=[
            pl.BlockSpec((G, ls, n_in), blk3),
            pl.BlockSpec((G, WINDOW, KV_W), blk3),
            pl.BlockSpec((G, WINDOW, KV_W), blk3),
            pl.BlockSpec((G, RET_HEADS, RET_DK, RET_DV), blk4),
            pl.BlockSpec((KV_HEADS, ql, WINDOW), c3),
            pl.BlockSpec((KV_HEADS, ql, ls), c3),
            pl.BlockSpec((KV_HEADS, ql, 1), c3),
            pl.BlockSpec((ls, RET_DK), c2),
            pl.BlockSpec((ls, RET_DK), c2),
            pl.BlockSpec((RET_HEADS, ls, ls), c3),
            pl.BlockSpec((ls, RQ_W), c2),
            pl.BlockSpec((ls, RQ_W), c2),
        ],
        out_specs=[
            pl.BlockSpec((G, ls, ATTN_W), blk3),
            pl.BlockSpec((G, ls, RV_W), blk3),
            pl.BlockSpec((G, WINDOW, KV_W), blk3),
            pl.BlockSpec((G, WINDOW, KV_W), blk3),
            pl.BlockSpec((G, RET_HEADS, RET_DK, RET_DV), blk4),
        ],
    )
    out_shape = [
        jax.ShapeDtypeStruct((NB, ls, ATTN_W), _F32),
        jax.ShapeDtypeStruct((NB, ls, RV_W), _F32),
        jax.ShapeDtypeStruct((NB, WINDOW, KV_W), _F32),
        jax.ShapeDtypeStruct((NB, WINDOW, KV_W), _F32),
        jax.ShapeDtypeStruct((NB, RET_HEADS, RET_DK, RET_DV), _F32),
    ]
    return pl.pallas_call(
        _sample_core_kernel,
        grid_spec=grid_spec,
        out_shape=out_shape,
        compiler_params=pltpu.CompilerParams(
            dimension_semantics=("arbitrary",), vmem_limit_bytes=VMEM_LIMIT),
        name="sample_core",
    )(cst['cdec_s'], proj3, ck, cv, st, cst['bias_hist'], cst['bias_new'], cst['sink_col'],
      cst['cos_s'], cst['sin_s'], cst['dec_s'], cst['qd_s'], cst['kd_s'])


def _sample_post_kernel(x_ref, attn_ref, ret_ref, ga_ref, gt_ref, wba_ref, wbr_ref, wout_ref,
                        nffn_ref, wrh_ref, wrl_ref, br_ref, cnt0_ref, x1_ref, route_ref, cnt_ref):
    x1, route, cnt = _post(x_ref[...], _dot(attn_ref[...].astype(_BF16), wba_ref[...]),
                           ret_ref[...], ga_ref[...], gt_ref[...],
                           wbr_ref[...], wout_ref[...], nffn_ref[...],
                           wrh_ref[...], wrl_ref[...], br_ref[...], cnt0_ref[:, 0:1])
    _store_rows(x1_ref, x1)
    route_ref[...] = route
    cnt_ref[...] = jnp.broadcast_to(cnt, cnt_ref.shape)


def _sample_post(x2d, attn, ret, ga, gt, w, cnt0):
    T, D = x2d.shape
    full = lambda s: pl.BlockSpec(s, lambda i: (0,) * len(s))
    return pl.pallas_call(
        _sample_post_kernel,
        grid=(1,),
        in_specs=[full((T, D)), full((T, ATTN_W)), full((T, RV_W)), full((T, D)), full((T, D)),
                  full((ATTN_W, D)), full((RV_W, D)), full((D, D)), full((1, D)),
                  full((D, ROUTE_LANES)), full((D, ROUTE_LANES)), full((1, ROUTE_LANES)),
                  full((ROUTE_ROWS, LANE))],
        out_specs=[full((T * SUB, LANE)), full((ROUTE_OUT, T)), full((ROUTE_ROWS, LANE))],
        out_shape=[jax.ShapeDtypeStruct((T * SUB, LANE), _F32),
                   jax.ShapeDtypeStruct((ROUTE_OUT, T), _F32),
                   jax.ShapeDtypeStruct((ROUTE_ROWS, LANE), _F32)],
        compiler_params=pltpu.CompilerParams(
            dimension_semantics=("arbitrary",), vmem_limit_bytes=VMEM_LIMIT),
        name="sample_post",
    )(x2d, attn, ret, ga, gt, w['wba'], w['wbr'], w['wout'], w['nffn'], w['wrh'], w['wrl'], w['br'],
      cnt0)


def _dispatch_kernel(dest_ref, xp_ref, xq_ref, xs_ref, sem, *, p_steps):
    i = pl.program_id(0)

    def copy_tile(src):
        def tile_copy(r, d):
            return pltpu.make_async_copy(src.at[r], xs_ref.at[d], sem)

        def start(r, c):
            t = i * ROW_TM + r
            tile_copy(r, dest_ref[2 * t]).start(priority=0)
            tile_copy(r, dest_ref[2 * t + 1]).start(priority=1)
            return c

        lax.fori_loop(0, ROW_TM, start, 0, unroll=DMA_UNROLL)

        def wait(r, c):
            tile_copy(0, 0).wait()
            tile_copy(0, 0).wait()
            return c

        lax.fori_loop(0, ROW_TM, wait, 0, unroll=DMA_UNROLL)

    @pl.when(i < p_steps)
    def _():
        copy_tile(xp_ref)

    @pl.when(i >= p_steps)
    def _():
        copy_tile(xq_ref)


def _dispatch(dest, xp3, xq3):
    Tp, Tq = xp3.shape[0], xq3.shape[0]
    assert Tp % ROW_TM == 0 and Tq % ROW_TM == 0
    p_steps = Tp // ROW_TM
    grid_spec = pltpu.PrefetchScalarGridSpec(
        num_scalar_prefetch=1,
        grid=((Tp + Tq) // ROW_TM,),
        in_specs=[
            pl.BlockSpec((ROW_TM, SUB, LANE), lambda i, *_: (jnp.minimum(i, p_steps - 1), 0, 0)),
            pl.BlockSpec((ROW_TM, SUB, LANE), lambda i, *_: (jnp.maximum(i - p_steps, 0), 0, 0)),
        ],
        out_specs=pl.BlockSpec(memory_space=pl.ANY),
        scratch_shapes=[pltpu.SemaphoreType.DMA],
    )
    return pl.pallas_call(
        functools.partial(_dispatch_kernel, p_steps=p_steps),
        grid_spec=grid_spec,
        out_shape=jax.ShapeDtypeStruct((2 * (Tp + Tq), SUB, LANE), _F32),
        compiler_params=pltpu.CompilerParams(dimension_semantics=("arbitrary",)),
        name="moe_dispatch",
    )(dest, xp3, xq3)


def _gmm_kernel(tile_ref, exp_ref, lo_ref, hi_ref, first_ref, chg_ref,
                x_ref, nffn_ref, wg_ref, wu_ref, wd_ref, y_ref, wg_s, wu_s, wd_s):
    m = pl.program_id(0)
    tm = x_ref.shape[0] // SUB

    @pl.when(chg_ref[m] == 1)
    def _():
        wg_s[...] = wg_ref[0].astype(_BF16)
        wu_s[...] = wu_ref[0].astype(_BF16)
        wd_s[...] = wd_ref[0].astype(_BF16)

    lo = lo_ref[m]
    hi = hi_ref[m]

    @pl.when(hi > lo)
    def _():
        rows = tile_ref[m] * tm + lax.broadcasted_iota(jnp.int32, (tm, 1), 0)
        mine = (rows >= lo) & (rows < hi)
        xn = _rms(_load_rows(x_ref, tm), nffn_ref[...])
        x = jnp.where(mine, xn, 0.0).astype(_BF16)
        a = _dot(x, wg_s[...])
        hmid = (a * jax.nn.sigmoid(a)) * _dot(x, wu_s[...])
        y = _dot(hmid.astype(_BF16), wd_s[...])

        @pl.when(first_ref[m] == 1)
        def _():
            _store_rows(y_ref, y)

        @pl.when(first_ref[m] == 0)
        def _():
            _store_rows(y_ref, _load_rows(y_ref, tm) + y)


def _gmm(work, xs2, nffn, wg, wu, wd):
    A = xs2.shape[0] // SUB
    E, D, F = wg.shape
    n_work = work[0].shape[0]
    grid_spec = pltpu.PrefetchScalarGridSpec(
        num_scalar_prefetch=6,
        grid=(n_work,),
        in_specs=[
            pl.BlockSpec((MOE_TM * SUB, LANE), lambda m, t, e, *_: (t[m], 0)),
            pl.BlockSpec((1, D), lambda m, t, e, *_: (0, 0)),
            pl.BlockSpec((1, D, F), lambda m, t, e, *_: (e[m], 0, 0)),
            pl.BlockSpec((1, D, F), lambda m, t, e, *_: (e[m], 0, 0)),
            pl.BlockSpec((1, F, D), lambda m, t, e, *_: (e[m], 0, 0)),
        ],
        out_specs=pl.BlockSpec((MOE_TM * SUB, LANE), lambda m, t, e, *_: (t[m], 0)),
        scratch_shapes=[pltpu.VMEM((D, F), _BF16), pltpu.VMEM((D, F), _BF16),
                        pltpu.VMEM((F, D), _BF16)],
    )
    return pl.pallas_call(
        _gmm_kernel,
        grid_spec=grid_spec,
        out_shape=jax.ShapeDtypeStruct((A * SUB, LANE), _F32),
        compiler_params=pltpu.CompilerParams(
            dimension_semantics=("arbitrary",), vmem_limit_bytes=VMEM_LIMIT),
        name="moe_gmm",
    )(*work, xs2, nffn, wg, wu, wd)


def _combine_kernel(dest_ref, x1_ref, route_ref, nfin_ref, yb_ref, o_ref, buf, sems, *, tok0):
    i = pl.program_id(0)
    n_steps = pl.num_programs(0)
    tm = o_ref.shape[0]

    def tile_copy(d, slot, k, r):
        rows = pl.ds(pl.multiple_of(r * SUB, SUB), SUB)
        return pltpu.make_async_copy(yb_ref.at[d], buf.at[slot, k, rows], sems.at[slot])

    def issue(step, slot):
        base = tok0 + step * tm

        def start(r, c):
            t = base + r
            tile_copy(dest_ref[2 * t], slot, 0, r).start(priority=0)
            tile_copy(dest_ref[2 * t + 1], slot, 1, r).start(priority=1)
            return c

        lax.fori_loop(0, tm, start, 0, unroll=DMA_UNROLL)

    @pl.when(i == 0)
    def _():
        issue(0, 0)

    @pl.when(i + 1 < n_steps)
    def _():
        issue(i + 1, (i + 1) % 2)

    slot = i % 2

    def wait(r, c):
        tile_copy(0, slot, 0, 0).wait()
        tile_copy(0, slot, 1, 0).wait()
        return c

    lax.fori_loop(0, tm, wait, 0, unroll=DMA_UNROLL)
    rt = jnp.concatenate([route_ref[...], jnp.zeros((LANE - ROUTE_OUT, tm), _F32)], axis=0).T
    g0 = rt[:, 2:3]
    g1 = rt[:, 3:4]
    y0 = _load_rows(buf.at[slot, 0], tm)
    y1 = _load_rows(buf.at[slot, 1], tm)
    y = _load_rows(x1_ref, tm) + (y0 * g0 + y1 * g1)
    o_ref[...] = _rms(y, nfin_ref[...])


def _combine(dest, x1_2, route, nfin, yb3, tok0):
    T = x1_2.shape[0] // SUB
    D = SUB * LANE
    tm = min(ROW_TM, T)
    grid_spec = pltpu.PrefetchScalarGridSpec(
        num_scalar_prefetch=1,
        grid=(T // tm,),
        in_specs=[
            pl.BlockSpec((tm * SUB, LANE), lambda i, *_: (i, 0)),
            pl.BlockSpec((ROUTE_OUT, tm), lambda i, *_: (0, i)),
            pl.BlockSpec((1, D), lambda i, *_: (0, 0)),
            pl.BlockSpec(memory_space=pl.ANY),
        ],
        out_specs=pl.BlockSpec((tm, D), lambda i, *_: (i, 0)),
        scratch_shapes=[pltpu.VMEM((2, 2, tm * SUB, LANE), _F32), pltpu.SemaphoreType.DMA((2,))],
    )
    return pl.pallas_call(
        functools.partial(_combine_kernel, tok0=tok0),
        grid_spec=grid_spec,
        out_shape=jax.ShapeDtypeStruct((T, D), _F32),
        compiler_params=pltpu.CompilerParams(
            dimension_semantics=("arbitrary",), vmem_limit_bytes=VMEM_LIMIT),
        name="moe_combine",
    )(dest, x1_2, route, nfin, yb3)


def _routing_tables(route, counts, n_tiles):
    experts = route[0:2].astype(jnp.int32).T.reshape(-1)
    ranks = route[4:6].astype(jnp.int32).T.reshape(-1)
    A = experts.shape[0]
    ids = jnp.arange(N_EXPERTS, dtype=jnp.int32)
    ends = jnp.cumsum(counts)
    starts = ends - counts
    dest = ranks + jnp.sum(jnp.where(experts[:, None] == ids[None, :], starts[None, :], 0), axis=1)
    tile_starts = jnp.arange(n_tiles, dtype=jnp.int32) * MOE_TM
    pos_t = jnp.arange(n_tiles, dtype=jnp.int32) + jnp.sum(starts[None, :] < tile_starts[:, None], axis=1)
    pos_e = ids + jnp.sum(tile_starts[None, :] <= starts[:, None], axis=1)
    slots = jnp.arange(n_tiles + N_EXPERTS, dtype=jnp.int32)
    pts = (jnp.sum(jnp.where(pos_t[None, :] == slots[:, None], tile_starts[None, :], 0), axis=1)
           + jnp.sum(jnp.where(pos_e[None, :] == slots[:, None], starts[None, :], 0), axis=1))
    lo = pts.astype(jnp.int32)
    hi = jnp.concatenate([lo[1:], jnp.array([A], jnp.int32)])
    tile = jnp.minimum(lo // MOE_TM, n_tiles - 1).astype(jnp.int32)
    expert = jnp.minimum(jnp.sum(ends[None, :] <= lo[:, None], axis=1), N_EXPERTS - 1).astype(jnp.int32)
    nonempty = hi > lo
    first = (nonempty & (lo % MOE_TM == 0)).astype(jnp.int32)
    chg = jnp.concatenate([jnp.ones((1,), jnp.int32), (expert[1:] != expert[:-1]).astype(jnp.int32)])
    return dest.astype(jnp.int32), (tile, expert, lo, hi, first, chg)


def _bucket_table(lq, lk):
    dist = np.arange(lq)[:, None] + WINDOW - np.arange(lk)[None, :]
    band = (dist >= 0) & (dist < WINDOW)
    d = np.clip(dist, 0, WINDOW - 1)
    max_exact = N_BUCKETS // 2
    d_f = np.maximum(d, 1).astype(np.float32)
    large = max_exact + (np.log(d_f / max_exact) / math.log(MAX_DISTANCE / max_exact)
                         * (N_BUCKETS - max_exact)).astype(np.int32)
    large = np.minimum(large, N_BUCKETS - 1)
    return np.where(d < max_exact, d, large).astype(np.int32), band


def _bias_table(rb, lq, lk):
    bkt, band = _bucket_table(lq, lk)
    onehot = jnp.asarray(bkt)[None, :, :] == jnp.arange(N_BUCKETS, dtype=jnp.int32)[:, None, None]
    bias = jnp.sum(jnp.where(onehot[:, None], rb[:, :, None, None], 0.0), axis=0)
    return jnp.where(jnp.asarray(band)[None], bias, NEG_INF)


def _decay_tables(C):
    log_gamma = jnp.log(1.0 - 2.0 ** (-5.0 - jnp.arange(RET_HEADS, dtype=_F32)))
    idx = jnp.arange(C, dtype=_F32)
    diff = idx[:, None] - idx[None, :]
    decay_in = jnp.where((diff >= 0)[..., None],
                         jnp.exp(jnp.maximum(diff, 0.0)[..., None] * log_gamma), 0.0)
    q_dec = jnp.exp((idx + 1.0)[:, None] * log_gamma)
    k_dec = jnp.exp((C - 1.0 - idx)[:, None] * log_gamma)
    c_dec = jnp.exp(C * log_gamma)
    dec = jnp.transpose(decay_in, (2, 0, 1))
    qd = jnp.repeat(q_dec, RET_DK, axis=1)
    kd = jnp.repeat(k_dec, RET_DK, axis=1)
    return dec, qd, kd, c_dec


def _rope_tables(pos):
    half = RET_DK // 2
    inv = ROPE_BASE ** (-jnp.arange(half, dtype=_F32) * 2.0 / RET_DK)
    ang = pos.astype(_F32)[:, None] * inv[None, :]
    cos = jnp.cos(ang)
    sin = jnp.sin(ang)
    return jnp.concatenate([cos, cos], axis=1), jnp.concatenate([-sin, sin], axis=1)


def _constants(rel_bias, attn_sink, L, ls):
    cst = {}
    rb = rel_bias.astype(_F32)
    cst['bias_p'] = jnp.transpose(_bias_table(rb, WINDOW, 2 * WINDOW), (0, 2, 1))
    cst['sink'] = attn_sink.astype(_F32)
    cst['cos_p'], cst['sin_p'] = _rope_tables(jnp.arange(L))
    cst['dec_p'], cst['qd_p'], cst['kd_p'], cst['cdec_p'] = _decay_tables(min(RET_CHUNK, L))
    bias = _bias_table(rb, ls, WINDOW + ls).reshape(KV_HEADS, GQA_GROUP * ls, WINDOW + ls)
    cst['bias_hist'] = bias[:, :, :WINDOW]
    cst['bias_new'] = bias[:, :, WINDOW:]
    cst['sink_col'] = jnp.repeat(attn_sink.astype(_F32).reshape(KV_HEADS, GQA_GROUP), ls,
                                 axis=1)[..., None]
    cst['cos_s'], cst['sin_s'] = _rope_tables(PAST_LEN + jnp.arange(ls))
    cst['dec_s'], cst['qd_s'], cst['kd_s'], cst['cdec_s'] = _decay_tables(min(RET_CHUNK, ls))
    return cst


def _layer_weights(layer, norm_mix, w_in, w_branch_attn, w_branch_ret, w_out, norm_ffn,
                   w_router_group, b_router_group, w_router_expert, b_router_expert):
    D = w_in.shape[1]
    wr = jnp.concatenate([w_router_group[layer].astype(_F32), w_router_expert[layer].astype(_F32)], axis=1)
    wr = jnp.pad(wr, ((0, 0), (0, ROUTE_LANES - wr.shape[1])))
    wrh = wr.astype(_BF16)
    wrl = (wr - wrh.astype(_F32)).astype(_BF16)
    br = jnp.concatenate([b_router_group[layer].astype(_F32), b_router_expert[layer].astype(_F32)])
    br = jnp.pad(br, (0, ROUTE_LANES - br.shape[0]))[None, :]
    return {
        'nmix': norm_mix[layer].astype(_F32)[None, :],
        'win': w_in[layer].astype(_BF16),
        'wba': w_branch_attn[layer].astype(_BF16),
        'wbr': w_branch_ret[layer].astype(_BF16),
        'wout': w_out[layer].astype(_BF16),
        'nffn': norm_ffn[layer].astype(_F32)[None, :],
        'wrh': wrh, 'wrl': wrl, 'br': br,
    }


def kernel(x_prompt, x_sample, cache_k, cache_v, state_ret, norm_mix, w_in, attn_sink, rel_bias,
           w_branch_attn, w_branch_ret, w_out, norm_ffn, w_router_group, b_router_group,
           w_router_expert, b_router_expert, w_gate, w_up, w_down, norm_final):
    depth = w_in.shape[0]
    assert depth == 1, "the final norm is fused into the MoE combine of the only layer"
    B, L, D = x_prompt.shape
    NB, ls, _ = x_sample.shape
    Tp, Ts = B * L, NB * ls
    nfin = norm_final.astype(_F32)[None, :]
    yp, ys = x_prompt, x_sample
    pk, pv, ps, sk, sv, ss = [], [], [], [], [], []
    for layer in range(depth):
        w = _layer_weights(layer, norm_mix, w_in, w_branch_attn, w_branch_ret, w_out, norm_ffn,
                           w_router_group, b_router_group, w_router_expert, b_router_expert)
        cst = _constants(rel_bias, attn_sink[layer], L, ls)
        x1p, routep, k1, v1, s1, cnt_p = _prompt_mixer(yp, cst, w)
        ys2 = ys.reshape(Ts, D)
        proj = _sample_inproj(ys2, w)
        attn_s, ret_s, k2, v2, s2 = _sample_core(
            proj.reshape(NB, ls, -1),
            cache_k[layer].reshape(NB, WINDOW, KV_W), cache_v[layer].reshape(NB, WINDOW, KV_W),
            state_ret[layer], cst)
        x1s, routes, cnt_all = _sample_post(
            ys2, attn_s.reshape(Ts, ATTN_W), ret_s.reshape(Ts, RV_W),
            proj[:, OFF_GA:OFF_GA + D], proj[:, OFF_GT:OFF_GT + D], w, cnt_p)
        n_rows = 2 * (Tp + Ts)
        assert n_rows % MOE_TM == 0
        counts = cnt_all[N_GROUPS:N_GROUPS + N_EXPERTS, 0].astype(jnp.int32)
        dest, work = _routing_tables(jnp.concatenate([routep, routes], axis=1), counts,
                                     n_rows // MOE_TM)
        xs3 = _dispatch(dest, x1p.reshape(Tp, SUB, LANE), x1s.reshape(Ts, SUB, LANE))
        yb2 = _gmm(work, xs3.reshape(n_rows * SUB, LANE), w['nffn'],
                   w_gate[layer], w_up[layer], w_down[layer])
        yb3 = yb2.reshape(n_rows, SUB, LANE)
        yp = _combine(dest, x1p, routep, nfin, yb3, 0).reshape(B, L, D)
        ys = _combine(dest, x1s, routes, nfin, yb3, Tp).reshape(NB, ls, D)
        pk.append(k1.reshape(B, WINDOW, KV_HEADS, HEAD_DIM))
        pv.append(v1.reshape(B, WINDOW, KV_HEADS, HEAD_DIM))
        ps.append(s1)
        sk.append(k2.reshape(NB, WINDOW, KV_HEADS, HEAD_DIM))
        sv.append(v2.reshape(NB, WINDOW, KV_HEADS, HEAD_DIM))
        ss.append(s2)
    return (yp, ys, jnp.stack(pk), jnp.stack(pv), jnp.stack(ps),
            jnp.stack(sk), jnp.stack(sv), jnp.stack(ss))
```

```python
import functools
import math

import jax
import jax.numpy as jnp
import numpy as np
from jax import lax
from jax.experimental import pallas as pl
from jax.experimental.pallas import tpu as pltpu

HEAD_DIM = 64
KV_HEADS = 4
GQA_GROUP = 4
ATTN_HEADS = KV_HEADS * GQA_GROUP
WINDOW = 128
N_BUCKETS = 32
MAX_DISTANCE = 128
RET_HEADS = 4
RET_DK = 128
RET_DV = 256
RET_CHUNK = 128
ROPE_BASE = 10000.0
N_GROUPS = 4
EXPERTS_PER_GROUP = 8
N_EXPERTS = N_GROUPS * EXPERTS_PER_GROUP
EXPERT_FF = 512
NORM_EPS = 1e-6
NEG_INF = -1e30
PAST_LEN = 16384

ATTN_W = ATTN_HEADS * HEAD_DIM
KV_W = KV_HEADS * HEAD_DIM
RQ_W = RET_HEADS * RET_DK
RV_W = RET_HEADS * RET_DV
OFF_QA = 0
OFF_KA = OFF_QA + ATTN_W
OFF_VA = OFF_KA + KV_W
OFF_QR = OFF_VA + KV_W
OFF_KR = OFF_QR + RQ_W
OFF_VR = OFF_KR + RQ_W
OFF_GR = OFF_VR + RV_W
OFF_GA = OFF_GR + RV_W
OFF_GT = OFF_GA + 1024
ROUTE_LANES = 128
ROUTE_ROWS = 40
ROUTE_OUT = 8

LANE = 128
SUB = 8
PROMPT_TM = 512
SAMPLE_GROUP = 8
SAMPLE_UNROLL = 2
MOE_TM = 256
ROW_TM = 256
DMA_UNROLL = 8
VMEM_LIMIT = 60 * 1024 * 1024

_F32 = jnp.float32
_BF16 = jnp.bfloat16


def _const_spec(shape):
    nd = len(shape)
    return pl.BlockSpec(shape, lambda *_: (0,) * nd, pipeline_mode=pl.Buffered(1))


def _rms(x, gain):
    return x * lax.rsqrt(jnp.mean(x * x, axis=-1, keepdims=True) + NORM_EPS) * gain


def _dot(a, b):
    return jnp.dot(a, b, preferred_element_type=_F32)


def _dot_nt(a, b):
    return lax.dot_general(a, b, (((1,), (1,)), ((), ())), preferred_element_type=_F32)


def _dot_tn(a, b):
    return lax.dot_general(a, b, (((0,), (0,)), ((), ())), preferred_element_type=_F32)


def _load_rows(ref, n):
    return jnp.concatenate([ref[pl.ds(s, n, stride=SUB), :] for s in range(SUB)], axis=1)


def _store_rows(ref, val):
    n = val.shape[0]
    for s in range(SUB):
        ref[pl.ds(s, n, stride=SUB), :] = val[:, s * LANE:(s + 1) * LANE]


def _rotary(x, cosf, sinf):
    return x * cosf + pltpu.roll(x, RET_DK // 2, 1) * sinf


def _post(x, attn_proj, ret, gate_a, gate_r, wbr, wout, nffn, wrh, wrl, br, cnt):
    merged = (jax.nn.sigmoid(gate_a) * attn_proj
              + jax.nn.sigmoid(gate_r) * _dot(ret.astype(_BF16), wbr))
    x1 = x + _dot(merged.astype(_BF16), wout)
    xn2 = _rms(x1, nffn)
    hi = xn2.astype(_BF16)
    lo = (xn2 - hi.astype(_F32)).astype(_BF16)
    logits = _dot(hi, wrh) + (_dot(hi, wrl) + _dot(lo, wrh)) + br
    n = logits.shape[0]
    lt = logits.T[0:ROUTE_ROWS, :]
    row = lax.broadcasted_iota(jnp.int32, (ROUTE_ROWS, n), 0)
    big = jnp.int32(1 << 20)
    neg = jnp.float32(-jnp.inf)
    gl = jnp.where(row < N_GROUPS, lt, neg)
    gmax = jnp.max(gl, axis=0, keepdims=True)
    gexp = jnp.exp(gl - gmax)
    gsum = jnp.sum(gexp, axis=0, keepdims=True)
    pg = gexp / gsum
    g_w = jnp.max(pg, axis=0, keepdims=True)
    g_idx = jnp.min(jnp.where(pg == g_w, row, big), axis=0, keepdims=True)
    e_row = row - N_GROUPS
    emask = (e_row >= 0) & (e_row < N_EXPERTS) & ((e_row >> 3) == g_idx)
    fl = jnp.where(emask, lt, neg)
    fmax = jnp.max(fl, axis=0, keepdims=True)
    fexp = jnp.exp(fl - fmax)
    fsum = jnp.sum(fexp, axis=0, keepdims=True)
    pe = jnp.where(emask, fexp / fsum, -1.0)
    p1 = jnp.max(pe, axis=0, keepdims=True)
    i1 = jnp.min(jnp.where(pe == p1, row, big), axis=0, keepdims=True)
    pe2 = jnp.where(row == i1, -1.0, pe)
    p2 = jnp.max(pe2, axis=0, keepdims=True)
    i2 = jnp.min(jnp.where(pe2 == p2, row, big), axis=0, keepdims=True)
    psum = p1 + p2
    gate1 = g_w * p1 / psum
    gate2 = g_w * p2 / psum
    oh1 = row == i1
    oh2 = row == i2
    c = jnp.where(oh1 | oh2, 1.0, 0.0)
    tt = lax.broadcasted_iota(jnp.int32, (n, n), 0)
    tc = lax.broadcasted_iota(jnp.int32, (n, n), 1)
    upper = jnp.where(tt < tc, 1.0, 0.0).astype(_BF16)
    before = _dot(c.astype(_BF16), upper) + cnt
    rank1 = jnp.sum(jnp.where(oh1, before, 0.0), axis=0, keepdims=True)
    rank2 = jnp.sum(jnp.where(oh2, before, 0.0), axis=0, keepdims=True)
    cnt = cnt + jnp.sum(c, axis=1, keepdims=True)
    r8 = lax.broadcasted_iota(jnp.int32, (ROUTE_OUT, n), 0)
    vals = [(i1 - N_GROUPS).astype(_F32), (i2 - N_GROUPS).astype(_F32), gate1, gate2, rank1, rank2]
    route = jnp.zeros((ROUTE_OUT, n), _F32)
    for k, v in enumerate(vals):
        route = jnp.where(r8 == k, v, route)
    return x1, route, cnt


def _prompt_mixer_kernel(sink_ref, cdec_ref,
                         x_ref, nmix_ref, win_ref, bias_ref, cos_ref, sin_ref, dec_ref,
                         qd_ref, kd_ref, wba_ref, wbr_ref, wout_ref, nffn_ref,
                         wrh_ref, wrl_ref, br_ref,
                         x1_ref, route_ref, knew_ref, vnew_ref, s_ref, cnt_ref,
                         qkv, proj, kctx, vctx, attn_t, ret):
    i = pl.program_id(1)
    last = pl.num_programs(1) - 1

    @pl.when((i == 0) & (pl.program_id(0) == 0))
    def _():
        cnt_ref[...] = jnp.zeros_like(cnt_ref)

    tm = x_ref.shape[1]
    n_sub = tm // WINDOW
    scale = HEAD_DIM ** -0.5

    @pl.when(i == 0)
    def _():
        s_ref[...] = jnp.zeros_like(s_ref)
        kctx[0:WINDOW, :] = jnp.zeros((WINDOW, KV_W), _BF16)
        vctx[0:WINDOW, :] = jnp.zeros((WINDOW, KV_W), _BF16)

    x = x_ref[0]
    xn = _rms(x, nmix_ref[...]).astype(_BF16)
    n_in = win_ref.shape[1]
    panel = 256

    def project(c0):
        res = _dot(xn, win_ref[:, c0:c0 + panel])
        if c0 < OFF_QR:
            qkv[:, c0:c0 + panel] = res
        else:
            proj[:, c0 - OFF_QR:c0 - OFF_QR + panel] = res

    def cols(lo, width):
        return slice(lo - OFF_QR, lo - OFF_QR + width)

    for c0 in range(0, OFF_QR, panel):
        project(c0)
    later_panels = list(range(OFF_QR, n_in, panel))

    @pl.when(i == last)
    def _():
        knew_ref[0] = qkv[tm - WINDOW:tm, OFF_KA:OFF_KA + KV_W]
        vnew_ref[0] = qkv[tm - WINDOW:tm, OFF_VA:OFF_VA + KV_W]

    krow = lax.broadcasted_iota(jnp.int32, (2 * WINDOW, 1), 0)
    for c in range(n_sub):
        r0 = c * WINDOW
        kctx[WINDOW:2 * WINDOW, :] = qkv[r0:r0 + WINDOW, OFF_KA:OFF_KA + KV_W].astype(_BF16)
        vctx[WINDOW:2 * WINDOW, :] = qkv[r0:r0 + WINDOW, OFF_VA:OFF_VA + KV_W].astype(_BF16)
        if c == 0:
            pen = jnp.where((krow < WINDOW) & (i == 0), NEG_INF, 0.0).astype(_F32)
        for h in range(KV_HEADS):
            k_h = kctx[:, h * HEAD_DIM:(h + 1) * HEAD_DIM]
            v_h = vctx[:, h * HEAD_DIM:(h + 1) * HEAD_DIM]
            probs = []
            for g in range(GQA_GROUP):
                hq = h * GQA_GROUP + g
                q = (qkv[r0:r0 + WINDOW, hq * HEAD_DIM:(hq + 1) * HEAD_DIM] * scale).astype(_BF16)
                s = _dot_nt(k_h, q) + bias_ref[hq]
                if c == 0:
                    s = s + pen
                snk = sink_ref[hq]
                m = jnp.maximum(jnp.max(s, axis=0, keepdims=True), snk)
                p = jnp.exp(s - m)
                den = jnp.sum(p, axis=0, keepdims=True) + jnp.exp(snk - m)
                probs.append((p * (1.0 / den)).astype(_BF16))
            groups_left = (n_sub - c) * KV_HEADS - h
            for _ in range(-(-len(later_panels) // groups_left)):
                project(later_panels.pop(0))
            o_t = _dot_tn(v_h, jnp.concatenate(probs, axis=1))
            for g in range(GQA_GROUP):
                hq = h * GQA_GROUP + g
                attn_t[hq * HEAD_DIM:(hq + 1) * HEAD_DIM, r0:r0 + WINDOW] = (
                    o_t[:, g * WINDOW:(g + 1) * WINDOW].astype(_BF16))
        kctx[0:WINDOW, :] = kctx[WINDOW:2 * WINDOW, :]
        vctx[0:WINDOW, :] = vctx[WINDOW:2 * WINDOW, :]

    for c0 in later_panels:
        project(c0)

    cosf = cos_ref[...]
    sinf = sin_ref[...]
    for h in range(RET_HEADS):
        qrot = _rotary(proj[:, cols(OFF_QR + h * RET_DK, RET_DK)], cosf, sinf)
        krot = _rotary(proj[:, cols(OFF_KR + h * RET_DK, RET_DK)], cosf, sinf) * (RET_DK ** -0.5)
        qd = qd_ref[:, h * RET_DK:(h + 1) * RET_DK]
        kd = kd_ref[:, h * RET_DK:(h + 1) * RET_DK]
        for c in range(n_sub):
            r0 = c * RET_CHUNK
            qc = qrot[r0:r0 + RET_CHUNK]
            kc = krot[r0:r0 + RET_CHUNK]
            vc = proj[r0:r0 + RET_CHUNK, cols(OFF_VR + h * RET_DV, RET_DV)].astype(_BF16)
            sc = _dot_nt(qc.astype(_BF16), kc.astype(_BF16))
            s_old = s_ref[0, h]
            cross = _dot((qc * qd).astype(_BF16), s_old.astype(_BF16))
            s_ref[0, h] = s_old * cdec_ref[h] + _dot_tn((kc * kd).astype(_BF16), vc)
            o = _dot((sc * dec_ref[h]).astype(_BF16), vc) + cross
            o = o * lax.rsqrt(jnp.mean(o * o, axis=-1, keepdims=True) + NORM_EPS)
            gr = proj[r0:r0 + RET_CHUNK, cols(OFF_GR + h * RET_DV, RET_DV)]
            ret[r0:r0 + RET_CHUNK, h * RET_DV:(h + 1) * RET_DV] = o * (gr * jax.nn.sigmoid(gr))

    x1, route, cnt = _post(x, _dot_tn(attn_t[...], wba_ref[...]), ret[...],
                           proj[:, cols(OFF_GA, 1024)], proj[:, cols(OFF_GT, 1024)],
                           wbr_ref[...], wout_ref[...], nffn_ref[...],
                           wrh_ref[...], wrl_ref[...], br_ref[...], cnt_ref[:, 0:1])
    _store_rows(x1_ref, x1)
    route_ref[...] = route
    cnt_ref[...] = jnp.broadcast_to(cnt, cnt_ref.shape)


def _prompt_mixer(x, cst, w):
    B, L, D = x.shape
    tm = min(PROMPT_TM, L)
    nb = L // tm
    n_in = w['win'].shape[1]
    step = lambda b, i, *_: (b, i, 0)
    per_b = lambda b, i, *_: (b, 0, 0)
    grid_spec = pltpu.PrefetchScalarGridSpec(
        num_scalar_prefetch=2,
        grid=(B, nb),
        in_specs=[
            pl.BlockSpec((1, tm, D), step),
            _const_spec((1, D)),
            _const_spec((D, n_in)),
            _const_spec((ATTN_HEADS, 2 * WINDOW, WINDOW)),
            pl.BlockSpec((tm, RET_DK), lambda b, i, *_: (i, 0)),
            pl.BlockSpec((tm, RET_DK), lambda b, i, *_: (i, 0)),
            _const_spec((RET_HEADS, RET_CHUNK, RET_CHUNK)),
            _const_spec((RET_CHUNK, RQ_W)),
            _const_spec((RET_CHUNK, RQ_W)),
            _const_spec((ATTN_W, D)),
            _const_spec((RV_W, D)),
            _const_spec((D, D)),
            _const_spec((1, D)),
            _const_spec((D, ROUTE_LANES)),
            _const_spec((D, ROUTE_LANES)),
            _const_spec((1, ROUTE_LANES)),
        ],
        out_specs=[
            pl.BlockSpec((tm * SUB, LANE), lambda b, i, *_: (b * nb + i, 0)),
            pl.BlockSpec((ROUTE_OUT, tm), lambda b, i, *_: (0, b * nb + i)),
            pl.BlockSpec((1, WINDOW, KV_W), per_b),
            pl.BlockSpec((1, WINDOW, KV_W), per_b),
            pl.BlockSpec((1, RET_HEADS, RET_DK, RET_DV), lambda b, i, *_: (b, 0, 0, 0)),
            pl.BlockSpec((ROUTE_ROWS, LANE), lambda b, i, *_: (0, 0)),
        ],
        scratch_shapes=[
            pltpu.VMEM((tm, OFF_QR), _F32),
            pltpu.VMEM((tm, n_in - OFF_QR), _F32),
            pltpu.VMEM((2 * WINDOW, KV_W), _BF16),
            pltpu.VMEM((2 * WINDOW, KV_W), _BF16),
            pltpu.VMEM((ATTN_W, tm), _BF16),
            pltpu.VMEM((tm, RV_W), _F32),
        ],
    )
    assert D == SUB * LANE
    out_shape = [
        jax.ShapeDtypeStruct((B * L * SUB, LANE), _F32),
        jax.ShapeDtypeStruct((ROUTE_OUT, B * L), _F32),
        jax.ShapeDtypeStruct((B, WINDOW, KV_W), _F32),
        jax.ShapeDtypeStruct((B, WINDOW, KV_W), _F32),
        jax.ShapeDtypeStruct((B, RET_HEADS, RET_DK, RET_DV), _F32),
        jax.ShapeDtypeStruct((ROUTE_ROWS, LANE), _F32),
    ]
    return pl.pallas_call(
        _prompt_mixer_kernel,
        grid_spec=grid_spec,
        out_shape=out_shape,
        compiler_params=pltpu.CompilerParams(
            dimension_semantics=("arbitrary", "arbitrary"), vmem_limit_bytes=VMEM_LIMIT),
        name="prompt_mixer",
    )(cst['sink'], cst['cdec_p'],
      x, w['nmix'], w['win'], cst['bias_p'], cst['cos_p'], cst['sin_p'], cst['dec_p'],
      cst['qd_p'], cst['kd_p'], w['wba'], w['wbr'], w['wout'], w['nffn'],
      w['wrh'], w['wrl'], w['br'])


def _inproj_kernel(x_ref, nmix_ref, win_ref, o_ref):
    xn = _rms(x_ref[...], nmix_ref[...]).astype(_BF16)
    o_ref[...] = _dot(xn, win_ref[...])


def _sample_inproj(x2d, w):
    T, D = x2d.shape
    n_in = w['win'].shape[1]
    panel = 512
    return pl.pallas_call(
        _inproj_kernel,
        grid=(n_in // panel,),
        in_specs=[pl.BlockSpec((T, D), lambda j: (0, 0)),
                  pl.BlockSpec((1, D), lambda j: (0, 0)),
                  pl.BlockSpec((D, panel), lambda j: (0, j))],
        out_specs=pl.BlockSpec((T, panel), lambda j: (0, j)),
        out_shape=jax.ShapeDtypeStruct((T, n_in), _F32),
        compiler_params=pltpu.CompilerParams(
            dimension_semantics=("arbitrary",), vmem_limit_bytes=VMEM_LIMIT),
        name="sample_inproj",
    )(x2d, w['nmix'], w['win'])


def _sample_core_kernel(cdec_ref, proj_ref, ck_ref, cv_ref, st_ref, bh_ref, bn_ref, snk_ref,
                        cos_ref, sin_ref, dec_ref, qd_ref, kd_ref,
                        attn_ref, ret_ref, nk_ref, nv_ref, ns_ref):
    G = proj_ref.shape[0]
    ls = proj_ref.shape[1]
    scale = HEAD_DIM ** -0.5
    cosf = cos_ref[...]
    sinf = sin_ref[...]

    def body(b, carry):
        row = proj_ref[b]
        k_new = row[:, OFF_KA:OFF_KA + KV_W]
        v_new = row[:, OFF_VA:OFF_VA + KV_W]
        ck = ck_ref[b]
        cv = cv_ref[b]
        nk_ref[b, 0:WINDOW - ls, :] = ck[ls:WINDOW, :]
        nk_ref[b, WINDOW - ls:WINDOW, :] = k_new
        nv_ref[b, 0:WINDOW - ls, :] = cv[ls:WINDOW, :]
        nv_ref[b, WINDOW - ls:WINDOW, :] = v_new
        ckb = ck.astype(_BF16)
        cvb = cv.astype(_BF16)
        knb = k_new.astype(_BF16)
        vnb = v_new.astype(_BF16)
        heads = [slice(h * HEAD_DIM, (h + 1) * HEAD_DIM) for h in range(KV_HEADS)]
        s1s, s2s = [], []
        for h in range(KV_HEADS):
            q4 = (jnp.concatenate(
                [row[:, (h * GQA_GROUP + g) * HEAD_DIM:(h * GQA_GROUP + g + 1) * HEAD_DIM]
                 for g in range(GQA_GROUP)], axis=0) * scale).astype(_BF16)
            s1s.append(_dot_nt(q4, ckb[:, heads[h]]))
            s2s.append(_dot_nt(q4, knb[:, heads[h]]))
        scs, crosses, vcs = [], [], []
        for h in range(RET_HEADS):
            qrot = _rotary(row[:, OFF_QR + h * RET_DK:OFF_QR + (h + 1) * RET_DK], cosf, sinf)
            krot = _rotary(row[:, OFF_KR + h * RET_DK:OFF_KR + (h + 1) * RET_DK], cosf, sinf) * (RET_DK ** -0.5)
            vc = row[:, OFF_VR + h * RET_DV:OFF_VR + (h + 1) * RET_DV].astype(_BF16)
            qd = qd_ref[:, h * RET_DK:(h + 1) * RET_DK]
            kd = kd_ref[:, h * RET_DK:(h + 1) * RET_DK]
            s_old = st_ref[b, h]
            scs.append(_dot_nt(qrot.astype(_BF16), krot.astype(_BF16)))
            crosses.append(_dot((qrot * qd).astype(_BF16), s_old.astype(_BF16)))
            ns_ref[b, h] = s_old * cdec_ref[h] + _dot_tn((krot * kd).astype(_BF16), vc)
            vcs.append(vc)
        p1s, p2s = [], []
        for h in range(KV_HEADS):
            s1 = s1s[h] + bh_ref[h]
            s2 = s2s[h] + bn_ref[h]
            snk = snk_ref[h]
            m = jnp.maximum(jnp.maximum(jnp.max(s1, axis=-1, keepdims=True),
                                        jnp.max(s2, axis=-1, keepdims=True)), snk)
            p1 = jnp.exp(s1 - m)
            p2 = jnp.exp(s2 - m)
            den = (jnp.sum(p1, axis=-1, keepdims=True) + jnp.sum(p2, axis=-1, keepdims=True)
                   + jnp.exp(snk - m))
            r = 1.0 / den
            p1s.append((p1 * r).astype(_BF16))
            p2s.append((p2 * r).astype(_BF16))
        scb = [(scs[h] * dec_ref[h]).astype(_BF16) for h in range(RET_HEADS)]
        outs = [_dot(p1s[h], cvb[:, heads[h]]) + _dot(p2s[h], vnb[:, heads[h]])
                for h in range(KV_HEADS)]
        rets = [_dot(scb[h], vcs[h]) + crosses[h] for h in range(RET_HEADS)]
        for h in range(KV_HEADS):
            for g in range(GQA_GROUP):
                hq = h * GQA_GROUP + g
                attn_ref[b, :, hq * HEAD_DIM:(hq + 1) * HEAD_DIM] = outs[h][g * ls:(g + 1) * ls]
        for h in range(RET_HEADS):
            o = rets[h]
            o = o * lax.rsqrt(jnp.mean(o * o, axis=-1, keepdims=True) + NORM_EPS)
            gr = row[:, OFF_GR + h * RET_DV:OFF_GR + (h + 1) * RET_DV]
            ret_ref[b, :, h * RET_DV:(h + 1) * RET_DV] = o * (gr * jax.nn.sigmoid(gr))
        return carry

    lax.fori_loop(0, G, body, 0, unroll=SAMPLE_UNROLL)


def _sample_core(proj3, ck, cv, st, cst):
    NB, ls, n_in = proj3.shape
    G = min(SAMPLE_GROUP, NB)
    blk3 = lambda i, *_: (i, 0, 0)
    blk4 = lambda i, *_: (i, 0, 0, 0)
    c2 = lambda i, *_: (0, 0)
    c3 = lambda i, *_: (0, 0, 0)
    ql = GQA_GROUP * ls
    grid_spec = pltpu.PrefetchScalarGridSpec(
        num_scalar_prefetch=1,
        grid=(NB // G,),
        in_specs=[
            pl.BlockSpec((G, ls, n_in), blk3),
            pl.BlockSpec((G, WINDOW, KV_W), blk3),
            pl.BlockSpec((G, WINDOW, KV_W), blk3),
            pl.BlockSpec((G, RET_HEADS, RET_DK, RET_DV), blk4),
            pl.BlockSpec((KV_HEADS, ql, WINDOW), c3),
            pl.BlockSpec((KV_HEADS, ql, ls), c3),
            pl.BlockSpec((KV_HEADS, ql, 1), c3),
            pl.BlockSpec((ls, RET_DK), c2),
            pl.BlockSpec((ls, RET_DK), c2),
            pl.BlockSpec((RET_HEADS, ls, ls), c3),
            pl.BlockSpec((ls, RQ_W), c2),
            pl.BlockSpec((ls, RQ_W), c2),
        ],
        out_specs=[
            pl.BlockSpec((G, ls, ATTN_W), blk3),
            pl.BlockSpec((G, ls, RV_W), blk3),
            pl.BlockSpec((G, WINDOW, KV_W), blk3),
            pl.BlockSpec((G, WINDOW, KV_W), blk3),
            pl.BlockSpec((G, RET_HEADS, RET_DK, RET_DV), blk4),
        ],
    )
    out_shape = [
        jax.ShapeDtypeStruct((NB, ls, ATTN_W), _F32),
        jax.ShapeDtypeStruct((NB, ls, RV_W), _F32),
        jax.ShapeDtypeStruct((NB, WINDOW, KV_W), _F32),
        jax.ShapeDtypeStruct((NB, WINDOW, KV_W), _F32),
        jax.ShapeDtypeStruct((NB, RET_HEADS, RET_DK, RET_DV), _F32),
    ]
    return pl.pallas_call(
        _sample_core_kernel,
        grid_spec=grid_spec,
        out_shape=out_shape,
        compiler_params=pltpu.CompilerParams(
            dimension_semantics=("arbitrary",), vmem_limit_bytes=VMEM_LIMIT),
        name="sample_core",
    )(cst['cdec_s'], proj3, ck, cv, st, cst['bias_hist'], cst['bias_new'], cst['sink_col'],
      cst['cos_s'], cst['sin_s'], cst['dec_s'], cst['qd_s'], cst['kd_s'])


def _sample_post_kernel(x_ref, attn_ref, ret_ref, ga_ref, gt_ref, wba_ref, wbr_ref, wout_ref,
                        nffn_ref, wrh_ref, wrl_ref, br_ref, cnt0_ref, x1_ref, route_ref, cnt_ref):
    x1, route, cnt = _post(x_ref[...], _dot(attn_ref[...].astype(_BF16), wba_ref[...]),
                           ret_ref[...], ga_ref[...], gt_ref[...],
                           wbr_ref[...], wout_ref[...], nffn_ref[...],
                           wrh_ref[...], wrl_ref[...], br_ref[...], cnt0_ref[:, 0:1])
    _store_rows(x1_ref, x1)
    route_ref[...] = route
    cnt_ref[...] = jnp.broadcast_to(cnt, cnt_ref.shape)


def _sample_post(x2d, attn, ret, ga, gt, w, cnt0):
    T, D = x2d.shape
    full = lambda s: pl.BlockSpec(s, lambda i: (0,) * len(s))
    return pl.pallas_call(
        _sample_post_kernel,
        grid=(1,),
        in_specs=[full((T, D)), full((T, ATTN_W)), full((T, RV_W)), full((T, D)), full((T, D)),
                  full((ATTN_W, D)), full((RV_W, D)), full((D, D)), full((1, D)),
                  full((D, ROUTE_LANES)), full((D, ROUTE_LANES)), full((1, ROUTE_LANES)),
                  full((ROUTE_ROWS, LANE))],
        out_specs=[full((T * SUB, LANE)), full((ROUTE_OUT, T)), full((ROUTE_ROWS, LANE))],
        out_shape=[jax.ShapeDtypeStruct((T * SUB, LANE), _F32),
                   jax.ShapeDtypeStruct((ROUTE_OUT, T), _F32),
                   jax.ShapeDtypeStruct((ROUTE_ROWS, LANE), _F32)],
        compiler_params=pltpu.CompilerParams(
            dimension_semantics=("arbitrary",), vmem_limit_bytes=VMEM_LIMIT),
        name="sample_post",
    )(x2d, attn, ret, ga, gt, w['wba'], w['wbr'], w['wout'], w['nffn'], w['wrh'], w['wrl'], w['br'],
      cnt0)


def _dispatch_kernel(dest_ref, xp_ref, xq_ref, xs_ref, sem, *, p_steps):
    i = pl.program_id(0)
    n_tok = pl.num_programs(0) * ROW_TM

    def copy_tile(src):
        def tile_copy(r, d):
            return pltpu.make_async_copy(src.at[r], xs_ref.at[d], sem)

        def start(r, c):
            t = i * ROW_TM + r
            tile_copy(r, dest_ref[t]).start(priority=0)
            tile_copy(r, dest_ref[n_tok + t]).start(priority=1)
            return c

        lax.fori_loop(0, ROW_TM, start, 0, unroll=DMA_UNROLL)

        def wait(r, c):
            tile_copy(0, 0).wait()
            tile_copy(0, 0).wait()
            return c

        lax.fori_loop(0, ROW_TM, wait, 0, unroll=DMA_UNROLL)

    @pl.when(i < p_steps)
    def _():
        copy_tile(xp_ref)

    @pl.when(i >= p_steps)
    def _():
        copy_tile(xq_ref)


def _dispatch(dest, xp3, xq3):
    Tp, Tq = xp3.shape[0], xq3.shape[0]
    assert Tp % ROW_TM == 0 and Tq % ROW_TM == 0
    p_steps = Tp // ROW_TM
    grid_spec = pltpu.PrefetchScalarGridSpec(
        num_scalar_prefetch=1,
        grid=((Tp + Tq) // ROW_TM,),
        in_specs=[
            pl.BlockSpec((ROW_TM, SUB, LANE), lambda i, *_: (jnp.minimum(i, p_steps - 1), 0, 0)),
            pl.BlockSpec((ROW_TM, SUB, LANE), lambda i, *_: (jnp.maximum(i - p_steps, 0), 0, 0)),
        ],
        out_specs=pl.BlockSpec(memory_space=pl.ANY),
        scratch_shapes=[pltpu.SemaphoreType.DMA],
    )
    return pl.pallas_call(
        functools.partial(_dispatch_kernel, p_steps=p_steps),
        grid_spec=grid_spec,
        out_shape=jax.ShapeDtypeStruct((2 * (Tp + Tq), SUB, LANE), _F32),
        compiler_params=pltpu.CompilerParams(dimension_semantics=("arbitrary",)),
        name="moe_dispatch",
    )(dest, xp3, xq3)


def _gmm_kernel(tile_ref, exp_ref, lo_ref, hi_ref, first_ref, chg_ref,
                x_ref, nffn_ref, wg_ref, wu_ref, wd_ref, y_ref, wg_s, wu_s, wd_s):
    m = pl.program_id(0)
    tm = x_ref.shape[0] // SUB

    @pl.when(chg_ref[m] == 1)
    def _():
        wg_s[...] = wg_ref[0].astype(_BF16)
        wu_s[...] = wu_ref[0].astype(_BF16)
        wd_s[...] = wd_ref[0].astype(_BF16)

    lo = lo_ref[m]
    hi = hi_ref[m]

    @pl.when(hi > lo)
    def _():
        rows = tile_ref[m] * tm + lax.broadcasted_iota(jnp.int32, (tm, 1), 0)
        mine = (rows >= lo) & (rows < hi)
        xn = _rms(_load_rows(x_ref, tm), nffn_ref[...])
        x = jnp.where(mine, xn, 0.0).astype(_BF16)
        a = _dot(x, wg_s[...])
        hmid = (a * jax.nn.sigmoid(a)) * _dot(x, wu_s[...])
        y = _dot(hmid.astype(_BF16), wd_s[...])

        @pl.when(first_ref[m] == 1)
        def _():
            _store_rows(y_ref, y)

        @pl.when(first_ref[m] == 0)
        def _():
            _store_rows(y_ref, _load_rows(y_ref, tm) + y)


def _gmm(work, xs2, nffn, wg, wu, wd):
    A = xs2.shape[0] // SUB
    E, D, F = wg.shape
    n_work = work[0].shape[0]
    grid_spec = pltpu.PrefetchScalarGridSpec(
        num_scalar_prefetch=6,
        grid=(n_work,),
        in_specs=[
            pl.BlockSpec((MOE_TM * SUB, LANE), lambda m, t, e, *_: (t[m], 0)),
            pl.BlockSpec((1, D), lambda m, t, e, *_: (0, 0)),
            pl.BlockSpec((1, D, F), lambda m, t, e, *_: (e[m], 0, 0)),
            pl.BlockSpec((1, D, F), lambda m, t, e, *_: (e[m], 0, 0)),
            pl.BlockSpec((1, F, D), lambda m, t, e, *_: (e[m], 0, 0)),
        ],
        out_specs=pl.BlockSpec((MOE_TM * SUB, LANE), lambda m, t, e, *_: (t[m], 0)),
        scratch_shapes=[pltpu.VMEM((D, F), _BF16), pltpu.VMEM((D, F), _BF16),
                        pltpu.VMEM((F, D), _BF16)],
    )
    return pl.pallas_call(
        _gmm_kernel,
        grid_spec=grid_spec,
        out_shape=jax.ShapeDtypeStruct((A * SUB, LANE), _F32),
        compiler_params=pltpu.CompilerParams(
            dimension_semantics=("arbitrary",), vmem_limit_bytes=VMEM_LIMIT),
        name="moe_gmm",
    )(*work, xs2, nffn, wg, wu, wd)


def _combine_kernel(dest_ref, x1_ref, route_ref, nfin_ref, yb_ref, o_ref, buf, sems, *, tok0, n_tok):
    i = pl.program_id(0)
    n_steps = pl.num_programs(0)
    tm = o_ref.shape[0]

    def tile_copy(d, slot, k, r):
        rows = pl.ds(pl.multiple_of(r * SUB, SUB), SUB)
        return pltpu.make_async_copy(yb_ref.at[d], buf.at[slot, k, rows], sems.at[slot])

    def issue(step, slot):
        base = tok0 + step * tm

        def start(r, c):
            t = base + r
            tile_copy(dest_ref[t], slot, 0, r).start(priority=0)
            tile_copy(dest_ref[n_tok + t], slot, 1, r).start(priority=1)
            return c

        lax.fori_loop(0, tm, start, 0, unroll=DMA_UNROLL)

    @pl.when(i == 0)
    def _():
        issue(0, 0)

    @pl.when(i + 1 < n_steps)
    def _():
        issue(i + 1, (i + 1) % 2)

    slot = i % 2

    def wait(r, c):
        tile_copy(0, slot, 0, 0).wait()
        tile_copy(0, slot, 1, 0).wait()
        return c

    lax.fori_loop(0, tm, wait, 0, unroll=DMA_UNROLL)
    rt = jnp.concatenate([route_ref[...], jnp.zeros((LANE - ROUTE_OUT, tm), _F32)], axis=0).T
    g0 = rt[:, 2:3]
    g1 = rt[:, 3:4]
    y0 = _load_rows(buf.at[slot, 0], tm)
    y1 = _load_rows(buf.at[slot, 1], tm)
    y = _load_rows(x1_ref, tm) + (y0 * g0 + y1 * g1)
    o_ref[...] = _rms(y, nfin_ref[...])


def _combine(dest, x1_2, route, nfin, yb3, tok0):
    T = x1_2.shape[0] // SUB
    D = SUB * LANE
    tm = min(ROW_TM, T)
    grid_spec = pltpu.PrefetchScalarGridSpec(
        num_scalar_prefetch=1,
        grid=(T // tm,),
        in_specs=[
            pl.BlockSpec((tm * SUB, LANE), lambda i, *_: (i, 0)),
            pl.BlockSpec((ROUTE_OUT, tm), lambda i, *_: (0, i)),
            pl.BlockSpec((1, D), lambda i, *_: (0, 0)),
            pl.BlockSpec(memory_space=pl.ANY),
        ],
        out_specs=pl.BlockSpec((tm, D), lambda i, *_: (i, 0)),
        scratch_shapes=[pltpu.VMEM((2, 2, tm * SUB, LANE), _F32), pltpu.SemaphoreType.DMA((2,))],
    )
    return pl.pallas_call(
        functools.partial(_combine_kernel, tok0=tok0, n_tok=dest.shape[0] // 2),
        grid_spec=grid_spec,
        out_shape=jax.ShapeDtypeStruct((T, D), _F32),
        compiler_params=pltpu.CompilerParams(
            dimension_semantics=("arbitrary",), vmem_limit_bytes=VMEM_LIMIT),
        name="moe_combine",
    )(dest, x1_2, route, nfin, yb3)


def _routing_tables(route, counts, n_tiles):
    experts = route[0:2].astype(jnp.int32)
    ranks = route[4:6].astype(jnp.int32)
    A = experts.size
    ids = jnp.arange(N_EXPERTS, dtype=jnp.int32)
    ends = jnp.cumsum(counts)
    starts = ends - counts
    dest = ranks + jnp.sum(jnp.where(experts[..., None] == ids, starts, 0), axis=-1)
    dest = dest.reshape(-1)
    tile_starts = jnp.arange(n_tiles, dtype=jnp.int32) * MOE_TM
    pos_t = jnp.arange(n_tiles, dtype=jnp.int32) + jnp.sum(starts[None, :] < tile_starts[:, None], axis=1)
    pos_e = ids + jnp.sum(tile_starts[None, :] <= starts[:, None], axis=1)
    slots = jnp.arange(n_tiles + N_EXPERTS, dtype=jnp.int32)
    pts = (jnp.sum(jnp.where(pos_t[None, :] == slots[:, None], tile_starts[None, :], 0), axis=1)
           + jnp.sum(jnp.where(pos_e[None, :] == slots[:, None], starts[None, :], 0), axis=1))
    lo = pts.astype(jnp.int32)
    hi = jnp.concatenate([lo[1:], jnp.array([A], jnp.int32)])
    tile = jnp.minimum(lo // MOE_TM, n_tiles - 1).astype(jnp.int32)
    expert = jnp.minimum(jnp.sum(ends[None, :] <= lo[:, None], axis=1), N_EXPERTS - 1).astype(jnp.int32)
    nonempty = hi > lo
    first = (nonempty & (lo % MOE_TM == 0)).astype(jnp.int32)
    chg = jnp.concatenate([jnp.ones((1,), jnp.int32), (expert[1:] != expert[:-1]).astype(jnp.int32)])
    return dest.astype(jnp.int32), (tile, expert, lo, hi, first, chg)


def _bucket_table(lq, lk):
    dist = np.arange(lq)[:, None] + WINDOW - np.arange(lk)[None, :]
    band = (dist >= 0) & (dist < WINDOW)
    d = np.clip(dist, 0, WINDOW - 1)
    max_exact = N_BUCKETS // 2
    d_f = np.maximum(d, 1).astype(np.float32)
    large = max_exact + (np.log(d_f / max_exact) / math.log(MAX_DISTANCE / max_exact)
                         * (N_BUCKETS - max_exact)).astype(np.int32)
    large = np.minimum(large, N_BUCKETS - 1)
    return np.where(d < max_exact, d, large).astype(np.int32), band


def _bias_table(rb, lq, lk):
    bkt, band = _bucket_table(lq, lk)
    onehot = jnp.asarray(bkt)[None, :, :] == jnp.arange(N_BUCKETS, dtype=jnp.int32)[:, None, None]
    bias = jnp.sum(jnp.where(onehot[:, None], rb[:, :, None, None], 0.0), axis=0)
    return jnp.where(jnp.asarray(band)[None], bias, NEG_INF)


def _decay_tables(C):
    log_gamma = jnp.log(1.0 - 2.0 ** (-5.0 - jnp.arange(RET_HEADS, dtype=_F32)))
    idx = jnp.arange(C, dtype=_F32)
    diff = idx[:, None] - idx[None, :]
    decay_in = jnp.where((diff >= 0)[..., None],
                         jnp.exp(jnp.maximum(diff, 0.0)[..., None] * log_gamma), 0.0)
    q_dec = jnp.exp((idx + 1.0)[:, None] * log_gamma)
    k_dec = jnp.exp((C - 1.0 - idx)[:, None] * log_gamma)
    c_dec = jnp.exp(C * log_gamma)
    dec = jnp.transpose(decay_in, (2, 0, 1))
    qd = jnp.repeat(q_dec, RET_DK, axis=1)
    kd = jnp.repeat(k_dec, RET_DK, axis=1)
    return dec, qd, kd, c_dec


def _rope_tables(pos):
    half = RET_DK // 2
    inv = ROPE_BASE ** (-jnp.arange(half, dtype=_F32) * 2.0 / RET_DK)
    ang = pos.astype(_F32)[:, None] * inv[None, :]
    cos = jnp.cos(ang)
    sin = jnp.sin(ang)
    return jnp.concatenate([cos, cos], axis=1), jnp.concatenate([-sin, sin], axis=1)


def _constants(rel_bias, attn_sink, L, ls):
    cst = {}
    rb = rel_bias.astype(_F32)
    cst['bias_p'] = jnp.transpose(_bias_table(rb, WINDOW, 2 * WINDOW), (0, 2, 1))
    cst['sink'] = attn_sink.astype(_F32)
    cst['cos_p'], cst['sin_p'] = _rope_tables(jnp.arange(L))
    cst['dec_p'], cst['qd_p'], cst['kd_p'], cst['cdec_p'] = _decay_tables(min(RET_CHUNK, L))
    bias = _bias_table(rb, ls, WINDOW + ls).reshape(KV_HEADS, GQA_GROUP * ls, WINDOW + ls)
    cst['bias_hist'] = bias[:, :, :WINDOW]
    cst['bias_new'] = bias[:, :, WINDOW:]
    cst['sink_col'] = jnp.repeat(attn_sink.astype(_F32).reshape(KV_HEADS, GQA_GROUP), ls,
                                 axis=1)[..., None]
    cst['cos_s'], cst['sin_s'] = _rope_tables(PAST_LEN + jnp.arange(ls))
    cst['dec_s'], cst['qd_s'], cst['kd_s'], cst['cdec_s'] = _decay_tables(min(RET_CHUNK, ls))
    return cst


def _layer_weights(layer, norm_mix, w_in, w_branch_attn, w_branch_ret, w_out, norm_ffn,
                   w_router_group, b_router_group, w_router_expert, b_router_expert):
    D = w_in.shape[1]
    wr = jnp.concatenate([w_router_group[layer].astype(_F32), w_router_expert[layer].astype(_F32)], axis=1)
    wr = jnp.pad(wr, ((0, 0), (0, ROUTE_LANES - wr.shape[1])))
    wrh = wr.astype(_BF16)
    wrl = (wr - wrh.astype(_F32)).astype(_BF16)
    br = jnp.concatenate([b_router_group[layer].astype(_F32), b_router_expert[layer].astype(_F32)])
    br = jnp.pad(br, (0, ROUTE_LANES - br.shape[0]))[None, :]
    return {
        'nmix': norm_mix[layer].astype(_F32)[None, :],
        'win': w_in[layer].astype(_BF16),
        'wba': w_branch_attn[layer].astype(_BF16),
        'wbr': w_branch_ret[layer].astype(_BF16),
        'wout': w_out[layer].astype(_BF16),
        'nffn': norm_ffn[layer].astype(_F32)[None, :],
        'wrh': wrh, 'wrl': wrl, 'br': br,
    }


def kernel(x_prompt, x_sample, cache_k, cache_v, state_ret, norm_mix, w_in, attn_sink, rel_bias,
           w_branch_attn, w_branch_ret, w_out, norm_ffn, w_router_group, b_router_group,
           w_router_expert, b_router_expert, w_gate, w_up, w_down, norm_final):
    depth = w_in.shape[0]
    assert depth == 1, "the final norm is fused into the MoE combine of the only layer"
    B, L, D = x_prompt.shape
    NB, ls, _ = x_sample.shape
    Tp, Ts = B * L, NB * ls
    nfin = norm_final.astype(_F32)[None, :]
    yp, ys = x_prompt, x_sample
    pk, pv, ps, sk, sv, ss = [], [], [], [], [], []
    for layer in range(depth):
        w = _layer_weights(layer, norm_mix, w_in, w_branch_attn, w_branch_ret, w_out, norm_ffn,
                           w_router_group, b_router_group, w_router_expert, b_router_expert)
        cst = _constants(rel_bias, attn_sink[layer], L, ls)
        x1p, routep, k1, v1, s1, cnt_p = _prompt_mixer(yp, cst, w)
        ys2 = ys.reshape(Ts, D)
        proj = _sample_inproj(ys2, w)
        attn_s, ret_s, k2, v2, s2 = _sample_core(
            proj.reshape(NB, ls, -1),
            cache_k[layer].reshape(NB, WINDOW, KV_W), cache_v[layer].reshape(NB, WINDOW, KV_W),
            state_ret[layer], cst)
        x1s, routes, cnt_all = _sample_post(
            ys2, attn_s.reshape(Ts, ATTN_W), ret_s.reshape(Ts, RV_W),
            proj[:, OFF_GA:OFF_GA + D], proj[:, OFF_GT:OFF_GT + D], w, cnt_p)
        n_rows = 2 * (Tp + Ts)
        assert n_rows % MOE_TM == 0
        counts = cnt_all[N_GROUPS:N_GROUPS + N_EXPERTS, 0].astype(jnp.int32)
        dest, work = _routing_tables(jnp.concatenate([routep, routes], axis=1), counts,
                                     n_rows // MOE_TM)
        xs3 = _dispatch(dest, x1p.reshape(Tp, SUB, LANE), x1s.reshape(Ts, SUB, LANE))
        yb2 = _gmm(work, xs3.reshape(n_rows * SUB, LANE), w['nffn'],
                   w_gate[layer], w_up[layer], w_down[layer])
        yb3 = yb2.reshape(n_rows, SUB, LANE)
        yp = _combine(dest, x1p, routep, nfin, yb3, 0).reshape(B, L, D)
        ys = _combine(dest, x1s, routes, nfin, yb3, Tp).reshape(NB, ls, D)
        pk.append(k1.reshape(B, WINDOW, KV_HEADS, HEAD_DIM))
        pv.append(v1.reshape(B, WINDOW, KV_HEADS, HEAD_DIM))
        ps.append(s1)
        sk.append(k2.reshape(NB, WINDOW, KV_HEADS, HEAD_DIM))
        sv.append(v2.reshape(NB, WINDOW, KV_HEADS, HEAD_DIM))
        ss.append(s2)
    return (yp, ys, jnp.stack(pk), jnp.stack(pv), jnp.stack(ps),
            jnp.stack(sk), jnp.stack(sv), jnp.stack(ss))
```

```python
import functools
import math

import jax
import jax.numpy as jnp
import numpy as np
from jax import lax
from jax.experimental import pallas as pl
from jax.experimental.pallas import tpu as pltpu

HEAD_DIM = 64
KV_HEADS = 4
GQA_GROUP = 4
ATTN_HEADS = KV_HEADS * GQA_GROUP
WINDOW = 128
N_BUCKETS = 32
MAX_DISTANCE = 128
RET_HEADS = 4
RET_DK = 128
RET_DV = 256
RET_CHUNK = 128
ROPE_BASE = 10000.0
N_GROUPS = 4
EXPERTS_PER_GROUP = 8
N_EXPERTS = N_GROUPS * EXPERTS_PER_GROUP
EXPERT_FF = 512
NORM_EPS = 1e-6
NEG_INF = -1e30
PAST_LEN = 16384

ATTN_W = ATTN_HEADS * HEAD_DIM
KV_W = KV_HEADS * HEAD_DIM
RQ_W = RET_HEADS * RET_DK
RV_W = RET_HEADS * RET_DV
OFF_QA = 0
OFF_KA = OFF_QA + ATTN_W
OFF_VA = OFF_KA + KV_W
OFF_QR = OFF_VA + KV_W
OFF_KR = OFF_QR + RQ_W
OFF_VR = OFF_KR + RQ_W
OFF_GR = OFF_VR + RV_W
OFF_GA = OFF_GR + RV_W
OFF_GT = OFF_GA + 1024
ROUTE_LANES = 128
ROUTE_ROWS = 40
ROUTE_OUT = 8

LANE = 128
SUB = 8
POST_PARTS = 2
PROMPT_TM = 512
SAMPLE_GROUP = 8
SAMPLE_UNROLL = 2
MOE_TM = 256
GMM_PARTS = 2
ROW_TM = 256
DMA_UNROLL = 8
VMEM_LIMIT = 60 * 1024 * 1024

_F32 = jnp.float32
_BF16 = jnp.bfloat16


def _const_spec(shape):
    nd = len(shape)
    return pl.BlockSpec(shape, lambda *_: (0,) * nd, pipeline_mode=pl.Buffered(1))


def _rms(x, gain):
    return x * lax.rsqrt(jnp.mean(x * x, axis=-1, keepdims=True) + NORM_EPS) * gain


def _dot(a, b):
    return jnp.dot(a, b, preferred_element_type=_F32)


def _dot_nt(a, b):
    return lax.dot_general(a, b, (((1,), (1,)), ((), ())), preferred_element_type=_F32)


def _dot_tn(a, b):
    return lax.dot_general(a, b, (((0,), (0,)), ((), ())), preferred_element_type=_F32)


def _load_rows(ref, n):
    return jnp.concatenate([ref[pl.ds(s, n, stride=SUB), :] for s in range(SUB)], axis=1)


def _store_rows(ref, val):
    n = val.shape[0]
    for s in range(SUB):
        ref[pl.ds(s, n, stride=SUB), :] = val[:, s * LANE:(s + 1) * LANE]


def _rotary(x, cosf, sinf):
    return x * cosf + pltpu.roll(x, RET_DK // 2, 1) * sinf


def _post(xs, attn_projs, rets, gates_a, gates_r, wbr, wout, nffn, wrh, wrl, br, cnt):
    parts = range(len(xs))
    ret_projs = [_dot(rets[j].astype(_BF16), wbr) for j in parts]
    x1s = []
    for j in parts:
        merged = jax.nn.sigmoid(gates_a[j]) * attn_projs[j] + jax.nn.sigmoid(gates_r[j]) * ret_projs[j]
        x1s.append(xs[j] + _dot(merged.astype(_BF16), wout))
    logit_parts = []
    for j in parts:
        xn2 = _rms(x1s[j], nffn)
        hi = xn2.astype(_BF16)
        lo = (xn2 - hi.astype(_F32)).astype(_BF16)
        logit_parts.append(_dot(hi, wrh) + (_dot(hi, wrl) + _dot(lo, wrh)) + br)
    x1 = x1s[0] if len(x1s) == 1 else jnp.concatenate(x1s, axis=0)
    logits = logit_parts[0] if len(x1s) == 1 else jnp.concatenate(logit_parts, axis=0)
    n = logits.shape[0]
    lt = logits.T[0:ROUTE_ROWS, :]
    row = lax.broadcasted_iota(jnp.int32, (ROUTE_ROWS, n), 0)
    big = jnp.int32(1 << 20)
    neg = jnp.float32(-jnp.inf)
    gl = jnp.where(row < N_GROUPS, lt, neg)
    gmax = jnp.max(gl, axis=0, keepdims=True)
    gexp = jnp.exp(gl - gmax)
    gsum = jnp.sum(gexp, axis=0, keepdims=True)
    pg = gexp / gsum
    g_w = jnp.max(pg, axis=0, keepdims=True)
    g_idx = jnp.min(jnp.where(pg == g_w, row, big), axis=0, keepdims=True)
    e_row = row - N_GROUPS
    emask = (e_row >= 0) & (e_row < N_EXPERTS) & ((e_row >> 3) == g_idx)
    fl = jnp.where(emask, lt, neg)
    fmax = jnp.max(fl, axis=0, keepdims=True)
    fexp = jnp.exp(fl - fmax)
    fsum = jnp.sum(fexp, axis=0, keepdims=True)
    pe = jnp.where(emask, fexp / fsum, -1.0)
    p1 = jnp.max(pe, axis=0, keepdims=True)
    i1 = jnp.min(jnp.where(pe == p1, row, big), axis=0, keepdims=True)
    pe2 = jnp.where(row == i1, -1.0, pe)
    p2 = jnp.max(pe2, axis=0, keepdims=True)
    i2 = jnp.min(jnp.where(pe2 == p2, row, big), axis=0, keepdims=True)
    psum = p1 + p2
    gate1 = g_w * p1 / psum
    gate2 = g_w * p2 / psum
    oh1 = row == i1
    oh2 = row == i2
    c = jnp.where(oh1 | oh2, 1.0, 0.0)
    tt = lax.broadcasted_iota(jnp.int32, (n, n), 0)
    tc = lax.broadcasted_iota(jnp.int32, (n, n), 1)
    upper = jnp.where(tt < tc, 1.0, 0.0).astype(_BF16)
    before = _dot(c.astype(_BF16), upper) + cnt
    rank1 = jnp.sum(jnp.where(oh1, before, 0.0), axis=0, keepdims=True)
    rank2 = jnp.sum(jnp.where(oh2, before, 0.0), axis=0, keepdims=True)
    cnt = cnt + jnp.sum(c, axis=1, keepdims=True)
    r8 = lax.broadcasted_iota(jnp.int32, (ROUTE_OUT, n), 0)
    vals = [(i1 - N_GROUPS).astype(_F32), (i2 - N_GROUPS).astype(_F32), gate1, gate2, rank1, rank2]
    route = jnp.zeros((ROUTE_OUT, n), _F32)
    for k, v in enumerate(vals):
        route = jnp.where(r8 == k, v, route)
    return x1, route, cnt


def _prompt_mixer_kernel(sink_ref, cdec_ref,
                         x_ref, nmix_ref, win_ref, bias_ref, cos_ref, sin_ref, dec_ref,
                         qd_ref, kd_ref, wba_ref, wbr_ref, wout_ref, nffn_ref,
                         wrh_ref, wrl_ref, br_ref,
                         x1_ref, route_ref, knew_ref, vnew_ref, s_ref, cnt_ref,
                         qkv, proj, kctx, vctx, attn_t, ret):
    i = pl.program_id(1)
    last = pl.num_programs(1) - 1

    @pl.when((i == 0) & (pl.program_id(0) == 0))
    def _():
        cnt_ref[...] = jnp.zeros_like(cnt_ref)

    tm = x_ref.shape[1]
    n_sub = tm // WINDOW
    scale = HEAD_DIM ** -0.5

    @pl.when(i == 0)
    def _():
        s_ref[...] = jnp.zeros_like(s_ref)
        kctx[0:WINDOW, :] = jnp.zeros((WINDOW, KV_W), _BF16)
        vctx[0:WINDOW, :] = jnp.zeros((WINDOW, KV_W), _BF16)

    x = x_ref[0]
    xn = _rms(x, nmix_ref[...]).astype(_BF16)
    n_in = win_ref.shape[1]
    panel = 256

    def project(c0):
        res = _dot(xn, win_ref[:, c0:c0 + panel])
        if c0 < OFF_QR:
            qkv[:, c0:c0 + panel] = res
        else:
            proj[:, c0 - OFF_QR:c0 - OFF_QR + panel] = res

    def cols(lo, width):
        return slice(lo - OFF_QR, lo - OFF_QR + width)

    for c0 in range(0, OFF_QR, panel):
        project(c0)
    later_panels = list(range(OFF_QR, n_in, panel))

    @pl.when(i == last)
    def _():
        knew_ref[0] = qkv[tm - WINDOW:tm, OFF_KA:OFF_KA + KV_W]
        vnew_ref[0] = qkv[tm - WINDOW:tm, OFF_VA:OFF_VA + KV_W]

    krow = lax.broadcasted_iota(jnp.int32, (2 * WINDOW, 1), 0)
    for c in range(n_sub):
        r0 = c * WINDOW
        kctx[WINDOW:2 * WINDOW, :] = qkv[r0:r0 + WINDOW, OFF_KA:OFF_KA + KV_W].astype(_BF16)
        vctx[WINDOW:2 * WINDOW, :] = qkv[r0:r0 + WINDOW, OFF_VA:OFF_VA + KV_W].astype(_BF16)
        if c == 0:
            pen = jnp.where((krow < WINDOW) & (i == 0), NEG_INF, 0.0).astype(_F32)
        for h in range(KV_HEADS):
            k_h = kctx[:, h * HEAD_DIM:(h + 1) * HEAD_DIM]
            v_h = vctx[:, h * HEAD_DIM:(h + 1) * HEAD_DIM]
            probs = []
            for g in range(GQA_GROUP):
                hq = h * GQA_GROUP + g
                q = (qkv[r0:r0 + WINDOW, hq * HEAD_DIM:(hq + 1) * HEAD_DIM] * scale).astype(_BF16)
                s = _dot_nt(k_h, q) + bias_ref[hq]
                if c == 0:
                    s = s + pen
                snk = sink_ref[hq]
                m = jnp.maximum(jnp.max(s, axis=0, keepdims=True), snk)
                p = jnp.exp(s - m)
                den = jnp.sum(p, axis=0, keepdims=True) + jnp.exp(snk - m)
                probs.append((p * (1.0 / den)).astype(_BF16))
            groups_left = (n_sub - c) * KV_HEADS - h
            for _ in range(-(-len(later_panels) // groups_left)):
                project(later_panels.pop(0))
            o_t = _dot_tn(v_h, jnp.concatenate(probs, axis=1))
            for g in range(GQA_GROUP):
                hq = h * GQA_GROUP + g
                attn_t[hq * HEAD_DIM:(hq + 1) * HEAD_DIM, r0:r0 + WINDOW] = (
                    o_t[:, g * WINDOW:(g + 1) * WINDOW].astype(_BF16))
        kctx[0:WINDOW, :] = kctx[WINDOW:2 * WINDOW, :]
        vctx[0:WINDOW, :] = vctx[WINDOW:2 * WINDOW, :]

    for c0 in later_panels:
        project(c0)

    for c in range(n_sub):
        r0 = c * RET_CHUNK
        cosf = cos_ref[r0:r0 + RET_CHUNK, :]
        sinf = sin_ref[r0:r0 + RET_CHUNK, :]
        for h in range(RET_HEADS):
            qc = _rotary(proj[r0:r0 + RET_CHUNK, cols(OFF_QR + h * RET_DK, RET_DK)], cosf, sinf)
            kc = _rotary(proj[r0:r0 + RET_CHUNK, cols(OFF_KR + h * RET_DK, RET_DK)], cosf, sinf) * (RET_DK ** -0.5)
            qd = qd_ref[:, h * RET_DK:(h + 1) * RET_DK]
            kd = kd_ref[:, h * RET_DK:(h + 1) * RET_DK]
            vc = proj[r0:r0 + RET_CHUNK, cols(OFF_VR + h * RET_DV, RET_DV)].astype(_BF16)
            sc = _dot_nt(qc.astype(_BF16), kc.astype(_BF16))
            s_old = s_ref[0, h]
            cross = _dot((qc * qd).astype(_BF16), s_old.astype(_BF16))
            s_ref[0, h] = s_old * cdec_ref[h] + _dot_tn((kc * kd).astype(_BF16), vc)
            o = _dot((sc * dec_ref[h]).astype(_BF16), vc) + cross
            o = o * lax.rsqrt(jnp.mean(o * o, axis=-1, keepdims=True) + NORM_EPS)
            gr = proj[r0:r0 + RET_CHUNK, cols(OFF_GR + h * RET_DV, RET_DV)]
            ret[r0:r0 + RET_CHUNK, h * RET_DV:(h + 1) * RET_DV] = o * (gr * jax.nn.sigmoid(gr))

    pn = tm // POST_PARTS
    rows = [slice(j * pn, (j + 1) * pn) for j in range(POST_PARTS)]
    x1, route, cnt = _post([x[r] for r in rows],
                           [_dot_tn(attn_t[:, r], wba_ref[...]) for r in rows],
                           [ret[r, :] for r in rows],
                           [proj[r, cols(OFF_GA, 1024)] for r in rows],
                           [proj[r, cols(OFF_GT, 1024)] for r in rows],
                           wbr_ref[...], wout_ref[...], nffn_ref[...],
                           wrh_ref[...], wrl_ref[...], br_ref[...], cnt_ref[:, 0:1])
    _store_rows(x1_ref, x1)
    route_ref[...] = route
    cnt_ref[...] = jnp.broadcast_to(cnt, cnt_ref.shape)


def _prompt_mixer(x, cst, w):
    B, L, D = x.shape
    tm = min(PROMPT_TM, L)
    nb = L // tm
    n_in = w['win'].shape[1]
    step = lambda b, i, *_: (b, i, 0)
    per_b = lambda b, i, *_: (b, 0, 0)

    grid_spec = pltpu.PrefetchScalarGridSpec(
        num_scalar_prefetch=2,
        grid=(B, nb),
        in_specs=[
            pl.BlockSpec((1, tm, D), step),
            _const_spec((1, D)),
            _const_spec((D, n_in)),
            _const_spec((ATTN_HEADS, 2 * WINDOW, WINDOW)),
            pl.BlockSpec((tm, RET_DK), lambda b, i, *_: (i, 0)),
            pl.BlockSpec((tm, RET_DK), lambda b, i, *_: (i, 0)),
            _const_spec((RET_HEADS, RET_CHUNK, RET_CHUNK)),
            _const_spec((RET_CHUNK, RQ_W)),
            _const_spec((RET_CHUNK, RQ_W)),
            _const_spec((ATTN_W, D)),
            _const_spec((RV_W, D)),
            _const_spec((D, D)),
            _const_spec((1, D)),
            _const_spec((D, ROUTE_LANES)),
            _const_spec((D, ROUTE_LANES)),
            _const_spec((1, ROUTE_LANES)),
        ],
        out_specs=[
            pl.BlockSpec((tm * SUB, LANE), lambda b, i, *_: (b * nb + i, 0)),
            pl.BlockSpec((ROUTE_OUT, tm), lambda b, i, *_: (0, b * nb + i)),
            pl.BlockSpec((1, WINDOW, KV_W), per_b),
            pl.BlockSpec((1, WINDOW, KV_W), per_b),
            pl.BlockSpec((1, RET_HEADS, RET_DK, RET_DV), lambda b, i, *_: (b, 0, 0, 0)),
            pl.BlockSpec((ROUTE_ROWS, LANE), lambda b, i, *_: (0, 0)),
        ],
        scratch_shapes=[
            pltpu.VMEM((tm, OFF_QR), _F32),
            pltpu.VMEM((tm, n_in - OFF_QR), _F32),
            pltpu.VMEM((2 * WINDOW, KV_W), _BF16),
            pltpu.VMEM((2 * WINDOW, KV_W), _BF16),
            pltpu.VMEM((ATTN_W, tm), _BF16),
            pltpu.VMEM((tm, RV_W), _F32),
        ],
    )
    assert D == SUB * LANE
    out_shape = [
        jax.ShapeDtypeStruct((B * L * SUB, LANE), _F32),
        jax.ShapeDtypeStruct((ROUTE_OUT, B * L), _F32),
        jax.ShapeDtypeStruct((B, WINDOW, KV_W), _F32),
        jax.ShapeDtypeStruct((B, WINDOW, KV_W), _F32),
        jax.ShapeDtypeStruct((B, RET_HEADS, RET_DK, RET_DV), _F32),
        jax.ShapeDtypeStruct((ROUTE_ROWS, LANE), _F32),
    ]
    return pl.pallas_call(
        _prompt_mixer_kernel,
        grid_spec=grid_spec,
        out_shape=out_shape,
        compiler_params=pltpu.CompilerParams(
            dimension_semantics=("arbitrary", "arbitrary"), vmem_limit_bytes=VMEM_LIMIT),
        name="prompt_mixer",
    )(cst['sink'], cst['cdec_p'],
      x, w['nmix'], w['win'], cst['bias_p'], cst['cos_p'], cst['sin_p'], cst['dec_p'],
      cst['qd_p'], cst['kd_p'], w['wba'], w['wbr'], w['wout'], w['nffn'],
      w['wrh'], w['wrl'], w['br'])


def _inproj_kernel(x_ref, nmix_ref, win_ref, o_ref):
    xn = _rms(x_ref[...], nmix_ref[...]).astype(_BF16)
    o_ref[...] = _dot(xn, win_ref[...])


def _sample_inproj(x2d, w):
    T, D = x2d.shape
    n_in = w['win'].shape[1]
    panel = 512
    return pl.pallas_call(
        _inproj_kernel,
        grid=(n_in // panel,),
        in_specs=[pl.BlockSpec((T, D), lambda j: (0, 0)),
                  pl.BlockSpec((1, D), lambda j: (0, 0)),
                  pl.BlockSpec((D, panel), lambda j: (0, j))],
        out_specs=pl.BlockSpec((T, panel), lambda j: (0, j)),
        out_shape=jax.ShapeDtypeStruct((T, n_in), _F32),
        compiler_params=pltpu.CompilerParams(
            dimension_semantics=("arbitrary",), vmem_limit_bytes=VMEM_LIMIT),
        name="sample_inproj",
    )(x2d, w['nmix'], w['win'])


def _sample_core_kernel(cdec_ref, proj_ref, ck_ref, cv_ref, st_ref, bh_ref, bn_ref, snk_ref,
                        cos_ref, sin_ref, dec_ref, qd_ref, kd_ref,
                        attn_ref, ret_ref, nk_ref, nv_ref, ns_ref):
    G = proj_ref.shape[0]
    ls = proj_ref.shape[1]
    scale = HEAD_DIM ** -0.5
    cosf = cos_ref[...]
    sinf = sin_ref[...]

    def body(b, carry):
        row = proj_ref[b]
        k_new = row[:, OFF_KA:OFF_KA + KV_W]
        v_new = row[:, OFF_VA:OFF_VA + KV_W]
        ck = ck_ref[b]
        cv = cv_ref[b]
        nk_ref[b, 0:WINDOW - ls, :] = ck[ls:WINDOW, :]
        nk_ref[b, WINDOW - ls:WINDOW, :] = k_new
        nv_ref[b, 0:WINDOW - ls, :] = cv[ls:WINDOW, :]
        nv_ref[b, WINDOW - ls:WINDOW, :] = v_new
        ckb = ck.astype(_BF16)
        cvb = cv.astype(_BF16)
        knb = k_new.astype(_BF16)
        vnb = v_new.astype(_BF16)
        heads = [slice(h * HEAD_DIM, (h + 1) * HEAD_DIM) for h in range(KV_HEADS)]
        s1s, s2s = [], []
        for h in range(KV_HEADS):
            q4 = (jnp.concatenate(
                [row[:, (h * GQA_GROUP + g) * HEAD_DIM:(h * GQA_GROUP + g + 1) * HEAD_DIM]
                 for g in range(GQA_GROUP)], axis=0) * scale).astype(_BF16)
            s1s.append(_dot_nt(q4, ckb[:, heads[h]]))
            s2s.append(_dot_nt(q4, knb[:, heads[h]]))
        scs, crosses, vcs = [], [], []
        for h in range(RET_HEADS):
            qrot = _rotary(row[:, OFF_QR + h * RET_DK:OFF_QR + (h + 1) * RET_DK], cosf, sinf)
            krot = _rotary(row[:, OFF_KR + h * RET_DK:OFF_KR + (h + 1) * RET_DK], cosf, sinf) * (RET_DK ** -0.5)
            vc = row[:, OFF_VR + h * RET_DV:OFF_VR + (h + 1) * RET_DV].astype(_BF16)
            qd = qd_ref[:, h * RET_DK:(h + 1) * RET_DK]
            kd = kd_ref[:, h * RET_DK:(h + 1) * RET_DK]
            s_old = st_ref[b, h]
            scs.append(_dot_nt(qrot.astype(_BF16), krot.astype(_BF16)))
            crosses.append(_dot((qrot * qd).astype(_BF16), s_old.astype(_BF16)))
            ns_ref[b, h] = s_old * cdec_ref[h] + _dot_tn((krot * kd).astype(_BF16), vc)
            vcs.append(vc)
        p1s, p2s = [], []
        for h in range(KV_HEADS):
            s1 = s1s[h] + bh_ref[h]
            s2 = s2s[h] + bn_ref[h]
            snk = snk_ref[h]
            m = jnp.maximum(jnp.maximum(jnp.max(s1, axis=-1, keepdims=True),
                                        jnp.max(s2, axis=-1, keepdims=True)), snk)
            p1 = jnp.exp(s1 - m)
            p2 = jnp.exp(s2 - m)
            den = (jnp.sum(p1, axis=-1, keepdims=True) + jnp.sum(p2, axis=-1, keepdims=True)
                   + jnp.exp(snk - m))
            r = 1.0 / den
            p1s.append((p1 * r).astype(_BF16))
            p2s.append((p2 * r).astype(_BF16))
        scb = [(scs[h] * dec_ref[h]).astype(_BF16) for h in range(RET_HEADS)]
        outs = [_dot(p1s[h], cvb[:, heads[h]]) + _dot(p2s[h], vnb[:, heads[h]])
                for h in range(KV_HEADS)]
        rets = [_dot(scb[h], vcs[h]) + crosses[h] for h in range(RET_HEADS)]
        for h in range(KV_HEADS):
            for g in range(GQA_GROUP):
                hq = h * GQA_GROUP + g
                attn_ref[b, :, hq * HEAD_DIM:(hq + 1) * HEAD_DIM] = outs[h][g * ls:(g + 1) * ls]
        for h in range(RET_HEADS):
            o = rets[h]
            o = o * lax.rsqrt(jnp.mean(o * o, axis=-1, keepdims=True) + NORM_EPS)
            gr = row[:, OFF_GR + h * RET_DV:OFF_GR + (h + 1) * RET_DV]
            ret_ref[b, :, h * RET_DV:(h + 1) * RET_DV] = o * (gr * jax.nn.sigmoid(gr))
        return carry

    lax.fori_loop(0, G, body, 0, unroll=SAMPLE_UNROLL)


def _sample_core(proj3, ck, cv, st, cst):
    NB, ls, n_in = proj3.shape
    G = min(SAMPLE_GROUP, NB)
    blk3 = lambda i, *_: (i, 0, 0)
    blk4 = lambda i, *_: (i, 0, 0, 0)
    c2 = lambda i, *_: (0, 0)
    c3 = lambda i, *_: (0, 0, 0)
    ql = GQA_GROUP * ls
    grid_spec = pltpu.PrefetchScalarGridSpec(
        num_scalar_prefetch=1,
        grid=(NB // G,),
        in_specs=[
            pl.BlockSpec((G, ls, n_in), blk3),
            pl.BlockSpec((G, WINDOW, KV_W), blk3),
            pl.BlockSpec((G, WINDOW, KV_W), blk3),
            pl.BlockSpec((G, RET_HEADS, RET_DK, RET_DV), blk4),
            pl.BlockSpec((KV_HEADS, ql, WINDOW), c3),
            pl.BlockSpec((KV_HEADS, ql, ls), c3),
            pl.BlockSpec((KV_HEADS, ql, 1), c3),
            pl.BlockSpec((ls, RET_DK), c2),
            pl.BlockSpec((ls, RET_DK), c2),
            pl.BlockSpec((RET_HEADS, ls, ls), c3),
            pl.BlockSpec((ls, RQ_W), c2),
            pl.BlockSpec((ls, RQ_W), c2),
        ],
        out_specs=[
            pl.BlockSpec((G, ls, ATTN_W), blk3),
            pl.BlockSpec((G, ls, RV_W), blk3),
            pl.BlockSpec((G, WINDOW, KV_W), blk3),
            pl.BlockSpec((G, WINDOW, KV_W), blk3),
            pl.BlockSpec((G, RET_HEADS, RET_DK, RET_DV), blk4),
        ],
    )
    out_shape = [
        jax.ShapeDtypeStruct((NB, ls, ATTN_W), _F32),
        jax.ShapeDtypeStruct((NB, ls, RV_W), _F32),
        jax.ShapeDtypeStruct((NB, WINDOW, KV_W), _F32),
        jax.ShapeDtypeStruct((NB, WINDOW, KV_W), _F32),
        jax.ShapeDtypeStruct((NB, RET_HEADS, RET_DK, RET_DV), _F32),
    ]
    return pl.pallas_call(
        _sample_core_kernel,
        grid_spec=grid_spec,
        out_shape=out_shape,
        compiler_params=pltpu.CompilerParams(
            dimension_semantics=("arbitrary",), vmem_limit_bytes=VMEM_LIMIT),
        name="sample_core",
    )(cst['cdec_s'], proj3, ck, cv, st, cst['bias_hist'], cst['bias_new'], cst['sink_col'],
      cst['cos_s'], cst['sin_s'], cst['dec_s'], cst['qd_s'], cst['kd_s'])


def _sample_post_kernel(x_ref, attn_ref, ret_ref, ga_ref, gt_ref, wba_ref, wbr_ref, wout_ref,
                        nffn_ref, wrh_ref, wrl_ref, br_ref, cnt0_ref, x1_ref, route_ref, cnt_ref):
    x1, route, cnt = _post([x_ref[...]], [_dot(attn_ref[...].astype(_BF16), wba_ref[...])],
                           [ret_ref[...]], [ga_ref[...]], [gt_ref[...]],
                           wbr_ref[...], wout_ref[...], nffn_ref[...],
                           wrh_ref[...], wrl_ref[...], br_ref[...], cnt0_ref[:, 0:1])
    _store_rows(x1_ref, x1)
    route_ref[...] = route
    cnt_ref[...] = jnp.broadcast_to(cnt, cnt_ref.shape)


def _sample_post(x2d, attn, ret, ga, gt, w, cnt0):
    T, D = x2d.shape
    full = lambda s: pl.BlockSpec(s, lambda i: (0,) * len(s))
    return pl.pallas_call(
        _sample_post_kernel,
        grid=(1,),
        in_specs=[full((T, D)), full((T, ATTN_W)), full((T, RV_W)), full((T, D)), full((T, D)),
                  full((ATTN_W, D)), full((RV_W, D)), full((D, D)), full((1, D)),
                  full((D, ROUTE_LANES)), full((D, ROUTE_LANES)), full((1, ROUTE_LANES)),
                  full((ROUTE_ROWS, LANE))],
        out_specs=[full((T * SUB, LANE)), full((ROUTE_OUT, T)), full((ROUTE_ROWS, LANE))],
        out_shape=[jax.ShapeDtypeStruct((T * SUB, LANE), _F32),
                   jax.ShapeDtypeStruct((ROUTE_OUT, T), _F32),
                   jax.ShapeDtypeStruct((ROUTE_ROWS, LANE), _F32)],
        compiler_params=pltpu.CompilerParams(
            dimension_semantics=("arbitrary",), vmem_limit_bytes=VMEM_LIMIT),
        name="sample_post",
    )(x2d, attn, ret, ga, gt, w['wba'], w['wbr'], w['wout'], w['nffn'], w['wrh'], w['wrl'], w['br'],
      cnt0)


def _dispatch_kernel(dest_ref, xp_ref, xq_ref, xs_ref, sem, *, p_steps):
    i = pl.program_id(0)
    n_tok = pl.num_programs(0) * ROW_TM

    def copy_tile(src):
        def tile_copy(r, d):
            return pltpu.make_async_copy(src.at[r], xs_ref.at[d], sem)

        def start(r, c):
            t = i * ROW_TM + r
            tile_copy(r, dest_ref[t]).start(priority=0)
            tile_copy(r, dest_ref[n_tok + t]).start(priority=1)
            return c

        lax.fori_loop(0, ROW_TM, start, 0, unroll=DMA_UNROLL)

        def wait(r, c):
            tile_copy(0, 0).wait()
            tile_copy(0, 0).wait()
            return c

        lax.fori_loop(0, ROW_TM, wait, 0, unroll=DMA_UNROLL)

    @pl.when(i < p_steps)
    def _():
        copy_tile(xp_ref)

    @pl.when(i >= p_steps)
    def _():
        copy_tile(xq_ref)


def _dispatch(dest, xp3, xq3):
    Tp, Tq = xp3.shape[0], xq3.shape[0]
    assert Tp % ROW_TM == 0 and Tq % ROW_TM == 0
    p_steps = Tp // ROW_TM
    grid_spec = pltpu.PrefetchScalarGridSpec(
        num_scalar_prefetch=1,
        grid=((Tp + Tq) // ROW_TM,),
        in_specs=[
            pl.BlockSpec((ROW_TM, SUB, LANE), lambda i, *_: (jnp.minimum(i, p_steps - 1), 0, 0)),
            pl.BlockSpec((ROW_TM, SUB, LANE), lambda i, *_: (jnp.maximum(i - p_steps, 0), 0, 0)),
        ],
        out_specs=pl.BlockSpec(memory_space=pl.ANY),
        scratch_shapes=[pltpu.SemaphoreType.DMA],
    )
    return pl.pallas_call(
        functools.partial(_dispatch_kernel, p_steps=p_steps),
        grid_spec=grid_spec,
        out_shape=jax.ShapeDtypeStruct((2 * (Tp + Tq), SUB, LANE), _F32),
        compiler_params=pltpu.CompilerParams(dimension_semantics=("arbitrary",)),
        name="moe_dispatch",
    )(dest, xp3, xq3)


def _gmm_kernel(tile_ref, exp_ref, lo_ref, hi_ref, first_ref, chg_ref,
                x_ref, nffn_ref, wg_ref, wu_ref, wd_ref, y_ref, wg_s, wu_s, wd_s):
    m = pl.program_id(0)
    tm = x_ref.shape[0] // SUB

    @pl.when(chg_ref[m] == 1)
    def _():
        wg_s[...] = wg_ref[0].astype(_BF16)
        wu_s[...] = wu_ref[0].astype(_BF16)
        wd_s[...] = wd_ref[0].astype(_BF16)

    lo = lo_ref[m]
    hi = hi_ref[m]

    @pl.when(hi > lo)
    def _():
        hn = tm // GMM_PARTS
        parts = [pl.ds(j * hn * SUB, hn * SUB) for j in range(GMM_PARTS)]
        xs = []
        for j in range(GMM_PARTS):
            rows = tile_ref[m] * tm + j * hn + lax.broadcasted_iota(jnp.int32, (hn, 1), 0)
            mine = (rows >= lo) & (rows < hi)
            xn = _rms(_load_rows(x_ref.at[parts[j]], hn), nffn_ref[...])
            xs.append(jnp.where(mine, xn, 0.0).astype(_BF16))
        gate_up = [(_dot(x, wg_s[...]), _dot(x, wu_s[...])) for x in xs]
        ys = [_dot(((a * jax.nn.sigmoid(a)) * u).astype(_BF16), wd_s[...]) for a, u in gate_up]

        @pl.when(first_ref[m] == 1)
        def _():
            for j in range(GMM_PARTS):
                _store_rows(y_ref.at[parts[j]], ys[j])

        @pl.when(first_ref[m] == 0)
        def _():
            for j in range(GMM_PARTS):
                _store_rows(y_ref.at[parts[j]], _load_rows(y_ref.at[parts[j]], hn) + ys[j])


def _gmm(work, xs2, nffn, wg, wu, wd):
    A = xs2.shape[0] // SUB
    E, D, F = wg.shape
    n_work = work[0].shape[0]
    grid_spec = pltpu.PrefetchScalarGridSpec(
        num_scalar_prefetch=6,
        grid=(n_work,),
        in_specs=[
            pl.BlockSpec((MOE_TM * SUB, LANE), lambda m, t, e, *_: (t[m], 0)),
            pl.BlockSpec((1, D), lambda m, t, e, *_: (0, 0)),
            pl.BlockSpec((1, D, F), lambda m, t, e, *_: (e[m], 0, 0)),
            pl.BlockSpec((1, D, F), lambda m, t, e, *_: (e[m], 0, 0)),
            pl.BlockSpec((1, F, D), lambda m, t, e, *_: (e[m], 0, 0)),
        ],
        out_specs=pl.BlockSpec((MOE_TM * SUB, LANE), lambda m, t, e, *_: (t[m], 0)),
        scratch_shapes=[pltpu.VMEM((D, F), _BF16), pltpu.VMEM((D, F), _BF16),
                        pltpu.VMEM((F, D), _BF16)],
    )
    return pl.pallas_call(
        _gmm_kernel,
        grid_spec=grid_spec,
        out_shape=jax.ShapeDtypeStruct((A * SUB, LANE), _F32),
        compiler_params=pltpu.CompilerParams(
            dimension_semantics=("arbitrary",), vmem_limit_bytes=VMEM_LIMIT),
        name="moe_gmm",
    )(*work, xs2, nffn, wg, wu, wd)


def _combine_kernel(dest_ref, x1_ref, route_ref, nfin_ref, yb_ref, o_ref, buf, sems, *, tok0, n_tok):
    i = pl.program_id(0)
    n_steps = pl.num_programs(0)
    tm = o_ref.shape[0]

    def tile_copy(d, slot, k, r):
        rows = pl.ds(pl.multiple_of(r * SUB, SUB), SUB)
        return pltpu.make_async_copy(yb_ref.at[d], buf.at[slot, k, rows], sems.at[slot])

    def issue(step, slot):
        base = tok0 + step * tm

        def start(r, c):
            t = base + r
            tile_copy(dest_ref[t], slot, 0, r).start(priority=0)
            tile_copy(dest_ref[n_tok + t], slot, 1, r).start(priority=1)
            return c

        lax.fori_loop(0, tm, start, 0, unroll=DMA_UNROLL)

    @pl.when(i == 0)
    def _():
        issue(0, 0)

    @pl.when(i + 1 < n_steps)
    def _():
        issue(i + 1, (i + 1) % 2)

    slot = i % 2

    def wait(r, c):
        tile_copy(0, slot, 0, 0).wait()
        tile_copy(0, slot, 1, 0).wait()
        return c

    lax.fori_loop(0, tm, wait, 0, unroll=DMA_UNROLL)
    rt = jnp.concatenate([route_ref[...], jnp.zeros((LANE - ROUTE_OUT, tm), _F32)], axis=0).T
    g0 = rt[:, 2:3]
    g1 = rt[:, 3:4]
    y0 = _load_rows(buf.at[slot, 0], tm)
    y1 = _load_rows(buf.at[slot, 1], tm)
    y = _load_rows(x1_ref, tm) + (y0 * g0 + y1 * g1)
    o_ref[...] = _rms(y, nfin_ref[...])


def _combine(dest, x1_2, route, nfin, yb3, tok0):
    T = x1_2.shape[0] // SUB
    D = SUB * LANE
    tm = min(ROW_TM, T)
    grid_spec = pltpu.PrefetchScalarGridSpec(
        num_scalar_prefetch=1,
        grid=(T // tm,),
        in_specs=[
            pl.BlockSpec((tm * SUB, LANE), lambda i, *_: (i, 0)),
            pl.BlockSpec((ROUTE_OUT, tm), lambda i, *_: (0, i)),
            pl.BlockSpec((1, D), lambda i, *_: (0, 0)),
            pl.BlockSpec(memory_space=pl.ANY),
        ],
        out_specs=pl.BlockSpec((tm, D), lambda i, *_: (i, 0)),
        scratch_shapes=[pltpu.VMEM((2, 2, tm * SUB, LANE), _F32), pltpu.SemaphoreType.DMA((2,))],
    )
    return pl.pallas_call(
        functools.partial(_combine_kernel, tok0=tok0, n_tok=dest.shape[0] // 2),
        grid_spec=grid_spec,
        out_shape=jax.ShapeDtypeStruct((T, D), _F32),
        compiler_params=pltpu.CompilerParams(
            dimension_semantics=("arbitrary",), vmem_limit_bytes=VMEM_LIMIT),
        name="moe_combine",
    )(dest, x1_2, route, nfin, yb3)


def _routing_tables(route, counts, n_tiles):
    experts = route[0:2].astype(jnp.int32)
    ranks = route[4:6].astype(jnp.int32)
    A = experts.size
    ids = jnp.arange(N_EXPERTS, dtype=jnp.int32)
    ends = jnp.cumsum(counts)
    starts = ends - counts
    dest = ranks + jnp.sum(jnp.where(experts[..., None] == ids, starts, 0), axis=-1)
    dest = dest.reshape(-1)
    tile_starts = jnp.arange(n_tiles, dtype=jnp.int32) * MOE_TM
    pos_t = jnp.arange(n_tiles, dtype=jnp.int32) + jnp.sum(starts[None, :] < tile_starts[:, None], axis=1)
    pos_e = ids + jnp.sum(tile_starts[None, :] <= starts[:, None], axis=1)
    slots = jnp.arange(n_tiles + N_EXPERTS, dtype=jnp.int32)
    pts = (jnp.sum(jnp.where(pos_t[None, :] == slots[:, None], tile_starts[None, :], 0), axis=1)
           + jnp.sum(jnp.where(pos_e[None, :] == slots[:, None], starts[None, :], 0), axis=1))
    lo = pts.astype(jnp.int32)
    hi = jnp.concatenate([lo[1:], jnp.array([A], jnp.int32)])
    tile = jnp.minimum(lo // MOE_TM, n_tiles - 1).astype(jnp.int32)
    expert = jnp.minimum(jnp.sum(ends[None, :] <= lo[:, None], axis=1), N_EXPERTS - 1).astype(jnp.int32)
    nonempty = hi > lo
    first = (nonempty & (lo % MOE_TM == 0)).astype(jnp.int32)
    chg = jnp.concatenate([jnp.ones((1,), jnp.int32), (expert[1:] != expert[:-1]).astype(jnp.int32)])
    return dest.astype(jnp.int32), (tile, expert, lo, hi, first, chg)


def _bucket_table(lq, lk):
    dist = np.arange(lq)[:, None] + WINDOW - np.arange(lk)[None, :]
    band = (dist >= 0) & (dist < WINDOW)
    d = np.clip(dist, 0, WINDOW - 1)
    max_exact = N_BUCKETS // 2
    d_f = np.maximum(d, 1).astype(np.float32)
    large = max_exact + (np.log(d_f / max_exact) / math.log(MAX_DISTANCE / max_exact)
                         * (N_BUCKETS - max_exact)).astype(np.int32)
    large = np.minimum(large, N_BUCKETS - 1)
    return np.where(d < max_exact, d, large).astype(np.int32), band


def _bias_table(rb, lq, lk):
    bkt, band = _bucket_table(lq, lk)
    onehot = jnp.asarray(bkt)[None, :, :] == jnp.arange(N_BUCKETS, dtype=jnp.int32)[:, None, None]
    bias = jnp.sum(jnp.where(onehot[:, None], rb[:, :, None, None], 0.0), axis=0)
    return jnp.where(jnp.asarray(band)[None], bias, NEG_INF)


def _decay_tables(C):
    log_gamma = jnp.log(1.0 - 2.0 ** (-5.0 - jnp.arange(RET_HEADS, dtype=_F32)))
    idx = jnp.arange(C, dtype=_F32)
    diff = idx[:, None] - idx[None, :]
    decay_in = jnp.where((diff >= 0)[..., None],
                         jnp.exp(jnp.maximum(diff, 0.0)[..., None] * log_gamma), 0.0)
    q_dec = jnp.exp((idx + 1.0)[:, None] * log_gamma)
    k_dec = jnp.exp((C - 1.0 - idx)[:, None] * log_gamma)
    c_dec = jnp.exp(C * log_gamma)
    dec = jnp.transpose(decay_in, (2, 0, 1))
    qd = jnp.repeat(q_dec, RET_DK, axis=1)
    kd = jnp.repeat(k_dec, RET_DK, axis=1)
    return dec, qd, kd, c_dec


def _rope_tables(pos):
    half = RET_DK // 2
    inv = ROPE_BASE ** (-jnp.arange(half, dtype=_F32) * 2.0 / RET_DK)
    ang = pos.astype(_F32)[:, None] * inv[None, :]
    cos = jnp.cos(ang)
    sin = jnp.sin(ang)
    return jnp.concatenate([cos, cos], axis=1), jnp.concatenate([-sin, sin], axis=1)


def _constants(rel_bias, attn_sink, L, ls):
    cst = {}
    rb = rel_bias.astype(_F32)
    cst['bias_p'] = jnp.transpose(_bias_table(rb, WINDOW, 2 * WINDOW), (0, 2, 1))
    cst['sink'] = attn_sink.astype(_F32)
    cst['cos_p'], cst['sin_p'] = _rope_tables(jnp.arange(L))
    cst['dec_p'], cst['qd_p'], cst['kd_p'], cst['cdec_p'] = _decay_tables(min(RET_CHUNK, L))
    bias = _bias_table(rb, ls, WINDOW + ls).reshape(KV_HEADS, GQA_GROUP * ls, WINDOW + ls)
    cst['bias_hist'] = bias[:, :, :WINDOW]
    cst['bias_new'] = bias[:, :, WINDOW:]
    cst['sink_col'] = jnp.repeat(attn_sink.astype(_F32).reshape(KV_HEADS, GQA_GROUP), ls,
                                 axis=1)[..., None]
    cst['cos_s'], cst['sin_s'] = _rope_tables(PAST_LEN + jnp.arange(ls))
    cst['dec_s'], cst['qd_s'], cst['kd_s'], cst['cdec_s'] = _decay_tables(min(RET_CHUNK, ls))
    return cst


def _layer_weights(layer, norm_mix, w_in, w_branch_attn, w_branch_ret, w_out, norm_ffn,
                   w_router_group, b_router_group, w_router_expert, b_router_expert):
    D = w_in.shape[1]
    wr = jnp.concatenate([w_router_group[layer].astype(_F32), w_router_expert[layer].astype(_F32)], axis=1)
    wr = jnp.pad(wr, ((0, 0), (0, ROUTE_LANES - wr.shape[1])))
    wrh = wr.astype(_BF16)
    wrl = (wr - wrh.astype(_F32)).astype(_BF16)
    br = jnp.concatenate([b_router_group[layer].astype(_F32), b_router_expert[layer].astype(_F32)])
    br = jnp.pad(br, (0, ROUTE_LANES - br.shape[0]))[None, :]
    return {
        'nmix': norm_mix[layer].astype(_F32)[None, :],
        'win': w_in[layer].astype(_BF16),
        'wba': w_branch_attn[layer].astype(_BF16),
        'wbr': w_branch_ret[layer].astype(_BF16),
        'wout': w_out[layer].astype(_BF16),
        'nffn': norm_ffn[layer].astype(_F32)[None, :],
        'wrh': wrh, 'wrl': wrl, 'br': br,
    }


def kernel(x_prompt, x_sample, cache_k, cache_v, state_ret, norm_mix, w_in, attn_sink, rel_bias,
           w_branch_attn, w_branch_ret, w_out, norm_ffn, w_router_group, b_router_group,
           w_router_expert, b_router_expert, w_gate, w_up, w_down, norm_final):
    depth = w_in.shape[0]
    assert depth == 1, "the final norm is fused into the MoE combine of the only layer"
    B, L, D = x_prompt.shape
    NB, ls, _ = x_sample.shape
    Tp, Ts = B * L, NB * ls
    nfin = norm_final.astype(_F32)[None, :]
    yp, ys = x_prompt, x_sample
    pk, pv, ps, sk, sv, ss = [], [], [], [], [], []
    for layer in range(depth):
        w = _layer_weights(layer, norm_mix, w_in, w_branch_attn, w_branch_ret, w_out, norm_ffn,
                           w_router_group, b_router_group, w_router_expert, b_router_expert)
        cst = _constants(rel_bias, attn_sink[layer], L, ls)
        x1p, routep, k1, v1, s1, cnt_p = _prompt_mixer(yp, cst, w)
        ys2 = ys.reshape(Ts, D)
        proj = _sample_inproj(ys2, w)
        attn_s, ret_s, k2, v2, s2 = _sample_core(
            proj.reshape(NB, ls, -1),
            cache_k[layer].reshape(NB, WINDOW, KV_W), cache_v[layer].reshape(NB, WINDOW, KV_W),
            state_ret[layer], cst)
        x1s, routes, cnt_all = _sample_post(
            ys2, attn_s.reshape(Ts, ATTN_W), ret_s.reshape(Ts, RV_W),
            proj[:, OFF_GA:OFF_GA + D], proj[:, OFF_GT:OFF_GT + D], w, cnt_p)
        n_rows = 2 * (Tp + Ts)
        assert n_rows % MOE_TM == 0
        counts = cnt_all[N_GROUPS:N_GROUPS + N_EXPERTS, 0].astype(jnp.int32)
        dest, work = _routing_tables(jnp.concatenate([routep, routes], axis=1), counts,
                                     n_rows // MOE_TM)
        xs3 = _dispatch(dest, x1p.reshape(Tp, SUB, LANE), x1s.reshape(Ts, SUB, LANE))
        yb2 = _gmm(work, xs3.reshape(n_rows * SUB, LANE), w['nffn'],
                   w_gate[layer], w_up[layer], w_down[layer])
        yb3 = yb2.reshape(n_rows, SUB, LANE)
        yp = _combine(dest, x1p, routep, nfin, yb3, 0).reshape(B, L, D)
        ys = _combine(dest, x1s, routes, nfin, yb3, Tp).reshape(NB, ls, D)
        pk.append(k1.reshape(B, WINDOW, KV_HEADS, HEAD_DIM))
        pv.append(v1.reshape(B, WINDOW, KV_HEADS, HEAD_DIM))
        ps.append(s1)
        sk.append(k2.reshape(NB, WINDOW, KV_HEADS, HEAD_DIM))
        sv.append(v2.reshape(NB, WINDOW, KV_HEADS, HEAD_DIM))
        ss.append(s2)
    return (yp, ys, jnp.stack(pk), jnp.stack(pv), jnp.stack(ps),
            jnp.stack(sk), jnp.stack(sv), jnp.stack(ss))
```

```python
import functools
import math

import jax
import jax.numpy as jnp
import numpy as np
from jax import lax
from jax.experimental import pallas as pl
from jax.experimental.pallas import tpu as pltpu

HEAD_DIM = 64
KV_HEADS = 4
GQA_GROUP = 4
ATTN_HEADS = KV_HEADS * GQA_GROUP
WINDOW = 128
N_BUCKETS = 32
MAX_DISTANCE = 128
RET_HEADS = 4
RET_DK = 128
RET_DV = 256
RET_CHUNK = 128
ROPE_BASE = 10000.0
N_GROUPS = 4
EXPERTS_PER_GROUP = 8
N_EXPERTS = N_GROUPS * EXPERTS_PER_GROUP
EXPERT_FF = 512
NORM_EPS = 1e-6
NEG_INF = -1e30
PAST_LEN = 16384

ATTN_W = ATTN_HEADS * HEAD_DIM
KV_W = KV_HEADS * HEAD_DIM
RQ_W = RET_HEADS * RET_DK
RV_W = RET_HEADS * RET_DV
OFF_QA = 0
OFF_KA = OFF_QA + ATTN_W
OFF_VA = OFF_KA + KV_W
OFF_QR = OFF_VA + KV_W
OFF_KR = OFF_QR + RQ_W
OFF_VR = OFF_KR + RQ_W
OFF_GR = OFF_VR + RV_W
OFF_GA = OFF_GR + RV_W
OFF_GT = OFF_GA + 1024
ROUTE_LANES = 128
ROUTE_ROWS = 40
ROUTE_OUT = 8

LANE = 128
SUB = 8
POST_PARTS = 2
PROMPT_TM = 512
SAMPLE_GROUP = 8
MOE_TM = 256
GMM_PARTS = 2
ROW_TM = 512
DMA_UNROLL = 8
VMEM_LIMIT = 60 * 1024 * 1024

_F32 = jnp.float32
_BF16 = jnp.bfloat16


def _const_spec(shape):
    nd = len(shape)
    return pl.BlockSpec(shape, lambda *_: (0,) * nd, pipeline_mode=pl.Buffered(1))


def _rms(x, gain):
    return x * lax.rsqrt(jnp.mean(x * x, axis=-1, keepdims=True) + NORM_EPS) * gain


def _dot(a, b):
    return jnp.dot(a, b, preferred_element_type=_F32)


def _dot_nt(a, b):
    return lax.dot_general(a, b, (((1,), (1,)), ((), ())), preferred_element_type=_F32)


def _dot_tn(a, b):
    return lax.dot_general(a, b, (((0,), (0,)), ((), ())), preferred_element_type=_F32)


def _load_rows(ref, n):
    return jnp.concatenate([ref[pl.ds(s, n, stride=SUB), :] for s in range(SUB)], axis=1)


def _store_rows(ref, val):
    n = val.shape[0]
    for s in range(SUB):
        ref[pl.ds(s, n, stride=SUB), :] = val[:, s * LANE:(s + 1) * LANE]


def _rotary(x, cosf, sinf):
    return x * cosf + pltpu.roll(x, RET_DK // 2, 1) * sinf


def _post(xs, attn_projs, rets, gates_a, gates_r, wbr, wout, nffn, wrh, wrl, br, cnt):
    parts = range(len(xs))
    ret_projs = [_dot(rets[j].astype(_BF16), wbr) for j in parts]
    x1s = []
    for j in parts:
        merged = jax.nn.sigmoid(gates_a[j]) * attn_projs[j] + jax.nn.sigmoid(gates_r[j]) * ret_projs[j]
        x1s.append(xs[j] + _dot(merged.astype(_BF16), wout))
    logit_parts = []
    for j in parts:
        xn2 = _rms(x1s[j], nffn)
        hi = xn2.astype(_BF16)
        lo = (xn2 - hi.astype(_F32)).astype(_BF16)
        logit_parts.append(_dot(hi, wrh) + (_dot(hi, wrl) + _dot(lo, wrh)) + br)
    x1 = x1s[0] if len(x1s) == 1 else jnp.concatenate(x1s, axis=0)
    logits = logit_parts[0] if len(x1s) == 1 else jnp.concatenate(logit_parts, axis=0)
    n = logits.shape[0]
    lt = logits.T[0:ROUTE_ROWS, :]
    row = lax.broadcasted_iota(jnp.int32, (ROUTE_ROWS, n), 0)
    big = jnp.int32(1 << 20)
    neg = jnp.float32(-jnp.inf)
    gl = jnp.where(row < N_GROUPS, lt, neg)
    gmax = jnp.max(gl, axis=0, keepdims=True)
    gexp = jnp.exp(gl - gmax)
    gsum = jnp.sum(gexp, axis=0, keepdims=True)
    pg = gexp / gsum
    g_w = jnp.max(pg, axis=0, keepdims=True)
    g_idx = jnp.min(jnp.where(pg == g_w, row, big), axis=0, keepdims=True)
    e_row = row - N_GROUPS
    emask = (e_row >= 0) & (e_row < N_EXPERTS) & ((e_row >> 3) == g_idx)
    fl = jnp.where(emask, lt, neg)
    fmax = jnp.max(fl, axis=0, keepdims=True)
    fexp = jnp.exp(fl - fmax)
    fsum = jnp.sum(fexp, axis=0, keepdims=True)
    pe = jnp.where(emask, fexp / fsum, -1.0)
    p1 = jnp.max(pe, axis=0, keepdims=True)
    i1 = jnp.min(jnp.where(pe == p1, row, big), axis=0, keepdims=True)
    pe2 = jnp.where(row == i1, -1.0, pe)
    p2 = jnp.max(pe2, axis=0, keepdims=True)
    i2 = jnp.min(jnp.where(pe2 == p2, row, big), axis=0, keepdims=True)
    psum = p1 + p2
    gate1 = g_w * p1 / psum
    gate2 = g_w * p2 / psum
    oh1 = row == i1
    oh2 = row == i2
    c = jnp.where(oh1 | oh2, 1.0, 0.0)
    tt = lax.broadcasted_iota(jnp.int32, (n, n), 0)
    tc = lax.broadcasted_iota(jnp.int32, (n, n), 1)
    upper = jnp.where(tt < tc, 1.0, 0.0).astype(_BF16)
    before = _dot(c.astype(_BF16), upper) + cnt
    rank1 = jnp.sum(jnp.where(oh1, before, 0.0), axis=0, keepdims=True)
    rank2 = jnp.sum(jnp.where(oh2, before, 0.0), axis=0, keepdims=True)
    cnt = cnt + jnp.sum(c, axis=1, keepdims=True)
    r8 = lax.broadcasted_iota(jnp.int32, (ROUTE_OUT, n), 0)
    vals = [(i1 - N_GROUPS).astype(_F32), (i2 - N_GROUPS).astype(_F32), gate1, gate2, rank1, rank2]
    route = jnp.zeros((ROUTE_OUT, n), _F32)
    for k, v in enumerate(vals):
        route = jnp.where(r8 == k, v, route)
    return x1, route, cnt


def _prompt_mixer_kernel(sink_ref, cdec_ref,
                         x_ref, nmix_ref, win_ref, bias_ref, cos_ref, sin_ref, dec_ref,
                         qd_ref, kd_ref, wba_ref, wbr_ref, wout_ref, nffn_ref,
                         wrh_ref, wrl_ref, br_ref,
                         x1_ref, route_ref, knew_ref, vnew_ref, s_ref, cnt_ref,
                         qkv, proj, kctx, vctx, attn_t, ret):
    i = pl.program_id(1)
    last = pl.num_programs(1) - 1

    @pl.when((i == 0) & (pl.program_id(0) == 0))
    def _():
        cnt_ref[...] = jnp.zeros_like(cnt_ref)

    tm = x_ref.shape[1]
    n_sub = tm // WINDOW
    scale = HEAD_DIM ** -0.5

    @pl.when(i == 0)
    def _():
        s_ref[...] = jnp.zeros_like(s_ref)
        kctx[0:WINDOW, :] = jnp.zeros((WINDOW, KV_W), _BF16)
        vctx[0:WINDOW, :] = jnp.zeros((WINDOW, KV_W), _BF16)

    x = x_ref[0]
    xn = _rms(x, nmix_ref[...]).astype(_BF16)
    n_in = win_ref.shape[1]
    panel = 256

    def project(c0):
        res = _dot(xn, win_ref[:, c0:c0 + panel])
        if c0 < OFF_QR:
            qkv[:, c0:c0 + panel] = res
        else:
            proj[:, c0 - OFF_QR:c0 - OFF_QR + panel] = res

    def cols(lo, width):
        return slice(lo - OFF_QR, lo - OFF_QR + width)

    for c0 in range(0, OFF_QR, panel):
        project(c0)
    later_panels = list(range(OFF_QR, n_in, panel))

    @pl.when(i == last)
    def _():
        knew_ref[0] = qkv[tm - WINDOW:tm, OFF_KA:OFF_KA + KV_W]
        vnew_ref[0] = qkv[tm - WINDOW:tm, OFF_VA:OFF_VA + KV_W]

    krow = lax.broadcasted_iota(jnp.int32, (2 * WINDOW, 1), 0)
    for c in range(n_sub):
        r0 = c * WINDOW
        kctx[WINDOW:2 * WINDOW, :] = qkv[r0:r0 + WINDOW, OFF_KA:OFF_KA + KV_W].astype(_BF16)
        vctx[WINDOW:2 * WINDOW, :] = qkv[r0:r0 + WINDOW, OFF_VA:OFF_VA + KV_W].astype(_BF16)
        if c == 0:
            pen = jnp.where((krow < WINDOW) & (i == 0), NEG_INF, 0.0).astype(_F32)
        for h in range(KV_HEADS):
            k_h = kctx[:, h * HEAD_DIM:(h + 1) * HEAD_DIM]
            v_h = vctx[:, h * HEAD_DIM:(h + 1) * HEAD_DIM]
            probs = []
            for g in range(GQA_GROUP):
                hq = h * GQA_GROUP + g
                q = (qkv[r0:r0 + WINDOW, hq * HEAD_DIM:(hq + 1) * HEAD_DIM] * scale).astype(_BF16)
                s = _dot_nt(k_h, q) + bias_ref[hq]
                if c == 0:
                    s = s + pen
                snk = sink_ref[hq]
                m = jnp.maximum(jnp.max(s, axis=0, keepdims=True), snk)
                p = jnp.exp(s - m)
                den = jnp.sum(p, axis=0, keepdims=True) + jnp.exp(snk - m)
                probs.append((p * (1.0 / den)).astype(_BF16))
            groups_left = (n_sub - c) * KV_HEADS - h
            for _ in range(-(-len(later_panels) // groups_left)):
                project(later_panels.pop(0))
            o_t = _dot_tn(v_h, jnp.concatenate(probs, axis=1))
            for g in range(GQA_GROUP):
                hq = h * GQA_GROUP + g
                attn_t[hq * HEAD_DIM:(hq + 1) * HEAD_DIM, r0:r0 + WINDOW] = (
                    o_t[:, g * WINDOW:(g + 1) * WINDOW].astype(_BF16))
        kctx[0:WINDOW, :] = kctx[WINDOW:2 * WINDOW, :]
        vctx[0:WINDOW, :] = vctx[WINDOW:2 * WINDOW, :]

    for c0 in later_panels:
        project(c0)

    for c in range(n_sub):
        r0 = c * RET_CHUNK
        cosf = cos_ref[r0:r0 + RET_CHUNK, :]
        sinf = sin_ref[r0:r0 + RET_CHUNK, :]
        for h in range(RET_HEADS):
            qc = _rotary(proj[r0:r0 + RET_CHUNK, cols(OFF_QR + h * RET_DK, RET_DK)], cosf, sinf)
            kc = _rotary(proj[r0:r0 + RET_CHUNK, cols(OFF_KR + h * RET_DK, RET_DK)], cosf, sinf) * (RET_DK ** -0.5)
            qd = qd_ref[:, h * RET_DK:(h + 1) * RET_DK]
            kd = kd_ref[:, h * RET_DK:(h + 1) * RET_DK]
            vc = proj[r0:r0 + RET_CHUNK, cols(OFF_VR + h * RET_DV, RET_DV)].astype(_BF16)
            sc = _dot_nt(qc.astype(_BF16), kc.astype(_BF16))
            s_old = s_ref[0, h]
            cross = _dot((qc * qd).astype(_BF16), s_old.astype(_BF16))
            s_ref[0, h] = s_old * cdec_ref[h] + _dot_tn((kc * kd).astype(_BF16), vc)
            o = _dot((sc * dec_ref[h]).astype(_BF16), vc) + cross
            o = o * lax.rsqrt(jnp.mean(o * o, axis=-1, keepdims=True) + NORM_EPS)
            gr = proj[r0:r0 + RET_CHUNK, cols(OFF_GR + h * RET_DV, RET_DV)]
            ret[r0:r0 + RET_CHUNK, h * RET_DV:(h + 1) * RET_DV] = o * (gr * jax.nn.sigmoid(gr))

    pn = tm // POST_PARTS
    rows = [slice(j * pn, (j + 1) * pn) for j in range(POST_PARTS)]
    x1, route, cnt = _post([x[r] for r in rows],
                           [_dot_tn(attn_t[:, r], wba_ref[...]) for r in rows],
                           [ret[r, :] for r in rows],
                           [proj[r, cols(OFF_GA, 1024)] for r in rows],
                           [proj[r, cols(OFF_GT, 1024)] for r in rows],
                           wbr_ref[...], wout_ref[...], nffn_ref[...],
                           wrh_ref[...], wrl_ref[...], br_ref[...], cnt_ref[:, 0:1])
    _store_rows(x1_ref, x1)
    route_ref[...] = route
    cnt_ref[...] = jnp.broadcast_to(cnt, cnt_ref.shape)


def _prompt_mixer(x, cst, w):
    B, L, D = x.shape
    tm = min(PROMPT_TM, L)
    nb = L // tm
    n_in = w['win'].shape[1]
    step = lambda b, i, *_: (b, i, 0)
    per_b = lambda b, i, *_: (b, 0, 0)

    grid_spec = pltpu.PrefetchScalarGridSpec(
        num_scalar_prefetch=2,
        grid=(B, nb),
        in_specs=[
            pl.BlockSpec((1, tm, D), step),
            _const_spec((1, D)),
            _const_spec((D, n_in)),
            _const_spec((ATTN_HEADS, 2 * WINDOW, WINDOW)),
            pl.BlockSpec((tm, RET_DK), lambda b, i, *_: (i, 0)),
            pl.BlockSpec((tm, RET_DK), lambda b, i, *_: (i, 0)),
            _const_spec((RET_HEADS, RET_CHUNK, RET_CHUNK)),
            _const_spec((RET_CHUNK, RQ_W)),
            _const_spec((RET_CHUNK, RQ_W)),
            _const_spec((ATTN_W, D)),
            _const_spec((RV_W, D)),
            _const_spec((D, D)),
            _const_spec((1, D)),
            _const_spec((D, ROUTE_LANES)),
            _const_spec((D, ROUTE_LANES)),
            _const_spec((1, ROUTE_LANES)),
        ],
        out_specs=[
            pl.BlockSpec((tm * SUB, LANE), lambda b, i, *_: (b * nb + i, 0)),
            pl.BlockSpec((ROUTE_OUT, tm), lambda b, i, *_: (0, b * nb + i)),
            pl.BlockSpec((1, WINDOW, KV_W), per_b),
            pl.BlockSpec((1, WINDOW, KV_W), per_b),
            pl.BlockSpec((1, RET_HEADS, RET_DK, RET_DV), lambda b, i, *_: (b, 0, 0, 0)),
            pl.BlockSpec((ROUTE_ROWS, LANE), lambda b, i, *_: (0, 0)),
        ],
        scratch_shapes=[
            pltpu.VMEM((tm, OFF_QR), _F32),
            pltpu.VMEM((tm, n_in - OFF_QR), _F32),
            pltpu.VMEM((2 * WINDOW, KV_W), _BF16),
            pltpu.VMEM((2 * WINDOW, KV_W), _BF16),
            pltpu.VMEM((ATTN_W, tm), _BF16),
            pltpu.VMEM((tm, RV_W), _F32),
        ],
    )
    assert D == SUB * LANE
    out_shape = [
        jax.ShapeDtypeStruct((B * L * SUB, LANE), _F32),
        jax.ShapeDtypeStruct((ROUTE_OUT, B * L), _F32),
        jax.ShapeDtypeStruct((B, WINDOW, KV_W), _F32),
        jax.ShapeDtypeStruct((B, WINDOW, KV_W), _F32),
        jax.ShapeDtypeStruct((B, RET_HEADS, RET_DK, RET_DV), _F32),
        jax.ShapeDtypeStruct((ROUTE_ROWS, LANE), _F32),
    ]
    return pl.pallas_call(
        _prompt_mixer_kernel,
        grid_spec=grid_spec,
        out_shape=out_shape,
        compiler_params=pltpu.CompilerParams(
            dimension_semantics=("arbitrary", "arbitrary"), vmem_limit_bytes=VMEM_LIMIT),
        name="prompt_mixer",
    )(cst['sink'], cst['cdec_p'],
      x, w['nmix'], w['win'], cst['bias_p'], cst['cos_p'], cst['sin_p'], cst['dec_p'],
      cst['qd_p'], cst['kd_p'], w['wba'], w['wbr'], w['wout'], w['nffn'],
      w['wrh'], w['wrl'], w['br'])


def _inproj_kernel(x_ref, nmix_ref, win_ref, o_ref):
    xn = _rms(x_ref[...], nmix_ref[...]).astype(_BF16)
    o_ref[...] = _dot(xn, win_ref[...])


def _sample_inproj(x2d, w):
    T, D = x2d.shape
    n_in = w['win'].shape[1]
    panel = 512
    return pl.pallas_call(
        _inproj_kernel,
        grid=(n_in // panel,),
        in_specs=[pl.BlockSpec((T, D), lambda j: (0, 0)),
                  pl.BlockSpec((1, D), lambda j: (0, 0)),
                  pl.BlockSpec((D, panel), lambda j: (0, j))],
        out_specs=pl.BlockSpec((T, panel), lambda j: (0, j)),
        out_shape=jax.ShapeDtypeStruct((T, n_in), _F32),
        compiler_params=pltpu.CompilerParams(
            dimension_semantics=("arbitrary",), vmem_limit_bytes=VMEM_LIMIT),
        name="sample_inproj",
    )(x2d, w['nmix'], w['win'])


def _sample_core_kernel(cdec_ref, proj_ref, ck_ref, cv_ref, st_ref, bh_ref, bn_ref, snk_ref,
                        cos_ref, sin_ref, dec_ref, qd_ref, kd_ref,
                        attn_ref, ret_ref, nk_ref, nv_ref, ns_ref):
    G = ck_ref.shape[0]
    ls = proj_ref.shape[0] // G
    per = SUB // ls
    scale = HEAD_DIM ** -0.5
    cosf = cos_ref[...]
    sinf = sin_ref[...]
    heads = [slice(h * HEAD_DIM, (h + 1) * HEAD_DIM) for h in range(KV_HEADS)]

    def body(j, carry):
        group = pl.ds(pl.multiple_of(j * SUB, SUB), SUB)
        rows = proj_ref[group, :]
        seqs = []
        for s in range(per):
            b = j * per + s
            row = rows[s * ls:(s + 1) * ls]
            k_new = row[:, OFF_KA:OFF_KA + KV_W]
            v_new = row[:, OFF_VA:OFF_VA + KV_W]
            ck = ck_ref[b]
            cv = cv_ref[b]
            nk_ref[b, 0:WINDOW - ls, :] = ck[ls:WINDOW, :]
            nk_ref[b, WINDOW - ls:WINDOW, :] = k_new
            nv_ref[b, 0:WINDOW - ls, :] = cv[ls:WINDOW, :]
            nv_ref[b, WINDOW - ls:WINDOW, :] = v_new
            seqs.append(dict(b=b, row=row, ckb=ck.astype(_BF16), cvb=cv.astype(_BF16),
                             knb=k_new.astype(_BF16), vnb=v_new.astype(_BF16)))
        for q in seqs:
            row = q['row']
            q['s1'], q['s2'] = [], []
            for h in range(KV_HEADS):
                q4 = (jnp.concatenate(
                    [row[:, (h * GQA_GROUP + g) * HEAD_DIM:(h * GQA_GROUP + g + 1) * HEAD_DIM]
                     for g in range(GQA_GROUP)], axis=0) * scale).astype(_BF16)
                q['s1'].append(_dot_nt(q4, q['ckb'][:, heads[h]]))
                q['s2'].append(_dot_nt(q4, q['knb'][:, heads[h]]))
        for q in seqs:
            row, b = q['row'], q['b']
            q['sc'], q['cross'], q['vc'] = [], [], []
            for h in range(RET_HEADS):
                qrot = _rotary(row[:, OFF_QR + h * RET_DK:OFF_QR + (h + 1) * RET_DK], cosf, sinf)
                krot = _rotary(row[:, OFF_KR + h * RET_DK:OFF_KR + (h + 1) * RET_DK], cosf, sinf) * (RET_DK ** -0.5)
                vc = row[:, OFF_VR + h * RET_DV:OFF_VR + (h + 1) * RET_DV].astype(_BF16)
                qd = qd_ref[:, h * RET_DK:(h + 1) * RET_DK]
                kd = kd_ref[:, h * RET_DK:(h + 1) * RET_DK]
                s_old = st_ref[b, h]
                q['sc'].append(_dot_nt(qrot.astype(_BF16), krot.astype(_BF16)))
                q['cross'].append(_dot((qrot * qd).astype(_BF16), s_old.astype(_BF16)))
                ns_ref[b, h] = s_old * cdec_ref[h] + _dot_tn((krot * kd).astype(_BF16), vc)
                q['vc'].append(vc)
        for q in seqs:
            q['p1'], q['p2'] = [], []
            for h in range(KV_HEADS):
                s1 = q['s1'][h] + bh_ref[h]
                s2 = q['s2'][h] + bn_ref[h]
                snk = snk_ref[h]
                m = jnp.maximum(jnp.maximum(jnp.max(s1, axis=-1, keepdims=True),
                                            jnp.max(s2, axis=-1, keepdims=True)), snk)
                p1 = jnp.exp(s1 - m)
                p2 = jnp.exp(s2 - m)
                den = (jnp.sum(p1, axis=-1, keepdims=True) + jnp.sum(p2, axis=-1, keepdims=True)
                       + jnp.exp(snk - m))
                r = 1.0 / den
                q['p1'].append((p1 * r).astype(_BF16))
                q['p2'].append((p2 * r).astype(_BF16))
            q['scb'] = [(q['sc'][h] * dec_ref[h]).astype(_BF16) for h in range(RET_HEADS)]
        for q in seqs:
            q['out'] = [_dot(q['p1'][h], q['cvb'][:, heads[h]]) + _dot(q['p2'][h], q['vnb'][:, heads[h]])
                        for h in range(KV_HEADS)]
            q['ret'] = [_dot(q['scb'][h], q['vc'][h]) + q['cross'][h] for h in range(RET_HEADS)]
        for h in range(KV_HEADS):
            for g in range(GQA_GROUP):
                hq = h * GQA_GROUP + g
                attn_ref[group, hq * HEAD_DIM:(hq + 1) * HEAD_DIM] = jnp.concatenate(
                    [q['out'][h][g * ls:(g + 1) * ls] for q in seqs], axis=0)
        for h in range(RET_HEADS):
            o = jnp.concatenate([q['ret'][h] for q in seqs], axis=0)
            o = o * lax.rsqrt(jnp.mean(o * o, axis=-1, keepdims=True) + NORM_EPS)
            gr = rows[:, OFF_GR + h * RET_DV:OFF_GR + (h + 1) * RET_DV]
            ret_ref[group, h * RET_DV:(h + 1) * RET_DV] = o * (gr * jax.nn.sigmoid(gr))
        return carry

    lax.fori_loop(0, G // per, body, 0)


def _sample_core(proj2, ls, ck, cv, st, cst):
    n_in = proj2.shape[1]
    NB = proj2.shape[0] // ls
    G = min(SAMPLE_GROUP, NB)
    assert SUB % ls == 0 and G % (SUB // ls) == 0 and NB % G == 0
    row_blk = lambda i, *_: (i, 0)
    blk3 = lambda i, *_: (i, 0, 0)
    blk4 = lambda i, *_: (i, 0, 0, 0)
    c2 = lambda i, *_: (0, 0)
    c3 = lambda i, *_: (0, 0, 0)
    ql = GQA_GROUP * ls
    grid_spec = pltpu.PrefetchScalarGridSpec(
        num_scalar_prefetch=1,
        grid=(NB // G,),
        in_specs=[
            pl.BlockSpec((G * ls, n_in), row_blk),
            pl.BlockSpec((G, WINDOW, KV_W), blk3),
            pl.BlockSpec((G, WINDOW, KV_W), blk3),
            pl.BlockSpec((G, RET_HEADS, RET_DK, RET_DV), blk4),
            pl.BlockSpec((KV_HEADS, ql, WINDOW), c3),
            pl.BlockSpec((KV_HEADS, ql, ls), c3),
            pl.BlockSpec((KV_HEADS, ql, 1), c3),
            pl.BlockSpec((ls, RET_DK), c2),
            pl.BlockSpec((ls, RET_DK), c2),
            pl.BlockSpec((RET_HEADS, ls, ls), c3),
            pl.BlockSpec((ls, RQ_W), c2),
            pl.BlockSpec((ls, RQ_W), c2),
        ],
        out_specs=[
            pl.BlockSpec((G * ls, ATTN_W), row_blk),
            pl.BlockSpec((G * ls, RV_W), row_blk),
            pl.BlockSpec((G, WINDOW, KV_W), blk3),
            pl.BlockSpec((G, WINDOW, KV_W), blk3),
            pl.BlockSpec((G, RET_HEADS, RET_DK, RET_DV), blk4),
        ],
    )
    out_shape = [
        jax.ShapeDtypeStruct((NB * ls, ATTN_W), _F32),
        jax.ShapeDtypeStruct((NB * ls, RV_W), _F32),
        jax.ShapeDtypeStruct((NB, WINDOW, KV_W), _F32),
        jax.ShapeDtypeStruct((NB, WINDOW, KV_W), _F32),
        jax.ShapeDtypeStruct((NB, RET_HEADS, RET_DK, RET_DV), _F32),
    ]
    return pl.pallas_call(
        _sample_core_kernel,
        grid_spec=grid_spec,
        out_shape=out_shape,
        compiler_params=pltpu.CompilerParams(
            dimension_semantics=("arbitrary",), vmem_limit_bytes=VMEM_LIMIT),
        name="sample_core",
    )(cst['cdec_s'], proj2, ck, cv, st, cst['bias_hist'], cst['bias_new'], cst['sink_col'],
      cst['cos_s'], cst['sin_s'], cst['dec_s'], cst['qd_s'], cst['kd_s'])


def _sample_post_kernel(x_ref, attn_ref, ret_ref, ga_ref, gt_ref, wba_ref, wbr_ref, wout_ref,
                        nffn_ref, wrh_ref, wrl_ref, br_ref, cnt0_ref, x1_ref, route_ref, cnt_ref):
    x1, route, cnt = _post([x_ref[...]], [_dot(attn_ref[...].astype(_BF16), wba_ref[...])],
                           [ret_ref[...]], [ga_ref[...]], [gt_ref[...]],
                           wbr_ref[...], wout_ref[...], nffn_ref[...],
                           wrh_ref[...], wrl_ref[...], br_ref[...], cnt0_ref[:, 0:1])
    _store_rows(x1_ref, x1)
    route_ref[...] = route
    cnt_ref[...] = jnp.broadcast_to(cnt, cnt_ref.shape)


def _sample_post(x2d, attn, ret, ga, gt, w, cnt0):
    T, D = x2d.shape
    full = lambda s: pl.BlockSpec(s, lambda i: (0,) * len(s))
    return pl.pallas_call(
        _sample_post_kernel,
        grid=(1,),
        in_specs=[full((T, D)), full((T, ATTN_W)), full((T, RV_W)), full((T, D)), full((T, D)),
                  full((ATTN_W, D)), full((RV_W, D)), full((D, D)), full((1, D)),
                  full((D, ROUTE_LANES)), full((D, ROUTE_LANES)), full((1, ROUTE_LANES)),
                  full((ROUTE_ROWS, LANE))],
        out_specs=[full((T * SUB, LANE)), full((ROUTE_OUT, T)), full((ROUTE_ROWS, LANE))],
        out_shape=[jax.ShapeDtypeStruct((T * SUB, LANE), _F32),
                   jax.ShapeDtypeStruct((ROUTE_OUT, T), _F32),
                   jax.ShapeDtypeStruct((ROUTE_ROWS, LANE), _F32)],
        compiler_params=pltpu.CompilerParams(
            dimension_semantics=("arbitrary",), vmem_limit_bytes=VMEM_LIMIT),
        name="sample_post",
    )(x2d, attn, ret, ga, gt, w['wba'], w['wbr'], w['wout'], w['nffn'], w['wrh'], w['wrl'], w['br'],
      cnt0)


def _dispatch_kernel(dest_ref, xp_ref, xq_ref, xs_ref, sem, *, p_steps):
    i = pl.program_id(0)
    n_tok = pl.num_programs(0) * ROW_TM

    def copy_tile(src):
        def tile_copy(r, d):
            return pltpu.make_async_copy(src.at[r], xs_ref.at[d], sem)

        def start(r, c):
            t = i * ROW_TM + r
            tile_copy(r, dest_ref[t]).start(priority=0)
            tile_copy(r, dest_ref[n_tok + t]).start(priority=1)
            return c

        lax.fori_loop(0, ROW_TM, start, 0, unroll=DMA_UNROLL)

        def wait(r, c):
            tile_copy(0, 0).wait()
            tile_copy(0, 0).wait()
            return c

        lax.fori_loop(0, ROW_TM, wait, 0, unroll=DMA_UNROLL)

    @pl.when(i < p_steps)
    def _():
        copy_tile(xp_ref)

    @pl.when(i >= p_steps)
    def _():
        copy_tile(xq_ref)


def _dispatch(dest, xp3, xq3):
    Tp, Tq = xp3.shape[0], xq3.shape[0]
    assert Tp % ROW_TM == 0 and Tq % ROW_TM == 0
    p_steps = Tp // ROW_TM
    grid_spec = pltpu.PrefetchScalarGridSpec(
        num_scalar_prefetch=1,
        grid=((Tp + Tq) // ROW_TM,),
        in_specs=[
            pl.BlockSpec((ROW_TM, SUB, LANE), lambda i, *_: (jnp.minimum(i, p_steps - 1), 0, 0)),
            pl.BlockSpec((ROW_TM, SUB, LANE), lambda i, *_: (jnp.maximum(i - p_steps, 0), 0, 0)),
        ],
        out_specs=pl.BlockSpec(memory_space=pl.ANY),
        scratch_shapes=[pltpu.SemaphoreType.DMA],
    )
    return pl.pallas_call(
        functools.partial(_dispatch_kernel, p_steps=p_steps),
        grid_spec=grid_spec,
        out_shape=jax.ShapeDtypeStruct((2 * (Tp + Tq), SUB, LANE), _F32),
        compiler_params=pltpu.CompilerParams(dimension_semantics=("arbitrary",)),
        name="moe_dispatch",
    )(dest, xp3, xq3)


def _gmm_kernel(tile_ref, exp_ref, lo_ref, hi_ref, first_ref, chg_ref,
                x_ref, nffn_ref, wg_ref, wu_ref, wd_ref, y_ref, wg_s, wu_s, wd_s):
    m = pl.program_id(0)
    tm = x_ref.shape[0] // SUB

    @pl.when(chg_ref[m] == 1)
    def _():
        wg_s[...] = wg_ref[0].astype(_BF16)
        wu_s[...] = wu_ref[0].astype(_BF16)
        wd_s[...] = wd_ref[0].astype(_BF16)

    lo = lo_ref[m]
    hi = hi_ref[m]

    @pl.when(hi > lo)
    def _():
        hn = tm // GMM_PARTS
        parts = [pl.ds(j * hn * SUB, hn * SUB) for j in range(GMM_PARTS)]
        xs = []
        for j in range(GMM_PARTS):
            rows = tile_ref[m] * tm + j * hn + lax.broadcasted_iota(jnp.int32, (hn, 1), 0)
            mine = (rows >= lo) & (rows < hi)
            xn = _rms(_load_rows(x_ref.at[parts[j]], hn), nffn_ref[...])
            xs.append(jnp.where(mine, xn, 0.0).astype(_BF16))
        gate_up = [(_dot(x, wg_s[...]), _dot(x, wu_s[...])) for x in xs]
        ys = [_dot(((a * jax.nn.sigmoid(a)) * u).astype(_BF16), wd_s[...]) for a, u in gate_up]

        @pl.when(first_ref[m] == 1)
        def _():
            for j in range(GMM_PARTS):
                _store_rows(y_ref.at[parts[j]], ys[j])

        @pl.when(first_ref[m] == 0)
        def _():
            for j in range(GMM_PARTS):
                _store_rows(y_ref.at[parts[j]], _load_rows(y_ref.at[parts[j]], hn) + ys[j])


def _gmm(work, xs2, nffn, wg, wu, wd):
    A = xs2.shape[0] // SUB
    E, D, F = wg.shape
    n_work = work[0].shape[0]
    grid_spec = pltpu.PrefetchScalarGridSpec(
        num_scalar_prefetch=6,
        grid=(n_work,),
        in_specs=[
            pl.BlockSpec((MOE_TM * SUB, LANE), lambda m, t, e, *_: (t[m], 0)),
            pl.BlockSpec((1, D), lambda m, t, e, *_: (0, 0)),
            pl.BlockSpec((1, D, F), lambda m, t, e, *_: (e[m], 0, 0)),
            pl.BlockSpec((1, D, F), lambda m, t, e, *_: (e[m], 0, 0)),
            pl.BlockSpec((1, F, D), lambda m, t, e, *_: (e[m], 0, 0)),
        ],
        out_specs=pl.BlockSpec((MOE_TM * SUB, LANE), lambda m, t, e, *_: (t[m], 0)),
        scratch_shapes=[pltpu.VMEM((D, F), _BF16), pltpu.VMEM((D, F), _BF16),
                        pltpu.VMEM((F, D), _BF16)],
    )
    return pl.pallas_call(
        _gmm_kernel,
        grid_spec=grid_spec,
        out_shape=jax.ShapeDtypeStruct((A * SUB, LANE), _F32),
        compiler_params=pltpu.CompilerParams(
            dimension_semantics=("arbitrary",), vmem_limit_bytes=VMEM_LIMIT),
        name="moe_gmm",
    )(*work, xs2, nffn, wg, wu, wd)


def _combine_kernel(dest_ref, x1_ref, route_ref, nfin_ref, yb_ref, o_ref, buf, sems, *, tok0, n_tok):
    i = pl.program_id(0)
    n_steps = pl.num_programs(0)
    tm = o_ref.shape[0]

    def tile_copy(d, slot, k, r):
        rows = pl.ds(pl.multiple_of(r * SUB, SUB), SUB)
        return pltpu.make_async_copy(yb_ref.at[d], buf.at[slot, k, rows], sems.at[slot])

    def issue(step, slot):
        base = tok0 + step * tm

        def start(r, c):
            t = base + r
            tile_copy(dest_ref[t], slot, 0, r).start(priority=0)
            tile_copy(dest_ref[n_tok + t], slot, 1, r).start(priority=1)
            return c

        lax.fori_loop(0, tm, start, 0, unroll=DMA_UNROLL)

    @pl.when(i == 0)
    def _():
        issue(0, 0)

    @pl.when(i + 1 < n_steps)
    def _():
        issue(i + 1, (i + 1) % 2)

    slot = i % 2

    def wait(r, c):
        tile_copy(0, slot, 0, 0).wait()
        tile_copy(0, slot, 1, 0).wait()
        return c

    lax.fori_loop(0, tm, wait, 0, unroll=DMA_UNROLL)
    rt = jnp.concatenate([route_ref[...], jnp.zeros((LANE - ROUTE_OUT, tm), _F32)], axis=0).T
    g0 = rt[:, 2:3]
    g1 = rt[:, 3:4]
    y0 = _load_rows(buf.at[slot, 0], tm)
    y1 = _load_rows(buf.at[slot, 1], tm)
    y = _load_rows(x1_ref, tm) + (y0 * g0 + y1 * g1)
    o_ref[...] = _rms(y, nfin_ref[...])


def _combine(dest, x1_2, route, nfin, yb3, tok0):
    T = x1_2.shape[0] // SUB
    D = SUB * LANE
    tm = min(ROW_TM, T)
    grid_spec = pltpu.PrefetchScalarGridSpec(
        num_scalar_prefetch=1,
        grid=(T // tm,),
        in_specs=[
            pl.BlockSpec((tm * SUB, LANE), lambda i, *_: (i, 0)),
            pl.BlockSpec((ROUTE_OUT, tm), lambda i, *_: (0, i)),
            pl.BlockSpec((1, D), lambda i, *_: (0, 0)),
            pl.BlockSpec(memory_space=pl.ANY),
        ],
        out_specs=pl.BlockSpec((tm, D), lambda i, *_: (i, 0)),
        scratch_shapes=[pltpu.VMEM((2, 2, tm * SUB, LANE), _F32), pltpu.SemaphoreType.DMA((2,))],
    )
    return pl.pallas_call(
        functools.partial(_combine_kernel, tok0=tok0, n_tok=dest.shape[0] // 2),
        grid_spec=grid_spec,
        out_shape=jax.ShapeDtypeStruct((T, D), _F32),
        compiler_params=pltpu.CompilerParams(
            dimension_semantics=("arbitrary",), vmem_limit_bytes=VMEM_LIMIT),
        name="moe_combine",
    )(dest, x1_2, route, nfin, yb3)


def _routing_tables(route, counts, n_tiles):
    experts = route[0:2].astype(jnp.int32)
    ranks = route[4:6].astype(jnp.int32)
    A = experts.size
    ids = jnp.arange(N_EXPERTS, dtype=jnp.int32)
    ends = jnp.cumsum(counts)
    starts = ends - counts
    dest = ranks + jnp.sum(jnp.where(experts[..., None] == ids, starts, 0), axis=-1)
    dest = dest.reshape(-1)
    tile_starts = jnp.arange(n_tiles, dtype=jnp.int32) * MOE_TM
    pos_t = jnp.arange(n_tiles, dtype=jnp.int32) + jnp.sum(starts[None, :] < tile_starts[:, None], axis=1)
    pos_e = ids + jnp.sum(tile_starts[None, :] <= starts[:, None], axis=1)
    slots = jnp.arange(n_tiles + N_EXPERTS, dtype=jnp.int32)
    pts = (jnp.sum(jnp.where(pos_t[None, :] == slots[:, None], tile_starts[None, :], 0), axis=1)
           + jnp.sum(jnp.where(pos_e[None, :] == slots[:, None], starts[None, :], 0), axis=1))
    lo = pts.astype(jnp.int32)
    hi = jnp.concatenate([lo[1:], jnp.array([A], jnp.int32)])
    tile = jnp.minimum(lo // MOE_TM, n_tiles - 1).astype(jnp.int32)
    expert = jnp.minimum(jnp.sum(ends[None, :] <= lo[:, None], axis=1), N_EXPERTS - 1).astype(jnp.int32)
    nonempty = hi > lo
    first = (nonempty & (lo % MOE_TM == 0)).astype(jnp.int32)
    chg = jnp.concatenate([jnp.ones((1,), jnp.int32), (expert[1:] != expert[:-1]).astype(jnp.int32)])
    return dest.astype(jnp.int32), (tile, expert, lo, hi, first, chg)


def _bucket_table(lq, lk):
    dist = np.arange(lq)[:, None] + WINDOW - np.arange(lk)[None, :]
    band = (dist >= 0) & (dist < WINDOW)
    d = np.clip(dist, 0, WINDOW - 1)
    max_exact = N_BUCKETS // 2
    d_f = np.maximum(d, 1).astype(np.float32)
    large = max_exact + (np.log(d_f / max_exact) / math.log(MAX_DISTANCE / max_exact)
                         * (N_BUCKETS - max_exact)).astype(np.int32)
    large = np.minimum(large, N_BUCKETS - 1)
    return np.where(d < max_exact, d, large).astype(np.int32), band


def _bias_table(rb, lq, lk):
    bkt, band = _bucket_table(lq, lk)
    onehot = jnp.asarray(bkt)[None, :, :] == jnp.arange(N_BUCKETS, dtype=jnp.int32)[:, None, None]
    bias = jnp.sum(jnp.where(onehot[:, None], rb[:, :, None, None], 0.0), axis=0)
    return jnp.where(jnp.asarray(band)[None], bias, NEG_INF)


def _decay_tables(C):
    log_gamma = jnp.log(1.0 - 2.0 ** (-5.0 - jnp.arange(RET_HEADS, dtype=_F32)))
    idx = jnp.arange(C, dtype=_F32)
    diff = idx[:, None] - idx[None, :]
    decay_in = jnp.where((diff >= 0)[..., None],
                         jnp.exp(jnp.maximum(diff, 0.0)[..., None] * log_gamma), 0.0)
    q_dec = jnp.exp((idx + 1.0)[:, None] * log_gamma)
    k_dec = jnp.exp((C - 1.0 - idx)[:, None] * log_gamma)
    c_dec = jnp.exp(C * log_gamma)
    dec = jnp.transpose(decay_in, (2, 0, 1))
    qd = jnp.repeat(q_dec, RET_DK, axis=1)
    kd = jnp.repeat(k_dec, RET_DK, axis=1)
    return dec, qd, kd, c_dec


def _rope_tables(pos):
    half = RET_DK // 2
    inv = ROPE_BASE ** (-jnp.arange(half, dtype=_F32) * 2.0 / RET_DK)
    ang = pos.astype(_F32)[:, None] * inv[None, :]
    cos = jnp.cos(ang)
    sin = jnp.sin(ang)
    return jnp.concatenate([cos, cos], axis=1), jnp.concatenate([-sin, sin], axis=1)


def _constants(rel_bias, attn_sink, L, ls):
    cst = {}
    rb = rel_bias.astype(_F32)
    cst['bias_p'] = jnp.transpose(_bias_table(rb, WINDOW, 2 * WINDOW), (0, 2, 1))
    cst['sink'] = attn_sink.astype(_F32)
    cst['cos_p'], cst['sin_p'] = _rope_tables(jnp.arange(L))
    cst['dec_p'], cst['qd_p'], cst['kd_p'], cst['cdec_p'] = _decay_tables(min(RET_CHUNK, L))
    bias = _bias_table(rb, ls, WINDOW + ls).reshape(KV_HEADS, GQA_GROUP * ls, WINDOW + ls)
    cst['bias_hist'] = bias[:, :, :WINDOW]
    cst['bias_new'] = bias[:, :, WINDOW:]
    cst['sink_col'] = jnp.repeat(attn_sink.astype(_F32).reshape(KV_HEADS, GQA_GROUP), ls,
                                 axis=1)[..., None]
    cst['cos_s'], cst['sin_s'] = _rope_tables(PAST_LEN + jnp.arange(ls))
    cst['dec_s'], cst['qd_s'], cst['kd_s'], cst['cdec_s'] = _decay_tables(min(RET_CHUNK, ls))
    return cst


def _layer_weights(layer, norm_mix, w_in, w_branch_attn, w_branch_ret, w_out, norm_ffn,
                   w_router_group, b_router_group, w_router_expert, b_router_expert):
    D = w_in.shape[1]
    wr = jnp.concatenate([w_router_group[layer].astype(_F32), w_router_expert[layer].astype(_F32)], axis=1)
    wr = jnp.pad(wr, ((0, 0), (0, ROUTE_LANES - wr.shape[1])))
    wrh = wr.astype(_BF16)
    wrl = (wr - wrh.astype(_F32)).astype(_BF16)
    br = jnp.concatenate([b_router_group[layer].astype(_F32), b_router_expert[layer].astype(_F32)])
    br = jnp.pad(br, (0, ROUTE_LANES - br.shape[0]))[None, :]
    return {
        'nmix': norm_mix[layer].astype(_F32)[None, :],
        'win': w_in[layer].astype(_BF16),
        'wba': w_branch_attn[layer].astype(_BF16),
        'wbr': w_branch_ret[layer].astype(_BF16),
        'wout': w_out[layer].astype(_BF16),
        'nffn': norm_ffn[layer].astype(_F32)[None, :],
        'wrh': wrh, 'wrl': wrl, 'br': br,
    }


def kernel(x_prompt, x_sample, cache_k, cache_v, state_ret, norm_mix, w_in, attn_sink, rel_bias,
           w_branch_attn, w_branch_ret, w_out, norm_ffn, w_router_group, b_router_group,
           w_router_expert, b_router_expert, w_gate, w_up, w_down, norm_final):
    depth = w_in.shape[0]
    assert depth == 1, "the final norm is fused into the MoE combine of the only layer"
    B, L, D = x_prompt.shape
    NB, ls, _ = x_sample.shape
    Tp, Ts = B * L, NB * ls
    nfin = norm_final.astype(_F32)[None, :]
    yp, ys = x_prompt, x_sample
    pk, pv, ps, sk, sv, ss = [], [], [], [], [], []
    for layer in range(depth):
        w = _layer_weights(layer, norm_mix, w_in, w_branch_attn, w_branch_ret, w_out, norm_ffn,
                           w_router_group, b_router_group, w_router_expert, b_router_expert)
        cst = _constants(rel_bias, attn_sink[layer], L, ls)
        x1p, routep, k1, v1, s1, cnt_p = _prompt_mixer(yp, cst, w)
        ys2 = ys.reshape(Ts, D)
        proj = _sample_inproj(ys2, w)
        attn_s, ret_s, k2, v2, s2 = _sample_core(
            proj, ls,
            cache_k[layer].reshape(NB, WINDOW, KV_W), cache_v[layer].reshape(NB, WINDOW, KV_W),
            state_ret[layer], cst)
        x1s, routes, cnt_all = _sample_post(
            ys2, attn_s, ret_s,
            proj[:, OFF_GA:OFF_GA + D], proj[:, OFF_GT:OFF_GT + D], w, cnt_p)
        n_rows = 2 * (Tp + Ts)
        assert n_rows % MOE_TM == 0
        counts = cnt_all[N_GROUPS:N_GROUPS + N_EXPERTS, 0].astype(jnp.int32)
        dest, work = _routing_tables(jnp.concatenate([routep, routes], axis=1), counts,
                                     n_rows // MOE_TM)
        xs3 = _dispatch(dest, x1p.reshape(Tp, SUB, LANE), x1s.reshape(Ts, SUB, LANE))
        yb2 = _gmm(work, xs3.reshape(n_rows * SUB, LANE), w['nffn'],
                   w_gate[layer], w_up[layer], w_down[layer])
        yb3 = yb2.reshape(n_rows, SUB, LANE)
        yp = _combine(dest, x1p, routep, nfin, yb3, 0).reshape(B, L, D)
        ys = _combine(dest, x1s, routes, nfin, yb3, Tp).reshape(NB, ls, D)
        pk.append(k1.reshape(B, WINDOW, KV_HEADS, HEAD_DIM))
        pv.append(v1.reshape(B, WINDOW, KV_HEADS, HEAD_DIM))
        ps.append(s1)
        sk.append(k2.reshape(NB, WINDOW, KV_HEADS, HEAD_DIM))
        sv.append(v2.reshape(NB, WINDOW, KV_HEADS, HEAD_DIM))
        ss.append(s2)
    return (yp, ys, jnp.stack(pk), jnp.stack(pv), jnp.stack(ps),
            jnp.stack(sk), jnp.stack(sv), jnp.stack(ss))
```

```python
import functools
import math

import jax
import jax.numpy as jnp
import numpy as np
from jax import lax
from jax.experimental import pallas as pl
from jax.experimental.pallas import tpu as pltpu

HEAD_DIM = 64
KV_HEADS = 4
GQA_GROUP = 4
ATTN_HEADS = KV_HEADS * GQA_GROUP
WINDOW = 128
N_BUCKETS = 32
MAX_DISTANCE = 128
RET_HEADS = 4
RET_DK = 128
RET_DV = 256
RET_CHUNK = 128
ROPE_BASE = 10000.0
N_GROUPS = 4
EXPERTS_PER_GROUP = 8
N_EXPERTS = N_GROUPS * EXPERTS_PER_GROUP
EXPERT_FF = 512
NORM_EPS = 1e-6
NEG_INF = -1e30
PAST_LEN = 16384

ATTN_W = ATTN_HEADS * HEAD_DIM
KV_W = KV_HEADS * HEAD_DIM
RQ_W = RET_HEADS * RET_DK
RV_W = RET_HEADS * RET_DV
OFF_QA = 0
OFF_KA = OFF_QA + ATTN_W
OFF_VA = OFF_KA + KV_W
OFF_QR = OFF_VA + KV_W
OFF_KR = OFF_QR + RQ_W
OFF_VR = OFF_KR + RQ_W
OFF_GR = OFF_VR + RV_W
OFF_GA = OFF_GR + RV_W
OFF_GT = OFF_GA + 1024
ROUTE_LANES = 128
ROUTE_ROWS = 40
ROUTE_OUT = 8

LANE = 128
SUB = 8
POST_PARTS = 2
PROMPT_TM = 512
SAMPLE_GROUP = 8
MOE_TM = 512
GMM_SUBTILE = 256
GMM_PARTS = 2
ROW_TM = 512
DMA_UNROLL = 8
VMEM_LIMIT = 60 * 1024 * 1024

_F32 = jnp.float32
_BF16 = jnp.bfloat16


def _const_spec(shape):
    nd = len(shape)
    return pl.BlockSpec(shape, lambda *_: (0,) * nd, pipeline_mode=pl.Buffered(1))


def _rms(x, gain):
    return x * lax.rsqrt(jnp.mean(x * x, axis=-1, keepdims=True) + NORM_EPS) * gain


def _dot(a, b):
    return jnp.dot(a, b, preferred_element_type=_F32)


def _dot_nt(a, b):
    return lax.dot_general(a, b, (((1,), (1,)), ((), ())), preferred_element_type=_F32)


def _dot_tn(a, b):
    return lax.dot_general(a, b, (((0,), (0,)), ((), ())), preferred_element_type=_F32)


def _load_rows(ref, n):
    return jnp.concatenate([ref[pl.ds(s, n, stride=SUB), :] for s in range(SUB)], axis=1)


def _store_rows(ref, val):
    n = val.shape[0]
    for s in range(SUB):
        ref[pl.ds(s, n, stride=SUB), :] = val[:, s * LANE:(s + 1) * LANE]


def _rotary(x, cosf, sinf):
    return x * cosf + pltpu.roll(x, RET_DK // 2, 1) * sinf


def _post(xs, attn_projs, rets, gates_a, gates_r, wbr, wout, nffn, wrh, wrl, br, cnt):
    parts = range(len(xs))
    ret_projs = [_dot(rets[j].astype(_BF16), wbr) for j in parts]
    x1s = []
    for j in parts:
        merged = jax.nn.sigmoid(gates_a[j]) * attn_projs[j] + jax.nn.sigmoid(gates_r[j]) * ret_projs[j]
        x1s.append(xs[j] + _dot(merged.astype(_BF16), wout))
    logit_parts = []
    for j in parts:
        xn2 = _rms(x1s[j], nffn)
        hi = xn2.astype(_BF16)
        lo = (xn2 - hi.astype(_F32)).astype(_BF16)
        logit_parts.append(_dot(hi, wrh) + (_dot(hi, wrl) + _dot(lo, wrh)) + br)
    x1 = x1s[0] if len(x1s) == 1 else jnp.concatenate(x1s, axis=0)
    logits = logit_parts[0] if len(x1s) == 1 else jnp.concatenate(logit_parts, axis=0)
    n = logits.shape[0]
    lt = logits.T[0:ROUTE_ROWS, :]
    row = lax.broadcasted_iota(jnp.int32, (ROUTE_ROWS, n), 0)
    big = jnp.int32(1 << 20)
    neg = jnp.float32(-jnp.inf)
    gl = jnp.where(row < N_GROUPS, lt, neg)
    gmax = jnp.max(gl, axis=0, keepdims=True)
    gexp = jnp.exp(gl - gmax)
    gsum = jnp.sum(gexp, axis=0, keepdims=True)
    pg = gexp / gsum
    g_w = jnp.max(pg, axis=0, keepdims=True)
    g_idx = jnp.min(jnp.where(pg == g_w, row, big), axis=0, keepdims=True)
    e_row = row - N_GROUPS
    emask = (e_row >= 0) & (e_row < N_EXPERTS) & ((e_row >> 3) == g_idx)
    fl = jnp.where(emask, lt, neg)
    fmax = jnp.max(fl, axis=0, keepdims=True)
    fexp = jnp.exp(fl - fmax)
    fsum = jnp.sum(fexp, axis=0, keepdims=True)
    pe = jnp.where(emask, fexp / fsum, -1.0)
    p1 = jnp.max(pe, axis=0, keepdims=True)
    i1 = jnp.min(jnp.where(pe == p1, row, big), axis=0, keepdims=True)
    pe2 = jnp.where(row == i1, -1.0, pe)
    p2 = jnp.max(pe2, axis=0, keepdims=True)
    i2 = jnp.min(jnp.where(pe2 == p2, row, big), axis=0, keepdims=True)
    psum = p1 + p2
    gate1 = g_w * p1 / psum
    gate2 = g_w * p2 / psum
    oh1 = row == i1
    oh2 = row == i2
    c = jnp.where(oh1 | oh2, 1.0, 0.0)
    tt = lax.broadcasted_iota(jnp.int32, (n, n), 0)
    tc = lax.broadcasted_iota(jnp.int32, (n, n), 1)
    upper = jnp.where(tt < tc, 1.0, 0.0).astype(_BF16)
    before = _dot(c.astype(_BF16), upper) + cnt
    rank1 = jnp.sum(jnp.where(oh1, before, 0.0), axis=0, keepdims=True)
    rank2 = jnp.sum(jnp.where(oh2, before, 0.0), axis=0, keepdims=True)
    cnt = cnt + jnp.sum(c, axis=1, keepdims=True)
    r8 = lax.broadcasted_iota(jnp.int32, (ROUTE_OUT, n), 0)
    vals = [(i1 - N_GROUPS).astype(_F32), (i2 - N_GROUPS).astype(_F32), gate1, gate2, rank1, rank2]
    route = jnp.zeros((ROUTE_OUT, n), _F32)
    for k, v in enumerate(vals):
        route = jnp.where(r8 == k, v, route)
    return x1, route, cnt


def _prompt_mixer_kernel(sink_ref, cdec_ref,
                         x_ref, nmix_ref, win_ref, bias_ref, cos_ref, sin_ref, dec_ref,
                         qd_ref, kd_ref, wba_ref, wbr_ref, wout_ref, nffn_ref,
                         wrh_ref, wrl_ref, br_ref,
                         x1_ref, route_ref, knew_ref, vnew_ref, s_ref, cnt_ref,
                         qkv, proj, kctx, vctx, attn_t, ret):
    i = pl.program_id(1)
    last = pl.num_programs(1) - 1

    @pl.when((i == 0) & (pl.program_id(0) == 0))
    def _():
        cnt_ref[...] = jnp.zeros_like(cnt_ref)

    tm = x_ref.shape[1]
    n_sub = tm // WINDOW
    scale = HEAD_DIM ** -0.5

    @pl.when(i == 0)
    def _():
        s_ref[...] = jnp.zeros_like(s_ref)
        kctx[0:WINDOW, :] = jnp.zeros((WINDOW, KV_W), _BF16)
        vctx[0:WINDOW, :] = jnp.zeros((WINDOW, KV_W), _BF16)

    x = x_ref[0]
    xn = _rms(x, nmix_ref[...]).astype(_BF16)
    n_in = win_ref.shape[1]
    panel = 256

    def project(c0):
        res = _dot(xn, win_ref[:, c0:c0 + panel])
        if c0 < OFF_QR:
            qkv[:, c0:c0 + panel] = res
        else:
            proj[:, c0 - OFF_QR:c0 - OFF_QR + panel] = res

    def cols(lo, width):
        return slice(lo - OFF_QR, lo - OFF_QR + width)

    for c0 in range(0, OFF_QR, panel):
        project(c0)
    later_panels = list(range(OFF_QR, n_in, panel))

    @pl.when(i == last)
    def _():
        knew_ref[0] = qkv[tm - WINDOW:tm, OFF_KA:OFF_KA + KV_W]
        vnew_ref[0] = qkv[tm - WINDOW:tm, OFF_VA:OFF_VA + KV_W]

    krow = lax.broadcasted_iota(jnp.int32, (2 * WINDOW, 1), 0)
    for c in range(n_sub):
        r0 = c * WINDOW
        kctx[WINDOW:2 * WINDOW, :] = qkv[r0:r0 + WINDOW, OFF_KA:OFF_KA + KV_W].astype(_BF16)
        vctx[WINDOW:2 * WINDOW, :] = qkv[r0:r0 + WINDOW, OFF_VA:OFF_VA + KV_W].astype(_BF16)
        if c == 0:
            pen = jnp.where((krow < WINDOW) & (i == 0), NEG_INF, 0.0).astype(_F32)
        for h in range(KV_HEADS):
            k_h = kctx[:, h * HEAD_DIM:(h + 1) * HEAD_DIM]
            v_h = vctx[:, h * HEAD_DIM:(h + 1) * HEAD_DIM]
            probs = []
            for g in range(GQA_GROUP):
                hq = h * GQA_GROUP + g
                q = (qkv[r0:r0 + WINDOW, hq * HEAD_DIM:(hq + 1) * HEAD_DIM] * scale).astype(_BF16)
                s = _dot_nt(k_h, q) + bias_ref[hq]
                if c == 0:
                    s = s + pen
                snk = sink_ref[hq]
                m = jnp.maximum(jnp.max(s, axis=0, keepdims=True), snk)
                p = jnp.exp(s - m)
                den = jnp.sum(p, axis=0, keepdims=True) + jnp.exp(snk - m)
                probs.append((p * (1.0 / den)).astype(_BF16))
            groups_left = (n_sub - c) * KV_HEADS - h
            for _ in range(-(-len(later_panels) // groups_left)):
                project(later_panels.pop(0))
            o_t = _dot_tn(v_h, jnp.concatenate(probs, axis=1))
            for g in range(GQA_GROUP):
                hq = h * GQA_GROUP + g
                attn_t[hq * HEAD_DIM:(hq + 1) * HEAD_DIM, r0:r0 + WINDOW] = (
                    o_t[:, g * WINDOW:(g + 1) * WINDOW].astype(_BF16))
        kctx[0:WINDOW, :] = kctx[WINDOW:2 * WINDOW, :]
        vctx[0:WINDOW, :] = vctx[WINDOW:2 * WINDOW, :]

    for c0 in later_panels:
        project(c0)

    for c in range(n_sub):
        r0 = c * RET_CHUNK
        cosf = cos_ref[r0:r0 + RET_CHUNK, :]
        sinf = sin_ref[r0:r0 + RET_CHUNK, :]
        for h in range(RET_HEADS):
            qc = _rotary(proj[r0:r0 + RET_CHUNK, cols(OFF_QR + h * RET_DK, RET_DK)], cosf, sinf)
            kc = _rotary(proj[r0:r0 + RET_CHUNK, cols(OFF_KR + h * RET_DK, RET_DK)], cosf, sinf) * (RET_DK ** -0.5)
            qd = qd_ref[:, h * RET_DK:(h + 1) * RET_DK]
            kd = kd_ref[:, h * RET_DK:(h + 1) * RET_DK]
            vc = proj[r0:r0 + RET_CHUNK, cols(OFF_VR + h * RET_DV, RET_DV)].astype(_BF16)
            sc = _dot_nt(qc.astype(_BF16), kc.astype(_BF16))
            s_old = s_ref[0, h]
            cross = _dot((qc * qd).astype(_BF16), s_old.astype(_BF16))
            s_ref[0, h] = s_old * cdec_ref[h] + _dot_tn((kc * kd).astype(_BF16), vc)
            o = _dot((sc * dec_ref[h]).astype(_BF16), vc) + cross
            o = o * lax.rsqrt(jnp.mean(o * o, axis=-1, keepdims=True) + NORM_EPS)
            gr = proj[r0:r0 + RET_CHUNK, cols(OFF_GR + h * RET_DV, RET_DV)]
            ret[r0:r0 + RET_CHUNK, h * RET_DV:(h + 1) * RET_DV] = o * (gr * jax.nn.sigmoid(gr))

    pn = tm // POST_PARTS
    rows = [slice(j * pn, (j + 1) * pn) for j in range(POST_PARTS)]
    x1, route, cnt = _post([x[r] for r in rows],
                           [_dot_tn(attn_t[:, r], wba_ref[...]) for r in rows],
                           [ret[r, :] for r in rows],
                           [proj[r, cols(OFF_GA, 1024)] for r in rows],
                           [proj[r, cols(OFF_GT, 1024)] for r in rows],
                           wbr_ref[...], wout_ref[...], nffn_ref[...],
                           wrh_ref[...], wrl_ref[...], br_ref[...], cnt_ref[:, 0:1])
    _store_rows(x1_ref, x1)
    route_ref[...] = route
    cnt_ref[...] = jnp.broadcast_to(cnt, cnt_ref.shape)


def _prompt_mixer(x, cst, w):
    B, L, D = x.shape
    tm = min(PROMPT_TM, L)
    nb = L // tm
    n_in = w['win'].shape[1]
    step = lambda b, i, *_: (b, i, 0)
    per_b = lambda b, i, *_: (b, 0, 0)

    grid_spec = pltpu.PrefetchScalarGridSpec(
        num_scalar_prefetch=2,
        grid=(B, nb),
        in_specs=[
            pl.BlockSpec((1, tm, D), step),
            _const_spec((1, D)),
            _const_spec((D, n_in)),
            _const_spec((ATTN_HEADS, 2 * WINDOW, WINDOW)),
            pl.BlockSpec((tm, RET_DK), lambda b, i, *_: (i, 0)),
            pl.BlockSpec((tm, RET_DK), lambda b, i, *_: (i, 0)),
            _const_spec((RET_HEADS, RET_CHUNK, RET_CHUNK)),
            _const_spec((RET_CHUNK, RQ_W)),
            _const_spec((RET_CHUNK, RQ_W)),
            _const_spec((ATTN_W, D)),
            _const_spec((RV_W, D)),
            _const_spec((D, D)),
            _const_spec((1, D)),
            _const_spec((D, ROUTE_LANES)),
            _const_spec((D, ROUTE_LANES)),
            _const_spec((1, ROUTE_LANES)),
        ],
        out_specs=[
            pl.BlockSpec((tm * SUB, LANE), lambda b, i, *_: (b * nb + i, 0)),
            pl.BlockSpec((ROUTE_OUT, tm), lambda b, i, *_: (0, b * nb + i)),
            pl.BlockSpec((1, WINDOW, KV_W), per_b),
            pl.BlockSpec((1, WINDOW, KV_W), per_b),
            pl.BlockSpec((1, RET_HEADS, RET_DK, RET_DV), lambda b, i, *_: (b, 0, 0, 0)),
            pl.BlockSpec((ROUTE_ROWS, LANE), lambda b, i, *_: (0, 0)),
        ],
        scratch_shapes=[
            pltpu.VMEM((tm, OFF_QR), _F32),
            pltpu.VMEM((tm, n_in - OFF_QR), _F32),
            pltpu.VMEM((2 * WINDOW, KV_W), _BF16),
            pltpu.VMEM((2 * WINDOW, KV_W), _BF16),
            pltpu.VMEM((ATTN_W, tm), _BF16),
            pltpu.VMEM((tm, RV_W), _F32),
        ],
    )
    assert D == SUB * LANE
    out_shape = [
        jax.ShapeDtypeStruct((B * L * SUB, LANE), _F32),
        jax.ShapeDtypeStruct((ROUTE_OUT, B * L), _F32),
        jax.ShapeDtypeStruct((B, WINDOW, KV_W), _F32),
        jax.ShapeDtypeStruct((B, WINDOW, KV_W), _F32),
        jax.ShapeDtypeStruct((B, RET_HEADS, RET_DK, RET_DV), _F32),
        jax.ShapeDtypeStruct((ROUTE_ROWS, LANE), _F32),
    ]
    return pl.pallas_call(
        _prompt_mixer_kernel,
        grid_spec=grid_spec,
        out_shape=out_shape,
        compiler_params=pltpu.CompilerParams(
            dimension_semantics=("arbitrary", "arbitrary"), vmem_limit_bytes=VMEM_LIMIT),
        name="prompt_mixer",
    )(cst['sink'], cst['cdec_p'],
      x, w['nmix'], w['win'], cst['bias_p'], cst['cos_p'], cst['sin_p'], cst['dec_p'],
      cst['qd_p'], cst['kd_p'], w['wba'], w['wbr'], w['wout'], w['nffn'],
      w['wrh'], w['wrl'], w['br'])


def _inproj_kernel(x_ref, nmix_ref, win_ref, o_ref):
    xn = _rms(x_ref[...], nmix_ref[...]).astype(_BF16)
    o_ref[...] = _dot(xn, win_ref[...])


def _sample_inproj(x2d, w):
    T, D = x2d.shape
    n_in = w['win'].shape[1]
    panel = 512
    return pl.pallas_call(
        _inproj_kernel,
        grid=(n_in // panel,),
        in_specs=[pl.BlockSpec((T, D), lambda j: (0, 0)),
                  pl.BlockSpec((1, D), lambda j: (0, 0)),
                  pl.BlockSpec((D, panel), lambda j: (0, j))],
        out_specs=pl.BlockSpec((T, panel), lambda j: (0, j)),
        out_shape=jax.ShapeDtypeStruct((T, n_in), _F32),
        compiler_params=pltpu.CompilerParams(
            dimension_semantics=("arbitrary",), vmem_limit_bytes=VMEM_LIMIT),
        name="sample_inproj",
    )(x2d, w['nmix'], w['win'])


def _sample_core_kernel(cdec_ref, proj_ref, ck_ref, cv_ref, st_ref, bh_ref, bn_ref, snk_ref,
                        cos_ref, sin_ref, dec_ref, qd_ref, kd_ref,
                        attn_ref, ret_ref, nk_ref, nv_ref, ns_ref):
    G = ck_ref.shape[0]
    ls = proj_ref.shape[0] // G
    per = SUB // ls
    scale = HEAD_DIM ** -0.5
    cosf = cos_ref[...]
    sinf = sin_ref[...]
    heads = [slice(h * HEAD_DIM, (h + 1) * HEAD_DIM) for h in range(KV_HEADS)]

    def body(j, carry):
        group = pl.ds(pl.multiple_of(j * SUB, SUB), SUB)
        rows = proj_ref[group, :]
        seqs = []
        for s in range(per):
            b = j * per + s
            row = rows[s * ls:(s + 1) * ls]
            k_new = row[:, OFF_KA:OFF_KA + KV_W]
            v_new = row[:, OFF_VA:OFF_VA + KV_W]
            ck = ck_ref[b]
            cv = cv_ref[b]
            nk_ref[b, 0:WINDOW - ls, :] = ck[ls:WINDOW, :]
            nk_ref[b, WINDOW - ls:WINDOW, :] = k_new
            nv_ref[b, 0:WINDOW - ls, :] = cv[ls:WINDOW, :]
            nv_ref[b, WINDOW - ls:WINDOW, :] = v_new
            seqs.append(dict(b=b, row=row, ckb=ck.astype(_BF16), cvb=cv.astype(_BF16),
                             knb=k_new.astype(_BF16), vnb=v_new.astype(_BF16)))
        for q in seqs:
            row = q['row']
            q['s1'], q['s2'] = [], []
            for h in range(KV_HEADS):
                q4 = (jnp.concatenate(
                    [row[:, (h * GQA_GROUP + g) * HEAD_DIM:(h * GQA_GROUP + g + 1) * HEAD_DIM]
                     for g in range(GQA_GROUP)], axis=0) * scale).astype(_BF16)
                q['s1'].append(_dot_nt(q4, q['ckb'][:, heads[h]]))
                q['s2'].append(_dot_nt(q4, q['knb'][:, heads[h]]))
        for q in seqs:
            row, b = q['row'], q['b']
            q['sc'], q['cross'], q['vc'] = [], [], []
            for h in range(RET_HEADS):
                qrot = _rotary(row[:, OFF_QR + h * RET_DK:OFF_QR + (h + 1) * RET_DK], cosf, sinf)
                krot = _rotary(row[:, OFF_KR + h * RET_DK:OFF_KR + (h + 1) * RET_DK], cosf, sinf) * (RET_DK ** -0.5)
                vc = row[:, OFF_VR + h * RET_DV:OFF_VR + (h + 1) * RET_DV].astype(_BF16)
                qd = qd_ref[:, h * RET_DK:(h + 1) * RET_DK]
                kd = kd_ref[:, h * RET_DK:(h + 1) * RET_DK]
                s_old = st_ref[b, h]
                q['sc'].append(_dot_nt(qrot.astype(_BF16), krot.astype(_BF16)))
                q['cross'].append(_dot((qrot * qd).astype(_BF16), s_old.astype(_BF16)))
                ns_ref[b, h] = s_old * cdec_ref[h] + _dot_tn((krot * kd).astype(_BF16), vc)
                q['vc'].append(vc)
        for q in seqs:
            q['p1'], q['p2'] = [], []
            for h in range(KV_HEADS):
                s1 = q['s1'][h] + bh_ref[h]
                s2 = q['s2'][h] + bn_ref[h]
                snk = snk_ref[h]
                m = jnp.maximum(jnp.maximum(jnp.max(s1, axis=-1, keepdims=True),
                                            jnp.max(s2, axis=-1, keepdims=True)), snk)
                p1 = jnp.exp(s1 - m)
                p2 = jnp.exp(s2 - m)
                den = (jnp.sum(p1, axis=-1, keepdims=True) + jnp.sum(p2, axis=-1, keepdims=True)
                       + jnp.exp(snk - m))
                r = 1.0 / den
                q['p1'].append((p1 * r).astype(_BF16))
                q['p2'].append((p2 * r).astype(_BF16))
            q['scb'] = [(q['sc'][h] * dec_ref[h]).astype(_BF16) for h in range(RET_HEADS)]
        for q in seqs:
            q['out'] = [_dot(q['p1'][h], q['cvb'][:, heads[h]]) + _dot(q['p2'][h], q['vnb'][:, heads[h]])
                        for h in range(KV_HEADS)]
            q['ret'] = [_dot(q['scb'][h], q['vc'][h]) + q['cross'][h] for h in range(RET_HEADS)]
        for h in range(KV_HEADS):
            for g in range(GQA_GROUP):
                hq = h * GQA_GROUP + g
                attn_ref[group, hq * HEAD_DIM:(hq + 1) * HEAD_DIM] = jnp.concatenate(
                    [q['out'][h][g * ls:(g + 1) * ls] for q in seqs], axis=0)
        for h in range(RET_HEADS):
            o = jnp.concatenate([q['ret'][h] for q in seqs], axis=0)
            o = o * lax.rsqrt(jnp.mean(o * o, axis=-1, keepdims=True) + NORM_EPS)
            gr = rows[:, OFF_GR + h * RET_DV:OFF_GR + (h + 1) * RET_DV]
            ret_ref[group, h * RET_DV:(h + 1) * RET_DV] = o * (gr * jax.nn.sigmoid(gr))
        return carry

    lax.fori_loop(0, G // per, body, 0)


def _sample_core(proj2, ls, ck, cv, st, cst):
    n_in = proj2.shape[1]
    NB = proj2.shape[0] // ls
    G = min(SAMPLE_GROUP, NB)
    assert SUB % ls == 0 and G % (SUB // ls) == 0 and NB % G == 0
    row_blk = lambda i, *_: (i, 0)
    blk3 = lambda i, *_: (i, 0, 0)
    blk4 = lambda i, *_: (i, 0, 0, 0)
    c2 = lambda i, *_: (0, 0)
    c3 = lambda i, *_: (0, 0, 0)
    ql = GQA_GROUP * ls
    grid_spec = pltpu.PrefetchScalarGridSpec(
        num_scalar_prefetch=1,
        grid=(NB // G,),
        in_specs=[
            pl.BlockSpec((G * ls, n_in), row_blk),
            pl.BlockSpec((G, WINDOW, KV_W), blk3),
            pl.BlockSpec((G, WINDOW, KV_W), blk3),
            pl.BlockSpec((G, RET_HEADS, RET_DK, RET_DV), blk4),
            pl.BlockSpec((KV_HEADS, ql, WINDOW), c3),
            pl.BlockSpec((KV_HEADS, ql, ls), c3),
            pl.BlockSpec((KV_HEADS, ql, 1), c3),
            pl.BlockSpec((ls, RET_DK), c2),
            pl.BlockSpec((ls, RET_DK), c2),
            pl.BlockSpec((RET_HEADS, ls, ls), c3),
            pl.BlockSpec((ls, RQ_W), c2),
            pl.BlockSpec((ls, RQ_W), c2),
        ],
        out_specs=[
            pl.BlockSpec((G * ls, ATTN_W), row_blk),
            pl.BlockSpec((G * ls, RV_W), row_blk),
            pl.BlockSpec((G, WINDOW, KV_W), blk3),
            pl.BlockSpec((G, WINDOW, KV_W), blk3),
            pl.BlockSpec((G, RET_HEADS, RET_DK, RET_DV), blk4),
        ],
    )
    out_shape = [
        jax.ShapeDtypeStruct((NB * ls, ATTN_W), _F32),
        jax.ShapeDtypeStruct((NB * ls, RV_W), _F32),
        jax.ShapeDtypeStruct((NB, WINDOW, KV_W), _F32),
        jax.ShapeDtypeStruct((NB, WINDOW, KV_W), _F32),
        jax.ShapeDtypeStruct((NB, RET_HEADS, RET_DK, RET_DV), _F32),
    ]
    return pl.pallas_call(
        _sample_core_kernel,
        grid_spec=grid_spec,
        out_shape=out_shape,
        compiler_params=pltpu.CompilerParams(
            dimension_semantics=("arbitrary",), vmem_limit_bytes=VMEM_LIMIT),
        name="sample_core",
    )(cst['cdec_s'], proj2, ck, cv, st, cst['bias_hist'], cst['bias_new'], cst['sink_col'],
      cst['cos_s'], cst['sin_s'], cst['dec_s'], cst['qd_s'], cst['kd_s'])


def _sample_post_kernel(x_ref, attn_ref, ret_ref, ga_ref, gt_ref, wba_ref, wbr_ref, wout_ref,
                        nffn_ref, wrh_ref, wrl_ref, br_ref, cnt0_ref, x1_ref, route_ref, cnt_ref):
    x1, route, cnt = _post([x_ref[...]], [_dot(attn_ref[...].astype(_BF16), wba_ref[...])],
                           [ret_ref[...]], [ga_ref[...]], [gt_ref[...]],
                           wbr_ref[...], wout_ref[...], nffn_ref[...],
                           wrh_ref[...], wrl_ref[...], br_ref[...], cnt0_ref[:, 0:1])
    _store_rows(x1_ref, x1)
    route_ref[...] = route
    cnt_ref[...] = jnp.broadcast_to(cnt, cnt_ref.shape)


def _sample_post(x2d, attn, ret, ga, gt, w, cnt0):
    T, D = x2d.shape
    full = lambda s: pl.BlockSpec(s, lambda i: (0,) * len(s))
    return pl.pallas_call(
        _sample_post_kernel,
        grid=(1,),
        in_specs=[full((T, D)), full((T, ATTN_W)), full((T, RV_W)), full((T, D)), full((T, D)),
                  full((ATTN_W, D)), full((RV_W, D)), full((D, D)), full((1, D)),
                  full((D, ROUTE_LANES)), full((D, ROUTE_LANES)), full((1, ROUTE_LANES)),
                  full((ROUTE_ROWS, LANE))],
        out_specs=[full((T * SUB, LANE)), full((ROUTE_OUT, T)), full((ROUTE_ROWS, LANE))],
        out_shape=[jax.ShapeDtypeStruct((T * SUB, LANE), _F32),
                   jax.ShapeDtypeStruct((ROUTE_OUT, T), _F32),
                   jax.ShapeDtypeStruct((ROUTE_ROWS, LANE), _F32)],
        compiler_params=pltpu.CompilerParams(
            dimension_semantics=("arbitrary",), vmem_limit_bytes=VMEM_LIMIT),
        name="sample_post",
    )(x2d, attn, ret, ga, gt, w['wba'], w['wbr'], w['wout'], w['nffn'], w['wrh'], w['wrl'], w['br'],
      cnt0)


def _dispatch_kernel(dest_ref, xp_ref, xq_ref, xs_ref, sem, *, p_steps):
    i = pl.program_id(0)
    n_tok = pl.num_programs(0) * ROW_TM

    def copy_tile(src):
        def tile_copy(r, d):
            return pltpu.make_async_copy(src.at[r], xs_ref.at[d], sem)

        def start(r, c):
            t = i * ROW_TM + r
            tile_copy(r, dest_ref[t]).start(priority=0)
            tile_copy(r, dest_ref[n_tok + t]).start(priority=1)
            return c

        lax.fori_loop(0, ROW_TM, start, 0, unroll=DMA_UNROLL)

        def wait(r, c):
            tile_copy(0, 0).wait()
            tile_copy(0, 0).wait()
            return c

        lax.fori_loop(0, ROW_TM, wait, 0, unroll=DMA_UNROLL)

    @pl.when(i < p_steps)
    def _():
        copy_tile(xp_ref)

    @pl.when(i >= p_steps)
    def _():
        copy_tile(xq_ref)


def _dispatch(dest, xp3, xq3):
    Tp, Tq = xp3.shape[0], xq3.shape[0]
    assert Tp % ROW_TM == 0 and Tq % ROW_TM == 0
    p_steps = Tp // ROW_TM
    grid_spec = pltpu.PrefetchScalarGridSpec(
        num_scalar_prefetch=1,
        grid=((Tp + Tq) // ROW_TM,),
        in_specs=[
            pl.BlockSpec((ROW_TM, SUB, LANE), lambda i, *_: (jnp.minimum(i, p_steps - 1), 0, 0)),
            pl.BlockSpec((ROW_TM, SUB, LANE), lambda i, *_: (jnp.maximum(i - p_steps, 0), 0, 0)),
        ],
        out_specs=pl.BlockSpec(memory_space=pl.ANY),
        scratch_shapes=[pltpu.SemaphoreType.DMA],
    )
    return pl.pallas_call(
        functools.partial(_dispatch_kernel, p_steps=p_steps),
        grid_spec=grid_spec,
        out_shape=jax.ShapeDtypeStruct((2 * (Tp + Tq), SUB, LANE), _F32),
        compiler_params=pltpu.CompilerParams(dimension_semantics=("arbitrary",)),
        name="moe_dispatch",
    )(dest, xp3, xq3)


def _gmm_kernel(tile_ref, exp_ref, lo_ref, hi_ref, chg_ref,
                x_ref, nffn_ref, wg_ref, wu_ref, wd_ref, y_ref, wg_s, wu_s, wd_s):
    m = pl.program_id(0)
    tm = x_ref.shape[0] // SUB

    @pl.when(chg_ref[m] == 1)
    def _():
        wg_s[...] = wg_ref[0].astype(_BF16)
        wu_s[...] = wu_ref[0].astype(_BF16)
        wd_s[...] = wd_ref[0].astype(_BF16)

    lo = lo_ref[m]
    hi = hi_ref[m]
    hn = GMM_SUBTILE // GMM_PARTS

    def subtile(row0):
        base = tile_ref[m] * tm + row0
        parts = [pl.ds((row0 + j * hn) * SUB, hn * SUB) for j in range(GMM_PARTS)]
        xs = []
        for j in range(GMM_PARTS):
            rows = base + j * hn + lax.broadcasted_iota(jnp.int32, (hn, 1), 0)
            mine = (rows >= lo) & (rows < hi)
            xn = _rms(_load_rows(x_ref.at[parts[j]], hn), nffn_ref[...])
            xs.append(jnp.where(mine, xn, 0.0).astype(_BF16))
        gate_up = [(_dot(x, wg_s[...]), _dot(x, wu_s[...])) for x in xs]
        ys = [_dot(((a * jax.nn.sigmoid(a)) * u).astype(_BF16), wd_s[...]) for a, u in gate_up]
        first = lo <= base

        @pl.when(first)
        def _():
            for j in range(GMM_PARTS):
                _store_rows(y_ref.at[parts[j]], ys[j])

        @pl.when(jnp.logical_not(first))
        def _():
            for j in range(GMM_PARTS):
                _store_rows(y_ref.at[parts[j]], _load_rows(y_ref.at[parts[j]], hn) + ys[j])

    for row0 in range(0, tm, GMM_SUBTILE):
        base = tile_ref[m] * tm + row0
        pl.when((hi > base) & (lo < base + GMM_SUBTILE) & (hi > lo))(
            functools.partial(subtile, row0))


def _gmm(work, xs2, nffn, wg, wu, wd):
    A = xs2.shape[0] // SUB
    E, D, F = wg.shape
    n_work = work[0].shape[0]
    grid_spec = pltpu.PrefetchScalarGridSpec(
        num_scalar_prefetch=5,
        grid=(n_work,),
        in_specs=[
            pl.BlockSpec((MOE_TM * SUB, LANE), lambda m, t, e, *_: (t[m], 0)),
            pl.BlockSpec((1, D), lambda m, t, e, *_: (0, 0)),
            pl.BlockSpec((1, D, F), lambda m, t, e, *_: (e[m], 0, 0)),
            pl.BlockSpec((1, D, F), lambda m, t, e, *_: (e[m], 0, 0)),
            pl.BlockSpec((1, F, D), lambda m, t, e, *_: (e[m], 0, 0)),
        ],
        out_specs=pl.BlockSpec((MOE_TM * SUB, LANE), lambda m, t, e, *_: (t[m], 0)),
        scratch_shapes=[pltpu.VMEM((D, F), _BF16), pltpu.VMEM((D, F), _BF16),
                        pltpu.VMEM((F, D), _BF16)],
    )
    return pl.pallas_call(
        _gmm_kernel,
        grid_spec=grid_spec,
        out_shape=jax.ShapeDtypeStruct((A * SUB, LANE), _F32),
        compiler_params=pltpu.CompilerParams(
            dimension_semantics=("arbitrary",), vmem_limit_bytes=VMEM_LIMIT),
        name="moe_gmm",
    )(*work, xs2, nffn, wg, wu, wd)


def _combine_kernel(dest_ref, x1_ref, route_ref, nfin_ref, yb_ref, o_ref, buf, sems, *, tok0, n_tok):
    i = pl.program_id(0)
    n_steps = pl.num_programs(0)
    tm = o_ref.shape[0]

    def tile_copy(d, slot, k, r):
        rows = pl.ds(pl.multiple_of(r * SUB, SUB), SUB)
        return pltpu.make_async_copy(yb_ref.at[d], buf.at[slot, k, rows], sems.at[slot])

    def issue(step, slot):
        base = tok0 + step * tm

        def start(r, c):
            t = base + r
            tile_copy(dest_ref[t], slot, 0, r).start(priority=0)
            tile_copy(dest_ref[n_tok + t], slot, 1, r).start(priority=1)
            return c

        lax.fori_loop(0, tm, start, 0, unroll=DMA_UNROLL)

    @pl.when(i == 0)
    def _():
        issue(0, 0)

    @pl.when(i + 1 < n_steps)
    def _():
        issue(i + 1, (i + 1) % 2)

    slot = i % 2

    def wait(r, c):
        tile_copy(0, slot, 0, 0).wait()
        tile_copy(0, slot, 1, 0).wait()
        return c

    lax.fori_loop(0, tm, wait, 0, unroll=DMA_UNROLL)
    rt = jnp.concatenate([route_ref[...], jnp.zeros((LANE - ROUTE_OUT, tm), _F32)], axis=0).T
    g0 = rt[:, 2:3]
    g1 = rt[:, 3:4]
    y0 = _load_rows(buf.at[slot, 0], tm)
    y1 = _load_rows(buf.at[slot, 1], tm)
    y = _load_rows(x1_ref, tm) + (y0 * g0 + y1 * g1)
    o_ref[...] = _rms(y, nfin_ref[...])


def _combine(dest, x1_2, route, nfin, yb3, tok0):
    T = x1_2.shape[0] // SUB
    D = SUB * LANE
    tm = min(ROW_TM, T)
    grid_spec = pltpu.PrefetchScalarGridSpec(
        num_scalar_prefetch=1,
        grid=(T // tm,),
        in_specs=[
            pl.BlockSpec((tm * SUB, LANE), lambda i, *_: (i, 0)),
            pl.BlockSpec((ROUTE_OUT, tm), lambda i, *_: (0, i)),
            pl.BlockSpec((1, D), lambda i, *_: (0, 0)),
            pl.BlockSpec(memory_space=pl.ANY),
        ],
        out_specs=pl.BlockSpec((tm, D), lambda i, *_: (i, 0)),
        scratch_shapes=[pltpu.VMEM((2, 2, tm * SUB, LANE), _F32), pltpu.SemaphoreType.DMA((2,))],
    )
    return pl.pallas_call(
        functools.partial(_combine_kernel, tok0=tok0, n_tok=dest.shape[0] // 2),
        grid_spec=grid_spec,
        out_shape=jax.ShapeDtypeStruct((T, D), _F32),
        compiler_params=pltpu.CompilerParams(
            dimension_semantics=("arbitrary",), vmem_limit_bytes=VMEM_LIMIT),
        name="moe_combine",
    )(dest, x1_2, route, nfin, yb3)


def _routing_tables(route, counts, n_tiles):
    experts = route[0:2].astype(jnp.int32)
    ranks = route[4:6].astype(jnp.int32)
    A = experts.size
    ids = jnp.arange(N_EXPERTS, dtype=jnp.int32)
    ends = jnp.cumsum(counts)
    starts = ends - counts
    dest = ranks + jnp.sum(jnp.where(experts[..., None] == ids, starts, 0), axis=-1)
    dest = dest.reshape(-1)
    tile_starts = jnp.arange(n_tiles, dtype=jnp.int32) * MOE_TM
    pos_t = jnp.arange(n_tiles, dtype=jnp.int32) + jnp.sum(starts[None, :] < tile_starts[:, None], axis=1)
    pos_e = ids + jnp.sum(tile_starts[None, :] <= starts[:, None], axis=1)
    slots = jnp.arange(n_tiles + N_EXPERTS, dtype=jnp.int32)
    pts = (jnp.sum(jnp.where(pos_t[None, :] == slots[:, None], tile_starts[None, :], 0), axis=1)
           + jnp.sum(jnp.where(pos_e[None, :] == slots[:, None], starts[None, :], 0), axis=1))
    lo = pts.astype(jnp.int32)
    hi = jnp.concatenate([lo[1:], jnp.array([A], jnp.int32)])
    tile = jnp.minimum(lo // MOE_TM, n_tiles - 1).astype(jnp.int32)
    expert = jnp.minimum(jnp.sum(ends[None, :] <= lo[:, None], axis=1), N_EXPERTS - 1).astype(jnp.int32)
    chg = jnp.concatenate([jnp.ones((1,), jnp.int32), (expert[1:] != expert[:-1]).astype(jnp.int32)])
    return dest.astype(jnp.int32), (tile, expert, lo, hi, chg)


def _bucket_table(lq, lk):
    dist = np.arange(lq)[:, None] + WINDOW - np.arange(lk)[None, :]
    band = (dist >= 0) & (dist < WINDOW)
    d = np.clip(dist, 0, WINDOW - 1)
    max_exact = N_BUCKETS // 2
    d_f = np.maximum(d, 1).astype(np.float32)
    large = max_exact + (np.log(d_f / max_exact) / math.log(MAX_DISTANCE / max_exact)
                         * (N_BUCKETS - max_exact)).astype(np.int32)
    large = np.minimum(large, N_BUCKETS - 1)
    return np.where(d < max_exact, d, large).astype(np.int32), band


def _bias_table(rb, lq, lk):
    bkt, band = _bucket_table(lq, lk)
    onehot = jnp.asarray(bkt)[None, :, :] == jnp.arange(N_BUCKETS, dtype=jnp.int32)[:, None, None]
    bias = jnp.sum(jnp.where(onehot[:, None], rb[:, :, None, None], 0.0), axis=0)
    return jnp.where(jnp.asarray(band)[None], bias, NEG_INF)


def _decay_tables(C):
    log_gamma = jnp.log(1.0 - 2.0 ** (-5.0 - jnp.arange(RET_HEADS, dtype=_F32)))
    idx = jnp.arange(C, dtype=_F32)
    diff = idx[:, None] - idx[None, :]
    decay_in = jnp.where((diff >= 0)[..., None],
                         jnp.exp(jnp.maximum(diff, 0.0)[..., None] * log_gamma), 0.0)
    q_dec = jnp.exp((idx + 1.0)[:, None] * log_gamma)
    k_dec = jnp.exp((C - 1.0 - idx)[:, None] * log_gamma)
    c_dec = jnp.exp(C * log_gamma)
    dec = jnp.transpose(decay_in, (2, 0, 1))
    qd = jnp.repeat(q_dec, RET_DK, axis=1)
    kd = jnp.repeat(k_dec, RET_DK, axis=1)
    return dec, qd, kd, c_dec


def _rope_tables(pos):
    half = RET_DK // 2
    inv = ROPE_BASE ** (-jnp.arange(half, dtype=_F32) * 2.0 / RET_DK)
    ang = pos.astype(_F32)[:, None] * inv[None, :]
    cos = jnp.cos(ang)
    sin = jnp.sin(ang)
    return jnp.concatenate([cos, cos], axis=1), jnp.concatenate([-sin, sin], axis=1)


def _constants(rel_bias, attn_sink, L, ls):
    cst = {}
    rb = rel_bias.astype(_F32)
    cst['bias_p'] = jnp.transpose(_bias_table(rb, WINDOW, 2 * WINDOW), (0, 2, 1))
    cst['sink'] = attn_sink.astype(_F32)
    cst['cos_p'], cst['sin_p'] = _rope_tables(jnp.arange(L))
    cst['dec_p'], cst['qd_p'], cst['kd_p'], cst['cdec_p'] = _decay_tables(min(RET_CHUNK, L))
    bias = _bias_table(rb, ls, WINDOW + ls).reshape(KV_HEADS, GQA_GROUP * ls, WINDOW + ls)
    cst['bias_hist'] = bias[:, :, :WINDOW]
    cst['bias_new'] = bias[:, :, WINDOW:]
    cst['sink_col'] = jnp.repeat(attn_sink.astype(_F32).reshape(KV_HEADS, GQA_GROUP), ls,
                                 axis=1)[..., None]
    cst['cos_s'], cst['sin_s'] = _rope_tables(PAST_LEN + jnp.arange(ls))
    cst['dec_s'], cst['qd_s'], cst['kd_s'], cst['cdec_s'] = _decay_tables(min(RET_CHUNK, ls))
    return cst


def _layer_weights(layer, norm_mix, w_in, w_branch_attn, w_branch_ret, w_out, norm_ffn,
                   w_router_group, b_router_group, w_router_expert, b_router_expert):
    D = w_in.shape[1]
    wr = jnp.concatenate([w_router_group[layer].astype(_F32), w_router_expert[layer].astype(_F32)], axis=1)
    wr = jnp.pad(wr, ((0, 0), (0, ROUTE_LANES - wr.shape[1])))
    wrh = wr.astype(_BF16)
    wrl = (wr - wrh.astype(_F32)).astype(_BF16)
    br = jnp.concatenate([b_router_group[layer].astype(_F32), b_router_expert[layer].astype(_F32)])
    br = jnp.pad(br, (0, ROUTE_LANES - br.shape[0]))[None, :]
    return {
        'nmix': norm_mix[layer].astype(_F32)[None, :],
        'win': w_in[layer].astype(_BF16),
        'wba': w_branch_attn[layer].astype(_BF16),
        'wbr': w_branch_ret[layer].astype(_BF16),
        'wout': w_out[layer].astype(_BF16),
        'nffn': norm_ffn[layer].astype(_F32)[None, :],
        'wrh': wrh, 'wrl': wrl, 'br': br,
    }


def kernel(x_prompt, x_sample, cache_k, cache_v, state_ret, norm_mix, w_in, attn_sink, rel_bias,
           w_branch_attn, w_branch_ret, w_out, norm_ffn, w_router_group, b_router_group,
           w_router_expert, b_router_expert, w_gate, w_up, w_down, norm_final):
    depth = w_in.shape[0]
    assert depth == 1, "the final norm is fused into the MoE combine of the only layer"
    B, L, D = x_prompt.shape
    NB, ls, _ = x_sample.shape
    Tp, Ts = B * L, NB * ls
    nfin = norm_final.astype(_F32)[None, :]
    yp, ys = x_prompt, x_sample
    pk, pv, ps, sk, sv, ss = [], [], [], [], [], []
    for layer in range(depth):
        w = _layer_weights(layer, norm_mix, w_in, w_branch_attn, w_branch_ret, w_out, norm_ffn,
                           w_router_group, b_router_group, w_router_expert, b_router_expert)
        cst = _constants(rel_bias, attn_sink[layer], L, ls)
        x1p, routep, k1, v1, s1, cnt_p = _prompt_mixer(yp, cst, w)
        ys2 = ys.reshape(Ts, D)
        proj = _sample_inproj(ys2, w)
        attn_s, ret_s, k2, v2, s2 = _sample_core(
            proj, ls,
            cache_k[layer].reshape(NB, WINDOW, KV_W), cache_v[layer].reshape(NB, WINDOW, KV_W),
            state_ret[layer], cst)
        x1s, routes, cnt_all = _sample_post(
            ys2, attn_s, ret_s,
            proj[:, OFF_GA:OFF_GA + D], proj[:, OFF_GT:OFF_GT + D], w, cnt_p)
        n_rows = 2 * (Tp + Ts)
        assert n_rows % MOE_TM == 0
        counts = cnt_all[N_GROUPS:N_GROUPS + N_EXPERTS, 0].astype(jnp.int32)
        dest, work = _routing_tables(jnp.concatenate([routep, routes], axis=1), counts,
                                     n_rows // MOE_TM)
        xs3 = _dispatch(dest, x1p.reshape(Tp, SUB, LANE), x1s.reshape(Ts, SUB, LANE))
        yb2 = _gmm(work, xs3.reshape(n_rows * SUB, LANE), w['nffn'],
                   w_gate[layer], w_up[layer], w_down[layer])
        yb3 = yb2.reshape(n_rows, SUB, LANE)
        yp = _combine(dest, x1p, routep, nfin, yb3, 0).reshape(B, L, D)
        ys = _combine(dest, x1s, routes, nfin, yb3, Tp).reshape(NB, ls, D)
        pk.append(k1.reshape(B, WINDOW, KV_HEADS, HEAD_DIM))
        pv.append(v1.reshape(B, WINDOW, KV_HEADS, HEAD_DIM))
        ps.append(s1)
        sk.append(k2.reshape(NB, WINDOW, KV_HEADS, HEAD_DIM))
        sv.append(v2.reshape(NB, WINDOW, KV_HEADS, HEAD_DIM))
        ss.append(s2)
    return (yp, ys, jnp.stack(pk), jnp.stack(pv), jnp.stack(ps),
            jnp.stack(sk), jnp.stack(sv), jnp.stack(ss))
```

```python
import functools
import math

import jax
import jax.numpy as jnp
import numpy as np
from jax import lax
from jax.experimental import pallas as pl
from jax.experimental.pallas import tpu as pltpu

HEAD_DIM = 64
KV_HEADS = 4
GQA_GROUP = 4
ATTN_HEADS = KV_HEADS * GQA_GROUP
WINDOW = 128
N_BUCKETS = 32
MAX_DISTANCE = 128
RET_HEADS = 4
RET_DK = 128
RET_DV = 256
RET_CHUNK = 128
ROPE_BASE = 10000.0
N_GROUPS = 4
EXPERTS_PER_GROUP = 8
N_EXPERTS = N_GROUPS * EXPERTS_PER_GROUP
EXPERT_FF = 512
NORM_EPS = 1e-6
NEG_INF = -1e30
PAST_LEN = 16384

ATTN_W = ATTN_HEADS * HEAD_DIM
KV_W = KV_HEADS * HEAD_DIM
RQ_W = RET_HEADS * RET_DK
RV_W = RET_HEADS * RET_DV
OFF_QA = 0
OFF_KA = OFF_QA + ATTN_W
OFF_VA = OFF_KA + KV_W
OFF_QR = OFF_VA + KV_W
OFF_KR = OFF_QR + RQ_W
OFF_VR = OFF_KR + RQ_W
OFF_GR = OFF_VR + RV_W
OFF_GA = OFF_GR + RV_W
OFF_GT = OFF_GA + 1024
ROUTE_LANES = 128
ROUTE_ROWS = 40
ROUTE_OUT = 8

LANE = 128
SUB = 8
POST_PARTS = 2
PROMPT_TM = 512
SAMPLE_GROUP = 8
SAMPLE_INPROJ_STEPS = 4
SAMPLE_ROWS = 16
MOE_TM = 1024
GMM_SUBTILE = 256
GMM_PARTS = 2
ROW_TM = 512
DMA_UNROLL = 8
VMEM_LIMIT = 60 * 1024 * 1024

_F32 = jnp.float32
_BF16 = jnp.bfloat16


def _const_spec(shape):
    nd = len(shape)
    return pl.BlockSpec(shape, lambda *_: (0,) * nd, pipeline_mode=pl.Buffered(1))


def _rms(x, gain):
    return x * lax.rsqrt(jnp.mean(x * x, axis=-1, keepdims=True) + NORM_EPS) * gain


def _dot(a, b):
    return jnp.dot(a, b, preferred_element_type=_F32)


def _dot_nt(a, b):
    return lax.dot_general(a, b, (((1,), (1,)), ((), ())), preferred_element_type=_F32)


def _dot_tn(a, b):
    return lax.dot_general(a, b, (((0,), (0,)), ((), ())), preferred_element_type=_F32)


def _load_rows(ref, n):
    return jnp.concatenate([ref[pl.ds(s, n, stride=SUB), :] for s in range(SUB)], axis=1)


def _store_rows(ref, val):
    n = val.shape[0]
    for s in range(SUB):
        ref[pl.ds(s, n, stride=SUB), :] = val[:, s * LANE:(s + 1) * LANE]


def _rotary(x, cosf, sinf):
    return x * cosf + pltpu.roll(x, RET_DK // 2, 1) * sinf


def _post(xs, attn_projs, rets, gates_a, gates_r, wbr, wout, nffn, wrh, wrl, br, cnt):
    parts = range(len(xs))
    ret_projs = [_dot(rets[j].astype(_BF16), wbr) for j in parts]
    x1s = []
    for j in parts:
        merged = jax.nn.sigmoid(gates_a[j]) * attn_projs[j] + jax.nn.sigmoid(gates_r[j]) * ret_projs[j]
        x1s.append(xs[j] + _dot(merged.astype(_BF16), wout))
    logit_parts = []
    for j in parts:
        xn2 = _rms(x1s[j], nffn)
        hi = xn2.astype(_BF16)
        lo = (xn2 - hi.astype(_F32)).astype(_BF16)
        logit_parts.append(_dot(hi, wrh) + (_dot(hi, wrl) + _dot(lo, wrh)) + br)
    x1 = x1s[0] if len(x1s) == 1 else jnp.concatenate(x1s, axis=0)
    logits = logit_parts[0] if len(x1s) == 1 else jnp.concatenate(logit_parts, axis=0)
    n = logits.shape[0]
    lt = logits.T[0:ROUTE_ROWS, :]
    row = lax.broadcasted_iota(jnp.int32, (ROUTE_ROWS, n), 0)
    big = jnp.int32(1 << 20)
    neg = jnp.float32(-jnp.inf)
    gl = jnp.where(row < N_GROUPS, lt, neg)
    gmax = jnp.max(gl, axis=0, keepdims=True)
    gexp = jnp.exp(gl - gmax)
    gsum = jnp.sum(gexp, axis=0, keepdims=True)
    pg = gexp / gsum
    g_w = jnp.max(pg, axis=0, keepdims=True)
    g_idx = jnp.min(jnp.where(pg == g_w, row, big), axis=0, keepdims=True)
    e_row = row - N_GROUPS
    emask = (e_row >= 0) & (e_row < N_EXPERTS) & ((e_row >> 3) == g_idx)
    fl = jnp.where(emask, lt, neg)
    fmax = jnp.max(fl, axis=0, keepdims=True)
    fexp = jnp.exp(fl - fmax)
    fsum = jnp.sum(fexp, axis=0, keepdims=True)
    pe = jnp.where(emask, fexp / fsum, -1.0)
    p1 = jnp.max(pe, axis=0, keepdims=True)
    i1 = jnp.min(jnp.where(pe == p1, row, big), axis=0, keepdims=True)
    pe2 = jnp.where(row == i1, -1.0, pe)
    p2 = jnp.max(pe2, axis=0, keepdims=True)
    i2 = jnp.min(jnp.where(pe2 == p2, row, big), axis=0, keepdims=True)
    psum = p1 + p2
    gate1 = g_w * p1 / psum
    gate2 = g_w * p2 / psum
    oh1 = row == i1
    oh2 = row == i2
    c = jnp.where(oh1 | oh2, 1.0, 0.0)
    tt = lax.broadcasted_iota(jnp.int32, (n, n), 0)
    tc = lax.broadcasted_iota(jnp.int32, (n, n), 1)
    upper = jnp.where(tt < tc, 1.0, 0.0).astype(_BF16)
    before = _dot(c.astype(_BF16), upper) + cnt
    rank1 = jnp.sum(jnp.where(oh1, before, 0.0), axis=0, keepdims=True)
    rank2 = jnp.sum(jnp.where(oh2, before, 0.0), axis=0, keepdims=True)
    cnt = cnt + jnp.sum(c, axis=1, keepdims=True)
    r8 = lax.broadcasted_iota(jnp.int32, (ROUTE_OUT, n), 0)
    vals = [(i1 - N_GROUPS).astype(_F32), (i2 - N_GROUPS).astype(_F32), gate1, gate2, rank1, rank2]
    route = jnp.zeros((ROUTE_OUT, n), _F32)
    for k, v in enumerate(vals):
        route = jnp.where(r8 == k, v, route)
    return x1, route, cnt


def _prompt_mixer_kernel(sink_ref, cdec_ref,
                         x_ref, nmix_ref, win_ref, bias_ref, cos_ref, sin_ref, dec_ref,
                         qd_ref, kd_ref, wba_ref, wbr_ref, wout_ref, nffn_ref,
                         wrh_ref, wrl_ref, br_ref,
                         x1_ref, route_ref, knew_ref, vnew_ref, s_ref, cnt_ref,
                         qkv, proj, kctx, vctx, attn_t, ret):
    i = pl.program_id(1)
    last = pl.num_programs(1) - 1

    @pl.when((i == 0) & (pl.program_id(0) == 0))
    def _():
        cnt_ref[...] = jnp.zeros_like(cnt_ref)

    tm = x_ref.shape[1]
    n_sub = tm // WINDOW
    scale = HEAD_DIM ** -0.5

    @pl.when(i == 0)
    def _():
        s_ref[...] = jnp.zeros_like(s_ref)
        kctx[0:WINDOW, :] = jnp.zeros((WINDOW, KV_W), _BF16)
        vctx[0:WINDOW, :] = jnp.zeros((WINDOW, KV_W), _BF16)

    x = x_ref[0]
    xn = _rms(x, nmix_ref[...]).astype(_BF16)
    n_in = win_ref.shape[1]
    panel = 256

    def project(c0):
        res = _dot(xn, win_ref[:, c0:c0 + panel])
        if c0 < OFF_QR:
            qkv[:, c0:c0 + panel] = res
        else:
            proj[:, c0 - OFF_QR:c0 - OFF_QR + panel] = res

    def cols(lo, width):
        return slice(lo - OFF_QR, lo - OFF_QR + width)

    for c0 in range(0, OFF_QR, panel):
        project(c0)
    later_panels = list(range(OFF_QR, n_in, panel))

    @pl.when(i == last)
    def _():
        knew_ref[0] = qkv[tm - WINDOW:tm, OFF_KA:OFF_KA + KV_W]
        vnew_ref[0] = qkv[tm - WINDOW:tm, OFF_VA:OFF_VA + KV_W]

    krow = lax.broadcasted_iota(jnp.int32, (2 * WINDOW, 1), 0)
    for c in range(n_sub):
        r0 = c * WINDOW
        kctx[WINDOW:2 * WINDOW, :] = qkv[r0:r0 + WINDOW, OFF_KA:OFF_KA + KV_W].astype(_BF16)
        vctx[WINDOW:2 * WINDOW, :] = qkv[r0:r0 + WINDOW, OFF_VA:OFF_VA + KV_W].astype(_BF16)
        if c == 0:
            pen = jnp.where((krow < WINDOW) & (i == 0), NEG_INF, 0.0).astype(_F32)
        for h in range(KV_HEADS):
            k_h = kctx[:, h * HEAD_DIM:(h + 1) * HEAD_DIM]
            v_h = vctx[:, h * HEAD_DIM:(h + 1) * HEAD_DIM]
            probs = []
            for g in range(GQA_GROUP):
                hq = h * GQA_GROUP + g
                q = (qkv[r0:r0 + WINDOW, hq * HEAD_DIM:(hq + 1) * HEAD_DIM] * scale).astype(_BF16)
                s = _dot_nt(k_h, q) + bias_ref[hq]
                if c == 0:
                    s = s + pen
                snk = sink_ref[hq]
                m = jnp.maximum(jnp.max(s, axis=0, keepdims=True), snk)
                p = jnp.exp(s - m)
                den = jnp.sum(p, axis=0, keepdims=True) + jnp.exp(snk - m)
                probs.append((p * (1.0 / den)).astype(_BF16))
            groups_left = (n_sub - c) * KV_HEADS - h
            for _ in range(-(-len(later_panels) // groups_left)):
                project(later_panels.pop(0))
            o_t = _dot_tn(v_h, jnp.concatenate(probs, axis=1))
            for g in range(GQA_GROUP):
                hq = h * GQA_GROUP + g
                attn_t[hq * HEAD_DIM:(hq + 1) * HEAD_DIM, r0:r0 + WINDOW] = (
                    o_t[:, g * WINDOW:(g + 1) * WINDOW].astype(_BF16))
        kctx[0:WINDOW, :] = kctx[WINDOW:2 * WINDOW, :]
        vctx[0:WINDOW, :] = vctx[WINDOW:2 * WINDOW, :]

    for c0 in later_panels:
        project(c0)

    for c in range(n_sub):
        r0 = c * RET_CHUNK
        cosf = cos_ref[r0:r0 + RET_CHUNK, :]
        sinf = sin_ref[r0:r0 + RET_CHUNK, :]
        for h in range(RET_HEADS):
            qc = _rotary(proj[r0:r0 + RET_CHUNK, cols(OFF_QR + h * RET_DK, RET_DK)], cosf, sinf)
            kc = _rotary(proj[r0:r0 + RET_CHUNK, cols(OFF_KR + h * RET_DK, RET_DK)], cosf, sinf) * (RET_DK ** -0.5)
            qd = qd_ref[:, h * RET_DK:(h + 1) * RET_DK]
            kd = kd_ref[:, h * RET_DK:(h + 1) * RET_DK]
            vc = proj[r0:r0 + RET_CHUNK, cols(OFF_VR + h * RET_DV, RET_DV)].astype(_BF16)
            sc = _dot_nt(qc.astype(_BF16), kc.astype(_BF16))
            s_old = s_ref[0, h]
            cross = _dot((qc * qd).astype(_BF16), s_old.astype(_BF16))
            s_ref[0, h] = s_old * cdec_ref[h] + _dot_tn((kc * kd).astype(_BF16), vc)
            o = _dot((sc * dec_ref[h]).astype(_BF16), vc) + cross
            o = o * lax.rsqrt(jnp.mean(o * o, axis=-1, keepdims=True) + NORM_EPS)
            gr = proj[r0:r0 + RET_CHUNK, cols(OFF_GR + h * RET_DV, RET_DV)]
            ret[r0:r0 + RET_CHUNK, h * RET_DV:(h + 1) * RET_DV] = o * (gr * jax.nn.sigmoid(gr))

    pn = tm // POST_PARTS
    rows = [slice(j * pn, (j + 1) * pn) for j in range(POST_PARTS)]
    x1, route, cnt = _post([x[r] for r in rows],
                           [_dot_tn(attn_t[:, r], wba_ref[...]) for r in rows],
                           [ret[r, :] for r in rows],
                           [proj[r, cols(OFF_GA, 1024)] for r in rows],
                           [proj[r, cols(OFF_GT, 1024)] for r in rows],
                           wbr_ref[...], wout_ref[...], nffn_ref[...],
                           wrh_ref[...], wrl_ref[...], br_ref[...], cnt_ref[:, 0:1])
    _store_rows(x1_ref, x1)
    route_ref[...] = route
    cnt_ref[...] = jnp.broadcast_to(cnt, cnt_ref.shape)


def _prompt_mixer(x, cst, w):
    B, L, D = x.shape
    tm = min(PROMPT_TM, L)
    nb = L // tm
    n_in = w['win'].shape[1]
    step = lambda b, i, *_: (b, i, 0)
    per_b = lambda b, i, *_: (b, 0, 0)

    grid_spec = pltpu.PrefetchScalarGridSpec(
        num_scalar_prefetch=2,
        grid=(B, nb),
        in_specs=[
            pl.BlockSpec((1, tm, D), step),
            _const_spec((1, D)),
            _const_spec((D, n_in)),
            _const_spec((ATTN_HEADS, 2 * WINDOW, WINDOW)),
            pl.BlockSpec((tm, RET_DK), lambda b, i, *_: (i, 0)),
            pl.BlockSpec((tm, RET_DK), lambda b, i, *_: (i, 0)),
            _const_spec((RET_HEADS, RET_CHUNK, RET_CHUNK)),
            _const_spec((RET_CHUNK, RQ_W)),
            _const_spec((RET_CHUNK, RQ_W)),
            _const_spec((ATTN_W, D)),
            _const_spec((RV_W, D)),
            _const_spec((D, D)),
            _const_spec((1, D)),
            _const_spec((D, ROUTE_LANES)),
            _const_spec((D, ROUTE_LANES)),
            _const_spec((1, ROUTE_LANES)),
        ],
        out_specs=[
            pl.BlockSpec((tm * SUB, LANE), lambda b, i, *_: (b * nb + i, 0)),
            pl.BlockSpec((ROUTE_OUT, tm), lambda b, i, *_: (0, b * nb + i)),
            pl.BlockSpec((1, WINDOW, KV_W), per_b),
            pl.BlockSpec((1, WINDOW, KV_W), per_b),
            pl.BlockSpec((1, RET_HEADS, RET_DK, RET_DV), lambda b, i, *_: (b, 0, 0, 0)),
            pl.BlockSpec((ROUTE_ROWS, LANE), lambda b, i, *_: (0, 0)),
        ],
        scratch_shapes=[
            pltpu.VMEM((tm, OFF_QR), _F32),
            pltpu.VMEM((tm, n_in - OFF_QR), _F32),
            pltpu.VMEM((2 * WINDOW, KV_W), _BF16),
            pltpu.VMEM((2 * WINDOW, KV_W), _BF16),
            pltpu.VMEM((ATTN_W, tm), _BF16),
            pltpu.VMEM((tm, RV_W), _F32),
        ],
    )
    assert D == SUB * LANE
    out_shape = [
        jax.ShapeDtypeStruct((B * L * SUB, LANE), _F32),
        jax.ShapeDtypeStruct((ROUTE_OUT, B * L), _F32),
        jax.ShapeDtypeStruct((B, WINDOW, KV_W), _F32),
        jax.ShapeDtypeStruct((B, WINDOW, KV_W), _F32),
        jax.ShapeDtypeStruct((B, RET_HEADS, RET_DK, RET_DV), _F32),
        jax.ShapeDtypeStruct((ROUTE_ROWS, LANE), _F32),
    ]
    return pl.pallas_call(
        _prompt_mixer_kernel,
        grid_spec=grid_spec,
        out_shape=out_shape,
        compiler_params=pltpu.CompilerParams(
            dimension_semantics=("arbitrary", "arbitrary"), vmem_limit_bytes=VMEM_LIMIT),
        name="prompt_mixer",
    )(cst['sink'], cst['cdec_p'],
      x, w['nmix'], w['win'], cst['bias_p'], cst['cos_p'], cst['sin_p'], cst['dec_p'],
      cst['qd_p'], cst['kd_p'], w['wba'], w['wbr'], w['wout'], w['nffn'],
      w['wrh'], w['wrl'], w['br'])


def _inproj_kernel(x_ref, nmix_ref, win_ref, o_ref):
    xn = _rms(x_ref[...], nmix_ref[...]).astype(_BF16)
    o_ref[...] = _dot(xn, win_ref[...])


def _sample_inproj(x2d, w):
    T, D = x2d.shape
    n_in = w['win'].shape[1]
    panel = n_in // SAMPLE_INPROJ_STEPS
    assert panel % LANE == 0
    return pl.pallas_call(
        _inproj_kernel,
        grid=(n_in // panel,),
        in_specs=[pl.BlockSpec((T, D), lambda j: (0, 0)),
                  pl.BlockSpec((1, D), lambda j: (0, 0)),
                  pl.BlockSpec((D, panel), lambda j: (0, j))],
        out_specs=pl.BlockSpec((T, panel), lambda j: (0, j)),
        out_shape=jax.ShapeDtypeStruct((T, n_in), _F32),
        compiler_params=pltpu.CompilerParams(
            dimension_semantics=("arbitrary",), vmem_limit_bytes=VMEM_LIMIT),
        name="sample_inproj",
    )(x2d, w['nmix'], w['win'])


def _sample_core_kernel(cdec_ref, proj_ref, ck_ref, cv_ref, st_ref, bh_ref, bn_ref, snk_ref,
                        cos_ref, sin_ref, dec_ref, qd_ref, kd_ref,
                        attn_ref, ret_ref, nk_ref, nv_ref, ns_ref):
    G = ck_ref.shape[0]
    ls = proj_ref.shape[0] // G
    per = SAMPLE_ROWS // ls
    scale = HEAD_DIM ** -0.5
    cosf = cos_ref[...]
    sinf = sin_ref[...]
    heads = [slice(h * HEAD_DIM, (h + 1) * HEAD_DIM) for h in range(KV_HEADS)]

    def body(j, carry):
        group = pl.ds(pl.multiple_of(j * SAMPLE_ROWS, SAMPLE_ROWS), SAMPLE_ROWS)
        rows = proj_ref[group, :]
        seqs = []
        for s in range(per):
            b = j * per + s
            row = rows[s * ls:(s + 1) * ls]
            k_new = row[:, OFF_KA:OFF_KA + KV_W]
            v_new = row[:, OFF_VA:OFF_VA + KV_W]
            ck = ck_ref[b]
            cv = cv_ref[b]
            nk_ref[b, 0:WINDOW - ls, :] = ck[ls:WINDOW, :]
            nk_ref[b, WINDOW - ls:WINDOW, :] = k_new
            nv_ref[b, 0:WINDOW - ls, :] = cv[ls:WINDOW, :]
            nv_ref[b, WINDOW - ls:WINDOW, :] = v_new
            seqs.append(dict(b=b, row=row, ckb=ck.astype(_BF16), cvb=cv.astype(_BF16),
                             knb=k_new.astype(_BF16), vnb=v_new.astype(_BF16)))
        for q in seqs:
            row = q['row']
            q['s1'], q['s2'] = [], []
            for h in range(KV_HEADS):
                q4 = (jnp.concatenate(
                    [row[:, (h * GQA_GROUP + g) * HEAD_DIM:(h * GQA_GROUP + g + 1) * HEAD_DIM]
                     for g in range(GQA_GROUP)], axis=0) * scale).astype(_BF16)
                q['s1'].append(_dot_nt(q4, q['ckb'][:, heads[h]]))
                q['s2'].append(_dot_nt(q4, q['knb'][:, heads[h]]))
        for q in seqs:
            row, b = q['row'], q['b']
            q['sc'], q['cross'], q['vc'] = [], [], []
            for h in range(RET_HEADS):
                qrot = _rotary(row[:, OFF_QR + h * RET_DK:OFF_QR + (h + 1) * RET_DK], cosf, sinf)
                krot = _rotary(row[:, OFF_KR + h * RET_DK:OFF_KR + (h + 1) * RET_DK], cosf, sinf) * (RET_DK ** -0.5)
                vc = row[:, OFF_VR + h * RET_DV:OFF_VR + (h + 1) * RET_DV].astype(_BF16)
                qd = qd_ref[:, h * RET_DK:(h + 1) * RET_DK]
                kd = kd_ref[:, h * RET_DK:(h + 1) * RET_DK]
                s_old = st_ref[b, h]
                q['sc'].append(_dot_nt(qrot.astype(_BF16), krot.astype(_BF16)))
                q['cross'].append(_dot((qrot * qd).astype(_BF16), s_old.astype(_BF16)))
                ns_ref[b, h] = s_old * cdec_ref[h] + _dot_tn((krot * kd).astype(_BF16), vc)
                q['vc'].append(vc)
        for q in seqs:
            q['p1'], q['p2'] = [], []
            for h in range(KV_HEADS):
                s1 = q['s1'][h] + bh_ref[h]
                s2 = q['s2'][h] + bn_ref[h]
                snk = snk_ref[h]
                m = jnp.maximum(jnp.maximum(jnp.max(s1, axis=-1, keepdims=True),
                                            jnp.max(s2, axis=-1, keepdims=True)), snk)
                p1 = jnp.exp(s1 - m)
                p2 = jnp.exp(s2 - m)
                den = (jnp.sum(p1, axis=-1, keepdims=True) + jnp.sum(p2, axis=-1, keepdims=True)
                       + jnp.exp(snk - m))
                r = 1.0 / den
                q['p1'].append((p1 * r).astype(_BF16))
                q['p2'].append((p2 * r).astype(_BF16))
            q['scb'] = [(q['sc'][h] * dec_ref[h]).astype(_BF16) for h in range(RET_HEADS)]
        for q in seqs:
            q['out'] = [_dot(q['p1'][h], q['cvb'][:, heads[h]]) + _dot(q['p2'][h], q['vnb'][:, heads[h]])
                        for h in range(KV_HEADS)]
            q['ret'] = [_dot(q['scb'][h], q['vc'][h]) + q['cross'][h] for h in range(RET_HEADS)]
        for h in range(KV_HEADS):
            for g in range(GQA_GROUP):
                hq = h * GQA_GROUP + g
                attn_ref[group, hq * HEAD_DIM:(hq + 1) * HEAD_DIM] = jnp.concatenate(
                    [q['out'][h][g * ls:(g + 1) * ls] for q in seqs], axis=0)
        for h in range(RET_HEADS):
            o = jnp.concatenate([q['ret'][h] for q in seqs], axis=0)
            o = o * lax.rsqrt(jnp.mean(o * o, axis=-1, keepdims=True) + NORM_EPS)
            gr = rows[:, OFF_GR + h * RET_DV:OFF_GR + (h + 1) * RET_DV]
            ret_ref[group, h * RET_DV:(h + 1) * RET_DV] = o * (gr * jax.nn.sigmoid(gr))
        return carry

    lax.fori_loop(0, G // per, body, 0)


def _sample_core(proj2, ls, ck, cv, st, cst):
    n_in = proj2.shape[1]
    NB = proj2.shape[0] // ls
    G = min(SAMPLE_GROUP, NB)
    assert SAMPLE_ROWS % ls == 0 and G % (SAMPLE_ROWS // ls) == 0 and NB % G == 0
    row_blk = lambda i, *_: (i, 0)
    blk3 = lambda i, *_: (i, 0, 0)
    blk4 = lambda i, *_: (i, 0, 0, 0)
    c2 = lambda i, *_: (0, 0)
    c3 = lambda i, *_: (0, 0, 0)
    ql = GQA_GROUP * ls
    grid_spec = pltpu.PrefetchScalarGridSpec(
        num_scalar_prefetch=1,
        grid=(NB // G,),
        in_specs=[
            pl.BlockSpec((G * ls, n_in), row_blk),
            pl.BlockSpec((G, WINDOW, KV_W), blk3),
            pl.BlockSpec((G, WINDOW, KV_W), blk3),
            pl.BlockSpec((G, RET_HEADS, RET_DK, RET_DV), blk4),
            pl.BlockSpec((KV_HEADS, ql, WINDOW), c3),
            pl.BlockSpec((KV_HEADS, ql, ls), c3),
            pl.BlockSpec((KV_HEADS, ql, 1), c3),
            pl.BlockSpec((ls, RET_DK), c2),
            pl.BlockSpec((ls, RET_DK), c2),
            pl.BlockSpec((RET_HEADS, ls, ls), c3),
            pl.BlockSpec((ls, RQ_W), c2),
            pl.BlockSpec((ls, RQ_W), c2),
        ],
        out_specs=[
            pl.BlockSpec((G * ls, ATTN_W), row_blk),
            pl.BlockSpec((G * ls, RV_W), row_blk),
            pl.BlockSpec((G, WINDOW, KV_W), blk3),
            pl.BlockSpec((G, WINDOW, KV_W), blk3),
            pl.BlockSpec((G, RET_HEADS, RET_DK, RET_DV), blk4),
        ],
    )
    out_shape = [
        jax.ShapeDtypeStruct((NB * ls, ATTN_W), _F32),
        jax.ShapeDtypeStruct((NB * ls, RV_W), _F32),
        jax.ShapeDtypeStruct((NB, WINDOW, KV_W), _F32),
        jax.ShapeDtypeStruct((NB, WINDOW, KV_W), _F32),
        jax.ShapeDtypeStruct((NB, RET_HEADS, RET_DK, RET_DV), _F32),
    ]
    return pl.pallas_call(
        _sample_core_kernel,
        grid_spec=grid_spec,
        out_shape=out_shape,
        compiler_params=pltpu.CompilerParams(
            dimension_semantics=("arbitrary",), vmem_limit_bytes=VMEM_LIMIT),
        name="sample_core",
    )(cst['cdec_s'], proj2, ck, cv, st, cst['bias_hist'], cst['bias_new'], cst['sink_col'],
      cst['cos_s'], cst['sin_s'], cst['dec_s'], cst['qd_s'], cst['kd_s'])


def _sample_post_kernel(x_ref, attn_ref, ret_ref, ga_ref, gt_ref, wba_ref, wbr_ref, wout_ref,
                        nffn_ref, wrh_ref, wrl_ref, br_ref, cnt0_ref, x1_ref, route_ref, cnt_ref):
    x1, route, cnt = _post([x_ref[...]], [_dot(attn_ref[...].astype(_BF16), wba_ref[...])],
                           [ret_ref[...]], [ga_ref[...]], [gt_ref[...]],
                           wbr_ref[...], wout_ref[...], nffn_ref[...],
                           wrh_ref[...], wrl_ref[...], br_ref[...], cnt0_ref[:, 0:1])
    _store_rows(x1_ref, x1)
    route_ref[...] = route
    cnt_ref[...] = jnp.broadcast_to(cnt, cnt_ref.shape)


def _sample_post(x2d, attn, ret, ga, gt, w, cnt0):
    T, D = x2d.shape
    full = lambda s: pl.BlockSpec(s, lambda i: (0,) * len(s))
    return pl.pallas_call(
        _sample_post_kernel,
        grid=(1,),
        in_specs=[full((T, D)), full((T, ATTN_W)), full((T, RV_W)), full((T, D)), full((T, D)),
                  full((ATTN_W, D)), full((RV_W, D)), full((D, D)), full((1, D)),
                  full((D, ROUTE_LANES)), full((D, ROUTE_LANES)), full((1, ROUTE_LANES)),
                  full((ROUTE_ROWS, LANE))],
        out_specs=[full((T * SUB, LANE)), full((ROUTE_OUT, T)), full((ROUTE_ROWS, LANE))],
        out_shape=[jax.ShapeDtypeStruct((T * SUB, LANE), _F32),
                   jax.ShapeDtypeStruct((ROUTE_OUT, T), _F32),
                   jax.ShapeDtypeStruct((ROUTE_ROWS, LANE), _F32)],
        compiler_params=pltpu.CompilerParams(
            dimension_semantics=("arbitrary",), vmem_limit_bytes=VMEM_LIMIT),
        name="sample_post",
    )(x2d, attn, ret, ga, gt, w['wba'], w['wbr'], w['wout'], w['nffn'], w['wrh'], w['wrl'], w['br'],
      cnt0)


def _dispatch_kernel(dest_ref, xp_ref, xq_ref, xs_ref, sem, *, p_steps):
    i = pl.program_id(0)
    n_tok = pl.num_programs(0) * ROW_TM

    def copy_tile(src):
        def tile_copy(r, d):
            return pltpu.make_async_copy(src.at[r], xs_ref.at[d], sem)

        def start(r, c):
            t = i * ROW_TM + r
            tile_copy(r, dest_ref[t]).start(priority=0)
            tile_copy(r, dest_ref[n_tok + t]).start(priority=1)
            return c

        lax.fori_loop(0, ROW_TM, start, 0, unroll=DMA_UNROLL)

        def wait(r, c):
            tile_copy(0, 0).wait()
            tile_copy(0, 0).wait()
            return c

        lax.fori_loop(0, ROW_TM, wait, 0, unroll=DMA_UNROLL)

    @pl.when(i < p_steps)
    def _():
        copy_tile(xp_ref)

    @pl.when(i >= p_steps)
    def _():
        copy_tile(xq_ref)


def _dispatch(dest, xp3, xq3):
    Tp, Tq = xp3.shape[0], xq3.shape[0]
    assert Tp % ROW_TM == 0 and Tq % ROW_TM == 0
    p_steps = Tp // ROW_TM
    grid_spec = pltpu.PrefetchScalarGridSpec(
        num_scalar_prefetch=1,
        grid=((Tp + Tq) // ROW_TM,),
        in_specs=[
            pl.BlockSpec((ROW_TM, SUB, LANE), lambda i, *_: (jnp.minimum(i, p_steps - 1), 0, 0)),
            pl.BlockSpec((ROW_TM, SUB, LANE), lambda i, *_: (jnp.maximum(i - p_steps, 0), 0, 0)),
        ],
        out_specs=pl.BlockSpec(memory_space=pl.ANY),
        scratch_shapes=[pltpu.SemaphoreType.DMA],
    )
    return pl.pallas_call(
        functools.partial(_dispatch_kernel, p_steps=p_steps),
        grid_spec=grid_spec,
        out_shape=jax.ShapeDtypeStruct((2 * (Tp + Tq), SUB, LANE), _F32),
        compiler_params=pltpu.CompilerParams(dimension_semantics=("arbitrary",)),
        name="moe_dispatch",
    )(dest, xp3, xq3)


def _gmm_kernel(tile_ref, exp_ref, lo_ref, hi_ref, chg_ref,
                x_ref, nffn_ref, wg_ref, wu_ref, wd_ref, y_ref, wg_s, wu_s, wd_s):
    m = pl.program_id(0)
    tm = x_ref.shape[0] // SUB

    @pl.when(chg_ref[m] == 1)
    def _():
        wg_s[...] = wg_ref[0].astype(_BF16)
        wu_s[...] = wu_ref[0].astype(_BF16)
        wd_s[...] = wd_ref[0].astype(_BF16)

    lo = lo_ref[m]
    hi = hi_ref[m]
    hn = GMM_SUBTILE // GMM_PARTS

    def subtile(row0):
        base = tile_ref[m] * tm + row0
        parts = [pl.ds((row0 + j * hn) * SUB, hn * SUB) for j in range(GMM_PARTS)]
        xs = []
        for j in range(GMM_PARTS):
            rows = base + j * hn + lax.broadcasted_iota(jnp.int32, (hn, 1), 0)
            mine = (rows >= lo) & (rows < hi)
            xn = _rms(_load_rows(x_ref.at[parts[j]], hn), nffn_ref[...])
            xs.append(jnp.where(mine, xn, 0.0).astype(_BF16))
        gate_up = [(_dot(x, wg_s[...]), _dot(x, wu_s[...])) for x in xs]
        ys = [_dot(((a * jax.nn.sigmoid(a)) * u).astype(_BF16), wd_s[...]) for a, u in gate_up]
        first = lo <= base

        @pl.when(first)
        def _():
            for j in range(GMM_PARTS):
                _store_rows(y_ref.at[parts[j]], ys[j])

        @pl.when(jnp.logical_not(first))
        def _():
            for j in range(GMM_PARTS):
                _store_rows(y_ref.at[parts[j]], _load_rows(y_ref.at[parts[j]], hn) + ys[j])

    for row0 in range(0, tm, GMM_SUBTILE):
        base = tile_ref[m] * tm + row0
        pl.when((hi > base) & (lo < base + GMM_SUBTILE) & (hi > lo))(
            functools.partial(subtile, row0))


def _gmm(work, xs2, nffn, wg, wu, wd):
    A = xs2.shape[0] // SUB
    E, D, F = wg.shape
    n_work = work[0].shape[0]
    grid_spec = pltpu.PrefetchScalarGridSpec(
        num_scalar_prefetch=5,
        grid=(n_work,),
        in_specs=[
            pl.BlockSpec((MOE_TM * SUB, LANE), lambda m, t, e, *_: (t[m], 0)),
            pl.BlockSpec((1, D), lambda m, t, e, *_: (0, 0)),
            pl.BlockSpec((1, D, F), lambda m, t, e, *_: (e[m], 0, 0)),
            pl.BlockSpec((1, D, F), lambda m, t, e, *_: (e[m], 0, 0)),
            pl.BlockSpec((1, F, D), lambda m, t, e, *_: (e[m], 0, 0)),
        ],
        out_specs=pl.BlockSpec((MOE_TM * SUB, LANE), lambda m, t, e, *_: (t[m], 0)),
        scratch_shapes=[pltpu.VMEM((D, F), _BF16), pltpu.VMEM((D, F), _BF16),
                        pltpu.VMEM((F, D), _BF16)],
    )
    return pl.pallas_call(
        _gmm_kernel,
        grid_spec=grid_spec,
        out_shape=jax.ShapeDtypeStruct((A * SUB, LANE), _F32),
        compiler_params=pltpu.CompilerParams(
            dimension_semantics=("arbitrary",), vmem_limit_bytes=VMEM_LIMIT),
        name="moe_gmm",
    )(*work, xs2, nffn, wg, wu, wd)


def _combine_kernel(dest_ref, x1_ref, route_ref, nfin_ref, yb_ref, o_ref, buf, sems, *, tok0, n_tok):
    i = pl.program_id(0)
    n_steps = pl.num_programs(0)
    tm = o_ref.shape[0]

    def tile_copy(d, slot, k, r):
        rows = pl.ds(pl.multiple_of(r * SUB, SUB), SUB)
        return pltpu.make_async_copy(yb_ref.at[d], buf.at[slot, k, rows], sems.at[slot])

    def issue(step, slot):
        base = tok0 + step * tm

        def start(r, c):
            t = base + r
            tile_copy(dest_ref[t], slot, 0, r).start(priority=0)
            tile_copy(dest_ref[n_tok + t], slot, 1, r).start(priority=1)
            return c

        lax.fori_loop(0, tm, start, 0, unroll=DMA_UNROLL)

    @pl.when(i == 0)
    def _():
        issue(0, 0)

    @pl.when(i + 1 < n_steps)
    def _():
        issue(i + 1, (i + 1) % 2)

    slot = i % 2

    def wait(r, c):
        tile_copy(0, slot, 0, 0).wait()
        tile_copy(0, slot, 1, 0).wait()
        return c

    lax.fori_loop(0, tm, wait, 0, unroll=DMA_UNROLL)
    rt = jnp.concatenate([route_ref[...], jnp.zeros((LANE - ROUTE_OUT, tm), _F32)], axis=0).T
    g0 = rt[:, 2:3]
    g1 = rt[:, 3:4]
    y0 = _load_rows(buf.at[slot, 0], tm)
    y1 = _load_rows(buf.at[slot, 1], tm)
    y = _load_rows(x1_ref, tm) + (y0 * g0 + y1 * g1)
    o_ref[...] = _rms(y, nfin_ref[...])


def _combine(dest, x1_2, route, nfin, yb3, tok0):
    T = x1_2.shape[0] // SUB
    D = SUB * LANE
    tm = min(ROW_TM, T)
    grid_spec = pltpu.PrefetchScalarGridSpec(
        num_scalar_prefetch=1,
        grid=(T // tm,),
        in_specs=[
            pl.BlockSpec((tm * SUB, LANE), lambda i, *_: (i, 0)),
            pl.BlockSpec((ROUTE_OUT, tm), lambda i, *_: (0, i)),
            pl.BlockSpec((1, D), lambda i, *_: (0, 0)),
            pl.BlockSpec(memory_space=pl.ANY),
        ],
        out_specs=pl.BlockSpec((tm, D), lambda i, *_: (i, 0)),
        scratch_shapes=[pltpu.VMEM((2, 2, tm * SUB, LANE), _F32), pltpu.SemaphoreType.DMA((2,))],
    )
    return pl.pallas_call(
        functools.partial(_combine_kernel, tok0=tok0, n_tok=dest.shape[0] // 2),
        grid_spec=grid_spec,
        out_shape=jax.ShapeDtypeStruct((T, D), _F32),
        compiler_params=pltpu.CompilerParams(
            dimension_semantics=("arbitrary",), vmem_limit_bytes=VMEM_LIMIT),
        name="moe_combine",
    )(dest, x1_2, route, nfin, yb3)


def _routing_tables(route, counts, n_tiles):
    experts = route[0:2].astype(jnp.int32)
    ranks = route[4:6].astype(jnp.int32)
    A = experts.size
    ids = jnp.arange(N_EXPERTS, dtype=jnp.int32)
    ends = jnp.cumsum(counts)
    starts = ends - counts
    dest = ranks + jnp.sum(jnp.where(experts[..., None] == ids, starts, 0), axis=-1)
    dest = dest.reshape(-1)
    tile_starts = jnp.arange(n_tiles, dtype=jnp.int32) * MOE_TM
    pos_t = jnp.arange(n_tiles, dtype=jnp.int32) + jnp.sum(starts[None, :] < tile_starts[:, None], axis=1)
    pos_e = ids + jnp.sum(tile_starts[None, :] <= starts[:, None], axis=1)
    slots = jnp.arange(n_tiles + N_EXPERTS, dtype=jnp.int32)
    pts = (jnp.sum(jnp.where(pos_t[None, :] == slots[:, None], tile_starts[None, :], 0), axis=1)
           + jnp.sum(jnp.where(pos_e[None, :] == slots[:, None], starts[None, :], 0), axis=1))
    lo = pts.astype(jnp.int32)
    hi = jnp.concatenate([lo[1:], jnp.array([A], jnp.int32)])
    tile = jnp.minimum(lo // MOE_TM, n_tiles - 1).astype(jnp.int32)
    expert = jnp.minimum(jnp.sum(ends[None, :] <= lo[:, None], axis=1), N_EXPERTS - 1).astype(jnp.int32)
    chg = jnp.concatenate([jnp.ones((1,), jnp.int32), (expert[1:] != expert[:-1]).astype(jnp.int32)])
    return dest.astype(jnp.int32), (tile, expert, lo, hi, chg)


def _bucket_table(lq, lk):
    dist = np.arange(lq)[:, None] + WINDOW - np.arange(lk)[None, :]
    band = (dist >= 0) & (dist < WINDOW)
    d = np.clip(dist, 0, WINDOW - 1)
    max_exact = N_BUCKETS // 2
    d_f = np.maximum(d, 1).astype(np.float32)
    large = max_exact + (np.log(d_f / max_exact) / math.log(MAX_DISTANCE / max_exact)
                         * (N_BUCKETS - max_exact)).astype(np.int32)
    large = np.minimum(large, N_BUCKETS - 1)
    return np.where(d < max_exact, d, large).astype(np.int32), band


def _bias_table(rb, lq, lk):
    bkt, band = _bucket_table(lq, lk)
    onehot = jnp.asarray(bkt)[None, :, :] == jnp.arange(N_BUCKETS, dtype=jnp.int32)[:, None, None]
    bias = jnp.sum(jnp.where(onehot[:, None], rb[:, :, None, None], 0.0), axis=0)
    return jnp.where(jnp.asarray(band)[None], bias, NEG_INF)


def _decay_tables(C):
    log_gamma = jnp.log(1.0 - 2.0 ** (-5.0 - jnp.arange(RET_HEADS, dtype=_F32)))
    idx = jnp.arange(C, dtype=_F32)
    diff = idx[:, None] - idx[None, :]
    decay_in = jnp.where((diff >= 0)[..., None],
                         jnp.exp(jnp.maximum(diff, 0.0)[..., None] * log_gamma), 0.0)
    q_dec = jnp.exp((idx + 1.0)[:, None] * log_gamma)
    k_dec = jnp.exp((C - 1.0 - idx)[:, None] * log_gamma)
    c_dec = jnp.exp(C * log_gamma)
    dec = jnp.transpose(decay_in, (2, 0, 1))
    qd = jnp.repeat(q_dec, RET_DK, axis=1)
    kd = jnp.repeat(k_dec, RET_DK, axis=1)
    return dec, qd, kd, c_dec


def _rope_tables(pos):
    half = RET_DK // 2
    inv = ROPE_BASE ** (-jnp.arange(half, dtype=_F32) * 2.0 / RET_DK)
    ang = pos.astype(_F32)[:, None] * inv[None, :]
    cos = jnp.cos(ang)
    sin = jnp.sin(ang)
    return jnp.concatenate([cos, cos], axis=1), jnp.concatenate([-sin, sin], axis=1)


def _constants(rel_bias, attn_sink, L, ls):
    cst = {}
    rb = rel_bias.astype(_F32)
    cst['bias_p'] = jnp.transpose(_bias_table(rb, WINDOW, 2 * WINDOW), (0, 2, 1))
    cst['sink'] = attn_sink.astype(_F32)
    cst['cos_p'], cst['sin_p'] = _rope_tables(jnp.arange(L))
    cst['dec_p'], cst['qd_p'], cst['kd_p'], cst['cdec_p'] = _decay_tables(min(RET_CHUNK, L))
    bias = _bias_table(rb, ls, WINDOW + ls).reshape(KV_HEADS, GQA_GROUP * ls, WINDOW + ls)
    cst['bias_hist'] = bias[:, :, :WINDOW]
    cst['bias_new'] = bias[:, :, WINDOW:]
    cst['sink_col'] = jnp.repeat(attn_sink.astype(_F32).reshape(KV_HEADS, GQA_GROUP), ls,
                                 axis=1)[..., None]
    cst['cos_s'], cst['sin_s'] = _rope_tables(PAST_LEN + jnp.arange(ls))
    cst['dec_s'], cst['qd_s'], cst['kd_s'], cst['cdec_s'] = _decay_tables(min(RET_CHUNK, ls))
    return cst


def _layer_weights(layer, norm_mix, w_in, w_branch_attn, w_branch_ret, w_out, norm_ffn,
                   w_router_group, b_router_group, w_router_expert, b_router_expert):
    D = w_in.shape[1]
    wr = jnp.concatenate([w_router_group[layer].astype(_F32), w_router_expert[layer].astype(_F32)], axis=1)
    wr = jnp.pad(wr, ((0, 0), (0, ROUTE_LANES - wr.shape[1])))
    wrh = wr.astype(_BF16)
    wrl = (wr - wrh.astype(_F32)).astype(_BF16)
    br = jnp.concatenate([b_router_group[layer].astype(_F32), b_router_expert[layer].astype(_F32)])
    br = jnp.pad(br, (0, ROUTE_LANES - br.shape[0]))[None, :]
    return {
        'nmix': norm_mix[layer].astype(_F32)[None, :],
        'win': w_in[layer].astype(_BF16),
        'wba': w_branch_attn[layer].astype(_BF16),
        'wbr': w_branch_ret[layer].astype(_BF16),
        'wout': w_out[layer].astype(_BF16),
        'nffn': norm_ffn[layer].astype(_F32)[None, :],
        'wrh': wrh, 'wrl': wrl, 'br': br,
    }


def kernel(x_prompt, x_sample, cache_k, cache_v, state_ret, norm_mix, w_in, attn_sink, rel_bias,
           w_branch_attn, w_branch_ret, w_out, norm_ffn, w_router_group, b_router_group,
           w_router_expert, b_router_expert, w_gate, w_up, w_down, norm_final):
    depth = w_in.shape[0]
    assert depth == 1, "the final norm is fused into the MoE combine of the only layer"
    B, L, D = x_prompt.shape
    NB, ls, _ = x_sample.shape
    Tp, Ts = B * L, NB * ls
    nfin = norm_final.astype(_F32)[None, :]
    yp, ys = x_prompt, x_sample
    pk, pv, ps, sk, sv, ss = [], [], [], [], [], []
    for layer in range(depth):
        w = _layer_weights(layer, norm_mix, w_in, w_branch_attn, w_branch_ret, w_out, norm_ffn,
                           w_router_group, b_router_group, w_router_expert, b_router_expert)
        cst = _constants(rel_bias, attn_sink[layer], L, ls)
        x1p, routep, k1, v1, s1, cnt_p = _prompt_mixer(yp, cst, w)
        ys2 = ys.reshape(Ts, D)
        proj = _sample_inproj(ys2, w)
        attn_s, ret_s, k2, v2, s2 = _sample_core(
            proj, ls,
            cache_k[layer].reshape(NB, WINDOW, KV_W), cache_v[layer].reshape(NB, WINDOW, KV_W),
            state_ret[layer], cst)
        x1s, routes, cnt_all = _sample_post(
            ys2, attn_s, ret_s,
            proj[:, OFF_GA:OFF_GA + D], proj[:, OFF_GT:OFF_GT + D], w, cnt_p)
        n_rows = 2 * (Tp + Ts)
        assert n_rows % MOE_TM == 0
        counts = cnt_all[N_GROUPS:N_GROUPS + N_EXPERTS, 0].astype(jnp.int32)
        dest, work = _routing_tables(jnp.concatenate([routep, routes], axis=1), counts,
                                     n_rows // MOE_TM)
        xs3 = _dispatch(dest, x1p.reshape(Tp, SUB, LANE), x1s.reshape(Ts, SUB, LANE))
        yb2 = _gmm(work, xs3.reshape(n_rows * SUB, LANE), w['nffn'],
                   w_gate[layer], w_up[layer], w_down[layer])
        yb3 = yb2.reshape(n_rows, SUB, LANE)
        yp = _combine(dest, x1p, routep, nfin, yb3, 0).reshape(B, L, D)
        ys = _combine(dest, x1s, routes, nfin, yb3, Tp).reshape(NB, ls, D)
        pk.append(k1.reshape(B, WINDOW, KV_HEADS, HEAD_DIM))
        pv.append(v1.reshape(B, WINDOW, KV_HEADS, HEAD_DIM))
        ps.append(s1)
        sk.append(k2.reshape(NB, WINDOW, KV_HEADS, HEAD_DIM))
        sv.append(v2.reshape(NB, WINDOW, KV_HEADS, HEAD_DIM))
        ss.append(s2)
    return (yp, ys, jnp.stack(pk), jnp.stack(pv), jnp.stack(ps),
            jnp.stack(sk), jnp.stack(sv), jnp.stack(ss))
```

```python
import functools
import math

import jax
import jax.numpy as jnp
import numpy as np
from jax import lax
from jax.experimental import pallas as pl
from jax.experimental.pallas import tpu as pltpu

HEAD_DIM = 64
KV_HEADS = 4
GQA_GROUP = 4
ATTN_HEADS = KV_HEADS * GQA_GROUP
WINDOW = 128
N_BUCKETS = 32
MAX_DISTANCE = 128
RET_HEADS = 4
RET_DK = 128
RET_DV = 256
RET_CHUNK = 128
ROPE_BASE = 10000.0
N_GROUPS = 4
EXPERTS_PER_GROUP = 8
N_EXPERTS = N_GROUPS * EXPERTS_PER_GROUP
EXPERT_FF = 512
NORM_EPS = 1e-6
NEG_INF = -1e30
PAST_LEN = 16384

ATTN_W = ATTN_HEADS * HEAD_DIM
KV_W = KV_HEADS * HEAD_DIM
RQ_W = RET_HEADS * RET_DK
RV_W = RET_HEADS * RET_DV
OFF_QA = 0
OFF_KA = OFF_QA + ATTN_W
OFF_VA = OFF_KA + KV_W
OFF_QR = OFF_VA + KV_W
OFF_KR = OFF_QR + RQ_W
OFF_VR = OFF_KR + RQ_W
OFF_GR = OFF_VR + RV_W
OFF_GA = OFF_GR + RV_W
OFF_GT = OFF_GA + 1024
ROUTE_LANES = 128
ROUTE_ROWS = 40
ROUTE_OUT = 8

LANE = 128
SUB = 8
POST_PARTS = 2
PROMPT_TM = 512
SAMPLE_GROUP = 8
SAMPLE_INPROJ_STEPS = 4
SAMPLE_ROWS = 16
MOE_TM = 1024
GMM_SUBTILE = 256
GMM_PARTS = 2
ROW_TM = 512
DMA_UNROLL = 8
VMEM_LIMIT = 60 * 1024 * 1024

_F32 = jnp.float32
_BF16 = jnp.bfloat16


def _const_spec(shape):
    nd = len(shape)
    return pl.BlockSpec(shape, lambda *_: (0,) * nd, pipeline_mode=pl.Buffered(1))


def _rms(x, gain):
    return x * lax.rsqrt(jnp.mean(x * x, axis=-1, keepdims=True) + NORM_EPS) * gain


def _dot(a, b):
    return jnp.dot(a, b, preferred_element_type=_F32)


def _dot_nt(a, b):
    return lax.dot_general(a, b, (((1,), (1,)), ((), ())), preferred_element_type=_F32)


def _dot_tn(a, b):
    return lax.dot_general(a, b, (((0,), (0,)), ((), ())), preferred_element_type=_F32)


def _load_rows(ref, n):
    return jnp.concatenate([ref[pl.ds(s, n, stride=SUB), :] for s in range(SUB)], axis=1)


def _store_rows(ref, val):
    n = val.shape[0]
    for s in range(SUB):
        ref[pl.ds(s, n, stride=SUB), :] = val[:, s * LANE:(s + 1) * LANE]


def _rotary(x, cosf, sinf):
    return x * cosf + pltpu.roll(x, RET_DK // 2, 1) * sinf


def _post(xs, attn_projs, rets, gates_a, gates_r, wbr, wout, nffn, wrh, wrl, br, cnt):
    parts = range(len(xs))
    ret_projs = [_dot(rets[j].astype(_BF16), wbr) for j in parts]
    x1s = []
    for j in parts:
        merged = jax.nn.sigmoid(gates_a[j]) * attn_projs[j] + jax.nn.sigmoid(gates_r[j]) * ret_projs[j]
        x1s.append(xs[j] + _dot(merged.astype(_BF16), wout))
    logit_parts = []
    for j in parts:
        xn2 = _rms(x1s[j], nffn)
        hi = xn2.astype(_BF16)
        lo = (xn2 - hi.astype(_F32)).astype(_BF16)
        t = _dot(hi, wrl)
        logit_parts.append(t[:, 0:ROUTE_LANES] + (t[:, ROUTE_LANES:] + _dot(lo, wrh)) + br)
    x1 = x1s[0] if len(x1s) == 1 else jnp.concatenate(x1s, axis=0)
    logits = logit_parts[0] if len(x1s) == 1 else jnp.concatenate(logit_parts, axis=0)
    n = logits.shape[0]
    lt = logits.T[0:ROUTE_ROWS, :]
    row = lax.broadcasted_iota(jnp.int32, (ROUTE_ROWS, n), 0)
    big = jnp.int32(1 << 20)
    neg = jnp.float32(-jnp.inf)
    gl = jnp.where(row < N_GROUPS, lt, neg)
    gmax = jnp.max(gl, axis=0, keepdims=True)
    gexp = jnp.exp(gl - gmax)
    gsum = jnp.sum(gexp, axis=0, keepdims=True)
    pg = gexp / gsum
    g_w = jnp.max(pg, axis=0, keepdims=True)
    g_idx = jnp.min(jnp.where(pg == g_w, row, big), axis=0, keepdims=True)
    e_row = row - N_GROUPS
    emask = (e_row >= 0) & (e_row < N_EXPERTS) & ((e_row >> 3) == g_idx)
    fl = jnp.where(emask, lt, neg)
    fmax = jnp.max(fl, axis=0, keepdims=True)
    fexp = jnp.exp(fl - fmax)
    fsum = jnp.sum(fexp, axis=0, keepdims=True)
    pe = jnp.where(emask, fexp / fsum, -1.0)
    p1 = jnp.max(pe, axis=0, keepdims=True)
    i1 = jnp.min(jnp.where(pe == p1, row, big), axis=0, keepdims=True)
    pe2 = jnp.where(row == i1, -1.0, pe)
    p2 = jnp.max(pe2, axis=0, keepdims=True)
    i2 = jnp.min(jnp.where(pe2 == p2, row, big), axis=0, keepdims=True)
    psum = p1 + p2
    gate1 = g_w * p1 / psum
    gate2 = g_w * p2 / psum
    oh1 = row == i1
    oh2 = row == i2
    c = jnp.where(oh1 | oh2, 1.0, 0.0)
    tt = lax.broadcasted_iota(jnp.int32, (n, n), 0)
    tc = lax.broadcasted_iota(jnp.int32, (n, n), 1)
    upper = jnp.where(tt < tc, 1.0, 0.0).astype(_BF16)
    before = _dot(c.astype(_BF16), upper) + cnt
    rank1 = jnp.sum(jnp.where(oh1, before, 0.0), axis=0, keepdims=True)
    rank2 = jnp.sum(jnp.where(oh2, before, 0.0), axis=0, keepdims=True)
    cnt = cnt + jnp.sum(c, axis=1, keepdims=True)
    r8 = lax.broadcasted_iota(jnp.int32, (ROUTE_OUT, n), 0)
    vals = [(i1 - N_GROUPS).astype(_F32), (i2 - N_GROUPS).astype(_F32), gate1, gate2, rank1, rank2]
    route = jnp.zeros((ROUTE_OUT, n), _F32)
    for k, v in enumerate(vals):
        route = jnp.where(r8 == k, v, route)
    return x1, route, cnt


def _prompt_mixer_kernel(sink_ref, cdec_ref,
                         x_ref, nmix_ref, win_ref, bias_ref, cos_ref, sin_ref, dec_ref,
                         qd_ref, kd_ref, wba_ref, wbr_ref, wout_ref, nffn_ref,
                         wrh_ref, wrl_ref, br_ref,
                         x1_ref, route_ref, knew_ref, vnew_ref, s_ref, cnt_ref,
                         qkv, proj, kctx, vctx, attn_t, ret):
    i = pl.program_id(1)
    last = pl.num_programs(1) - 1

    @pl.when((i == 0) & (pl.program_id(0) == 0))
    def _():
        cnt_ref[...] = jnp.zeros_like(cnt_ref)

    tm = x_ref.shape[1]
    n_sub = tm // WINDOW
    scale = HEAD_DIM ** -0.5

    @pl.when(i == 0)
    def _():
        s_ref[...] = jnp.zeros_like(s_ref)
        kctx[0:WINDOW, :] = jnp.zeros((WINDOW, KV_W), _BF16)
        vctx[0:WINDOW, :] = jnp.zeros((WINDOW, KV_W), _BF16)

    x = x_ref[0]
    xn = _rms(x, nmix_ref[...]).astype(_BF16)
    n_in = win_ref.shape[1]
    panel = 256

    def project(c0):
        res = _dot(xn, win_ref[:, c0:c0 + panel])
        if c0 < OFF_QR:
            qkv[:, c0:c0 + panel] = res
        else:
            proj[:, c0 - OFF_QR:c0 - OFF_QR + panel] = res

    def cols(lo, width):
        return slice(lo - OFF_QR, lo - OFF_QR + width)

    for c0 in range(0, OFF_QR, panel):
        project(c0)
    later_panels = list(range(OFF_QR, n_in, panel))

    @pl.when(i == last)
    def _():
        knew_ref[0] = qkv[tm - WINDOW:tm, OFF_KA:OFF_KA + KV_W]
        vnew_ref[0] = qkv[tm - WINDOW:tm, OFF_VA:OFF_VA + KV_W]

    krow = lax.broadcasted_iota(jnp.int32, (2 * WINDOW, 1), 0)
    for c in range(n_sub):
        r0 = c * WINDOW
        kctx[WINDOW:2 * WINDOW, :] = qkv[r0:r0 + WINDOW, OFF_KA:OFF_KA + KV_W].astype(_BF16)
        vctx[WINDOW:2 * WINDOW, :] = qkv[r0:r0 + WINDOW, OFF_VA:OFF_VA + KV_W].astype(_BF16)
        if c == 0:
            pen = jnp.where((krow < WINDOW) & (i == 0), NEG_INF, 0.0).astype(_F32)
        for h in range(KV_HEADS):
            k_h = kctx[:, h * HEAD_DIM:(h + 1) * HEAD_DIM]
            v_h = vctx[:, h * HEAD_DIM:(h + 1) * HEAD_DIM]
            probs = []
            for g in range(GQA_GROUP):
                hq = h * GQA_GROUP + g
                q = (qkv[r0:r0 + WINDOW, hq * HEAD_DIM:(hq + 1) * HEAD_DIM] * scale).astype(_BF16)
                s = _dot_nt(k_h, q) + bias_ref[hq]
                if c == 0:
                    s = s + pen
                snk = sink_ref[hq]
                m = jnp.maximum(jnp.max(s, axis=0, keepdims=True), snk)
                p = jnp.exp(s - m)
                den = jnp.sum(p, axis=0, keepdims=True) + jnp.exp(snk - m)
                probs.append((p * (1.0 / den)).astype(_BF16))
            groups_left = (n_sub - c) * KV_HEADS - h
            for _ in range(-(-len(later_panels) // groups_left)):
                project(later_panels.pop(0))
            o_t = _dot_tn(v_h, jnp.concatenate(probs, axis=1))
            for g in range(GQA_GROUP):
                hq = h * GQA_GROUP + g
                attn_t[hq * HEAD_DIM:(hq + 1) * HEAD_DIM, r0:r0 + WINDOW] = (
                    o_t[:, g * WINDOW:(g + 1) * WINDOW].astype(_BF16))
        kctx[0:WINDOW, :] = kctx[WINDOW:2 * WINDOW, :]
        vctx[0:WINDOW, :] = vctx[WINDOW:2 * WINDOW, :]

    for c0 in later_panels:
        project(c0)

    for c in range(n_sub):
        r0 = c * RET_CHUNK
        cosf = cos_ref[r0:r0 + RET_CHUNK, :]
        sinf = sin_ref[r0:r0 + RET_CHUNK, :]
        for h in range(RET_HEADS):
            qc = _rotary(proj[r0:r0 + RET_CHUNK, cols(OFF_QR + h * RET_DK, RET_DK)], cosf, sinf)
            kc = _rotary(proj[r0:r0 + RET_CHUNK, cols(OFF_KR + h * RET_DK, RET_DK)], cosf, sinf) * (RET_DK ** -0.5)
            qd = qd_ref[:, h * RET_DK:(h + 1) * RET_DK]
            kd = kd_ref[:, h * RET_DK:(h + 1) * RET_DK]
            vc = proj[r0:r0 + RET_CHUNK, cols(OFF_VR + h * RET_DV, RET_DV)].astype(_BF16)
            sc = _dot_nt(qc.astype(_BF16), kc.astype(_BF16))
            s_old = s_ref[0, h]
            cross = _dot((qc * qd).astype(_BF16), s_old.astype(_BF16))
            s_ref[0, h] = s_old * cdec_ref[h] + _dot_tn((kc * kd).astype(_BF16), vc)
            o = _dot((sc * dec_ref[h]).astype(_BF16), vc) + cross
            o = o * lax.rsqrt(jnp.mean(o * o, axis=-1, keepdims=True) + NORM_EPS)
            gr = proj[r0:r0 + RET_CHUNK, cols(OFF_GR + h * RET_DV, RET_DV)]
            ret[r0:r0 + RET_CHUNK, h * RET_DV:(h + 1) * RET_DV] = o * (gr * jax.nn.sigmoid(gr))

    pn = tm // POST_PARTS
    rows = [slice(j * pn, (j + 1) * pn) for j in range(POST_PARTS)]
    x1, route, cnt = _post([x[r] for r in rows],
                           [_dot_tn(attn_t[:, r], wba_ref[...]) for r in rows],
                           [ret[r, :] for r in rows],
                           [proj[r, cols(OFF_GA, 1024)] for r in rows],
                           [proj[r, cols(OFF_GT, 1024)] for r in rows],
                           wbr_ref[...], wout_ref[...], nffn_ref[...],
                           wrh_ref[...], wrl_ref[...], br_ref[...], cnt_ref[:, 0:1])
    _store_rows(x1_ref, x1)
    route_ref[...] = route
    cnt_ref[...] = jnp.broadcast_to(cnt, cnt_ref.shape)


def _prompt_mixer(x, cst, w):
    B, L, D = x.shape
    tm = min(PROMPT_TM, L)
    nb = L // tm
    n_in = w['win'].shape[1]
    step = lambda b, i, *_: (b, i, 0)
    per_b = lambda b, i, *_: (b, 0, 0)

    grid_spec = pltpu.PrefetchScalarGridSpec(
        num_scalar_prefetch=2,
        grid=(B, nb),
        in_specs=[
            pl.BlockSpec((1, tm, D), step),
            _const_spec((1, D)),
            _const_spec((D, n_in)),
            _const_spec((ATTN_HEADS, 2 * WINDOW, WINDOW)),
            pl.BlockSpec((tm, RET_DK), lambda b, i, *_: (i, 0)),
            pl.BlockSpec((tm, RET_DK), lambda b, i, *_: (i, 0)),
            _const_spec((RET_HEADS, RET_CHUNK, RET_CHUNK)),
            _const_spec((RET_CHUNK, RQ_W)),
            _const_spec((RET_CHUNK, RQ_W)),
            _const_spec((ATTN_W, D)),
            _const_spec((RV_W, D)),
            _const_spec((D, D)),
            _const_spec((1, D)),
            _const_spec((D, ROUTE_LANES)),
            _const_spec((D, 2 * ROUTE_LANES)),
            _const_spec((1, ROUTE_LANES)),
        ],
        out_specs=[
            pl.BlockSpec((tm * SUB, LANE), lambda b, i, *_: (b * nb + i, 0)),
            pl.BlockSpec((ROUTE_OUT, tm), lambda b, i, *_: (0, b * nb + i)),
            pl.BlockSpec((1, WINDOW, KV_W), per_b),
            pl.BlockSpec((1, WINDOW, KV_W), per_b),
            pl.BlockSpec((1, RET_HEADS, RET_DK, RET_DV), lambda b, i, *_: (b, 0, 0, 0)),
            pl.BlockSpec((ROUTE_ROWS, LANE), lambda b, i, *_: (0, 0)),
        ],
        scratch_shapes=[
            pltpu.VMEM((tm, OFF_QR), _F32),
            pltpu.VMEM((tm, n_in - OFF_QR), _F32),
            pltpu.VMEM((2 * WINDOW, KV_W), _BF16),
            pltpu.VMEM((2 * WINDOW, KV_W), _BF16),
            pltpu.VMEM((ATTN_W, tm), _BF16),
            pltpu.VMEM((tm, RV_W), _F32),
        ],
    )
    assert D == SUB * LANE
    out_shape = [
        jax.ShapeDtypeStruct((B * L * SUB, LANE), _F32),
        jax.ShapeDtypeStruct((ROUTE_OUT, B * L), _F32),
        jax.ShapeDtypeStruct((B, WINDOW, KV_W), _F32),
        jax.ShapeDtypeStruct((B, WINDOW, KV_W), _F32),
        jax.ShapeDtypeStruct((B, RET_HEADS, RET_DK, RET_DV), _F32),
        jax.ShapeDtypeStruct((ROUTE_ROWS, LANE), _F32),
    ]
    return pl.pallas_call(
        _prompt_mixer_kernel,
        grid_spec=grid_spec,
        out_shape=out_shape,
        compiler_params=pltpu.CompilerParams(
            dimension_semantics=("arbitrary", "arbitrary"), vmem_limit_bytes=VMEM_LIMIT),
        name="prompt_mixer",
    )(cst['sink'], cst['cdec_p'],
      x, w['nmix'], w['win'], cst['bias_p'], cst['cos_p'], cst['sin_p'], cst['dec_p'],
      cst['qd_p'], cst['kd_p'], w['wba'], w['wbr'], w['wout'], w['nffn'],
      w['wrh'], w['wrl'], w['br'])


def _inproj_kernel(x_ref, nmix_ref, win_ref, o_ref):
    xn = _rms(x_ref[...], nmix_ref[...]).astype(_BF16)
    o_ref[...] = _dot(xn, win_ref[...])


def _sample_inproj(x2d, w):
    T, D = x2d.shape
    n_in = w['win'].shape[1]
    panel = n_in // SAMPLE_INPROJ_STEPS
    assert panel % LANE == 0
    return pl.pallas_call(
        _inproj_kernel,
        grid=(n_in // panel,),
        in_specs=[pl.BlockSpec((T, D), lambda j: (0, 0)),
                  pl.BlockSpec((1, D), lambda j: (0, 0)),
                  pl.BlockSpec((D, panel), lambda j: (0, j))],
        out_specs=pl.BlockSpec((T, panel), lambda j: (0, j)),
        out_shape=jax.ShapeDtypeStruct((T, n_in), _F32),
        compiler_params=pltpu.CompilerParams(
            dimension_semantics=("arbitrary",), vmem_limit_bytes=VMEM_LIMIT),
        name="sample_inproj",
    )(x2d, w['nmix'], w['win'])


def _sample_core_kernel(cdec_ref, proj_ref, ck_ref, cv_ref, st_ref, bh_ref, bn_ref, snk_ref,
                        cos_ref, sin_ref, dec_ref, qd_ref, kd_ref,
                        attn_ref, ret_ref, nk_ref, nv_ref, ns_ref):
    G = ck_ref.shape[0]
    ls = proj_ref.shape[0] // G
    per = SAMPLE_ROWS // ls
    scale = HEAD_DIM ** -0.5
    cosf = cos_ref[...]
    sinf = sin_ref[...]
    heads = [slice(h * HEAD_DIM, (h + 1) * HEAD_DIM) for h in range(KV_HEADS)]

    def body(j, carry):
        group = pl.ds(pl.multiple_of(j * SAMPLE_ROWS, SAMPLE_ROWS), SAMPLE_ROWS)
        rows = proj_ref[group, :]
        seqs = []
        for s in range(per):
            b = j * per + s
            row = rows[s * ls:(s + 1) * ls]
            k_new = row[:, OFF_KA:OFF_KA + KV_W]
            v_new = row[:, OFF_VA:OFF_VA + KV_W]
            ck = ck_ref[b]
            cv = cv_ref[b]
            nk_ref[b, 0:WINDOW - ls, :] = ck[ls:WINDOW, :]
            nk_ref[b, WINDOW - ls:WINDOW, :] = k_new
            nv_ref[b, 0:WINDOW - ls, :] = cv[ls:WINDOW, :]
            nv_ref[b, WINDOW - ls:WINDOW, :] = v_new
            seqs.append(dict(b=b, row=row, ckb=ck.astype(_BF16), cvb=cv.astype(_BF16),
                             knb=k_new.astype(_BF16), vnb=v_new.astype(_BF16)))
        for q in seqs:
            row = q['row']
            q['s1'], q['s2'] = [], []
            for h in range(KV_HEADS):
                q4 = (jnp.concatenate(
                    [row[:, (h * GQA_GROUP + g) * HEAD_DIM:(h * GQA_GROUP + g + 1) * HEAD_DIM]
                     for g in range(GQA_GROUP)], axis=0) * scale).astype(_BF16)
                q['s1'].append(_dot_nt(q4, q['ckb'][:, heads[h]]))
                q['s2'].append(_dot_nt(q4, q['knb'][:, heads[h]]))
        for q in seqs:
            row, b = q['row'], q['b']
            q['sc'], q['cross'], q['vc'] = [], [], []
            for h in range(RET_HEADS):
                qrot = _rotary(row[:, OFF_QR + h * RET_DK:OFF_QR + (h + 1) * RET_DK], cosf, sinf)
                krot = _rotary(row[:, OFF_KR + h * RET_DK:OFF_KR + (h + 1) * RET_DK], cosf, sinf) * (RET_DK ** -0.5)
                vc = row[:, OFF_VR + h * RET_DV:OFF_VR + (h + 1) * RET_DV].astype(_BF16)
                qd = qd_ref[:, h * RET_DK:(h + 1) * RET_DK]
                kd = kd_ref[:, h * RET_DK:(h + 1) * RET_DK]
                s_old = st_ref[b, h]
                q['sc'].append(_dot_nt(qrot.astype(_BF16), krot.astype(_BF16)))
                q['cross'].append(_dot((qrot * qd).astype(_BF16), s_old.astype(_BF16)))
                ns_ref[b, h] = s_old * cdec_ref[h] + _dot_tn((krot * kd).astype(_BF16), vc)
                q['vc'].append(vc)
        for q in seqs:
            q['p1'], q['p2'] = [], []
            for h in range(KV_HEADS):
                s1 = q['s1'][h] + bh_ref[h]
                s2 = q['s2'][h] + bn_ref[h]
                snk = snk_ref[h]
                m = jnp.maximum(jnp.maximum(jnp.max(s1, axis=-1, keepdims=True),
                                            jnp.max(s2, axis=-1, keepdims=True)), snk)
                p1 = jnp.exp(s1 - m)
                p2 = jnp.exp(s2 - m)
                den = (jnp.sum(p1, axis=-1, keepdims=True) + jnp.sum(p2, axis=-1, keepdims=True)
                       + jnp.exp(snk - m))
                r = 1.0 / den
                q['p1'].append((p1 * r).astype(_BF16))
                q['p2'].append((p2 * r).astype(_BF16))
            q['scb'] = [(q['sc'][h] * dec_ref[h]).astype(_BF16) for h in range(RET_HEADS)]
        for q in seqs:
            q['out'] = [_dot(q['p1'][h], q['cvb'][:, heads[h]]) + _dot(q['p2'][h], q['vnb'][:, heads[h]])
                        for h in range(KV_HEADS)]
            q['ret'] = [_dot(q['scb'][h], q['vc'][h]) + q['cross'][h] for h in range(RET_HEADS)]
        for h in range(KV_HEADS):
            for g in range(GQA_GROUP):
                hq = h * GQA_GROUP + g
                attn_ref[group, hq * HEAD_DIM:(hq + 1) * HEAD_DIM] = jnp.concatenate(
                    [q['out'][h][g * ls:(g + 1) * ls] for q in seqs], axis=0)
        for h in range(RET_HEADS):
            o = jnp.concatenate([q['ret'][h] for q in seqs], axis=0)
            o = o * lax.rsqrt(jnp.mean(o * o, axis=-1, keepdims=True) + NORM_EPS)
            gr = rows[:, OFF_GR + h * RET_DV:OFF_GR + (h + 1) * RET_DV]
            ret_ref[group, h * RET_DV:(h + 1) * RET_DV] = o * (gr * jax.nn.sigmoid(gr))
        return carry

    lax.fori_loop(0, G // per, body, 0)


def _sample_core(proj2, ls, ck, cv, st, cst):
    n_in = proj2.shape[1]
    NB = proj2.shape[0] // ls
    G = min(SAMPLE_GROUP, NB)
    assert SAMPLE_ROWS % ls == 0 and G % (SAMPLE_ROWS // ls) == 0 and NB % G == 0
    row_blk = lambda i, *_: (i, 0)
    blk3 = lambda i, *_: (i, 0, 0)
    blk4 = lambda i, *_: (i, 0, 0, 0)
    c2 = lambda i, *_: (0, 0)
    c3 = lambda i, *_: (0, 0, 0)
    ql = GQA_GROUP * ls
    grid_spec = pltpu.PrefetchScalarGridSpec(
        num_scalar_prefetch=1,
        grid=(NB // G,),
        in_specs=[
            pl.BlockSpec((G * ls, n_in), row_blk),
            pl.BlockSpec((G, WINDOW, KV_W), blk3),
            pl.BlockSpec((G, WINDOW, KV_W), blk3),
            pl.BlockSpec((G, RET_HEADS, RET_DK, RET_DV), blk4),
            pl.BlockSpec((KV_HEADS, ql, WINDOW), c3),
            pl.BlockSpec((KV_HEADS, ql, ls), c3),
            pl.BlockSpec((KV_HEADS, ql, 1), c3),
            pl.BlockSpec((ls, RET_DK), c2),
            pl.BlockSpec((ls, RET_DK), c2),
            pl.BlockSpec((RET_HEADS, ls, ls), c3),
            pl.BlockSpec((ls, RQ_W), c2),
            pl.BlockSpec((ls, RQ_W), c2),
        ],
        out_specs=[
            pl.BlockSpec((G * ls, ATTN_W), row_blk),
            pl.BlockSpec((G * ls, RV_W), row_blk),
            pl.BlockSpec((G, WINDOW, KV_W), blk3),
            pl.BlockSpec((G, WINDOW, KV_W), blk3),
            pl.BlockSpec((G, RET_HEADS, RET_DK, RET_DV), blk4),
        ],
    )
    out_shape = [
        jax.ShapeDtypeStruct((NB * ls, ATTN_W), _F32),
        jax.ShapeDtypeStruct((NB * ls, RV_W), _F32),
        jax.ShapeDtypeStruct((NB, WINDOW, KV_W), _F32),
        jax.ShapeDtypeStruct((NB, WINDOW, KV_W), _F32),
        jax.ShapeDtypeStruct((NB, RET_HEADS, RET_DK, RET_DV), _F32),
    ]
    return pl.pallas_call(
        _sample_core_kernel,
        grid_spec=grid_spec,
        out_shape=out_shape,
        compiler_params=pltpu.CompilerParams(
            dimension_semantics=("arbitrary",), vmem_limit_bytes=VMEM_LIMIT),
        name="sample_core",
    )(cst['cdec_s'], proj2, ck, cv, st, cst['bias_hist'], cst['bias_new'], cst['sink_col'],
      cst['cos_s'], cst['sin_s'], cst['dec_s'], cst['qd_s'], cst['kd_s'])


def _sample_post_kernel(x_ref, attn_ref, ret_ref, ga_ref, gt_ref, wba_ref, wbr_ref, wout_ref,
                        nffn_ref, wrh_ref, wrl_ref, br_ref, cnt0_ref, x1_ref, route_ref, cnt_ref):
    x1, route, cnt = _post([x_ref[...]], [_dot(attn_ref[...].astype(_BF16), wba_ref[...])],
                           [ret_ref[...]], [ga_ref[...]], [gt_ref[...]],
                           wbr_ref[...], wout_ref[...], nffn_ref[...],
                           wrh_ref[...], wrl_ref[...], br_ref[...], cnt0_ref[:, 0:1])
    _store_rows(x1_ref, x1)
    route_ref[...] = route
    cnt_ref[...] = jnp.broadcast_to(cnt, cnt_ref.shape)


def _sample_post(x2d, attn, ret, ga, gt, w, cnt0):
    T, D = x2d.shape
    full = lambda s: pl.BlockSpec(s, lambda i: (0,) * len(s))
    return pl.pallas_call(
        _sample_post_kernel,
        grid=(1,),
        in_specs=[full((T, D)), full((T, ATTN_W)), full((T, RV_W)), full((T, D)), full((T, D)),
                  full((ATTN_W, D)), full((RV_W, D)), full((D, D)), full((1, D)),
                  full((D, ROUTE_LANES)), full((D, 2 * ROUTE_LANES)), full((1, ROUTE_LANES)),
                  full((ROUTE_ROWS, LANE))],
        out_specs=[full((T * SUB, LANE)), full((ROUTE_OUT, T)), full((ROUTE_ROWS, LANE))],
        out_shape=[jax.ShapeDtypeStruct((T * SUB, LANE), _F32),
                   jax.ShapeDtypeStruct((ROUTE_OUT, T), _F32),
                   jax.ShapeDtypeStruct((ROUTE_ROWS, LANE), _F32)],
        compiler_params=pltpu.CompilerParams(
            dimension_semantics=("arbitrary",), vmem_limit_bytes=VMEM_LIMIT),
        name="sample_post",
    )(x2d, attn, ret, ga, gt, w['wba'], w['wbr'], w['wout'], w['nffn'], w['wrh'], w['wrl'], w['br'],
      cnt0)


def _dispatch_kernel(dest_ref, xp_ref, xq_ref, xs_ref, sem, *, p_steps):
    i = pl.program_id(0)
    n_tok = pl.num_programs(0) * ROW_TM

    def copy_tile(src):
        def tile_copy(r, d):
            return pltpu.make_async_copy(src.at[r], xs_ref.at[d], sem)

        def start(r, c):
            t = i * ROW_TM + r
            tile_copy(r, dest_ref[t]).start(priority=0)
            tile_copy(r, dest_ref[n_tok + t]).start(priority=1)
            return c

        lax.fori_loop(0, ROW_TM, start, 0, unroll=DMA_UNROLL)

        def wait(r, c):
            tile_copy(0, 0).wait()
            tile_copy(0, 0).wait()
            return c

        lax.fori_loop(0, ROW_TM, wait, 0, unroll=DMA_UNROLL)

    @pl.when(i < p_steps)
    def _():
        copy_tile(xp_ref)

    @pl.when(i >= p_steps)
    def _():
        copy_tile(xq_ref)


def _dispatch(dest, xp3, xq3):
    Tp, Tq = xp3.shape[0], xq3.shape[0]
    assert Tp % ROW_TM == 0 and Tq % ROW_TM == 0
    p_steps = Tp // ROW_TM
    grid_spec = pltpu.PrefetchScalarGridSpec(
        num_scalar_prefetch=1,
        grid=((Tp + Tq) // ROW_TM,),
        in_specs=[
            pl.BlockSpec((ROW_TM, SUB, LANE), lambda i, *_: (jnp.minimum(i, p_steps - 1), 0, 0)),
            pl.BlockSpec((ROW_TM, SUB, LANE), lambda i, *_: (jnp.maximum(i - p_steps, 0), 0, 0)),
        ],
        out_specs=pl.BlockSpec(memory_space=pl.ANY),
        scratch_shapes=[pltpu.SemaphoreType.DMA],
    )
    return pl.pallas_call(
        functools.partial(_dispatch_kernel, p_steps=p_steps),
        grid_spec=grid_spec,
        out_shape=jax.ShapeDtypeStruct((2 * (Tp + Tq), SUB, LANE), _F32),
        compiler_params=pltpu.CompilerParams(dimension_semantics=("arbitrary",)),
        name="moe_dispatch",
    )(dest, xp3, xq3)


def _gmm_kernel(tile_ref, exp_ref, lo_ref, hi_ref, chg_ref,
                x_ref, nffn_ref, wg_ref, wu_ref, wd_ref, y_ref, wg_s, wu_s, wd_s):
    m = pl.program_id(0)
    tm = x_ref.shape[0] // SUB

    @pl.when(chg_ref[m] == 1)
    def _():
        wg_s[...] = wg_ref[0].astype(_BF16)
        wu_s[...] = wu_ref[0].astype(_BF16)
        wd_s[...] = wd_ref[0].astype(_BF16)

    lo = lo_ref[m]
    hi = hi_ref[m]
    hn = GMM_SUBTILE // GMM_PARTS

    def subtile(row0):
        base = tile_ref[m] * tm + row0
        parts = [pl.ds((row0 + j * hn) * SUB, hn * SUB) for j in range(GMM_PARTS)]
        xs = []
        for j in range(GMM_PARTS):
            rows = base + j * hn + lax.broadcasted_iota(jnp.int32, (hn, 1), 0)
            mine = (rows >= lo) & (rows < hi)
            xn = _rms(_load_rows(x_ref.at[parts[j]], hn), nffn_ref[...])
            xs.append(jnp.where(mine, xn, 0.0).astype(_BF16))
        gate_up = [(_dot(x, wg_s[...]), _dot(x, wu_s[...])) for x in xs]
        ys = [_dot(((a * jax.nn.sigmoid(a)) * u).astype(_BF16), wd_s[...]) for a, u in gate_up]
        first = lo <= base

        @pl.when(first)
        def _():
            for j in range(GMM_PARTS):
                _store_rows(y_ref.at[parts[j]], ys[j])

        @pl.when(jnp.logical_not(first))
        def _():
            for j in range(GMM_PARTS):
                _store_rows(y_ref.at[parts[j]], _load_rows(y_ref.at[parts[j]], hn) + ys[j])

    tile_base = tile_ref[m] * tm
    whole = (lo <= tile_base) & (hi >= tile_base + tm)

    @pl.when(whole)
    def _():
        n_parts = tm // hn
        parts = [pl.ds(j * hn * SUB, hn * SUB) for j in range(n_parts)]
        def gate_up(j):
            x = _rms(_load_rows(x_ref.at[parts[j]], hn), nffn_ref[...]).astype(_BF16)
            return _dot(x, wg_s[...]), _dot(x, wu_s[...])

        nxt = gate_up(0)
        for j in range(n_parts):
            a, u = nxt
            if j + 1 < n_parts:
                nxt = gate_up(j + 1)
            _store_rows(y_ref.at[parts[j]],
                        _dot(((a * jax.nn.sigmoid(a)) * u).astype(_BF16), wd_s[...]))

    for row0 in range(0, tm, GMM_SUBTILE):
        base = tile_base + row0
        pl.when(jnp.logical_not(whole) & (hi > base) & (lo < base + GMM_SUBTILE) & (hi > lo))(
            functools.partial(subtile, row0))


def _gmm(work, xs2, nffn, wg, wu, wd):
    A = xs2.shape[0] // SUB
    E, D, F = wg.shape
    n_work = work[0].shape[0]
    grid_spec = pltpu.PrefetchScalarGridSpec(
        num_scalar_prefetch=5,
        grid=(n_work,),
        in_specs=[
            pl.BlockSpec((MOE_TM * SUB, LANE), lambda m, t, e, *_: (t[m], 0)),
            pl.BlockSpec((1, D), lambda m, t, e, *_: (0, 0)),
            pl.BlockSpec((1, D, F), lambda m, t, e, *_: (e[m], 0, 0)),
            pl.BlockSpec((1, D, F), lambda m, t, e, *_: (e[m], 0, 0)),
            pl.BlockSpec((1, F, D), lambda m, t, e, *_: (e[m], 0, 0)),
        ],
        out_specs=pl.BlockSpec((MOE_TM * SUB, LANE), lambda m, t, e, *_: (t[m], 0)),
        scratch_shapes=[pltpu.VMEM((D, F), _BF16), pltpu.VMEM((D, F), _BF16),
                        pltpu.VMEM((F, D), _BF16)],
    )
    return pl.pallas_call(
        _gmm_kernel,
        grid_spec=grid_spec,
        out_shape=jax.ShapeDtypeStruct((A * SUB, LANE), _F32),
        compiler_params=pltpu.CompilerParams(
            dimension_semantics=("arbitrary",), vmem_limit_bytes=VMEM_LIMIT),
        name="moe_gmm",
    )(*work, xs2, nffn, wg, wu, wd)


def _combine_kernel(dest_ref, x1_ref, route_ref, nfin_ref, yb_ref, o_ref, buf, sems, *, tok0, n_tok):
    i = pl.program_id(0)
    n_steps = pl.num_programs(0)
    tm = o_ref.shape[0]

    def tile_copy(d, slot, k, r):
        rows = pl.ds(pl.multiple_of(r * SUB, SUB), SUB)
        return pltpu.make_async_copy(yb_ref.at[d], buf.at[slot, k, rows], sems.at[slot])

    def issue(step, slot):
        base = tok0 + step * tm

        def start(r, c):
            t = base + r
            tile_copy(dest_ref[t], slot, 0, r).start(priority=0)
            tile_copy(dest_ref[n_tok + t], slot, 1, r).start(priority=1)
            return c

        lax.fori_loop(0, tm, start, 0, unroll=DMA_UNROLL)

    @pl.when(i == 0)
    def _():
        issue(0, 0)

    @pl.when(i + 1 < n_steps)
    def _():
        issue(i + 1, (i + 1) % 2)

    slot = i % 2

    def wait(r, c):
        tile_copy(0, slot, 0, 0).wait()
        tile_copy(0, slot, 1, 0).wait()
        return c

    lax.fori_loop(0, tm, wait, 0, unroll=DMA_UNROLL)
    rt = jnp.concatenate([route_ref[...], jnp.zeros((LANE - ROUTE_OUT, tm), _F32)], axis=0).T
    g0 = rt[:, 2:3]
    g1 = rt[:, 3:4]
    y0 = _load_rows(buf.at[slot, 0], tm)
    y1 = _load_rows(buf.at[slot, 1], tm)
    y = _load_rows(x1_ref, tm) + (y0 * g0 + y1 * g1)
    o_ref[...] = _rms(y, nfin_ref[...])


def _combine(dest, x1_2, route, nfin, yb3, tok0):
    T = x1_2.shape[0] // SUB
    D = SUB * LANE
    tm = min(ROW_TM, T)
    grid_spec = pltpu.PrefetchScalarGridSpec(
        num_scalar_prefetch=1,
        grid=(T // tm,),
        in_specs=[
            pl.BlockSpec((tm * SUB, LANE), lambda i, *_: (i, 0)),
            pl.BlockSpec((ROUTE_OUT, tm), lambda i, *_: (0, i)),
            pl.BlockSpec((1, D), lambda i, *_: (0, 0)),
            pl.BlockSpec(memory_space=pl.ANY),
        ],
        out_specs=pl.BlockSpec((tm, D), lambda i, *_: (i, 0)),
        scratch_shapes=[pltpu.VMEM((2, 2, tm * SUB, LANE), _F32), pltpu.SemaphoreType.DMA((2,))],
    )
    return pl.pallas_call(
        functools.partial(_combine_kernel, tok0=tok0, n_tok=dest.shape[0] // 2),
        grid_spec=grid_spec,
        out_shape=jax.ShapeDtypeStruct((T, D), _F32),
        compiler_params=pltpu.CompilerParams(
            dimension_semantics=("arbitrary",), vmem_limit_bytes=VMEM_LIMIT),
        name="moe_combine",
    )(dest, x1_2, route, nfin, yb3)


def _routing_tables(route, counts, n_tiles):
    experts = route[0:2].astype(jnp.int32)
    ranks = route[4:6].astype(jnp.int32)
    A = experts.size
    ids = jnp.arange(N_EXPERTS, dtype=jnp.int32)
    ends = jnp.cumsum(counts)
    starts = ends - counts
    dest = ranks + jnp.sum(jnp.where(experts[..., None] == ids, starts, 0), axis=-1)
    dest = dest.reshape(-1)
    tile_starts = jnp.arange(n_tiles, dtype=jnp.int32) * MOE_TM
    pos_t = jnp.arange(n_tiles, dtype=jnp.int32) + jnp.sum(starts[None, :] < tile_starts[:, None], axis=1)
    pos_e = ids + jnp.sum(tile_starts[None, :] <= starts[:, None], axis=1)
    slots = jnp.arange(n_tiles + N_EXPERTS, dtype=jnp.int32)
    pts = (jnp.sum(jnp.where(pos_t[None, :] == slots[:, None], tile_starts[None, :], 0), axis=1)
           + jnp.sum(jnp.where(pos_e[None, :] == slots[:, None], starts[None, :], 0), axis=1))
    lo = pts.astype(jnp.int32)
    hi = jnp.concatenate([lo[1:], jnp.array([A], jnp.int32)])
    tile = jnp.minimum(lo // MOE_TM, n_tiles - 1).astype(jnp.int32)
    expert = jnp.minimum(jnp.sum(ends[None, :] <= lo[:, None], axis=1), N_EXPERTS - 1).astype(jnp.int32)
    chg = jnp.concatenate([jnp.ones((1,), jnp.int32), (expert[1:] != expert[:-1]).astype(jnp.int32)])
    return dest.astype(jnp.int32), (tile, expert, lo, hi, chg)


def _bucket_table(lq, lk):
    dist = np.arange(lq)[:, None] + WINDOW - np.arange(lk)[None, :]
    band = (dist >= 0) & (dist < WINDOW)
    d = np.clip(dist, 0, WINDOW - 1)
    max_exact = N_BUCKETS // 2
    d_f = np.maximum(d, 1).astype(np.float32)
    large = max_exact + (np.log(d_f / max_exact) / math.log(MAX_DISTANCE / max_exact)
                         * (N_BUCKETS - max_exact)).astype(np.int32)
    large = np.minimum(large, N_BUCKETS - 1)
    return np.where(d < max_exact, d, large).astype(np.int32), band


def _bias_table(rb, lq, lk):
    bkt, band = _bucket_table(lq, lk)
    onehot = jnp.asarray(bkt)[None, :, :] == jnp.arange(N_BUCKETS, dtype=jnp.int32)[:, None, None]
    bias = jnp.sum(jnp.where(onehot[:, None], rb[:, :, None, None], 0.0), axis=0)
    return jnp.where(jnp.asarray(band)[None], bias, NEG_INF)


def _decay_tables(C):
    log_gamma = jnp.log(1.0 - 2.0 ** (-5.0 - jnp.arange(RET_HEADS, dtype=_F32)))
    idx = jnp.arange(C, dtype=_F32)
    diff = idx[:, None] - idx[None, :]
    decay_in = jnp.where((diff >= 0)[..., None],
                         jnp.exp(jnp.maximum(diff, 0.0)[..., None] * log_gamma), 0.0)
    q_dec = jnp.exp((idx + 1.0)[:, None] * log_gamma)
    k_dec = jnp.exp((C - 1.0 - idx)[:, None] * log_gamma)
    c_dec = jnp.exp(C * log_gamma)
    dec = jnp.transpose(decay_in, (2, 0, 1))
    qd = jnp.repeat(q_dec, RET_DK, axis=1)
    kd = jnp.repeat(k_dec, RET_DK, axis=1)
    return dec, qd, kd, c_dec


def _rope_tables(pos):
    half = RET_DK // 2
    inv = ROPE_BASE ** (-jnp.arange(half, dtype=_F32) * 2.0 / RET_DK)
    ang = pos.astype(_F32)[:, None] * inv[None, :]
    cos = jnp.cos(ang)
    sin = jnp.sin(ang)
    return jnp.concatenate([cos, cos], axis=1), jnp.concatenate([-sin, sin], axis=1)


def _constants(rel_bias, attn_sink, L, ls):
    cst = {}
    rb = rel_bias.astype(_F32)
    cst['bias_p'] = jnp.transpose(_bias_table(rb, WINDOW, 2 * WINDOW), (0, 2, 1))
    cst['sink'] = attn_sink.astype(_F32)
    cst['cos_p'], cst['sin_p'] = _rope_tables(jnp.arange(L))
    cst['dec_p'], cst['qd_p'], cst['kd_p'], cst['cdec_p'] = _decay_tables(min(RET_CHUNK, L))
    bias = _bias_table(rb, ls, WINDOW + ls).reshape(KV_HEADS, GQA_GROUP * ls, WINDOW + ls)
    cst['bias_hist'] = bias[:, :, :WINDOW]
    cst['bias_new'] = bias[:, :, WINDOW:]
    cst['sink_col'] = jnp.repeat(attn_sink.astype(_F32).reshape(KV_HEADS, GQA_GROUP), ls,
                                 axis=1)[..., None]
    cst['cos_s'], cst['sin_s'] = _rope_tables(PAST_LEN + jnp.arange(ls))
    cst['dec_s'], cst['qd_s'], cst['kd_s'], cst['cdec_s'] = _decay_tables(min(RET_CHUNK, ls))
    return cst


def _layer_weights(layer, norm_mix, w_in, w_branch_attn, w_branch_ret, w_out, norm_ffn,
                   w_router_group, b_router_group, w_router_expert, b_router_expert):
    D = w_in.shape[1]
    wr = jnp.concatenate([w_router_group[layer].astype(_F32), w_router_expert[layer].astype(_F32)], axis=1)
    wr = jnp.pad(wr, ((0, 0), (0, ROUTE_LANES - wr.shape[1])))
    wrh = wr.astype(_BF16)
    wrl = jnp.concatenate([wrh, (wr - wrh.astype(_F32)).astype(_BF16)], axis=1)
    br = jnp.concatenate([b_router_group[layer].astype(_F32), b_router_expert[layer].astype(_F32)])
    br = jnp.pad(br, (0, ROUTE_LANES - br.shape[0]))[None, :]
    return {
        'nmix': norm_mix[layer].astype(_F32)[None, :],
        'win': w_in[layer].astype(_BF16),
        'wba': w_branch_attn[layer].astype(_BF16),
        'wbr': w_branch_ret[layer].astype(_BF16),
        'wout': w_out[layer].astype(_BF16),
        'nffn': norm_ffn[layer].astype(_F32)[None, :],
        'wrh': wrh, 'wrl': wrl, 'br': br,
    }


def kernel(x_prompt, x_sample, cache_k, cache_v, state_ret, norm_mix, w_in, attn_sink, rel_bias,
           w_branch_attn, w_branch_ret, w_out, norm_ffn, w_router_group, b_router_group,
           w_router_expert, b_router_expert, w_gate, w_up, w_down, norm_final):
    depth = w_in.shape[0]
    assert depth == 1, "the final norm is fused into the MoE combine of the only layer"
    B, L, D = x_prompt.shape
    NB, ls, _ = x_sample.shape
    Tp, Ts = B * L, NB * ls
    nfin = norm_final.astype(_F32)[None, :]
    yp, ys = x_prompt, x_sample
    pk, pv, ps, sk, sv, ss = [], [], [], [], [], []
    for layer in range(depth):
        w = _layer_weights(layer, norm_mix, w_in, w_branch_attn, w_branch_ret, w_out, norm_ffn,
                           w_router_group, b_router_group, w_router_expert, b_router_expert)
        cst = _constants(rel_bias, attn_sink[layer], L, ls)
        x1p, routep, k1, v1, s1, cnt_p = _prompt_mixer(yp, cst, w)
        ys2 = ys.reshape(Ts, D)
        proj = _sample_inproj(ys2, w)
        attn_s, ret_s, k2, v2, s2 = _sample_core(
            proj, ls,
            cache_k[layer].reshape(NB, WINDOW, KV_W), cache_v[layer].reshape(NB, WINDOW, KV_W),
            state_ret[layer], cst)
        x1s, routes, cnt_all = _sample_post(
            ys2, attn_s, ret_s,
            proj[:, OFF_GA:OFF_GA + D], proj[:, OFF_GT:OFF_GT + D], w, cnt_p)
        n_rows = 2 * (Tp + Ts)
        assert n_rows % MOE_TM == 0
        counts = cnt_all[N_GROUPS:N_GROUPS + N_EXPERTS, 0].astype(jnp.int32)
        dest, work = _routing_tables(jnp.concatenate([routep, routes], axis=1), counts,
                                     n_rows // MOE_TM)
        xs3 = _dispatch(dest, x1p.reshape(Tp, SUB, LANE), x1s.reshape(Ts, SUB, LANE))
        yb2 = _gmm(work, xs3.reshape(n_rows * SUB, LANE), w['nffn'],
                   w_gate[layer], w_up[layer], w_down[layer])
        yb3 = yb2.reshape(n_rows, SUB, LANE)
        yp = _combine(dest, x1p, routep, nfin, yb3, 0).reshape(B, L, D)
        ys = _combine(dest, x1s, routes, nfin, yb3, Tp).reshape(NB, ls, D)
        pk.append(k1.reshape(B, WINDOW, KV_HEADS, HEAD_DIM))
        pv.append(v1.reshape(B, WINDOW, KV_HEADS, HEAD_DIM))
        ps.append(s1)
        sk.append(k2.reshape(NB, WINDOW, KV_HEADS, HEAD_DIM))
        sv.append(v2.reshape(NB, WINDOW, KV_HEADS, HEAD_DIM))
        ss.append(s2)
    return (yp, ys, jnp.stack(pk), jnp.stack(pv), jnp.stack(ps),
            jnp.stack(sk), jnp.stack(sv), jnp.stack(ss))
```

```python
import functools
import math

import jax
import jax.numpy as jnp
import numpy as np
from jax import lax
from jax.experimental import pallas as pl
from jax.experimental.pallas import tpu as pltpu

HEAD_DIM = 64
KV_HEADS = 4
GQA_GROUP = 4
ATTN_HEADS = KV_HEADS * GQA_GROUP
WINDOW = 128
N_BUCKETS = 32
MAX_DISTANCE = 128
RET_HEADS = 4
RET_DK = 128
RET_DV = 256
RET_CHUNK = 128
ROPE_BASE = 10000.0
N_GROUPS = 4
EXPERTS_PER_GROUP = 8
N_EXPERTS = N_GROUPS * EXPERTS_PER_GROUP
EXPERT_FF = 512
NORM_EPS = 1e-6
NEG_INF = -1e30
PAST_LEN = 16384

ATTN_W = ATTN_HEADS * HEAD_DIM
KV_W = KV_HEADS * HEAD_DIM
RQ_W = RET_HEADS * RET_DK
RV_W = RET_HEADS * RET_DV
OFF_QA = 0
OFF_KA = OFF_QA + ATTN_W
OFF_VA = OFF_KA + KV_W
OFF_QR = OFF_VA + KV_W
OFF_KR = OFF_QR + RQ_W
OFF_VR = OFF_KR + RQ_W
OFF_GR = OFF_VR + RV_W
OFF_GA = OFF_GR + RV_W
OFF_GT = OFF_GA + 1024
ROUTE_LANES = 128
ROUTE_ROWS = 40
ROUTE_OUT = 8

LANE = 128
SUB = 8
POST_PARTS = 2
PROMPT_TM = 512
SAMPLE_GROUP = 8
SAMPLE_INPROJ_STEPS = 4
SAMPLE_ROWS = 16
MOE_TM = 1024
GMM_SUBTILE = 256
GMM_PARTS = 2
ROW_TM = 512
COMBINE_CHUNKS = 2
DMA_UNROLL = 8
VMEM_LIMIT = 60 * 1024 * 1024

_F32 = jnp.float32
_BF16 = jnp.bfloat16


def _const_spec(shape):
    nd = len(shape)
    return pl.BlockSpec(shape, lambda *_: (0,) * nd, pipeline_mode=pl.Buffered(1))


def _rms(x, gain):
    return x * lax.rsqrt(jnp.mean(x * x, axis=-1, keepdims=True) + NORM_EPS) * gain


def _dot(a, b):
    return jnp.dot(a, b, preferred_element_type=_F32)


def _dot_nt(a, b):
    return lax.dot_general(a, b, (((1,), (1,)), ((), ())), preferred_element_type=_F32)


def _dot_tn(a, b):
    return lax.dot_general(a, b, (((0,), (0,)), ((), ())), preferred_element_type=_F32)


def _load_rows(ref, n):
    return jnp.concatenate([ref[pl.ds(s, n, stride=SUB), :] for s in range(SUB)], axis=1)


def _store_rows(ref, val):
    n = val.shape[0]
    for s in range(SUB):
        ref[pl.ds(s, n, stride=SUB), :] = val[:, s * LANE:(s + 1) * LANE]


def _rotary(x, cosf, sinf):
    return x * cosf + pltpu.roll(x, RET_DK // 2, 1) * sinf


def _post(xs, attn_projs, rets, gates_a, gates_r, wbr, wout, nffn, wrh, wrl, br, cnt):
    parts = range(len(xs))
    ret_projs = [_dot(rets[j].astype(_BF16), wbr) for j in parts]
    x1s = []
    for j in parts:
        merged = jax.nn.sigmoid(gates_a[j]) * attn_projs[j] + jax.nn.sigmoid(gates_r[j]) * ret_projs[j]
        x1s.append(xs[j] + _dot(merged.astype(_BF16), wout))
    logit_parts = []
    for j in parts:
        xn2 = _rms(x1s[j], nffn)
        hi = xn2.astype(_BF16)
        lo = (xn2 - hi.astype(_F32)).astype(_BF16)
        t = _dot(hi, wrl)
        logit_parts.append(t[:, 0:ROUTE_LANES] + (t[:, ROUTE_LANES:] + _dot(lo, wrh)) + br)
    x1 = x1s[0] if len(x1s) == 1 else jnp.concatenate(x1s, axis=0)
    logits = logit_parts[0] if len(x1s) == 1 else jnp.concatenate(logit_parts, axis=0)
    n = logits.shape[0]
    lt = logits.T[0:ROUTE_ROWS, :]
    row = lax.broadcasted_iota(jnp.int32, (ROUTE_ROWS, n), 0)
    big = jnp.int32(1 << 20)
    neg = jnp.float32(-jnp.inf)
    gl = jnp.where(row < N_GROUPS, lt, neg)
    gmax = jnp.max(gl, axis=0, keepdims=True)
    gexp = jnp.exp(gl - gmax)
    gsum = jnp.sum(gexp, axis=0, keepdims=True)
    pg = gexp / gsum
    g_w = jnp.max(pg, axis=0, keepdims=True)
    g_idx = jnp.min(jnp.where(pg == g_w, row, big), axis=0, keepdims=True)
    e_row = row - N_GROUPS
    emask = (e_row >= 0) & (e_row < N_EXPERTS) & ((e_row >> 3) == g_idx)
    fl = jnp.where(emask, lt, neg)
    fmax = jnp.max(fl, axis=0, keepdims=True)
    fexp = jnp.exp(fl - fmax)
    fsum = jnp.sum(fexp, axis=0, keepdims=True)
    pe = jnp.where(emask, fexp / fsum, -1.0)
    p1 = jnp.max(pe, axis=0, keepdims=True)
    i1 = jnp.min(jnp.where(pe == p1, row, big), axis=0, keepdims=True)
    pe2 = jnp.where(row == i1, -1.0, pe)
    p2 = jnp.max(pe2, axis=0, keepdims=True)
    i2 = jnp.min(jnp.where(pe2 == p2, row, big), axis=0, keepdims=True)
    psum = p1 + p2
    gate1 = g_w * p1 / psum
    gate2 = g_w * p2 / psum
    oh1 = row == i1
    oh2 = row == i2
    c = jnp.where(oh1 | oh2, 1.0, 0.0)
    tt = lax.broadcasted_iota(jnp.int32, (n, n), 0)
    tc = lax.broadcasted_iota(jnp.int32, (n, n), 1)
    upper = jnp.where(tt < tc, 1.0, 0.0).astype(_BF16)
    before = _dot(c.astype(_BF16), upper) + cnt
    rank1 = jnp.sum(jnp.where(oh1, before, 0.0), axis=0, keepdims=True)
    rank2 = jnp.sum(jnp.where(oh2, before, 0.0), axis=0, keepdims=True)
    cnt = cnt + jnp.sum(c, axis=1, keepdims=True)
    r8 = lax.broadcasted_iota(jnp.int32, (ROUTE_OUT, n), 0)
    vals = [(i1 - N_GROUPS).astype(_F32), (i2 - N_GROUPS).astype(_F32), gate1, gate2, rank1, rank2]
    route = jnp.zeros((ROUTE_OUT, n), _F32)
    for k, v in enumerate(vals):
        route = jnp.where(r8 == k, v, route)
    return x1, route, cnt


def _prompt_mixer_kernel(sink_ref, cdec_ref,
                         x_ref, nmix_ref, win_ref, bias_ref, cos_ref, sin_ref, dec_ref,
                         qd_ref, kd_ref, wba_ref, wbr_ref, wout_ref, nffn_ref,
                         wrh_ref, wrl_ref, br_ref,
                         x1_ref, route_ref, knew_ref, vnew_ref, s_ref, cnt_ref,
                         qkv, proj, kctx, vctx, attn_t, ret):
    i = pl.program_id(1)
    last = pl.num_programs(1) - 1

    @pl.when((i == 0) & (pl.program_id(0) == 0))
    def _():
        cnt_ref[...] = jnp.zeros_like(cnt_ref)

    tm = x_ref.shape[1]
    n_sub = tm // WINDOW
    scale = HEAD_DIM ** -0.5

    @pl.when(i == 0)
    def _():
        s_ref[...] = jnp.zeros_like(s_ref)
        kctx[0:WINDOW, :] = jnp.zeros((WINDOW, KV_W), _BF16)
        vctx[0:WINDOW, :] = jnp.zeros((WINDOW, KV_W), _BF16)

    x = x_ref[0]
    xn = _rms(x, nmix_ref[...]).astype(_BF16)
    n_in = win_ref.shape[1]
    panel = 256

    def project(c0):
        res = _dot(xn, win_ref[:, c0:c0 + panel])
        if c0 < OFF_QR:
            qkv[:, c0:c0 + panel] = res
        else:
            proj[:, c0 - OFF_QR:c0 - OFF_QR + panel] = res

    def cols(lo, width):
        return slice(lo - OFF_QR, lo - OFF_QR + width)

    for c0 in range(0, OFF_QR, panel):
        project(c0)
    later_panels = list(range(OFF_QR, n_in, panel))

    @pl.when(i == last)
    def _():
        knew_ref[0] = qkv[tm - WINDOW:tm, OFF_KA:OFF_KA + KV_W]
        vnew_ref[0] = qkv[tm - WINDOW:tm, OFF_VA:OFF_VA + KV_W]

    krow = lax.broadcasted_iota(jnp.int32, (2 * WINDOW, 1), 0)
    for c in range(n_sub):
        r0 = c * WINDOW
        kctx[WINDOW:2 * WINDOW, :] = qkv[r0:r0 + WINDOW, OFF_KA:OFF_KA + KV_W].astype(_BF16)
        vctx[WINDOW:2 * WINDOW, :] = qkv[r0:r0 + WINDOW, OFF_VA:OFF_VA + KV_W].astype(_BF16)
        if c == 0:
            pen = jnp.where((krow < WINDOW) & (i == 0), NEG_INF, 0.0).astype(_F32)
        for h in range(KV_HEADS):
            k_h = kctx[:, h * HEAD_DIM:(h + 1) * HEAD_DIM]
            v_h = vctx[:, h * HEAD_DIM:(h + 1) * HEAD_DIM]
            probs = []
            for g in range(GQA_GROUP):
                hq = h * GQA_GROUP + g
                q = (qkv[r0:r0 + WINDOW, hq * HEAD_DIM:(hq + 1) * HEAD_DIM] * scale).astype(_BF16)
                s = _dot_nt(k_h, q) + bias_ref[hq]
                if c == 0:
                    s = s + pen
                snk = sink_ref[hq]
                m = jnp.maximum(jnp.max(s, axis=0, keepdims=True), snk)
                p = jnp.exp(s - m)
                den = jnp.sum(p, axis=0, keepdims=True) + jnp.exp(snk - m)
                probs.append((p * (1.0 / den)).astype(_BF16))
            groups_left = (n_sub - c) * KV_HEADS - h
            for _ in range(-(-len(later_panels) // groups_left)):
                project(later_panels.pop(0))
            o_t = _dot_tn(v_h, jnp.concatenate(probs, axis=1))
            for g in range(GQA_GROUP):
                hq = h * GQA_GROUP + g
                attn_t[hq * HEAD_DIM:(hq + 1) * HEAD_DIM, r0:r0 + WINDOW] = (
                    o_t[:, g * WINDOW:(g + 1) * WINDOW].astype(_BF16))
        kctx[0:WINDOW, :] = kctx[WINDOW:2 * WINDOW, :]
        vctx[0:WINDOW, :] = vctx[WINDOW:2 * WINDOW, :]

    for c0 in later_panels:
        project(c0)

    for c in range(n_sub):
        r0 = c * RET_CHUNK
        cosf = cos_ref[r0:r0 + RET_CHUNK, :]
        sinf = sin_ref[r0:r0 + RET_CHUNK, :]
        for h in range(RET_HEADS):
            qc = _rotary(proj[r0:r0 + RET_CHUNK, cols(OFF_QR + h * RET_DK, RET_DK)], cosf, sinf)
            kc = _rotary(proj[r0:r0 + RET_CHUNK, cols(OFF_KR + h * RET_DK, RET_DK)], cosf, sinf) * (RET_DK ** -0.5)
            qd = qd_ref[:, h * RET_DK:(h + 1) * RET_DK]
            kd = kd_ref[:, h * RET_DK:(h + 1) * RET_DK]
            vc = proj[r0:r0 + RET_CHUNK, cols(OFF_VR + h * RET_DV, RET_DV)].astype(_BF16)
            sc = _dot_nt(qc.astype(_BF16), kc.astype(_BF16))
            s_old = s_ref[0, h]
            cross = _dot((qc * qd).astype(_BF16), s_old.astype(_BF16))
            s_ref[0, h] = s_old * cdec_ref[h] + _dot_tn((kc * kd).astype(_BF16), vc)
            o = _dot((sc * dec_ref[h]).astype(_BF16), vc) + cross
            o = o * lax.rsqrt(jnp.mean(o * o, axis=-1, keepdims=True) + NORM_EPS)
            gr = proj[r0:r0 + RET_CHUNK, cols(OFF_GR + h * RET_DV, RET_DV)]
            ret[r0:r0 + RET_CHUNK, h * RET_DV:(h + 1) * RET_DV] = o * (gr * jax.nn.sigmoid(gr))

    pn = tm // POST_PARTS
    rows = [slice(j * pn, (j + 1) * pn) for j in range(POST_PARTS)]
    x1, route, cnt = _post([x[r] for r in rows],
                           [_dot_tn(attn_t[:, r], wba_ref[...]) for r in rows],
                           [ret[r, :] for r in rows],
                           [proj[r, cols(OFF_GA, 1024)] for r in rows],
                           [proj[r, cols(OFF_GT, 1024)] for r in rows],
                           wbr_ref[...], wout_ref[...], nffn_ref[...],
                           wrh_ref[...], wrl_ref[...], br_ref[...], cnt_ref[:, 0:1])
    _store_rows(x1_ref, x1)
    route_ref[...] = route
    cnt_ref[...] = jnp.broadcast_to(cnt, cnt_ref.shape)


def _prompt_mixer(x, cst, w):
    B, L, D = x.shape
    tm = min(PROMPT_TM, L)
    nb = L // tm
    n_in = w['win'].shape[1]
    step = lambda b, i, *_: (b, i, 0)
    per_b = lambda b, i, *_: (b, 0, 0)

    grid_spec = pltpu.PrefetchScalarGridSpec(
        num_scalar_prefetch=2,
        grid=(B, nb),
        in_specs=[
            pl.BlockSpec((1, tm, D), step),
            _const_spec((1, D)),
            _const_spec((D, n_in)),
            _const_spec((ATTN_HEADS, 2 * WINDOW, WINDOW)),
            pl.BlockSpec((tm, RET_DK), lambda b, i, *_: (i, 0)),
            pl.BlockSpec((tm, RET_DK), lambda b, i, *_: (i, 0)),
            _const_spec((RET_HEADS, RET_CHUNK, RET_CHUNK)),
            _const_spec((RET_CHUNK, RQ_W)),
            _const_spec((RET_CHUNK, RQ_W)),
            _const_spec((ATTN_W, D)),
            _const_spec((RV_W, D)),
            _const_spec((D, D)),
            _const_spec((1, D)),
            _const_spec((D, ROUTE_LANES)),
            _const_spec((D, 2 * ROUTE_LANES)),
            _const_spec((1, ROUTE_LANES)),
        ],
        out_specs=[
            pl.BlockSpec((tm * SUB, LANE), lambda b, i, *_: (b * nb + i, 0)),
            pl.BlockSpec((ROUTE_OUT, tm), lambda b, i, *_: (0, b * nb + i)),
            pl.BlockSpec((1, WINDOW, KV_W), per_b),
            pl.BlockSpec((1, WINDOW, KV_W), per_b),
            pl.BlockSpec((1, RET_HEADS, RET_DK, RET_DV), lambda b, i, *_: (b, 0, 0, 0)),
            pl.BlockSpec((ROUTE_ROWS, LANE), lambda b, i, *_: (0, 0)),
        ],
        scratch_shapes=[
            pltpu.VMEM((tm, OFF_QR), _F32),
            pltpu.VMEM((tm, n_in - OFF_QR), _F32),
            pltpu.VMEM((2 * WINDOW, KV_W), _BF16),
            pltpu.VMEM((2 * WINDOW, KV_W), _BF16),
            pltpu.VMEM((ATTN_W, tm), _BF16),
            pltpu.VMEM((tm, RV_W), _F32),
        ],
    )
    assert D == SUB * LANE
    out_shape = [
        jax.ShapeDtypeStruct((B * L * SUB, LANE), _F32),
        jax.ShapeDtypeStruct((ROUTE_OUT, B * L), _F32),
        jax.ShapeDtypeStruct((B, WINDOW, KV_W), _F32),
        jax.ShapeDtypeStruct((B, WINDOW, KV_W), _F32),
        jax.ShapeDtypeStruct((B, RET_HEADS, RET_DK, RET_DV), _F32),
        jax.ShapeDtypeStruct((ROUTE_ROWS, LANE), _F32),
    ]
    return pl.pallas_call(
        _prompt_mixer_kernel,
        grid_spec=grid_spec,
        out_shape=out_shape,
        compiler_params=pltpu.CompilerParams(
            dimension_semantics=("arbitrary", "arbitrary"), vmem_limit_bytes=VMEM_LIMIT),
        name="prompt_mixer",
    )(cst['sink'], cst['cdec_p'],
      x, w['nmix'], w['win'], cst['bias_p'], cst['cos_p'], cst['sin_p'], cst['dec_p'],
      cst['qd_p'], cst['kd_p'], w['wba'], w['wbr'], w['wout'], w['nffn'],
      w['wrh'], w['wrl'], w['br'])


def _inproj_kernel(x_ref, nmix_ref, win_ref, o_ref):
    xn = _rms(x_ref[...], nmix_ref[...]).astype(_BF16)
    o_ref[...] = _dot(xn, win_ref[...])


def _sample_inproj(x2d, w):
    T, D = x2d.shape
    n_in = w['win'].shape[1]
    panel = n_in // SAMPLE_INPROJ_STEPS
    assert panel % LANE == 0
    return pl.pallas_call(
        _inproj_kernel,
        grid=(n_in // panel,),
        in_specs=[pl.BlockSpec((T, D), lambda j: (0, 0)),
                  pl.BlockSpec((1, D), lambda j: (0, 0)),
                  pl.BlockSpec((D, panel), lambda j: (0, j))],
        out_specs=pl.BlockSpec((T, panel), lambda j: (0, j)),
        out_shape=jax.ShapeDtypeStruct((T, n_in), _F32),
        compiler_params=pltpu.CompilerParams(
            dimension_semantics=("arbitrary",), vmem_limit_bytes=VMEM_LIMIT),
        name="sample_inproj",
    )(x2d, w['nmix'], w['win'])


def _sample_core_kernel(cdec_ref, proj_ref, ck_ref, cv_ref, st_ref, bh_ref, bn_ref, snk_ref,
                        cos_ref, sin_ref, dec_ref, qd_ref, kd_ref,
                        attn_ref, ret_ref, nk_ref, nv_ref, ns_ref):
    G = ck_ref.shape[0]
    ls = proj_ref.shape[0] // G
    per = SAMPLE_ROWS // ls
    scale = HEAD_DIM ** -0.5
    cosf = cos_ref[...]
    sinf = sin_ref[...]
    heads = [slice(h * HEAD_DIM, (h + 1) * HEAD_DIM) for h in range(KV_HEADS)]

    def body(j, carry):
        group = pl.ds(pl.multiple_of(j * SAMPLE_ROWS, SAMPLE_ROWS), SAMPLE_ROWS)
        rows = proj_ref[group, :]
        seqs = []
        for s in range(per):
            b = j * per + s
            row = rows[s * ls:(s + 1) * ls]
            k_new = row[:, OFF_KA:OFF_KA + KV_W]
            v_new = row[:, OFF_VA:OFF_VA + KV_W]
            ck = ck_ref[b]
            cv = cv_ref[b]
            nk_ref[b, 0:WINDOW - ls, :] = ck[ls:WINDOW, :]
            nk_ref[b, WINDOW - ls:WINDOW, :] = k_new
            nv_ref[b, 0:WINDOW - ls, :] = cv[ls:WINDOW, :]
            nv_ref[b, WINDOW - ls:WINDOW, :] = v_new
            seqs.append(dict(b=b, row=row, ckb=ck.astype(_BF16), cvb=cv.astype(_BF16),
                             knb=k_new.astype(_BF16), vnb=v_new.astype(_BF16)))
        for q in seqs:
            row = q['row']
            q['s1'], q['s2'] = [], []
            for h in range(KV_HEADS):
                q4 = (jnp.concatenate(
                    [row[:, (h * GQA_GROUP + g) * HEAD_DIM:(h * GQA_GROUP + g + 1) * HEAD_DIM]
                     for g in range(GQA_GROUP)], axis=0) * scale).astype(_BF16)
                q['s1'].append(_dot_nt(q4, q['ckb'][:, heads[h]]))
                q['s2'].append(_dot_nt(q4, q['knb'][:, heads[h]]))
        for q in seqs:
            row, b = q['row'], q['b']
            q['sc'], q['cross'], q['vc'] = [], [], []
            for h in range(RET_HEADS):
                qrot = _rotary(row[:, OFF_QR + h * RET_DK:OFF_QR + (h + 1) * RET_DK], cosf, sinf)
                krot = _rotary(row[:, OFF_KR + h * RET_DK:OFF_KR + (h + 1) * RET_DK], cosf, sinf) * (RET_DK ** -0.5)
                vc = row[:, OFF_VR + h * RET_DV:OFF_VR + (h + 1) * RET_DV].astype(_BF16)
                qd = qd_ref[:, h * RET_DK:(h + 1) * RET_DK]
                kd = kd_ref[:, h * RET_DK:(h + 1) * RET_DK]
                s_old = st_ref[b, h]
                q['sc'].append(_dot_nt(qrot.astype(_BF16), krot.astype(_BF16)))
                q['cross'].append(_dot((qrot * qd).astype(_BF16), s_old.astype(_BF16)))
                ns_ref[b, h] = s_old * cdec_ref[h] + _dot_tn((krot * kd).astype(_BF16), vc)
                q['vc'].append(vc)
        for q in seqs:
            q['p1'], q['p2'] = [], []
            for h in range(KV_HEADS):
                s1 = q['s1'][h] + bh_ref[h]
                s2 = q['s2'][h] + bn_ref[h]
                snk = snk_ref[h]
                m = jnp.maximum(jnp.maximum(jnp.max(s1, axis=-1, keepdims=True),
                                            jnp.max(s2, axis=-1, keepdims=True)), snk)
                p1 = jnp.exp(s1 - m)
                p2 = jnp.exp(s2 - m)
                den = (jnp.sum(p1, axis=-1, keepdims=True) + jnp.sum(p2, axis=-1, keepdims=True)
                       + jnp.exp(snk - m))
                r = 1.0 / den
                q['p1'].append((p1 * r).astype(_BF16))
                q['p2'].append((p2 * r).astype(_BF16))
            q['scb'] = [(q['sc'][h] * dec_ref[h]).astype(_BF16) for h in range(RET_HEADS)]
        for q in seqs:
            q['out'] = [_dot(q['p1'][h], q['cvb'][:, heads[h]]) + _dot(q['p2'][h], q['vnb'][:, heads[h]])
                        for h in range(KV_HEADS)]
            q['ret'] = [_dot(q['scb'][h], q['vc'][h]) + q['cross'][h] for h in range(RET_HEADS)]
        for h in range(KV_HEADS):
            for g in range(GQA_GROUP):
                hq = h * GQA_GROUP + g
                attn_ref[group, hq * HEAD_DIM:(hq + 1) * HEAD_DIM] = jnp.concatenate(
                    [q['out'][h][g * ls:(g + 1) * ls] for q in seqs], axis=0)
        for h in range(RET_HEADS):
            o = jnp.concatenate([q['ret'][h] for q in seqs], axis=0)
            o = o * lax.rsqrt(jnp.mean(o * o, axis=-1, keepdims=True) + NORM_EPS)
            gr = rows[:, OFF_GR + h * RET_DV:OFF_GR + (h + 1) * RET_DV]
            ret_ref[group, h * RET_DV:(h + 1) * RET_DV] = o * (gr * jax.nn.sigmoid(gr))
        return carry

    lax.fori_loop(0, G // per, body, 0)


def _sample_core(proj2, ls, ck, cv, st, cst):
    n_in = proj2.shape[1]
    NB = proj2.shape[0] // ls
    G = min(SAMPLE_GROUP, NB)
    assert SAMPLE_ROWS % ls == 0 and G % (SAMPLE_ROWS // ls) == 0 and NB % G == 0
    row_blk = lambda i, *_: (i, 0)
    blk3 = lambda i, *_: (i, 0, 0)
    blk4 = lambda i, *_: (i, 0, 0, 0)
    c2 = lambda i, *_: (0, 0)
    c3 = lambda i, *_: (0, 0, 0)
    ql = GQA_GROUP * ls
    grid_spec = pltpu.PrefetchScalarGridSpec(
        num_scalar_prefetch=1,
        grid=(NB // G,),
        in_specs=[
            pl.BlockSpec((G * ls, n_in), row_blk),
            pl.BlockSpec((G, WINDOW, KV_W), blk3),
            pl.BlockSpec((G, WINDOW, KV_W), blk3),
            pl.BlockSpec((G, RET_HEADS, RET_DK, RET_DV), blk4),
            pl.BlockSpec((KV_HEADS, ql, WINDOW), c3),
            pl.BlockSpec((KV_HEADS, ql, ls), c3),
            pl.BlockSpec((KV_HEADS, ql, 1), c3),
            pl.BlockSpec((ls, RET_DK), c2),
            pl.BlockSpec((ls, RET_DK), c2),
            pl.BlockSpec((RET_HEADS, ls, ls), c3),
            pl.BlockSpec((ls, RQ_W), c2),
            pl.BlockSpec((ls, RQ_W), c2),
        ],
        out_specs=[
            pl.BlockSpec((G * ls, ATTN_W), row_blk),
            pl.BlockSpec((G * ls, RV_W), row_blk),
            pl.BlockSpec((G, WINDOW, KV_W), blk3),
            pl.BlockSpec((G, WINDOW, KV_W), blk3),
            pl.BlockSpec((G, RET_HEADS, RET_DK, RET_DV), blk4),
        ],
    )
    out_shape = [
        jax.ShapeDtypeStruct((NB * ls, ATTN_W), _F32),
        jax.ShapeDtypeStruct((NB * ls, RV_W), _F32),
        jax.ShapeDtypeStruct((NB, WINDOW, KV_W), _F32),
        jax.ShapeDtypeStruct((NB, WINDOW, KV_W), _F32),
        jax.ShapeDtypeStruct((NB, RET_HEADS, RET_DK, RET_DV), _F32),
    ]
    return pl.pallas_call(
        _sample_core_kernel,
        grid_spec=grid_spec,
        out_shape=out_shape,
        compiler_params=pltpu.CompilerParams(
            dimension_semantics=("arbitrary",), vmem_limit_bytes=VMEM_LIMIT),
        name="sample_core",
    )(cst['cdec_s'], proj2, ck, cv, st, cst['bias_hist'], cst['bias_new'], cst['sink_col'],
      cst['cos_s'], cst['sin_s'], cst['dec_s'], cst['qd_s'], cst['kd_s'])


def _sample_post_kernel(x_ref, attn_ref, ret_ref, ga_ref, gt_ref, wba_ref, wbr_ref, wout_ref,
                        nffn_ref, wrh_ref, wrl_ref, br_ref, cnt0_ref, x1_ref, route_ref, cnt_ref):
    x1, route, cnt = _post([x_ref[...]], [_dot(attn_ref[...].astype(_BF16), wba_ref[...])],
                           [ret_ref[...]], [ga_ref[...]], [gt_ref[...]],
                           wbr_ref[...], wout_ref[...], nffn_ref[...],
                           wrh_ref[...], wrl_ref[...], br_ref[...], cnt0_ref[:, 0:1])
    _store_rows(x1_ref, x1)
    route_ref[...] = route
    cnt_ref[...] = jnp.broadcast_to(cnt, cnt_ref.shape)


def _sample_post(x2d, attn, ret, ga, gt, w, cnt0):
    T, D = x2d.shape
    full = lambda s: pl.BlockSpec(s, lambda i: (0,) * len(s))
    return pl.pallas_call(
        _sample_post_kernel,
        grid=(1,),
        in_specs=[full((T, D)), full((T, ATTN_W)), full((T, RV_W)), full((T, D)), full((T, D)),
                  full((ATTN_W, D)), full((RV_W, D)), full((D, D)), full((1, D)),
                  full((D, ROUTE_LANES)), full((D, 2 * ROUTE_LANES)), full((1, ROUTE_LANES)),
                  full((ROUTE_ROWS, LANE))],
        out_specs=[full((T * SUB, LANE)), full((ROUTE_OUT, T)), full((ROUTE_ROWS, LANE))],
        out_shape=[jax.ShapeDtypeStruct((T * SUB, LANE), _F32),
                   jax.ShapeDtypeStruct((ROUTE_OUT, T), _F32),
                   jax.ShapeDtypeStruct((ROUTE_ROWS, LANE), _F32)],
        compiler_params=pltpu.CompilerParams(
            dimension_semantics=("arbitrary",), vmem_limit_bytes=VMEM_LIMIT),
        name="sample_post",
    )(x2d, attn, ret, ga, gt, w['wba'], w['wbr'], w['wout'], w['nffn'], w['wrh'], w['wrl'], w['br'],
      cnt0)


def _dispatch_kernel(dest_ref, xp_ref, xq_ref, xs_ref, sem, *, p_steps):
    i = pl.program_id(0)
    n_tok = pl.num_programs(0) * ROW_TM

    def copy_tile(src):
        def tile_copy(r, d):
            return pltpu.make_async_copy(src.at[r], xs_ref.at[d], sem)

        def start(r, c):
            t = i * ROW_TM + r
            tile_copy(r, dest_ref[t]).start(priority=0)
            tile_copy(r, dest_ref[n_tok + t]).start(priority=1)
            return c

        lax.fori_loop(0, ROW_TM, start, 0, unroll=DMA_UNROLL)

        def wait(r, c):
            tile_copy(0, 0).wait()
            tile_copy(0, 0).wait()
            return c

        lax.fori_loop(0, ROW_TM, wait, 0, unroll=DMA_UNROLL)

    @pl.when(i < p_steps)
    def _():
        copy_tile(xp_ref)

    @pl.when(i >= p_steps)
    def _():
        copy_tile(xq_ref)


def _dispatch(dest, xp3, xq3):
    Tp, Tq = xp3.shape[0], xq3.shape[0]
    assert Tp % ROW_TM == 0 and Tq % ROW_TM == 0
    p_steps = Tp // ROW_TM
    grid_spec = pltpu.PrefetchScalarGridSpec(
        num_scalar_prefetch=1,
        grid=((Tp + Tq) // ROW_TM,),
        in_specs=[
            pl.BlockSpec((ROW_TM, SUB, LANE), lambda i, *_: (jnp.minimum(i, p_steps - 1), 0, 0)),
            pl.BlockSpec((ROW_TM, SUB, LANE), lambda i, *_: (jnp.maximum(i - p_steps, 0), 0, 0)),
        ],
        out_specs=pl.BlockSpec(memory_space=pl.ANY),
        scratch_shapes=[pltpu.SemaphoreType.DMA],
    )
    return pl.pallas_call(
        functools.partial(_dispatch_kernel, p_steps=p_steps),
        grid_spec=grid_spec,
        out_shape=jax.ShapeDtypeStruct((2 * (Tp + Tq), SUB, LANE), _F32),
        compiler_params=pltpu.CompilerParams(dimension_semantics=("arbitrary",)),
        name="moe_dispatch",
    )(dest, xp3, xq3)


def _gmm_kernel(tile_ref, exp_ref, lo_ref, hi_ref, chg_ref,
                x_ref, nffn_ref, wg_ref, wu_ref, wd_ref, y_ref, wg_s, wu_s, wd_s):
    m = pl.program_id(0)
    tm = x_ref.shape[0] // SUB

    @pl.when(chg_ref[m] == 1)
    def _():
        wg_s[...] = wg_ref[0].astype(_BF16)
        wu_s[...] = wu_ref[0].astype(_BF16)
        wd_s[...] = wd_ref[0].astype(_BF16)

    lo = lo_ref[m]
    hi = hi_ref[m]
    hn = GMM_SUBTILE // GMM_PARTS

    def subtile(row0):
        base = tile_ref[m] * tm + row0
        parts = [pl.ds((row0 + j * hn) * SUB, hn * SUB) for j in range(GMM_PARTS)]
        xs = []
        for j in range(GMM_PARTS):
            rows = base + j * hn + lax.broadcasted_iota(jnp.int32, (hn, 1), 0)
            mine = (rows >= lo) & (rows < hi)
            xn = _rms(_load_rows(x_ref.at[parts[j]], hn), nffn_ref[...])
            xs.append(jnp.where(mine, xn, 0.0).astype(_BF16))
        gate_up = [(_dot(x, wg_s[...]), _dot(x, wu_s[...])) for x in xs]
        ys = [_dot(((a * jax.nn.sigmoid(a)) * u).astype(_BF16), wd_s[...]) for a, u in gate_up]
        first = lo <= base

        @pl.when(first)
        def _():
            for j in range(GMM_PARTS):
                _store_rows(y_ref.at[parts[j]], ys[j])

        @pl.when(jnp.logical_not(first))
        def _():
            for j in range(GMM_PARTS):
                _store_rows(y_ref.at[parts[j]], _load_rows(y_ref.at[parts[j]], hn) + ys[j])

    tile_base = tile_ref[m] * tm
    whole = (lo <= tile_base) & (hi >= tile_base + tm)

    @pl.when(whole)
    def _():
        n_parts = tm // hn
        parts = [pl.ds(j * hn * SUB, hn * SUB) for j in range(n_parts)]
        def gate_up(j):
            x = _rms(_load_rows(x_ref.at[parts[j]], hn), nffn_ref[...]).astype(_BF16)
            return _dot(x, wg_s[...]), _dot(x, wu_s[...])

        nxt = gate_up(0)
        for j in range(n_parts):
            a, u = nxt
            if j + 1 < n_parts:
                nxt = gate_up(j + 1)
            _store_rows(y_ref.at[parts[j]],
                        _dot(((a * jax.nn.sigmoid(a)) * u).astype(_BF16), wd_s[...]))

    for row0 in range(0, tm, GMM_SUBTILE):
        base = tile_base + row0
        pl.when(jnp.logical_not(whole) & (hi > base) & (lo < base + GMM_SUBTILE) & (hi > lo))(
            functools.partial(subtile, row0))


def _gmm(work, xs2, nffn, wg, wu, wd):
    A = xs2.shape[0] // SUB
    E, D, F = wg.shape
    n_work = work[0].shape[0]
    grid_spec = pltpu.PrefetchScalarGridSpec(
        num_scalar_prefetch=5,
        grid=(n_work,),
        in_specs=[
            pl.BlockSpec((MOE_TM * SUB, LANE), lambda m, t, e, *_: (t[m], 0)),
            pl.BlockSpec((1, D), lambda m, t, e, *_: (0, 0)),
            pl.BlockSpec((1, D, F), lambda m, t, e, *_: (e[m], 0, 0)),
            pl.BlockSpec((1, D, F), lambda m, t, e, *_: (e[m], 0, 0)),
            pl.BlockSpec((1, F, D), lambda m, t, e, *_: (e[m], 0, 0)),
        ],
        out_specs=pl.BlockSpec((MOE_TM * SUB, LANE), lambda m, t, e, *_: (t[m], 0)),
        scratch_shapes=[pltpu.VMEM((D, F), _BF16), pltpu.VMEM((D, F), _BF16),
                        pltpu.VMEM((F, D), _BF16)],
    )
    return pl.pallas_call(
        _gmm_kernel,
        grid_spec=grid_spec,
        out_shape=jax.ShapeDtypeStruct((A * SUB, LANE), _F32),
        compiler_params=pltpu.CompilerParams(
            dimension_semantics=("arbitrary",), vmem_limit_bytes=VMEM_LIMIT),
        name="moe_gmm",
    )(*work, xs2, nffn, wg, wu, wd)


def _combine_kernel(dest_ref, x1_ref, route_ref, nfin_ref, yb_ref, o_ref, buf, sems, *, tok0, n_tok):
    i = pl.program_id(0)
    n_steps = pl.num_programs(0)
    tm = o_ref.shape[0]

    def tile_copy(d, slot, k, r):
        rows = pl.ds(pl.multiple_of(r * SUB, SUB), SUB)
        return pltpu.make_async_copy(yb_ref.at[d], buf.at[slot, k, rows], sems.at[slot])

    def issue(step, slot):
        base = tok0 + step * tm

        def start(r, c):
            t = base + r
            tile_copy(dest_ref[t], slot, 0, r).start(priority=0)
            tile_copy(dest_ref[n_tok + t], slot, 1, r).start(priority=1)
            return c

        lax.fori_loop(0, tm, start, 0, unroll=DMA_UNROLL)

    @pl.when(i == 0)
    def _():
        issue(0, 0)

    slot = i % 2

    def wait(r, c):
        tile_copy(0, slot, 0, 0).wait()
        tile_copy(0, slot, 1, 0).wait()
        return c

    lax.fori_loop(0, tm, wait, 0, unroll=DMA_UNROLL)

    def combine_rows(prefetch):
        rt = jnp.concatenate([route_ref[...], jnp.zeros((LANE - ROUTE_OUT, tm), _F32)], axis=0).T
        cn = tm // COMBINE_CHUNKS
        for ch in range(COMBINE_CHUNKS):
            if prefetch:
                base = tok0 + (i + 1) * tm
                for r in range(ch * cn, (ch + 1) * cn):
                    tile_copy(dest_ref[base + r], 1 - slot, 0, r).start(priority=0)
                    tile_copy(dest_ref[n_tok + base + r], 1 - slot, 1, r).start(priority=1)
            rows = pl.ds(ch * cn * SUB, cn * SUB)
            g0 = rt[ch * cn:(ch + 1) * cn, 2:3]
            g1 = rt[ch * cn:(ch + 1) * cn, 3:4]
            y0 = _load_rows(buf.at[slot, 0, rows], cn)
            y1 = _load_rows(buf.at[slot, 1, rows], cn)
            y = _load_rows(x1_ref.at[rows], cn) + (y0 * g0 + y1 * g1)
            o_ref[ch * cn:(ch + 1) * cn, :] = _rms(y, nfin_ref[...])

    @pl.when(i + 1 < n_steps)
    def _():
        combine_rows(True)

    @pl.when(i + 1 == n_steps)
    def _():
        combine_rows(False)


def _combine(dest, x1_2, route, nfin, yb3, tok0):
    T = x1_2.shape[0] // SUB
    D = SUB * LANE
    tm = min(ROW_TM, T)
    grid_spec = pltpu.PrefetchScalarGridSpec(
        num_scalar_prefetch=1,
        grid=(T // tm,),
        in_specs=[
            pl.BlockSpec((tm * SUB, LANE), lambda i, *_: (i, 0)),
            pl.BlockSpec((ROUTE_OUT, tm), lambda i, *_: (0, i)),
            pl.BlockSpec((1, D), lambda i, *_: (0, 0)),
            pl.BlockSpec(memory_space=pl.ANY),
        ],
        out_specs=pl.BlockSpec((tm, D), lambda i, *_: (i, 0)),
        scratch_shapes=[pltpu.VMEM((2, 2, tm * SUB, LANE), _F32), pltpu.SemaphoreType.DMA((2,))],
    )
    return pl.pallas_call(
        functools.partial(_combine_kernel, tok0=tok0, n_tok=dest.shape[0] // 2),
        grid_spec=grid_spec,
        out_shape=jax.ShapeDtypeStruct((T, D), _F32),
        compiler_params=pltpu.CompilerParams(
            dimension_semantics=("arbitrary",), vmem_limit_bytes=VMEM_LIMIT),
        name="moe_combine",
    )(dest, x1_2, route, nfin, yb3)


def _routing_tables(route, counts, n_tiles):
    experts = route[0:2].astype(jnp.int32)
    ranks = route[4:6].astype(jnp.int32)
    A = experts.size
    ids = jnp.arange(N_EXPERTS, dtype=jnp.int32)
    ends = jnp.cumsum(counts)
    starts = ends - counts
    dest = ranks + jnp.sum(jnp.where(experts[..., None] == ids, starts, 0), axis=-1)
    dest = dest.reshape(-1)
    tile_starts = jnp.arange(n_tiles, dtype=jnp.int32) * MOE_TM
    pos_t = jnp.arange(n_tiles, dtype=jnp.int32) + jnp.sum(starts[None, :] < tile_starts[:, None], axis=1)
    pos_e = ids + jnp.sum(tile_starts[None, :] <= starts[:, None], axis=1)
    slots = jnp.arange(n_tiles + N_EXPERTS, dtype=jnp.int32)
    pts = (jnp.sum(jnp.where(pos_t[None, :] == slots[:, None], tile_starts[None, :], 0), axis=1)
           + jnp.sum(jnp.where(pos_e[None, :] == slots[:, None], starts[None, :], 0), axis=1))
    lo = pts.astype(jnp.int32)
    hi = jnp.concatenate([lo[1:], jnp.array([A], jnp.int32)])
    tile = jnp.minimum(lo // MOE_TM, n_tiles - 1).astype(jnp.int32)
    expert = jnp.minimum(jnp.sum(ends[None, :] <= lo[:, None], axis=1), N_EXPERTS - 1).astype(jnp.int32)
    chg = jnp.concatenate([jnp.ones((1,), jnp.int32), (expert[1:] != expert[:-1]).astype(jnp.int32)])
    return dest.astype(jnp.int32), (tile, expert, lo, hi, chg)


def _bucket_table(lq, lk):
    dist = np.arange(lq)[:, None] + WINDOW - np.arange(lk)[None, :]
    band = (dist >= 0) & (dist < WINDOW)
    d = np.clip(dist, 0, WINDOW - 1)
    max_exact = N_BUCKETS // 2
    d_f = np.maximum(d, 1).astype(np.float32)
    large = max_exact + (np.log(d_f / max_exact) / math.log(MAX_DISTANCE / max_exact)
                         * (N_BUCKETS - max_exact)).astype(np.int32)
    large = np.minimum(large, N_BUCKETS - 1)
    return np.where(d < max_exact, d, large).astype(np.int32), band


def _bias_table(rb, lq, lk):
    bkt, band = _bucket_table(lq, lk)
    onehot = jnp.asarray(bkt)[None, :, :] == jnp.arange(N_BUCKETS, dtype=jnp.int32)[:, None, None]
    bias = jnp.sum(jnp.where(onehot[:, None], rb[:, :, None, None], 0.0), axis=0)
    return jnp.where(jnp.asarray(band)[None], bias, NEG_INF)


def _decay_tables(C):
    log_gamma = jnp.log(1.0 - 2.0 ** (-5.0 - jnp.arange(RET_HEADS, dtype=_F32)))
    idx = jnp.arange(C, dtype=_F32)
    diff = idx[:, None] - idx[None, :]
    decay_in = jnp.where((diff >= 0)[..., None],
                         jnp.exp(jnp.maximum(diff, 0.0)[..., None] * log_gamma), 0.0)
    q_dec = jnp.exp((idx + 1.0)[:, None] * log_gamma)
    k_dec = jnp.exp((C - 1.0 - idx)[:, None] * log_gamma)
    c_dec = jnp.exp(C * log_gamma)
    dec = jnp.transpose(decay_in, (2, 0, 1))
    qd = jnp.repeat(q_dec, RET_DK, axis=1)
    kd = jnp.repeat(k_dec, RET_DK, axis=1)
    return dec, qd, kd, c_dec


def _rope_tables(pos):
    half = RET_DK // 2
    inv = ROPE_BASE ** (-jnp.arange(half, dtype=_F32) * 2.0 / RET_DK)
    ang = pos.astype(_F32)[:, None] * inv[None, :]
    cos = jnp.cos(ang)
    sin = jnp.sin(ang)
    return jnp.concatenate([cos, cos], axis=1), jnp.concatenate([-sin, sin], axis=1)


def _constants(rel_bias, attn_sink, L, ls):
    cst = {}
    rb = rel_bias.astype(_F32)
    cst['bias_p'] = jnp.transpose(_bias_table(rb, WINDOW, 2 * WINDOW), (0, 2, 1))
    cst['sink'] = attn_sink.astype(_F32)
    cst['cos_p'], cst['sin_p'] = _rope_tables(jnp.arange(L))
    cst['dec_p'], cst['qd_p'], cst['kd_p'], cst['cdec_p'] = _decay_tables(min(RET_CHUNK, L))
    bias = _bias_table(rb, ls, WINDOW + ls).reshape(KV_HEADS, GQA_GROUP * ls, WINDOW + ls)
    cst['bias_hist'] = bias[:, :, :WINDOW]
    cst['bias_new'] = bias[:, :, WINDOW:]
    cst['sink_col'] = jnp.repeat(attn_sink.astype(_F32).reshape(KV_HEADS, GQA_GROUP), ls,
                                 axis=1)[..., None]
    cst['cos_s'], cst['sin_s'] = _rope_tables(PAST_LEN + jnp.arange(ls))
    cst['dec_s'], cst['qd_s'], cst['kd_s'], cst['cdec_s'] = _decay_tables(min(RET_CHUNK, ls))
    return cst


def _layer_weights(layer, norm_mix, w_in, w_branch_attn, w_branch_ret, w_out, norm_ffn,
                   w_router_group, b_router_group, w_router_expert, b_router_expert):
    D = w_in.shape[1]
    wr = jnp.concatenate([w_router_group[layer].astype(_F32), w_router_expert[layer].astype(_F32)], axis=1)
    wr = jnp.pad(wr, ((0, 0), (0, ROUTE_LANES - wr.shape[1])))
    wrh = wr.astype(_BF16)
    wrl = jnp.concatenate([wrh, (wr - wrh.astype(_F32)).astype(_BF16)], axis=1)
    br = jnp.concatenate([b_router_group[layer].astype(_F32), b_router_expert[layer].astype(_F32)])
    br = jnp.pad(br, (0, ROUTE_LANES - br.shape[0]))[None, :]
    return {
        'nmix': norm_mix[layer].astype(_F32)[None, :],
        'win': w_in[layer].astype(_BF16),
        'wba': w_branch_attn[layer].astype(_BF16),
        'wbr': w_branch_ret[layer].astype(_BF16),
        'wout': w_out[layer].astype(_BF16),
        'nffn': norm_ffn[layer].astype(_F32)[None, :],
        'wrh': wrh, 'wrl': wrl, 'br': br,
    }


def kernel(x_prompt, x_sample, cache_k, cache_v, state_ret, norm_mix, w_in, attn_sink, rel_bias,
           w_branch_attn, w_branch_ret, w_out, norm_ffn, w_router_group, b_router_group,
           w_router_expert, b_router_expert, w_gate, w_up, w_down, norm_final):
    depth = w_in.shape[0]
    assert depth == 1, "the final norm is fused into the MoE combine of the only layer"
    B, L, D = x_prompt.shape
    NB, ls, _ = x_sample.shape
    Tp, Ts = B * L, NB * ls
    nfin = norm_final.astype(_F32)[None, :]
    yp, ys = x_prompt, x_sample
    pk, pv, ps, sk, sv, ss = [], [], [], [], [], []
    for layer in range(depth):
        w = _layer_weights(layer, norm_mix, w_in, w_branch_attn, w_branch_ret, w_out, norm_ffn,
                           w_router_group, b_router_group, w_router_expert, b_router_expert)
        cst = _constants(rel_bias, attn_sink[layer], L, ls)
        x1p, routep, k1, v1, s1, cnt_p = _prompt_mixer(yp, cst, w)
        ys2 = ys.reshape(Ts, D)
        proj = _sample_inproj(ys2, w)
        attn_s, ret_s, k2, v2, s2 = _sample_core(
            proj, ls,
            cache_k[layer].reshape(NB, WINDOW, KV_W), cache_v[layer].reshape(NB, WINDOW, KV_W),
            state_ret[layer], cst)
        x1s, routes, cnt_all = _sample_post(
            ys2, attn_s, ret_s,
            proj[:, OFF_GA:OFF_GA + D], proj[:, OFF_GT:OFF_GT + D], w, cnt_p)
        n_rows = 2 * (Tp + Ts)
        assert n_rows % MOE_TM == 0
        counts = cnt_all[N_GROUPS:N_GROUPS + N_EXPERTS, 0].astype(jnp.int32)
        dest, work = _routing_tables(jnp.concatenate([routep, routes], axis=1), counts,
                                     n_rows // MOE_TM)
        xs3 = _dispatch(dest, x1p.reshape(Tp, SUB, LANE), x1s.reshape(Ts, SUB, LANE))
        yb2 = _gmm(work, xs3.reshape(n_rows * SUB, LANE), w['nffn'],
                   w_gate[layer], w_up[layer], w_down[layer])
        yb3 = yb2.reshape(n_rows, SUB, LANE)
        yp = _combine(dest, x1p, routep, nfin, yb3, 0).reshape(B, L, D)
        ys = _combine(dest, x1s, routes, nfin, yb3, Tp).reshape(NB, ls, D)
        pk.append(k1.reshape(B, WINDOW, KV_HEADS, HEAD_DIM))
        pv.append(v1.reshape(B, WINDOW, KV_HEADS, HEAD_DIM))
        ps.append(s1)
        sk.append(k2.reshape(NB, WINDOW, KV_HEADS, HEAD_DIM))
        sv.append(v2.reshape(NB, WINDOW, KV_HEADS, HEAD_DIM))
        ss.append(s2)
    return (yp, ys, jnp.stack(pk), jnp.stack(pv), jnp.stack(ps),
            jnp.stack(sk), jnp.stack(sv), jnp.stack(ss))
```

```python
import functools
import math

import jax
import jax.numpy as jnp
import numpy as np
from jax import lax
from jax.experimental import pallas as pl
from jax.experimental.pallas import tpu as pltpu

D_MODEL = 1024
HEAD_DIM = 64
KV_HEADS = 4
GQA_GROUP = 4
ATTN_HEADS = KV_HEADS * GQA_GROUP
WINDOW = 128
N_BUCKETS = 32
MAX_DISTANCE = 128
RET_HEADS = 4
RET_DK = 128
RET_DV = 256
RET_CHUNK = 128
ROPE_BASE = 10000.0
N_GROUPS = 4
EXPERTS_PER_GROUP = 8
N_EXPERTS = N_GROUPS * EXPERTS_PER_GROUP
EXPERT_FF = 512
NORM_EPS = 1e-6
NEG_INF = -1e30
PAST_LEN = 16384

ATTN_W = ATTN_HEADS * HEAD_DIM
KV_W = KV_HEADS * HEAD_DIM
RQ_W = RET_HEADS * RET_DK
RV_W = RET_HEADS * RET_DV
OFF_QA = 0
OFF_KA = OFF_QA + ATTN_W
OFF_VA = OFF_KA + KV_W
OFF_QR = OFF_VA + KV_W
OFF_KR = OFF_QR + RQ_W
OFF_VR = OFF_KR + RQ_W
OFF_GR = OFF_VR + RV_W
OFF_GA = OFF_GR + RV_W
OFF_GT = OFF_GA + D_MODEL
ROUTE_LANES = 128
ROUTE_ROWS = 40
ROUTE_OUT = 8

LANE = 128
SUB = 8
PROMPT_TM = 512
PROJ_PANEL = 256
POST_PARTS = 2
SAMPLE_GROUP = 8
SAMPLE_INPROJ_STEPS = 4
SAMPLE_ROWS = 16
MOE_TM = 1024
GMM_SUBTILE = 256
GMM_PARTS = 2
ROW_TM = 512
COMBINE_CHUNKS = 2
DMA_UNROLL = 8
VMEM_LIMIT = 60 * 1024 * 1024

_F32 = jnp.float32
_BF16 = jnp.bfloat16


def _const_spec(shape):
    nd = len(shape)
    return pl.BlockSpec(shape, lambda *_: (0,) * nd, pipeline_mode=pl.Buffered(1))


def _rms(x, gain):
    return x * lax.rsqrt(jnp.mean(x * x, axis=-1, keepdims=True) + NORM_EPS) * gain


def _dot(a, b):
    return jnp.dot(a, b, preferred_element_type=_F32)


def _dot_nt(a, b):
    return lax.dot_general(a, b, (((1,), (1,)), ((), ())), preferred_element_type=_F32)


def _dot_tn(a, b):
    return lax.dot_general(a, b, (((0,), (0,)), ((), ())), preferred_element_type=_F32)


def _load_rows(ref, n):
    return jnp.concatenate([ref[pl.ds(s, n, stride=SUB), :] for s in range(SUB)], axis=1)


def _store_rows(ref, val):
    n = val.shape[0]
    for s in range(SUB):
        ref[pl.ds(s, n, stride=SUB), :] = val[:, s * LANE:(s + 1) * LANE]


def _rotary(x, cosf, sinf):
    return x * cosf + pltpu.roll(x, RET_DK // 2, 1) * sinf


def _post(xs, attn_projs, rets, gates_a, gates_r, wbr, wout, nffn, wrh, wrl, br, cnt):
    parts = range(len(xs))
    ret_projs = [_dot(rets[j].astype(_BF16), wbr) for j in parts]
    x1s = []
    for j in parts:
        merged = jax.nn.sigmoid(gates_a[j]) * attn_projs[j] + jax.nn.sigmoid(gates_r[j]) * ret_projs[j]
        x1s.append(xs[j] + _dot(merged.astype(_BF16), wout))
    logit_parts = []
    for j in parts:
        xn2 = _rms(x1s[j], nffn)
        hi = xn2.astype(_BF16)
        lo = (xn2 - hi.astype(_F32)).astype(_BF16)
        t = _dot(hi, wrl)
        logit_parts.append(t[:, 0:ROUTE_LANES] + (t[:, ROUTE_LANES:] + _dot(lo, wrh)) + br)
    x1 = x1s[0] if len(x1s) == 1 else jnp.concatenate(x1s, axis=0)
    logits = logit_parts[0] if len(x1s) == 1 else jnp.concatenate(logit_parts, axis=0)
    n = logits.shape[0]
    lt = logits.T[0:ROUTE_ROWS, :]
    row = lax.broadcasted_iota(jnp.int32, (ROUTE_ROWS, n), 0)
    big = jnp.int32(1 << 20)
    neg = jnp.float32(-jnp.inf)
    gl = jnp.where(row < N_GROUPS, lt, neg)
    gmax = jnp.max(gl, axis=0, keepdims=True)
    gexp = jnp.exp(gl - gmax)
    gsum = jnp.sum(gexp, axis=0, keepdims=True)
    pg = gexp / gsum
    g_w = jnp.max(pg, axis=0, keepdims=True)
    g_idx = jnp.min(jnp.where(pg == g_w, row, big), axis=0, keepdims=True)
    e_row = row - N_GROUPS
    emask = (e_row >= 0) & (e_row < N_EXPERTS) & ((e_row >> 3) == g_idx)
    fl = jnp.where(emask, lt, neg)
    fmax = jnp.max(fl, axis=0, keepdims=True)
    fexp = jnp.exp(fl - fmax)
    fsum = jnp.sum(fexp, axis=0, keepdims=True)
    pe = jnp.where(emask, fexp / fsum, -1.0)
    p1 = jnp.max(pe, axis=0, keepdims=True)
    i1 = jnp.min(jnp.where(pe == p1, row, big), axis=0, keepdims=True)
    pe2 = jnp.where(row == i1, -1.0, pe)
    p2 = jnp.max(pe2, axis=0, keepdims=True)
    i2 = jnp.min(jnp.where(pe2 == p2, row, big), axis=0, keepdims=True)
    psum = p1 + p2
    gate1 = g_w * p1 / psum
    gate2 = g_w * p2 / psum
    oh1 = row == i1
    oh2 = row == i2
    c = jnp.where(oh1 | oh2, 1.0, 0.0)
    tt = lax.broadcasted_iota(jnp.int32, (n, n), 0)
    tc = lax.broadcasted_iota(jnp.int32, (n, n), 1)
    upper = jnp.where(tt < tc, 1.0, 0.0).astype(_BF16)
    before = _dot(c.astype(_BF16), upper) + cnt
    rank1 = jnp.sum(jnp.where(oh1, before, 0.0), axis=0, keepdims=True)
    rank2 = jnp.sum(jnp.where(oh2, before, 0.0), axis=0, keepdims=True)
    cnt = cnt + jnp.sum(c, axis=1, keepdims=True)
    r8 = lax.broadcasted_iota(jnp.int32, (ROUTE_OUT, n), 0)
    vals = [(i1 - N_GROUPS).astype(_F32), (i2 - N_GROUPS).astype(_F32), gate1, gate2, rank1, rank2]
    route = jnp.zeros((ROUTE_OUT, n), _F32)
    for k, v in enumerate(vals):
        route = jnp.where(r8 == k, v, route)
    return x1, route, cnt


def _prompt_mixer_kernel(sink_ref, cdec_ref,
                         x_ref, nmix_ref, win_ref, bias_ref, cos_ref, sin_ref, dec_ref,
                         qd_ref, kd_ref, wba_ref, wbr_ref, wout_ref, nffn_ref,
                         wrh_ref, wrl_ref, br_ref,
                         x1_ref, route_ref, knew_ref, vnew_ref, s_ref, cnt_ref,
                         qkv, proj, kctx, vctx, attn_t, ret):
    i = pl.program_id(1)
    last = pl.num_programs(1) - 1

    @pl.when((i == 0) & (pl.program_id(0) == 0))
    def _():
        cnt_ref[...] = jnp.zeros_like(cnt_ref)

    tm = x_ref.shape[1]
    n_sub = tm // WINDOW
    scale = HEAD_DIM ** -0.5

    @pl.when(i == 0)
    def _():
        s_ref[...] = jnp.zeros_like(s_ref)
        kctx[0:WINDOW, :] = jnp.zeros((WINDOW, KV_W), _BF16)
        vctx[0:WINDOW, :] = jnp.zeros((WINDOW, KV_W), _BF16)

    x = x_ref[0]
    xn = _rms(x, nmix_ref[...]).astype(_BF16)
    n_in = win_ref.shape[1]
    panel = PROJ_PANEL

    def project(c0):
        res = _dot(xn, win_ref[:, c0:c0 + panel])
        if c0 < OFF_QR:
            qkv[:, c0:c0 + panel] = res
        else:
            proj[:, c0 - OFF_QR:c0 - OFF_QR + panel] = res

    def cols(lo, width):
        return slice(lo - OFF_QR, lo - OFF_QR + width)

    for c0 in range(0, OFF_QR, panel):
        project(c0)
    later_panels = list(range(OFF_QR, n_in, panel))

    @pl.when(i == last)
    def _():
        knew_ref[0] = qkv[tm - WINDOW:tm, OFF_KA:OFF_KA + KV_W]
        vnew_ref[0] = qkv[tm - WINDOW:tm, OFF_VA:OFF_VA + KV_W]

    krow = lax.broadcasted_iota(jnp.int32, (2 * WINDOW, 1), 0)
    for c in range(n_sub):
        r0 = c * WINDOW
        kctx[WINDOW:2 * WINDOW, :] = qkv[r0:r0 + WINDOW, OFF_KA:OFF_KA + KV_W].astype(_BF16)
        vctx[WINDOW:2 * WINDOW, :] = qkv[r0:r0 + WINDOW, OFF_VA:OFF_VA + KV_W].astype(_BF16)
        if c == 0:
            pen = jnp.where((krow < WINDOW) & (i == 0), NEG_INF, 0.0).astype(_F32)
        for h in range(KV_HEADS):
            k_h = kctx[:, h * HEAD_DIM:(h + 1) * HEAD_DIM]
            v_h = vctx[:, h * HEAD_DIM:(h + 1) * HEAD_DIM]
            probs = []
            for g in range(GQA_GROUP):
                hq = h * GQA_GROUP + g
                q = (qkv[r0:r0 + WINDOW, hq * HEAD_DIM:(hq + 1) * HEAD_DIM] * scale).astype(_BF16)
                s = _dot_nt(k_h, q) + bias_ref[hq]
                if c == 0:
                    s = s + pen
                snk = sink_ref[hq]
                m = jnp.maximum(jnp.max(s, axis=0, keepdims=True), snk)
                p = jnp.exp(s - m)
                den = jnp.sum(p, axis=0, keepdims=True) + jnp.exp(snk - m)
                probs.append((p * (1.0 / den)).astype(_BF16))
            groups_left = (n_sub - c) * KV_HEADS - h
            for _ in range(-(-len(later_panels) // groups_left)):
                project(later_panels.pop(0))
            o_t = _dot_tn(v_h, jnp.concatenate(probs, axis=1))
            for g in range(GQA_GROUP):
                hq = h * GQA_GROUP + g
                attn_t[hq * HEAD_DIM:(hq + 1) * HEAD_DIM, r0:r0 + WINDOW] = (
                    o_t[:, g * WINDOW:(g + 1) * WINDOW].astype(_BF16))
        kctx[0:WINDOW, :] = kctx[WINDOW:2 * WINDOW, :]
        vctx[0:WINDOW, :] = vctx[WINDOW:2 * WINDOW, :]

    for c0 in later_panels:
        project(c0)

    for c in range(n_sub):
        r0 = c * RET_CHUNK
        cosf = cos_ref[r0:r0 + RET_CHUNK, :]
        sinf = sin_ref[r0:r0 + RET_CHUNK, :]
        for h in range(RET_HEADS):
            qc = _rotary(proj[r0:r0 + RET_CHUNK, cols(OFF_QR + h * RET_DK, RET_DK)], cosf, sinf)
            kc = _rotary(proj[r0:r0 + RET_CHUNK, cols(OFF_KR + h * RET_DK, RET_DK)], cosf, sinf) * (RET_DK ** -0.5)
            qd = qd_ref[:, h * RET_DK:(h + 1) * RET_DK]
            kd = kd_ref[:, h * RET_DK:(h + 1) * RET_DK]
            vc = proj[r0:r0 + RET_CHUNK, cols(OFF_VR + h * RET_DV, RET_DV)].astype(_BF16)
            sc = _dot_nt(qc.astype(_BF16), kc.astype(_BF16))
            s_old = s_ref[0, h]
            cross = _dot((qc * qd).astype(_BF16), s_old.astype(_BF16))
            s_ref[0, h] = s_old * cdec_ref[h] + _dot_tn((kc * kd).astype(_BF16), vc)
            o = _dot((sc * dec_ref[h]).astype(_BF16), vc) + cross
            o = o * lax.rsqrt(jnp.mean(o * o, axis=-1, keepdims=True) + NORM_EPS)
            gr = proj[r0:r0 + RET_CHUNK, cols(OFF_GR + h * RET_DV, RET_DV)]
            ret[r0:r0 + RET_CHUNK, h * RET_DV:(h + 1) * RET_DV] = o * (gr * jax.nn.sigmoid(gr))

    pn = tm // POST_PARTS
    rows = [slice(j * pn, (j + 1) * pn) for j in range(POST_PARTS)]
    x1, route, cnt = _post([x[r] for r in rows],
                           [_dot_tn(attn_t[:, r], wba_ref[...]) for r in rows],
                           [ret[r, :] for r in rows],
                           [proj[r, cols(OFF_GA, D_MODEL)] for r in rows],
                           [proj[r, cols(OFF_GT, D_MODEL)] for r in rows],
                           wbr_ref[...], wout_ref[...], nffn_ref[...],
                           wrh_ref[...], wrl_ref[...], br_ref[...], cnt_ref[:, 0:1])
    _store_rows(x1_ref, x1)
    route_ref[...] = route
    cnt_ref[...] = jnp.broadcast_to(cnt, cnt_ref.shape)


def _prompt_mixer(x, cst, w):
    B, L, D = x.shape
    tm = min(PROMPT_TM, L)
    nb = L // tm
    n_in = w['win'].shape[1]
    step = lambda b, i, *_: (b, i, 0)
    per_b = lambda b, i, *_: (b, 0, 0)

    grid_spec = pltpu.PrefetchScalarGridSpec(
        num_scalar_prefetch=2,
        grid=(B, nb),
        in_specs=[
            pl.BlockSpec((1, tm, D), step),
            _const_spec((1, D)),
            _const_spec((D, n_in)),
            _const_spec((ATTN_HEADS, 2 * WINDOW, WINDOW)),
            pl.BlockSpec((tm, RET_DK), lambda b, i, *_: (i, 0)),
            pl.BlockSpec((tm, RET_DK), lambda b, i, *_: (i, 0)),
            _const_spec((RET_HEADS, RET_CHUNK, RET_CHUNK)),
            _const_spec((RET_CHUNK, RQ_W)),
            _const_spec((RET_CHUNK, RQ_W)),
            _const_spec((ATTN_W, D)),
            _const_spec((RV_W, D)),
            _const_spec((D, D)),
            _const_spec((1, D)),
            _const_spec((D, ROUTE_LANES)),
            _const_spec((D, 2 * ROUTE_LANES)),
            _const_spec((1, ROUTE_LANES)),
        ],
        out_specs=[
            pl.BlockSpec((tm * SUB, LANE), lambda b, i, *_: (b * nb + i, 0)),
            pl.BlockSpec((ROUTE_OUT, tm), lambda b, i, *_: (0, b * nb + i)),
            pl.BlockSpec((1, WINDOW, KV_W), per_b),
            pl.BlockSpec((1, WINDOW, KV_W), per_b),
            pl.BlockSpec((1, RET_HEADS, RET_DK, RET_DV), lambda b, i, *_: (b, 0, 0, 0)),
            pl.BlockSpec((ROUTE_ROWS, LANE), lambda b, i, *_: (0, 0)),
        ],
        scratch_shapes=[
            pltpu.VMEM((tm, OFF_QR), _F32),
            pltpu.VMEM((tm, n_in - OFF_QR), _F32),
            pltpu.VMEM((2 * WINDOW, KV_W), _BF16),
            pltpu.VMEM((2 * WINDOW, KV_W), _BF16),
            pltpu.VMEM((ATTN_W, tm), _BF16),
            pltpu.VMEM((tm, RV_W), _F32),
        ],
    )
    assert D == SUB * LANE
    out_shape = [
        jax.ShapeDtypeStruct((B * L * SUB, LANE), _F32),
        jax.ShapeDtypeStruct((ROUTE_OUT, B * L), _F32),
        jax.ShapeDtypeStruct((B, WINDOW, KV_W), _F32),
        jax.ShapeDtypeStruct((B, WINDOW, KV_W), _F32),
        jax.ShapeDtypeStruct((B, RET_HEADS, RET_DK, RET_DV), _F32),
        jax.ShapeDtypeStruct((ROUTE_ROWS, LANE), _F32),
    ]
    return pl.pallas_call(
        _prompt_mixer_kernel,
        grid_spec=grid_spec,
        out_shape=out_shape,
        compiler_params=pltpu.CompilerParams(
            dimension_semantics=("arbitrary", "arbitrary"), vmem_limit_bytes=VMEM_LIMIT),
        name="prompt_mixer",
    )(cst['sink'], cst['cdec_p'],
      x, w['nmix'], w['win'], cst['bias_p'], cst['cos_p'], cst['sin_p'], cst['dec_p'],
      cst['qd_p'], cst['kd_p'], w['wba'], w['wbr'], w['wout'], w['nffn'],
      w['wrh'], w['wrl'], w['br'])


def _inproj_kernel(x_ref, nmix_ref, win_ref, o_ref):
    xn = _rms(x_ref[...], nmix_ref[...]).astype(_BF16)
    o_ref[...] = _dot(xn, win_ref[...])


def _sample_inproj(x2d, w):
    T, D = x2d.shape
    n_in = w['win'].shape[1]
    panel = n_in // SAMPLE_INPROJ_STEPS
    assert panel % LANE == 0
    return pl.pallas_call(
        _inproj_kernel,
        grid=(n_in // panel,),
        in_specs=[pl.BlockSpec((T, D), lambda j: (0, 0)),
                  pl.BlockSpec((1, D), lambda j: (0, 0)),
                  pl.BlockSpec((D, panel), lambda j: (0, j))],
        out_specs=pl.BlockSpec((T, panel), lambda j: (0, j)),
        out_shape=jax.ShapeDtypeStruct((T, n_in), _F32),
        compiler_params=pltpu.CompilerParams(
            dimension_semantics=("arbitrary",), vmem_limit_bytes=VMEM_LIMIT),
        name="sample_inproj",
    )(x2d, w['nmix'], w['win'])


def _sample_core_kernel(cdec_ref, proj_ref, ck_ref, cv_ref, st_ref, bh_ref, bn_ref, snk_ref,
                        cos_ref, sin_ref, dec_ref, qd_ref, kd_ref,
                        attn_ref, ret_ref, nk_ref, nv_ref, ns_ref):
    G = ck_ref.shape[0]
    ls = proj_ref.shape[0] // G
    per = SAMPLE_ROWS // ls
    scale = HEAD_DIM ** -0.5
    cosf = cos_ref[...]
    sinf = sin_ref[...]
    heads = [slice(h * HEAD_DIM, (h + 1) * HEAD_DIM) for h in range(KV_HEADS)]

    def body(j, carry):
        group = pl.ds(pl.multiple_of(j * SAMPLE_ROWS, SAMPLE_ROWS), SAMPLE_ROWS)
        rows = proj_ref[group, :]
        seqs = []
        for s in range(per):
            b = j * per + s
            row = rows[s * ls:(s + 1) * ls]
            k_new = row[:, OFF_KA:OFF_KA + KV_W]
            v_new = row[:, OFF_VA:OFF_VA + KV_W]
            ck = ck_ref[b]
            cv = cv_ref[b]
            nk_ref[b, 0:WINDOW - ls, :] = ck[ls:WINDOW, :]
            nk_ref[b, WINDOW - ls:WINDOW, :] = k_new
            nv_ref[b, 0:WINDOW - ls, :] = cv[ls:WINDOW, :]
            nv_ref[b, WINDOW - ls:WINDOW, :] = v_new
            seqs.append(dict(b=b, row=row, ckb=ck.astype(_BF16), cvb=cv.astype(_BF16),
                             knb=k_new.astype(_BF16), vnb=v_new.astype(_BF16)))
        for q in seqs:
            row = q['row']
            q['s1'], q['s2'] = [], []
            for h in range(KV_HEADS):
                q4 = (jnp.concatenate(
                    [row[:, (h * GQA_GROUP + g) * HEAD_DIM:(h * GQA_GROUP + g + 1) * HEAD_DIM]
                     for g in range(GQA_GROUP)], axis=0) * scale).astype(_BF16)
                q['s1'].append(_dot_nt(q4, q['ckb'][:, heads[h]]))
                q['s2'].append(_dot_nt(q4, q['knb'][:, heads[h]]))
        for q in seqs:
            row, b = q['row'], q['b']
            q['sc'], q['cross'], q['vc'] = [], [], []
            for h in range(RET_HEADS):
                qrot = _rotary(row[:, OFF_QR + h * RET_DK:OFF_QR + (h + 1) * RET_DK], cosf, sinf)
                krot = _rotary(row[:, OFF_KR + h * RET_DK:OFF_KR + (h + 1) * RET_DK], cosf, sinf) * (RET_DK ** -0.5)
                vc = row[:, OFF_VR + h * RET_DV:OFF_VR + (h + 1) * RET_DV].astype(_BF16)
                qd = qd_ref[:, h * RET_DK:(h + 1) * RET_DK]
                kd = kd_ref[:, h * RET_DK:(h + 1) * RET_DK]
                s_old = st_ref[b, h]
                q['sc'].append(_dot_nt(qrot.astype(_BF16), krot.astype(_BF16)))
                q['cross'].append(_dot((qrot * qd).astype(_BF16), s_old.astype(_BF16)))
                ns_ref[b, h] = s_old * cdec_ref[h] + _dot_tn((krot * kd).astype(_BF16), vc)
                q['vc'].append(vc)
        for q in seqs:
            q['p1'], q['p2'] = [], []
            for h in range(KV_HEADS):
                s1 = q['s1'][h] + bh_ref[h]
                s2 = q['s2'][h] + bn_ref[h]
                snk = snk_ref[h]
                m = jnp.maximum(jnp.maximum(jnp.max(s1, axis=-1, keepdims=True),
                                            jnp.max(s2, axis=-1, keepdims=True)), snk)
                p1 = jnp.exp(s1 - m)
                p2 = jnp.exp(s2 - m)
                den = (jnp.sum(p1, axis=-1, keepdims=True) + jnp.sum(p2, axis=-1, keepdims=True)
                       + jnp.exp(snk - m))
                r = 1.0 / den
                q['p1'].append((p1 * r).astype(_BF16))
                q['p2'].append((p2 * r).astype(_BF16))
            q['scb'] = [(q['sc'][h] * dec_ref[h]).astype(_BF16) for h in range(RET_HEADS)]
        for q in seqs:
            q['out'] = [_dot(q['p1'][h], q['cvb'][:, heads[h]]) + _dot(q['p2'][h], q['vnb'][:, heads[h]])
                        for h in range(KV_HEADS)]
            q['ret'] = [_dot(q['scb'][h], q['vc'][h]) + q['cross'][h] for h in range(RET_HEADS)]
        for h in range(KV_HEADS):
            for g in range(GQA_GROUP):
                hq = h * GQA_GROUP + g
                attn_ref[group, hq * HEAD_DIM:(hq + 1) * HEAD_DIM] = jnp.concatenate(
                    [q['out'][h][g * ls:(g + 1) * ls] for q in seqs], axis=0)
        for h in range(RET_HEADS):
            o = jnp.concatenate([q['ret'][h] for q in seqs], axis=0)
            o = o * lax.rsqrt(jnp.mean(o * o, axis=-1, keepdims=True) + NORM_EPS)
            gr = rows[:, OFF_GR + h * RET_DV:OFF_GR + (h + 1) * RET_DV]
            ret_ref[group, h * RET_DV:(h + 1) * RET_DV] = o * (gr * jax.nn.sigmoid(gr))
        return carry

    lax.fori_loop(0, G // per, body, 0)


def _sample_core(proj2, ls, ck, cv, st, cst):
    n_in = proj2.shape[1]
    NB = proj2.shape[0] // ls
    G = min(SAMPLE_GROUP, NB)
    assert SAMPLE_ROWS % ls == 0 and G % (SAMPLE_ROWS // ls) == 0 and NB % G == 0
    row_blk = lambda i, *_: (i, 0)
    blk3 = lambda i, *_: (i, 0, 0)
    blk4 = lambda i, *_: (i, 0, 0, 0)
    c2 = lambda i, *_: (0, 0)
    c3 = lambda i, *_: (0, 0, 0)
    ql = GQA_GROUP * ls
    grid_spec = pltpu.PrefetchScalarGridSpec(
        num_scalar_prefetch=1,
        grid=(NB // G,),
        in_specs=[
            pl.BlockSpec((G * ls, n_in), row_blk),
            pl.BlockSpec((G, WINDOW, KV_W), blk3),
            pl.BlockSpec((G, WINDOW, KV_W), blk3),
            pl.BlockSpec((G, RET_HEADS, RET_DK, RET_DV), blk4),
            pl.BlockSpec((KV_HEADS, ql, WINDOW), c3),
            pl.BlockSpec((KV_HEADS, ql, ls), c3),
            pl.BlockSpec((KV_HEADS, ql, 1), c3),
            pl.BlockSpec((ls, RET_DK), c2),
            pl.BlockSpec((ls, RET_DK), c2),
            pl.BlockSpec((RET_HEADS, ls, ls), c3),
            pl.BlockSpec((ls, RQ_W), c2),
            pl.BlockSpec((ls, RQ_W), c2),
        ],
        out_specs=[
            pl.BlockSpec((G * ls, ATTN_W), row_blk),
            pl.BlockSpec((G * ls, RV_W), row_blk),
            pl.BlockSpec((G, WINDOW, KV_W), blk3),
            pl.BlockSpec((G, WINDOW, KV_W), blk3),
            pl.BlockSpec((G, RET_HEADS, RET_DK, RET_DV), blk4),
        ],
    )
    out_shape = [
        jax.ShapeDtypeStruct((NB * ls, ATTN_W), _F32),
        jax.ShapeDtypeStruct((NB * ls, RV_W), _F32),
        jax.ShapeDtypeStruct((NB, WINDOW, KV_W), _F32),
        jax.ShapeDtypeStruct((NB, WINDOW, KV_W), _F32),
        jax.ShapeDtypeStruct((NB, RET_HEADS, RET_DK, RET_DV), _F32),
    ]
    return pl.pallas_call(
        _sample_core_kernel,
        grid_spec=grid_spec,
        out_shape=out_shape,
        compiler_params=pltpu.CompilerParams(
            dimension_semantics=("arbitrary",), vmem_limit_bytes=VMEM_LIMIT),
        name="sample_core",
    )(cst['cdec_s'], proj2, ck, cv, st, cst['bias_hist'], cst['bias_new'], cst['sink_col'],
      cst['cos_s'], cst['sin_s'], cst['dec_s'], cst['qd_s'], cst['kd_s'])


def _sample_post_kernel(x_ref, attn_ref, ret_ref, ga_ref, gt_ref, wba_ref, wbr_ref, wout_ref,
                        nffn_ref, wrh_ref, wrl_ref, br_ref, cnt0_ref, x1_ref, route_ref, cnt_ref):
    x1, route, cnt = _post([x_ref[...]], [_dot(attn_ref[...].astype(_BF16), wba_ref[...])],
                           [ret_ref[...]], [ga_ref[...]], [gt_ref[...]],
                           wbr_ref[...], wout_ref[...], nffn_ref[...],
                           wrh_ref[...], wrl_ref[...], br_ref[...], cnt0_ref[:, 0:1])
    _store_rows(x1_ref, x1)
    route_ref[...] = route
    cnt_ref[...] = jnp.broadcast_to(cnt, cnt_ref.shape)


def _sample_post(x2d, attn, ret, ga, gt, w, cnt0):
    T, D = x2d.shape
    full = lambda s: pl.BlockSpec(s, lambda i: (0,) * len(s))
    return pl.pallas_call(
        _sample_post_kernel,
        grid=(1,),
        in_specs=[full((T, D)), full((T, ATTN_W)), full((T, RV_W)), full((T, D)), full((T, D)),
                  full((ATTN_W, D)), full((RV_W, D)), full((D, D)), full((1, D)),
                  full((D, ROUTE_LANES)), full((D, 2 * ROUTE_LANES)), full((1, ROUTE_LANES)),
                  full((ROUTE_ROWS, LANE))],
        out_specs=[full((T * SUB, LANE)), full((ROUTE_OUT, T)), full((ROUTE_ROWS, LANE))],
        out_shape=[jax.ShapeDtypeStruct((T * SUB, LANE), _F32),
                   jax.ShapeDtypeStruct((ROUTE_OUT, T), _F32),
                   jax.ShapeDtypeStruct((ROUTE_ROWS, LANE), _F32)],
        compiler_params=pltpu.CompilerParams(
            dimension_semantics=("arbitrary",), vmem_limit_bytes=VMEM_LIMIT),
        name="sample_post",
    )(x2d, attn, ret, ga, gt, w['wba'], w['wbr'], w['wout'], w['nffn'], w['wrh'], w['wrl'], w['br'],
      cnt0)


def _dispatch_kernel(dest_ref, xp_ref, xq_ref, xs_ref, sem, *, p_steps):
    i = pl.program_id(0)
    n_tok = pl.num_programs(0) * ROW_TM

    def copy_tile(src):
        def tile_copy(r, d):
            return pltpu.make_async_copy(src.at[r], xs_ref.at[d], sem)

        def start(r, c):
            t = i * ROW_TM + r
            tile_copy(r, dest_ref[t]).start(priority=0)
            tile_copy(r, dest_ref[n_tok + t]).start(priority=1)
            return c

        lax.fori_loop(0, ROW_TM, start, 0, unroll=DMA_UNROLL)

        def wait(r, c):
            tile_copy(0, 0).wait()
            tile_copy(0, 0).wait()
            return c

        lax.fori_loop(0, ROW_TM, wait, 0, unroll=DMA_UNROLL)

    @pl.when(i < p_steps)
    def _():
        copy_tile(xp_ref)

    @pl.when(i >= p_steps)
    def _():
        copy_tile(xq_ref)


def _dispatch(dest, xp3, xq3):
    Tp, Tq = xp3.shape[0], xq3.shape[0]
    assert Tp % ROW_TM == 0 and Tq % ROW_TM == 0
    p_steps = Tp // ROW_TM
    grid_spec = pltpu.PrefetchScalarGridSpec(
        num_scalar_prefetch=1,
        grid=((Tp + Tq) // ROW_TM,),
        in_specs=[
            pl.BlockSpec((ROW_TM, SUB, LANE), lambda i, *_: (jnp.minimum(i, p_steps - 1), 0, 0)),
            pl.BlockSpec((ROW_TM, SUB, LANE), lambda i, *_: (jnp.maximum(i - p_steps, 0), 0, 0)),
        ],
        out_specs=pl.BlockSpec(memory_space=pl.ANY),
        scratch_shapes=[pltpu.SemaphoreType.DMA],
    )
    return pl.pallas_call(
        functools.partial(_dispatch_kernel, p_steps=p_steps),
        grid_spec=grid_spec,
        out_shape=jax.ShapeDtypeStruct((2 * (Tp + Tq), SUB, LANE), _F32),
        compiler_params=pltpu.CompilerParams(dimension_semantics=("arbitrary",)),
        name="moe_dispatch",
    )(dest, xp3, xq3)


def _gmm_kernel(tile_ref, exp_ref, lo_ref, hi_ref, chg_ref,
                x_ref, nffn_ref, wg_ref, wu_ref, wd_ref, y_ref, wg_s, wu_s, wd_s):
    m = pl.program_id(0)
    tm = x_ref.shape[0] // SUB

    @pl.when(chg_ref[m] == 1)
    def _():
        wg_s[...] = wg_ref[0].astype(_BF16)
        wu_s[...] = wu_ref[0].astype(_BF16)
        wd_s[...] = wd_ref[0].astype(_BF16)

    lo = lo_ref[m]
    hi = hi_ref[m]
    hn = GMM_SUBTILE // GMM_PARTS

    def subtile(row0):
        base = tile_ref[m] * tm + row0
        parts = [pl.ds((row0 + j * hn) * SUB, hn * SUB) for j in range(GMM_PARTS)]
        xs = []
        for j in range(GMM_PARTS):
            rows = base + j * hn + lax.broadcasted_iota(jnp.int32, (hn, 1), 0)
            mine = (rows >= lo) & (rows < hi)
            xn = _rms(_load_rows(x_ref.at[parts[j]], hn), nffn_ref[...])
            xs.append(jnp.where(mine, xn, 0.0).astype(_BF16))
        gate_up = [(_dot(x, wg_s[...]), _dot(x, wu_s[...])) for x in xs]
        ys = [_dot(((a * jax.nn.sigmoid(a)) * u).astype(_BF16), wd_s[...]) for a, u in gate_up]
        first = lo <= base

        @pl.when(first)
        def _():
            for j in range(GMM_PARTS):
                _store_rows(y_ref.at[parts[j]], ys[j])

        @pl.when(jnp.logical_not(first))
        def _():
            for j in range(GMM_PARTS):
                _store_rows(y_ref.at[parts[j]], _load_rows(y_ref.at[parts[j]], hn) + ys[j])

    tile_base = tile_ref[m] * tm
    whole = (lo <= tile_base) & (hi >= tile_base + tm)

    @pl.when(whole)
    def _():
        n_parts = tm // hn
        parts = [pl.ds(j * hn * SUB, hn * SUB) for j in range(n_parts)]
        def gate_up(j):
            x = _rms(_load_rows(x_ref.at[parts[j]], hn), nffn_ref[...]).astype(_BF16)
            return _dot(x, wg_s[...]), _dot(x, wu_s[...])

        nxt = gate_up(0)
        for j in range(n_parts):
            a, u = nxt
            if j + 1 < n_parts:
                nxt = gate_up(j + 1)
            _store_rows(y_ref.at[parts[j]],
                        _dot(((a * jax.nn.sigmoid(a)) * u).astype(_BF16), wd_s[...]))

    for row0 in range(0, tm, GMM_SUBTILE):
        base = tile_base + row0
        pl.when(jnp.logical_not(whole) & (hi > base) & (lo < base + GMM_SUBTILE) & (hi > lo))(
            functools.partial(subtile, row0))


def _gmm(work, xs2, nffn, wg, wu, wd):
    A = xs2.shape[0] // SUB
    E, D, F = wg.shape
    n_work = work[0].shape[0]
    grid_spec = pltpu.PrefetchScalarGridSpec(
        num_scalar_prefetch=5,
        grid=(n_work,),
        in_specs=[
            pl.BlockSpec((MOE_TM * SUB, LANE), lambda m, t, e, *_: (t[m], 0)),
            pl.BlockSpec((1, D), lambda m, t, e, *_: (0, 0)),
            pl.BlockSpec((1, D, F), lambda m, t, e, *_: (e[m], 0, 0)),
            pl.BlockSpec((1, D, F), lambda m, t, e, *_: (e[m], 0, 0)),
            pl.BlockSpec((1, F, D), lambda m, t, e, *_: (e[m], 0, 0)),
        ],
        out_specs=pl.BlockSpec((MOE_TM * SUB, LANE), lambda m, t, e, *_: (t[m], 0)),
        scratch_shapes=[pltpu.VMEM((D, F), _BF16), pltpu.VMEM((D, F), _BF16),
                        pltpu.VMEM((F, D), _BF16)],
    )
    return pl.pallas_call(
        _gmm_kernel,
        grid_spec=grid_spec,
        out_shape=jax.ShapeDtypeStruct((A * SUB, LANE), _F32),
        compiler_params=pltpu.CompilerParams(
            dimension_semantics=("arbitrary",), vmem_limit_bytes=VMEM_LIMIT),
        name="moe_gmm",
    )(*work, xs2, nffn, wg, wu, wd)


def _combine_kernel(dest_ref, x1_ref, route_ref, nfin_ref, yb_ref, o_ref, buf, sems, *, tok0, n_tok):
    i = pl.program_id(0)
    n_steps = pl.num_programs(0)
    tm = o_ref.shape[0]

    def tile_copy(d, slot, k, r):
        rows = pl.ds(pl.multiple_of(r * SUB, SUB), SUB)
        return pltpu.make_async_copy(yb_ref.at[d], buf.at[slot, k, rows], sems.at[slot])

    def issue(step, slot):
        base = tok0 + step * tm

        def start(r, c):
            t = base + r
            tile_copy(dest_ref[t], slot, 0, r).start(priority=0)
            tile_copy(dest_ref[n_tok + t], slot, 1, r).start(priority=1)
            return c

        lax.fori_loop(0, tm, start, 0, unroll=DMA_UNROLL)

    @pl.when(i == 0)
    def _():
        issue(0, 0)

    slot = i % 2

    def wait(r, c):
        tile_copy(0, slot, 0, 0).wait()
        tile_copy(0, slot, 1, 0).wait()
        return c

    lax.fori_loop(0, tm, wait, 0, unroll=DMA_UNROLL)

    def combine_rows(prefetch):
        rt = jnp.concatenate([route_ref[...], jnp.zeros((LANE - ROUTE_OUT, tm), _F32)], axis=0).T
        cn = tm // COMBINE_CHUNKS
        for ch in range(COMBINE_CHUNKS):
            if prefetch:
                base = tok0 + (i + 1) * tm
                for r in range(ch * cn, (ch + 1) * cn):
                    tile_copy(dest_ref[base + r], 1 - slot, 0, r).start(priority=0)
                    tile_copy(dest_ref[n_tok + base + r], 1 - slot, 1, r).start(priority=1)
            rows = pl.ds(ch * cn * SUB, cn * SUB)
            g0 = rt[ch * cn:(ch + 1) * cn, 2:3]
            g1 = rt[ch * cn:(ch + 1) * cn, 3:4]
            y0 = _load_rows(buf.at[slot, 0, rows], cn)
            y1 = _load_rows(buf.at[slot, 1, rows], cn)
            y = _load_rows(x1_ref.at[rows], cn) + (y0 * g0 + y1 * g1)
            o_ref[ch * cn:(ch + 1) * cn, :] = _rms(y, nfin_ref[...])

    @pl.when(i + 1 < n_steps)
    def _():
        combine_rows(True)

    @pl.when(i + 1 == n_steps)
    def _():
        combine_rows(False)


def _combine(dest, x1_2, route, nfin, yb3, tok0):
    T = x1_2.shape[0] // SUB
    D = SUB * LANE
    tm = min(ROW_TM, T)
    grid_spec = pltpu.PrefetchScalarGridSpec(
        num_scalar_prefetch=1,
        grid=(T // tm,),
        in_specs=[
            pl.BlockSpec((tm * SUB, LANE), lambda i, *_: (i, 0)),
            pl.BlockSpec((ROUTE_OUT, tm), lambda i, *_: (0, i)),
            pl.BlockSpec((1, D), lambda i, *_: (0, 0)),
            pl.BlockSpec(memory_space=pl.ANY),
        ],
        out_specs=pl.BlockSpec((tm, D), lambda i, *_: (i, 0)),
        scratch_shapes=[pltpu.VMEM((2, 2, tm * SUB, LANE), _F32), pltpu.SemaphoreType.DMA((2,))],
    )
    return pl.pallas_call(
        functools.partial(_combine_kernel, tok0=tok0, n_tok=dest.shape[0] // 2),
        grid_spec=grid_spec,
        out_shape=jax.ShapeDtypeStruct((T, D), _F32),
        compiler_params=pltpu.CompilerParams(
            dimension_semantics=("arbitrary",), vmem_limit_bytes=VMEM_LIMIT),
        name="moe_combine",
    )(dest, x1_2, route, nfin, yb3)


def _routing_tables(route, counts, n_tiles):
    experts = route[0:2].astype(jnp.int32)
    ranks = route[4:6].astype(jnp.int32)
    A = experts.size
    ids = jnp.arange(N_EXPERTS, dtype=jnp.int32)
    ends = jnp.cumsum(counts)
    starts = ends - counts
    dest = ranks + jnp.sum(jnp.where(experts[..., None] == ids, starts, 0), axis=-1)
    dest = dest.reshape(-1)
    tile_starts = jnp.arange(n_tiles, dtype=jnp.int32) * MOE_TM
    pos_t = jnp.arange(n_tiles, dtype=jnp.int32) + jnp.sum(starts[None, :] < tile_starts[:, None], axis=1)
    pos_e = ids + jnp.sum(tile_starts[None, :] <= starts[:, None], axis=1)
    slots = jnp.arange(n_tiles + N_EXPERTS, dtype=jnp.int32)
    pts = (jnp.sum(jnp.where(pos_t[None, :] == slots[:, None], tile_starts[None, :], 0), axis=1)
           + jnp.sum(jnp.where(pos_e[None, :] == slots[:, None], starts[None, :], 0), axis=1))
    lo = pts.astype(jnp.int32)
    hi = jnp.concatenate([lo[1:], jnp.array([A], jnp.int32)])
    tile = jnp.minimum(lo // MOE_TM, n_tiles - 1).astype(jnp.int32)
    expert = jnp.minimum(jnp.sum(ends[None, :] <= lo[:, None], axis=1), N_EXPERTS - 1).astype(jnp.int32)
    chg = jnp.concatenate([jnp.ones((1,), jnp.int32), (expert[1:] != expert[:-1]).astype(jnp.int32)])
    return dest.astype(jnp.int32), (tile, expert, lo, hi, chg)


def _bucket_table(lq, lk):
    dist = np.arange(lq)[:, None] + WINDOW - np.arange(lk)[None, :]
    band = (dist >= 0) & (dist < WINDOW)
    d = np.clip(dist, 0, WINDOW - 1)
    max_exact = N_BUCKETS // 2
    d_f = np.maximum(d, 1).astype(np.float32)
    large = max_exact + (np.log(d_f / max_exact) / math.log(MAX_DISTANCE / max_exact)
                         * (N_BUCKETS - max_exact)).astype(np.int32)
    large = np.minimum(large, N_BUCKETS - 1)
    return np.where(d < max_exact, d, large).astype(np.int32), band


def _bias_table(rb, lq, lk):
    bkt, band = _bucket_table(lq, lk)
    onehot = jnp.asarray(bkt)[None, :, :] == jnp.arange(N_BUCKETS, dtype=jnp.int32)[:, None, None]
    bias = jnp.sum(jnp.where(onehot[:, None], rb[:, :, None, None], 0.0), axis=0)
    return jnp.where(jnp.asarray(band)[None], bias, NEG_INF)


def _decay_tables(C):
    log_gamma = jnp.log(1.0 - 2.0 ** (-5.0 - jnp.arange(RET_HEADS, dtype=_F32)))
    idx = jnp.arange(C, dtype=_F32)
    diff = idx[:, None] - idx[None, :]
    decay_in = jnp.where((diff >= 0)[..., None],
                         jnp.exp(jnp.maximum(diff, 0.0)[..., None] * log_gamma), 0.0)
    q_dec = jnp.exp((idx + 1.0)[:, None] * log_gamma)
    k_dec = jnp.exp((C - 1.0 - idx)[:, None] * log_gamma)
    c_dec = jnp.exp(C * log_gamma)
    dec = jnp.transpose(decay_in, (2, 0, 1))
    qd = jnp.repeat(q_dec, RET_DK, axis=1)
    kd = jnp.repeat(k_dec, RET_DK, axis=1)
    return dec, qd, kd, c_dec


def _rope_tables(pos):
    half = RET_DK // 2
    inv = ROPE_BASE ** (-jnp.arange(half, dtype=_F32) * 2.0 / RET_DK)
    ang = pos.astype(_F32)[:, None] * inv[None, :]
    cos = jnp.cos(ang)
    sin = jnp.sin(ang)
    return jnp.concatenate([cos, cos], axis=1), jnp.concatenate([-sin, sin], axis=1)


def _constants(rel_bias, attn_sink, L, ls):
    cst = {}
    rb = rel_bias.astype(_F32)
    cst['bias_p'] = jnp.transpose(_bias_table(rb, WINDOW, 2 * WINDOW), (0, 2, 1))
    cst['sink'] = attn_sink.astype(_F32)
    cst['cos_p'], cst['sin_p'] = _rope_tables(jnp.arange(L))
    cst['dec_p'], cst['qd_p'], cst['kd_p'], cst['cdec_p'] = _decay_tables(min(RET_CHUNK, L))
    bias = _bias_table(rb, ls, WINDOW + ls).reshape(KV_HEADS, GQA_GROUP * ls, WINDOW + ls)
    cst['bias_hist'] = bias[:, :, :WINDOW]
    cst['bias_new'] = bias[:, :, WINDOW:]
    cst['sink_col'] = jnp.repeat(attn_sink.astype(_F32).reshape(KV_HEADS, GQA_GROUP), ls,
                                 axis=1)[..., None]
    cst['cos_s'], cst['sin_s'] = _rope_tables(PAST_LEN + jnp.arange(ls))
    cst['dec_s'], cst['qd_s'], cst['kd_s'], cst['cdec_s'] = _decay_tables(min(RET_CHUNK, ls))
    return cst


def _layer_weights(layer, norm_mix, w_in, w_branch_attn, w_branch_ret, w_out, norm_ffn,
                   w_router_group, b_router_group, w_router_expert, b_router_expert):
    D = w_in.shape[1]
    wr = jnp.concatenate([w_router_group[layer].astype(_F32), w_router_expert[layer].astype(_F32)], axis=1)
    wr = jnp.pad(wr, ((0, 0), (0, ROUTE_LANES - wr.shape[1])))
    wrh = wr.astype(_BF16)
    wrl = jnp.concatenate([wrh, (wr - wrh.astype(_F32)).astype(_BF16)], axis=1)
    br = jnp.concatenate([b_router_group[layer].astype(_F32), b_router_expert[layer].astype(_F32)])
    br = jnp.pad(br, (0, ROUTE_LANES - br.shape[0]))[None, :]
    return {
        'nmix': norm_mix[layer].astype(_F32)[None, :],
        'win': w_in[layer].astype(_BF16),
        'wba': w_branch_attn[layer].astype(_BF16),
        'wbr': w_branch_ret[layer].astype(_BF16),
        'wout': w_out[layer].astype(_BF16),
        'nffn': norm_ffn[layer].astype(_F32)[None, :],
        'wrh': wrh, 'wrl': wrl, 'br': br,
    }


def kernel(x_prompt, x_sample, cache_k, cache_v, state_ret, norm_mix, w_in, attn_sink, rel_bias,
           w_branch_attn, w_branch_ret, w_out, norm_ffn, w_router_group, b_router_group,
           w_router_expert, b_router_expert, w_gate, w_up, w_down, norm_final):
    depth = w_in.shape[0]
    assert depth == 1, "the final norm is fused into the MoE combine of the only layer"
    B, L, D = x_prompt.shape
    NB, ls, _ = x_sample.shape
    Tp, Ts = B * L, NB * ls
    nfin = norm_final.astype(_F32)[None, :]
    yp, ys = x_prompt, x_sample
    pk, pv, ps, sk, sv, ss = [], [], [], [], [], []
    for layer in range(depth):
        w = _layer_weights(layer, norm_mix, w_in, w_branch_attn, w_branch_ret, w_out, norm_ffn,
                           w_router_group, b_router_group, w_router_expert, b_router_expert)
        cst = _constants(rel_bias, attn_sink[layer], L, ls)
        x1p, routep, k1, v1, s1, cnt_p = _prompt_mixer(yp, cst, w)
        ys2 = ys.reshape(Ts, D)
        proj = _sample_inproj(ys2, w)
        attn_s, ret_s, k2, v2, s2 = _sample_core(
            proj, ls,
            cache_k[layer].reshape(NB, WINDOW, KV_W), cache_v[layer].reshape(NB, WINDOW, KV_W),
            state_ret[layer], cst)
        x1s, routes, cnt_all = _sample_post(
            ys2, attn_s, ret_s,
            proj[:, OFF_GA:OFF_GA + D], proj[:, OFF_GT:OFF_GT + D], w, cnt_p)
        n_rows = 2 * (Tp + Ts)
        assert n_rows % MOE_TM == 0
        counts = cnt_all[N_GROUPS:N_GROUPS + N_EXPERTS, 0].astype(jnp.int32)
        dest, work = _routing_tables(jnp.concatenate([routep, routes], axis=1), counts,
                                     n_rows // MOE_TM)
        xs3 = _dispatch(dest, x1p.reshape(Tp, SUB, LANE), x1s.reshape(Ts, SUB, LANE))
        yb2 = _gmm(work, xs3.reshape(n_rows * SUB, LANE), w['nffn'],
                   w_gate[layer], w_up[layer], w_down[layer])
        yb3 = yb2.reshape(n_rows, SUB, LANE)
        yp = _combine(dest, x1p, routep, nfin, yb3, 0).reshape(B, L, D)
        ys = _combine(dest, x1s, routes, nfin, yb3, Tp).reshape(NB, ls, D)
        pk.append(k1.reshape(B, WINDOW, KV_HEADS, HEAD_DIM))
        pv.append(v1.reshape(B, WINDOW, KV_HEADS, HEAD_DIM))
        ps.append(s1)
        sk.append(k2.reshape(NB, WINDOW, KV_HEADS, HEAD_DIM))
        sv.append(v2.reshape(NB, WINDOW, KV_HEADS, HEAD_DIM))
        ss.append(s2)
    return (yp, ys, jnp.stack(pk), jnp.stack(pv), jnp.stack(ps),
            jnp.stack(sk), jnp.stack(sv), jnp.stack(ss))
```

```python
import functools
import math

import jax
import jax.numpy as jnp
import numpy as np
from jax import lax
from jax.experimental import pallas as pl
from jax.experimental.pallas import tpu as pltpu

D_MODEL = 1024
HEAD_DIM = 64
KV_HEADS = 4
GQA_GROUP = 4
ATTN_HEADS = KV_HEADS * GQA_GROUP
WINDOW = 128
N_BUCKETS = 32
MAX_DISTANCE = 128
RET_HEADS = 4
RET_DK = 128
RET_DV = 256
RET_CHUNK = 128
ROPE_BASE = 10000.0
N_GROUPS = 4
EXPERTS_PER_GROUP = 8
N_EXPERTS = N_GROUPS * EXPERTS_PER_GROUP
EXPERT_FF = 512
NORM_EPS = 1e-6
NEG_INF = -1e30
PAST_LEN = 16384

ATTN_W = ATTN_HEADS * HEAD_DIM
KV_W = KV_HEADS * HEAD_DIM
RQ_W = RET_HEADS * RET_DK
RV_W = RET_HEADS * RET_DV
OFF_QA = 0
OFF_KA = OFF_QA + ATTN_W
OFF_VA = OFF_KA + KV_W
OFF_QR = OFF_VA + KV_W
OFF_KR = OFF_QR + RQ_W
OFF_VR = OFF_KR + RQ_W
OFF_GR = OFF_VR + RV_W
OFF_GA = OFF_GR + RV_W
OFF_GT = OFF_GA + D_MODEL
ROUTE_LANES = 128
ROUTE_ROWS = 40
ROUTE_OUT = 8

LANE = 128
SUB = 8
PROMPT_TM = 512
PROJ_PANEL = 256
POST_PARTS = 2
SAMPLE_GROUP = 8
SAMPLE_INPROJ_STEPS = 4
SAMPLE_ROWS = 16
MOE_TM = 1024
GMM_SUBTILE = 256
GMM_PARTS = 2
ROW_TM = 512
COMBINE_TM = 1024
COMBINE_CHUNKS = 2
DMA_UNROLL = 8
VMEM_LIMIT = 60 * 1024 * 1024

_F32 = jnp.float32
_BF16 = jnp.bfloat16


def _const_spec(shape):
    nd = len(shape)
    return pl.BlockSpec(shape, lambda *_: (0,) * nd, pipeline_mode=pl.Buffered(1))


def _rms(x, gain):
    return x * lax.rsqrt(jnp.mean(x * x, axis=-1, keepdims=True) + NORM_EPS) * gain


def _dot(a, b):
    return jnp.dot(a, b, preferred_element_type=_F32)


def _dot_nt(a, b):
    return lax.dot_general(a, b, (((1,), (1,)), ((), ())), preferred_element_type=_F32)


def _dot_tn(a, b):
    return lax.dot_general(a, b, (((0,), (0,)), ((), ())), preferred_element_type=_F32)


def _load_rows(ref, n):
    return jnp.concatenate([ref[pl.ds(s, n, stride=SUB), :] for s in range(SUB)], axis=1)


def _store_rows(ref, val):
    n = val.shape[0]
    for s in range(SUB):
        ref[pl.ds(s, n, stride=SUB), :] = val[:, s * LANE:(s + 1) * LANE]


def _rotary(x, cosf, sinf):
    return x * cosf + pltpu.roll(x, RET_DK // 2, 1) * sinf


def _post(xs, attn_projs, rets, gates_a, gates_r, wbr, wout, nffn, wrh, wrl, br, cnt):
    parts = range(len(xs))
    ret_projs = [_dot(rets[j].astype(_BF16), wbr) for j in parts]
    x1s = []
    for j in parts:
        merged = jax.nn.sigmoid(gates_a[j]) * attn_projs[j] + jax.nn.sigmoid(gates_r[j]) * ret_projs[j]
        x1s.append(xs[j] + _dot(merged.astype(_BF16), wout))
    logit_parts = []
    for j in parts:
        xn2 = _rms(x1s[j], nffn)
        hi = xn2.astype(_BF16)
        lo = (xn2 - hi.astype(_F32)).astype(_BF16)
        t = _dot(hi, wrl)
        logit_parts.append(t[:, 0:ROUTE_LANES] + (t[:, ROUTE_LANES:] + _dot(lo, wrh)) + br)
    x1 = x1s[0] if len(x1s) == 1 else jnp.concatenate(x1s, axis=0)
    logits = logit_parts[0] if len(x1s) == 1 else jnp.concatenate(logit_parts, axis=0)
    n = logits.shape[0]
    lt = logits.T[0:ROUTE_ROWS, :]
    row = lax.broadcasted_iota(jnp.int32, (ROUTE_ROWS, n), 0)
    big = jnp.int32(1 << 20)
    neg = jnp.float32(-jnp.inf)
    gl = jnp.where(row < N_GROUPS, lt, neg)
    gmax = jnp.max(gl, axis=0, keepdims=True)
    gexp = jnp.exp(gl - gmax)
    gsum = jnp.sum(gexp, axis=0, keepdims=True)
    pg = gexp / gsum
    g_w = jnp.max(pg, axis=0, keepdims=True)
    g_idx = jnp.min(jnp.where(pg == g_w, row, big), axis=0, keepdims=True)
    e_row = row - N_GROUPS
    emask = (e_row >= 0) & (e_row < N_EXPERTS) & ((e_row >> 3) == g_idx)
    fl = jnp.where(emask, lt, neg)
    fmax = jnp.max(fl, axis=0, keepdims=True)
    fexp = jnp.exp(fl - fmax)
    fsum = jnp.sum(fexp, axis=0, keepdims=True)
    pe = jnp.where(emask, fexp / fsum, -1.0)
    p1 = jnp.max(pe, axis=0, keepdims=True)
    i1 = jnp.min(jnp.where(pe == p1, row, big), axis=0, keepdims=True)
    pe2 = jnp.where(row == i1, -1.0, pe)
    p2 = jnp.max(pe2, axis=0, keepdims=True)
    i2 = jnp.min(jnp.where(pe2 == p2, row, big), axis=0, keepdims=True)
    psum = p1 + p2
    gate1 = g_w * p1 / psum
    gate2 = g_w * p2 / psum
    oh1 = row == i1
    oh2 = row == i2
    c = jnp.where(oh1 | oh2, 1.0, 0.0)
    tt = lax.broadcasted_iota(jnp.int32, (n, n), 0)
    tc = lax.broadcasted_iota(jnp.int32, (n, n), 1)
    upper = jnp.where(tt < tc, 1.0, 0.0).astype(_BF16)
    before = _dot(c.astype(_BF16), upper) + cnt
    rank1 = jnp.sum(jnp.where(oh1, before, 0.0), axis=0, keepdims=True)
    rank2 = jnp.sum(jnp.where(oh2, before, 0.0), axis=0, keepdims=True)
    cnt = cnt + jnp.sum(c, axis=1, keepdims=True)
    r8 = lax.broadcasted_iota(jnp.int32, (ROUTE_OUT, n), 0)
    vals = [(i1 - N_GROUPS).astype(_F32), (i2 - N_GROUPS).astype(_F32), gate1, gate2, rank1, rank2]
    route = jnp.zeros((ROUTE_OUT, n), _F32)
    for k, v in enumerate(vals):
        route = jnp.where(r8 == k, v, route)
    return x1, route, cnt


def _prompt_mixer_kernel(sink_ref, cdec_ref,
                         x_ref, nmix_ref, win_ref, bias_ref, cos_ref, sin_ref, dec_ref,
                         qd_ref, kd_ref, wba_ref, wbr_ref, wout_ref, nffn_ref,
                         wrh_ref, wrl_ref, br_ref,
                         x1_ref, route_ref, knew_ref, vnew_ref, s_ref, cnt_ref,
                         qkv, proj, kctx, vctx, attn_t, ret):
    i = pl.program_id(1)
    last = pl.num_programs(1) - 1

    @pl.when((i == 0) & (pl.program_id(0) == 0))
    def _():
        cnt_ref[...] = jnp.zeros_like(cnt_ref)

    tm = x_ref.shape[1]
    n_sub = tm // WINDOW
    scale = HEAD_DIM ** -0.5

    @pl.when(i == 0)
    def _():
        s_ref[...] = jnp.zeros_like(s_ref)
        kctx[0:WINDOW, :] = jnp.zeros((WINDOW, KV_W), _BF16)
        vctx[0:WINDOW, :] = jnp.zeros((WINDOW, KV_W), _BF16)

    x = x_ref[0]
    xn = _rms(x, nmix_ref[...]).astype(_BF16)
    n_in = win_ref.shape[1]
    panel = PROJ_PANEL

    def project(c0):
        res = _dot(xn, win_ref[:, c0:c0 + panel])
        if c0 < OFF_QR:
            qkv[:, c0:c0 + panel] = res
        else:
            proj[:, c0 - OFF_QR:c0 - OFF_QR + panel] = res

    def cols(lo, width):
        return slice(lo - OFF_QR, lo - OFF_QR + width)

    for c0 in range(0, OFF_QR, panel):
        project(c0)
    later_panels = list(range(OFF_QR, n_in, panel))

    @pl.when(i == last)
    def _():
        knew_ref[0] = qkv[tm - WINDOW:tm, OFF_KA:OFF_KA + KV_W]
        vnew_ref[0] = qkv[tm - WINDOW:tm, OFF_VA:OFF_VA + KV_W]

    krow = lax.broadcasted_iota(jnp.int32, (2 * WINDOW, 1), 0)
    for c in range(n_sub):
        r0 = c * WINDOW
        kctx[WINDOW:2 * WINDOW, :] = qkv[r0:r0 + WINDOW, OFF_KA:OFF_KA + KV_W].astype(_BF16)
        vctx[WINDOW:2 * WINDOW, :] = qkv[r0:r0 + WINDOW, OFF_VA:OFF_VA + KV_W].astype(_BF16)
        if c == 0:
            pen = jnp.where((krow < WINDOW) & (i == 0), NEG_INF, 0.0).astype(_F32)
        for h in range(KV_HEADS):
            k_h = kctx[:, h * HEAD_DIM:(h + 1) * HEAD_DIM]
            v_h = vctx[:, h * HEAD_DIM:(h + 1) * HEAD_DIM]
            probs = []
            for g in range(GQA_GROUP):
                hq = h * GQA_GROUP + g
                q = (qkv[r0:r0 + WINDOW, hq * HEAD_DIM:(hq + 1) * HEAD_DIM] * scale).astype(_BF16)
                s = _dot_nt(k_h, q) + bias_ref[hq]
                if c == 0:
                    s = s + pen
                snk = sink_ref[hq]
                m = jnp.maximum(jnp.max(s, axis=0, keepdims=True), snk)
                p = jnp.exp(s - m)
                den = jnp.sum(p, axis=0, keepdims=True) + jnp.exp(snk - m)
                probs.append((p * (1.0 / den)).astype(_BF16))
            groups_left = (n_sub - c) * KV_HEADS - h
            for _ in range(-(-len(later_panels) // groups_left)):
                project(later_panels.pop(0))
            o_t = _dot_tn(v_h, jnp.concatenate(probs, axis=1))
            for g in range(GQA_GROUP):
                hq = h * GQA_GROUP + g
                attn_t[hq * HEAD_DIM:(hq + 1) * HEAD_DIM, r0:r0 + WINDOW] = (
                    o_t[:, g * WINDOW:(g + 1) * WINDOW].astype(_BF16))
        kctx[0:WINDOW, :] = kctx[WINDOW:2 * WINDOW, :]
        vctx[0:WINDOW, :] = vctx[WINDOW:2 * WINDOW, :]

    for c0 in later_panels:
        project(c0)

    for c in range(n_sub):
        r0 = c * RET_CHUNK
        cosf = cos_ref[r0:r0 + RET_CHUNK, :]
        sinf = sin_ref[r0:r0 + RET_CHUNK, :]
        for h in range(RET_HEADS):
            qc = _rotary(proj[r0:r0 + RET_CHUNK, cols(OFF_QR + h * RET_DK, RET_DK)], cosf, sinf)
            kc = _rotary(proj[r0:r0 + RET_CHUNK, cols(OFF_KR + h * RET_DK, RET_DK)], cosf, sinf) * (RET_DK ** -0.5)
            qd = qd_ref[:, h * RET_DK:(h + 1) * RET_DK]
            kd = kd_ref[:, h * RET_DK:(h + 1) * RET_DK]
            vc = proj[r0:r0 + RET_CHUNK, cols(OFF_VR + h * RET_DV, RET_DV)].astype(_BF16)
            sc = _dot_nt(qc.astype(_BF16), kc.astype(_BF16))
            s_old = s_ref[0, h]
            cross = _dot((qc * qd).astype(_BF16), s_old.astype(_BF16))
            s_ref[0, h] = s_old * cdec_ref[h] + _dot_tn((kc * kd).astype(_BF16), vc)
            o = _dot((sc * dec_ref[h]).astype(_BF16), vc) + cross
            o = o * lax.rsqrt(jnp.mean(o * o, axis=-1, keepdims=True) + NORM_EPS)
            gr = proj[r0:r0 + RET_CHUNK, cols(OFF_GR + h * RET_DV, RET_DV)]
            ret[r0:r0 + RET_CHUNK, h * RET_DV:(h + 1) * RET_DV] = o * (gr * jax.nn.sigmoid(gr))

    pn = tm // POST_PARTS
    rows = [slice(j * pn, (j + 1) * pn) for j in range(POST_PARTS)]
    x1, route, cnt = _post([x[r] for r in rows],
                           [_dot_tn(attn_t[:, r], wba_ref[...]) for r in rows],
                           [ret[r, :] for r in rows],
                           [proj[r, cols(OFF_GA, D_MODEL)] for r in rows],
                           [proj[r, cols(OFF_GT, D_MODEL)] for r in rows],
                           wbr_ref[...], wout_ref[...], nffn_ref[...],
                           wrh_ref[...], wrl_ref[...], br_ref[...], cnt_ref[:, 0:1])
    _store_rows(x1_ref, x1)
    route_ref[...] = route
    cnt_ref[...] = jnp.broadcast_to(cnt, cnt_ref.shape)


def _prompt_mixer(x, cst, w):
    B, L, D = x.shape
    tm = min(PROMPT_TM, L)
    nb = L // tm
    n_in = w['win'].shape[1]
    step = lambda b, i, *_: (b, i, 0)
    per_b = lambda b, i, *_: (b, 0, 0)

    grid_spec = pltpu.PrefetchScalarGridSpec(
        num_scalar_prefetch=2,
        grid=(B, nb),
        in_specs=[
            pl.BlockSpec((1, tm, D), step),
            _const_spec((1, D)),
            _const_spec((D, n_in)),
            _const_spec((ATTN_HEADS, 2 * WINDOW, WINDOW)),
            pl.BlockSpec((tm, RET_DK), lambda b, i, *_: (i, 0)),
            pl.BlockSpec((tm, RET_DK), lambda b, i, *_: (i, 0)),
            _const_spec((RET_HEADS, RET_CHUNK, RET_CHUNK)),
            _const_spec((RET_CHUNK, RQ_W)),
            _const_spec((RET_CHUNK, RQ_W)),
            _const_spec((ATTN_W, D)),
            _const_spec((RV_W, D)),
            _const_spec((D, D)),
            _const_spec((1, D)),
            _const_spec((D, ROUTE_LANES)),
            _const_spec((D, 2 * ROUTE_LANES)),
            _const_spec((1, ROUTE_LANES)),
        ],
        out_specs=[
            pl.BlockSpec((tm * SUB, LANE), lambda b, i, *_: (b * nb + i, 0)),
            pl.BlockSpec((ROUTE_OUT, tm), lambda b, i, *_: (0, b * nb + i)),
            pl.BlockSpec((1, WINDOW, KV_W), per_b),
            pl.BlockSpec((1, WINDOW, KV_W), per_b),
            pl.BlockSpec((1, RET_HEADS, RET_DK, RET_DV), lambda b, i, *_: (b, 0, 0, 0)),
            pl.BlockSpec((ROUTE_ROWS, LANE), lambda b, i, *_: (0, 0)),
        ],
        scratch_shapes=[
            pltpu.VMEM((tm, OFF_QR), _F32),
            pltpu.VMEM((tm, n_in - OFF_QR), _F32),
            pltpu.VMEM((2 * WINDOW, KV_W), _BF16),
            pltpu.VMEM((2 * WINDOW, KV_W), _BF16),
            pltpu.VMEM((ATTN_W, tm), _BF16),
            pltpu.VMEM((tm, RV_W), _F32),
        ],
    )
    assert D == SUB * LANE
    out_shape = [
        jax.ShapeDtypeStruct((B * L * SUB, LANE), _F32),
        jax.ShapeDtypeStruct((ROUTE_OUT, B * L), _F32),
        jax.ShapeDtypeStruct((B, WINDOW, KV_W), _F32),
        jax.ShapeDtypeStruct((B, WINDOW, KV_W), _F32),
        jax.ShapeDtypeStruct((B, RET_HEADS, RET_DK, RET_DV), _F32),
        jax.ShapeDtypeStruct((ROUTE_ROWS, LANE), _F32),
    ]
    return pl.pallas_call(
        _prompt_mixer_kernel,
        grid_spec=grid_spec,
        out_shape=out_shape,
        compiler_params=pltpu.CompilerParams(
            dimension_semantics=("arbitrary", "arbitrary"), vmem_limit_bytes=VMEM_LIMIT),
        name="prompt_mixer",
    )(cst['sink'], cst['cdec_p'],
      x, w['nmix'], w['win'], cst['bias_p'], cst['cos_p'], cst['sin_p'], cst['dec_p'],
      cst['qd_p'], cst['kd_p'], w['wba'], w['wbr'], w['wout'], w['nffn'],
      w['wrh'], w['wrl'], w['br'])


def _inproj_kernel(x_ref, nmix_ref, win_ref, o_ref):
    xn = _rms(x_ref[...], nmix_ref[...]).astype(_BF16)
    o_ref[...] = _dot(xn, win_ref[...])


def _sample_inproj(x2d, w):
    T, D = x2d.shape
    n_in = w['win'].shape[1]
    panel = n_in // SAMPLE_INPROJ_STEPS
    assert panel % LANE == 0
    return pl.pallas_call(
        _inproj_kernel,
        grid=(n_in // panel,),
        in_specs=[pl.BlockSpec((T, D), lambda j: (0, 0)),
                  pl.BlockSpec((1, D), lambda j: (0, 0)),
                  pl.BlockSpec((D, panel), lambda j: (0, j))],
        out_specs=pl.BlockSpec((T, panel), lambda j: (0, j)),
        out_shape=jax.ShapeDtypeStruct((T, n_in), _F32),
        compiler_params=pltpu.CompilerParams(
            dimension_semantics=("arbitrary",), vmem_limit_bytes=VMEM_LIMIT),
        name="sample_inproj",
    )(x2d, w['nmix'], w['win'])


def _sample_core_kernel(cdec_ref, proj_ref, ck_ref, cv_ref, st_ref, bh_ref, bn_ref, snk_ref,
                        cos_ref, sin_ref, dec_ref, qd_ref, kd_ref,
                        attn_ref, ret_ref, nk_ref, nv_ref, ns_ref):
    G = ck_ref.shape[0]
    ls = proj_ref.shape[0] // G
    per = SAMPLE_ROWS // ls
    scale = HEAD_DIM ** -0.5
    cosf = cos_ref[...]
    sinf = sin_ref[...]
    heads = [slice(h * HEAD_DIM, (h + 1) * HEAD_DIM) for h in range(KV_HEADS)]

    def body(j, carry):
        group = pl.ds(pl.multiple_of(j * SAMPLE_ROWS, SAMPLE_ROWS), SAMPLE_ROWS)
        rows = proj_ref[group, :]
        seqs = []
        for s in range(per):
            b = j * per + s
            row = rows[s * ls:(s + 1) * ls]
            k_new = row[:, OFF_KA:OFF_KA + KV_W]
            v_new = row[:, OFF_VA:OFF_VA + KV_W]
            ck = ck_ref[b]
            cv = cv_ref[b]
            nk_ref[b, 0:WINDOW - ls, :] = ck[ls:WINDOW, :]
            nk_ref[b, WINDOW - ls:WINDOW, :] = k_new
            nv_ref[b, 0:WINDOW - ls, :] = cv[ls:WINDOW, :]
            nv_ref[b, WINDOW - ls:WINDOW, :] = v_new
            seqs.append(dict(b=b, row=row, ckb=ck.astype(_BF16), cvb=cv.astype(_BF16),
                             knb=k_new.astype(_BF16), vnb=v_new.astype(_BF16)))
        for q in seqs:
            row = q['row']
            q['s1'], q['s2'] = [], []
            for h in range(KV_HEADS):
                q4 = (jnp.concatenate(
                    [row[:, (h * GQA_GROUP + g) * HEAD_DIM:(h * GQA_GROUP + g + 1) * HEAD_DIM]
                     for g in range(GQA_GROUP)], axis=0) * scale).astype(_BF16)
                q['s1'].append(_dot_nt(q4, q['ckb'][:, heads[h]]))
                q['s2'].append(_dot_nt(q4, q['knb'][:, heads[h]]))
        for q in seqs:
            row, b = q['row'], q['b']
            q['sc'], q['cross'], q['vc'] = [], [], []
            for h in range(RET_HEADS):
                qrot = _rotary(row[:, OFF_QR + h * RET_DK:OFF_QR + (h + 1) * RET_DK], cosf, sinf)
                krot = _rotary(row[:, OFF_KR + h * RET_DK:OFF_KR + (h + 1) * RET_DK], cosf, sinf) * (RET_DK ** -0.5)
                vc = row[:, OFF_VR + h * RET_DV:OFF_VR + (h + 1) * RET_DV].astype(_BF16)
                qd = qd_ref[:, h * RET_DK:(h + 1) * RET_DK]
                kd = kd_ref[:, h * RET_DK:(h + 1) * RET_DK]
                s_old = st_ref[b, h]
                q['sc'].append(_dot_nt(qrot.astype(_BF16), krot.astype(_BF16)))
                q['cross'].append(_dot((qrot * qd).astype(_BF16), s_old.astype(_BF16)))
                ns_ref[b, h] = s_old * cdec_ref[h] + _dot_tn((krot * kd).astype(_BF16), vc)
                q['vc'].append(vc)
        for q in seqs:
            q['p1'], q['p2'] = [], []
            for h in range(KV_HEADS):
                s1 = q['s1'][h] + bh_ref[h]
                s2 = q['s2'][h] + bn_ref[h]
                snk = snk_ref[h]
                m = jnp.maximum(jnp.maximum(jnp.max(s1, axis=-1, keepdims=True),
                                            jnp.max(s2, axis=-1, keepdims=True)), snk)
                p1 = jnp.exp(s1 - m)
                p2 = jnp.exp(s2 - m)
                den = (jnp.sum(p1, axis=-1, keepdims=True) + jnp.sum(p2, axis=-1, keepdims=True)
                       + jnp.exp(snk - m))
                r = 1.0 / den
                q['p1'].append((p1 * r).astype(_BF16))
                q['p2'].append((p2 * r).astype(_BF16))
            q['scb'] = [(q['sc'][h] * dec_ref[h]).astype(_BF16) for h in range(RET_HEADS)]
        for q in seqs:
            q['out'] = [_dot(q['p1'][h], q['cvb'][:, heads[h]]) + _dot(q['p2'][h], q['vnb'][:, heads[h]])
                        for h in range(KV_HEADS)]
            q['ret'] = [_dot(q['scb'][h], q['vc'][h]) + q['cross'][h] for h in range(RET_HEADS)]
        for h in range(KV_HEADS):
            for g in range(GQA_GROUP):
                hq = h * GQA_GROUP + g
                attn_ref[group, hq * HEAD_DIM:(hq + 1) * HEAD_DIM] = jnp.concatenate(
                    [q['out'][h][g * ls:(g + 1) * ls] for q in seqs], axis=0)
        for h in range(RET_HEADS):
            o = jnp.concatenate([q['ret'][h] for q in seqs], axis=0)
            o = o * lax.rsqrt(jnp.mean(o * o, axis=-1, keepdims=True) + NORM_EPS)
            gr = rows[:, OFF_GR + h * RET_DV:OFF_GR + (h + 1) * RET_DV]
            ret_ref[group, h * RET_DV:(h + 1) * RET_DV] = o * (gr * jax.nn.sigmoid(gr))
        return carry

    lax.fori_loop(0, G // per, body, 0)


def _sample_core(proj2, ls, ck, cv, st, cst):
    n_in = proj2.shape[1]
    NB = proj2.shape[0] // ls
    G = min(SAMPLE_GROUP, NB)
    assert SAMPLE_ROWS % ls == 0 and G % (SAMPLE_ROWS // ls) == 0 and NB % G == 0
    row_blk = lambda i, *_: (i, 0)
    blk3 = lambda i, *_: (i, 0, 0)
    blk4 = lambda i, *_: (i, 0, 0, 0)
    c2 = lambda i, *_: (0, 0)
    c3 = lambda i, *_: (0, 0, 0)
    ql = GQA_GROUP * ls
    grid_spec = pltpu.PrefetchScalarGridSpec(
        num_scalar_prefetch=1,
        grid=(NB // G,),
        in_specs=[
            pl.BlockSpec((G * ls, n_in), row_blk),
            pl.BlockSpec((G, WINDOW, KV_W), blk3),
            pl.BlockSpec((G, WINDOW, KV_W), blk3),
            pl.BlockSpec((G, RET_HEADS, RET_DK, RET_DV), blk4),
            pl.BlockSpec((KV_HEADS, ql, WINDOW), c3),
            pl.BlockSpec((KV_HEADS, ql, ls), c3),
            pl.BlockSpec((KV_HEADS, ql, 1), c3),
            pl.BlockSpec((ls, RET_DK), c2),
            pl.BlockSpec((ls, RET_DK), c2),
            pl.BlockSpec((RET_HEADS, ls, ls), c3),
            pl.BlockSpec((ls, RQ_W), c2),
            pl.BlockSpec((ls, RQ_W), c2),
        ],
        out_specs=[
            pl.BlockSpec((G * ls, ATTN_W), row_blk),
            pl.BlockSpec((G * ls, RV_W), row_blk),
            pl.BlockSpec((G, WINDOW, KV_W), blk3),
            pl.BlockSpec((G, WINDOW, KV_W), blk3),
            pl.BlockSpec((G, RET_HEADS, RET_DK, RET_DV), blk4),
        ],
    )
    out_shape = [
        jax.ShapeDtypeStruct((NB * ls, ATTN_W), _F32),
        jax.ShapeDtypeStruct((NB * ls, RV_W), _F32),
        jax.ShapeDtypeStruct((NB, WINDOW, KV_W), _F32),
        jax.ShapeDtypeStruct((NB, WINDOW, KV_W), _F32),
        jax.ShapeDtypeStruct((NB, RET_HEADS, RET_DK, RET_DV), _F32),
    ]
    return pl.pallas_call(
        _sample_core_kernel,
        grid_spec=grid_spec,
        out_shape=out_shape,
        compiler_params=pltpu.CompilerParams(
            dimension_semantics=("arbitrary",), vmem_limit_bytes=VMEM_LIMIT),
        name="sample_core",
    )(cst['cdec_s'], proj2, ck, cv, st, cst['bias_hist'], cst['bias_new'], cst['sink_col'],
      cst['cos_s'], cst['sin_s'], cst['dec_s'], cst['qd_s'], cst['kd_s'])


def _sample_post_kernel(x_ref, attn_ref, ret_ref, ga_ref, gt_ref, wba_ref, wbr_ref, wout_ref,
                        nffn_ref, wrh_ref, wrl_ref, br_ref, cnt0_ref, x1_ref, route_ref, cnt_ref):
    x1, route, cnt = _post([x_ref[...]], [_dot(attn_ref[...].astype(_BF16), wba_ref[...])],
                           [ret_ref[...]], [ga_ref[...]], [gt_ref[...]],
                           wbr_ref[...], wout_ref[...], nffn_ref[...],
                           wrh_ref[...], wrl_ref[...], br_ref[...], cnt0_ref[:, 0:1])
    _store_rows(x1_ref, x1)
    route_ref[...] = route
    cnt_ref[...] = jnp.broadcast_to(cnt, cnt_ref.shape)


def _sample_post(x2d, attn, ret, ga, gt, w, cnt0):
    T, D = x2d.shape
    full = lambda s: pl.BlockSpec(s, lambda i: (0,) * len(s))
    return pl.pallas_call(
        _sample_post_kernel,
        grid=(1,),
        in_specs=[full((T, D)), full((T, ATTN_W)), full((T, RV_W)), full((T, D)), full((T, D)),
                  full((ATTN_W, D)), full((RV_W, D)), full((D, D)), full((1, D)),
                  full((D, ROUTE_LANES)), full((D, 2 * ROUTE_LANES)), full((1, ROUTE_LANES)),
                  full((ROUTE_ROWS, LANE))],
        out_specs=[full((T * SUB, LANE)), full((ROUTE_OUT, T)), full((ROUTE_ROWS, LANE))],
        out_shape=[jax.ShapeDtypeStruct((T * SUB, LANE), _F32),
                   jax.ShapeDtypeStruct((ROUTE_OUT, T), _F32),
                   jax.ShapeDtypeStruct((ROUTE_ROWS, LANE), _F32)],
        compiler_params=pltpu.CompilerParams(
            dimension_semantics=("arbitrary",), vmem_limit_bytes=VMEM_LIMIT),
        name="sample_post",
    )(x2d, attn, ret, ga, gt, w['wba'], w['wbr'], w['wout'], w['nffn'], w['wrh'], w['wrl'], w['br'],
      cnt0)


def _dispatch_kernel(dest_ref, xp_ref, xq_ref, xs_ref, sem, *, p_steps):
    i = pl.program_id(0)
    n_tok = pl.num_programs(0) * ROW_TM

    def copy_tile(src):
        def tile_copy(r, d):
            return pltpu.make_async_copy(src.at[r], xs_ref.at[d], sem)

        def start(r, c):
            t = i * ROW_TM + r
            tile_copy(r, dest_ref[t]).start(priority=0)
            tile_copy(r, dest_ref[n_tok + t]).start(priority=1)
            return c

        lax.fori_loop(0, ROW_TM, start, 0, unroll=DMA_UNROLL)

        def wait(r, c):
            tile_copy(0, 0).wait()
            tile_copy(0, 0).wait()
            return c

        lax.fori_loop(0, ROW_TM, wait, 0, unroll=DMA_UNROLL)

    @pl.when(i < p_steps)
    def _():
        copy_tile(xp_ref)

    @pl.when(i >= p_steps)
    def _():
        copy_tile(xq_ref)


def _dispatch(dest, xp3, xq3):
    Tp, Tq = xp3.shape[0], xq3.shape[0]
    assert Tp % ROW_TM == 0 and Tq % ROW_TM == 0
    p_steps = Tp // ROW_TM
    grid_spec = pltpu.PrefetchScalarGridSpec(
        num_scalar_prefetch=1,
        grid=((Tp + Tq) // ROW_TM,),
        in_specs=[
            pl.BlockSpec((ROW_TM, SUB, LANE), lambda i, *_: (jnp.minimum(i, p_steps - 1), 0, 0)),
            pl.BlockSpec((ROW_TM, SUB, LANE), lambda i, *_: (jnp.maximum(i - p_steps, 0), 0, 0)),
        ],
        out_specs=pl.BlockSpec(memory_space=pl.ANY),
        scratch_shapes=[pltpu.SemaphoreType.DMA],
    )
    return pl.pallas_call(
        functools.partial(_dispatch_kernel, p_steps=p_steps),
        grid_spec=grid_spec,
        out_shape=jax.ShapeDtypeStruct((2 * (Tp + Tq), SUB, LANE), _F32),
        compiler_params=pltpu.CompilerParams(dimension_semantics=("arbitrary",)),
        name="moe_dispatch",
    )(dest, xp3, xq3)


def _gmm_kernel(tile_ref, exp_ref, lo_ref, hi_ref, chg_ref,
                x_ref, nffn_ref, wg_ref, wu_ref, wd_ref, y_ref, wg_s, wu_s, wd_s):
    m = pl.program_id(0)
    tm = x_ref.shape[0] // SUB

    @pl.when(chg_ref[m] == 1)
    def _():
        wg_s[...] = wg_ref[0].astype(_BF16)
        wu_s[...] = wu_ref[0].astype(_BF16)
        wd_s[...] = wd_ref[0].astype(_BF16)

    lo = lo_ref[m]
    hi = hi_ref[m]
    hn = GMM_SUBTILE // GMM_PARTS

    def subtile(row0):
        base = tile_ref[m] * tm + row0
        parts = [pl.ds((row0 + j * hn) * SUB, hn * SUB) for j in range(GMM_PARTS)]
        xs = []
        for j in range(GMM_PARTS):
            rows = base + j * hn + lax.broadcasted_iota(jnp.int32, (hn, 1), 0)
            mine = (rows >= lo) & (rows < hi)
            xn = _rms(_load_rows(x_ref.at[parts[j]], hn), nffn_ref[...])
            xs.append(jnp.where(mine, xn, 0.0).astype(_BF16))
        gate_up = [(_dot(x, wg_s[...]), _dot(x, wu_s[...])) for x in xs]
        ys = [_dot(((a * jax.nn.sigmoid(a)) * u).astype(_BF16), wd_s[...]) for a, u in gate_up]
        first = lo <= base

        @pl.when(first)
        def _():
            for j in range(GMM_PARTS):
                _store_rows(y_ref.at[parts[j]], ys[j])

        @pl.when(jnp.logical_not(first))
        def _():
            for j in range(GMM_PARTS):
                _store_rows(y_ref.at[parts[j]], _load_rows(y_ref.at[parts[j]], hn) + ys[j])

    tile_base = tile_ref[m] * tm
    whole = (lo <= tile_base) & (hi >= tile_base + tm)

    @pl.when(whole)
    def _():
        n_parts = tm // hn
        parts = [pl.ds(j * hn * SUB, hn * SUB) for j in range(n_parts)]
        def gate_up(j):
            x = _rms(_load_rows(x_ref.at[parts[j]], hn), nffn_ref[...]).astype(_BF16)
            return _dot(x, wg_s[...]), _dot(x, wu_s[...])

        nxt = gate_up(0)
        for j in range(n_parts):
            a, u = nxt
            if j + 1 < n_parts:
                nxt = gate_up(j + 1)
            _store_rows(y_ref.at[parts[j]],
                        _dot(((a * jax.nn.sigmoid(a)) * u).astype(_BF16), wd_s[...]))

    for row0 in range(0, tm, GMM_SUBTILE):
        base = tile_base + row0
        pl.when(jnp.logical_not(whole) & (hi > base) & (lo < base + GMM_SUBTILE) & (hi > lo))(
            functools.partial(subtile, row0))


def _gmm(work, xs2, nffn, wg, wu, wd):
    A = xs2.shape[0] // SUB
    E, D, F = wg.shape
    n_work = work[0].shape[0]
    grid_spec = pltpu.PrefetchScalarGridSpec(
        num_scalar_prefetch=5,
        grid=(n_work,),
        in_specs=[
            pl.BlockSpec((MOE_TM * SUB, LANE), lambda m, t, e, *_: (t[m], 0)),
            pl.BlockSpec((1, D), lambda m, t, e, *_: (0, 0)),
            pl.BlockSpec((1, D, F), lambda m, t, e, *_: (e[m], 0, 0)),
            pl.BlockSpec((1, D, F), lambda m, t, e, *_: (e[m], 0, 0)),
            pl.BlockSpec((1, F, D), lambda m, t, e, *_: (e[m], 0, 0)),
        ],
        out_specs=pl.BlockSpec((MOE_TM * SUB, LANE), lambda m, t, e, *_: (t[m], 0)),
        scratch_shapes=[pltpu.VMEM((D, F), _BF16), pltpu.VMEM((D, F), _BF16),
                        pltpu.VMEM((F, D), _BF16)],
    )
    return pl.pallas_call(
        _gmm_kernel,
        grid_spec=grid_spec,
        out_shape=jax.ShapeDtypeStruct((A * SUB, LANE), _F32),
        compiler_params=pltpu.CompilerParams(
            dimension_semantics=("arbitrary",), vmem_limit_bytes=VMEM_LIMIT),
        name="moe_gmm",
    )(*work, xs2, nffn, wg, wu, wd)


def _combine_kernel(dest_ref, x1_ref, route_ref, nfin_ref, yb_ref, o_ref, buf, sems, *, tok0, n_tok):
    i = pl.program_id(0)
    n_steps = pl.num_programs(0)
    tm = o_ref.shape[0]

    def tile_copy(d, slot, k, r):
        rows = pl.ds(pl.multiple_of(r * SUB, SUB), SUB)
        return pltpu.make_async_copy(yb_ref.at[d], buf.at[slot, k, rows], sems.at[slot])

    def issue(step, slot):
        base = tok0 + step * tm

        def start(r, c):
            t = base + r
            tile_copy(dest_ref[t], slot, 0, r).start(priority=0)
            tile_copy(dest_ref[n_tok + t], slot, 1, r).start(priority=1)
            return c

        lax.fori_loop(0, tm, start, 0, unroll=DMA_UNROLL)

    @pl.when(i == 0)
    def _():
        issue(0, 0)

    slot = i % 2

    def wait(r, c):
        tile_copy(0, slot, 0, 0).wait()
        tile_copy(0, slot, 1, 0).wait()
        return c

    lax.fori_loop(0, tm, wait, 0, unroll=DMA_UNROLL)

    def combine_rows(prefetch):
        rt = jnp.concatenate([route_ref[...], jnp.zeros((LANE - ROUTE_OUT, tm), _F32)], axis=0).T
        cn = tm // COMBINE_CHUNKS
        for ch in range(COMBINE_CHUNKS):
            if prefetch:
                base = tok0 + (i + 1) * tm
                for r in range(ch * cn, (ch + 1) * cn):
                    tile_copy(dest_ref[base + r], 1 - slot, 0, r).start(priority=0)
                    tile_copy(dest_ref[n_tok + base + r], 1 - slot, 1, r).start(priority=1)
            rows = pl.ds(ch * cn * SUB, cn * SUB)
            g0 = rt[ch * cn:(ch + 1) * cn, 2:3]
            g1 = rt[ch * cn:(ch + 1) * cn, 3:4]
            y0 = _load_rows(buf.at[slot, 0, rows], cn)
            y1 = _load_rows(buf.at[slot, 1, rows], cn)
            y = _load_rows(x1_ref.at[rows], cn) + (y0 * g0 + y1 * g1)
            o_ref[ch * cn:(ch + 1) * cn, :] = _rms(y, nfin_ref[...])

    @pl.when(i + 1 < n_steps)
    def _():
        combine_rows(True)

    @pl.when(i + 1 == n_steps)
    def _():
        combine_rows(False)


def _combine(dest, x1_2, route, nfin, yb3, tok0):
    T = x1_2.shape[0] // SUB
    D = SUB * LANE
    tm = min(COMBINE_TM, T)
    grid_spec = pltpu.PrefetchScalarGridSpec(
        num_scalar_prefetch=1,
        grid=(T // tm,),
        in_specs=[
            pl.BlockSpec((tm * SUB, LANE), lambda i, *_: (i, 0)),
            pl.BlockSpec((ROUTE_OUT, tm), lambda i, *_: (0, i)),
            pl.BlockSpec((1, D), lambda i, *_: (0, 0)),
            pl.BlockSpec(memory_space=pl.ANY),
        ],
        out_specs=pl.BlockSpec((tm, D), lambda i, *_: (i, 0)),
        scratch_shapes=[pltpu.VMEM((2, 2, tm * SUB, LANE), _F32), pltpu.SemaphoreType.DMA((2,))],
    )
    return pl.pallas_call(
        functools.partial(_combine_kernel, tok0=tok0, n_tok=dest.shape[0] // 2),
        grid_spec=grid_spec,
        out_shape=jax.ShapeDtypeStruct((T, D), _F32),
        compiler_params=pltpu.CompilerParams(
            dimension_semantics=("arbitrary",), vmem_limit_bytes=VMEM_LIMIT),
        name="moe_combine",
    )(dest, x1_2, route, nfin, yb3)


def _routing_tables(route, counts, n_tiles):
    experts = route[0:2].astype(jnp.int32)
    ranks = route[4:6].astype(jnp.int32)
    A = experts.size
    ids = jnp.arange(N_EXPERTS, dtype=jnp.int32)
    ends = jnp.cumsum(counts)
    starts = ends - counts
    dest = ranks + jnp.sum(jnp.where(experts[..., None] == ids, starts, 0), axis=-1)
    dest = dest.reshape(-1)
    tile_starts = jnp.arange(n_tiles, dtype=jnp.int32) * MOE_TM
    pos_t = jnp.arange(n_tiles, dtype=jnp.int32) + jnp.sum(starts[None, :] < tile_starts[:, None], axis=1)
    pos_e = ids + jnp.sum(tile_starts[None, :] <= starts[:, None], axis=1)
    slots = jnp.arange(n_tiles + N_EXPERTS, dtype=jnp.int32)
    pts = (jnp.sum(jnp.where(pos_t[None, :] == slots[:, None], tile_starts[None, :], 0), axis=1)
           + jnp.sum(jnp.where(pos_e[None, :] == slots[:, None], starts[None, :], 0), axis=1))
    lo = pts.astype(jnp.int32)
    hi = jnp.concatenate([lo[1:], jnp.array([A], jnp.int32)])
    tile = jnp.minimum(lo // MOE_TM, n_tiles - 1).astype(jnp.int32)
    expert = jnp.minimum(jnp.sum(ends[None, :] <= lo[:, None], axis=1), N_EXPERTS - 1).astype(jnp.int32)
    chg = jnp.concatenate([jnp.ones((1,), jnp.int32), (expert[1:] != expert[:-1]).astype(jnp.int32)])
    return dest.astype(jnp.int32), (tile, expert, lo, hi, chg)


def _bucket_table(lq, lk):
    dist = np.arange(lq)[:, None] + WINDOW - np.arange(lk)[None, :]
    band = (dist >= 0) & (dist < WINDOW)
    d = np.clip(dist, 0, WINDOW - 1)
    max_exact = N_BUCKETS // 2
    d_f = np.maximum(d, 1).astype(np.float32)
    large = max_exact + (np.log(d_f / max_exact) / math.log(MAX_DISTANCE / max_exact)
                         * (N_BUCKETS - max_exact)).astype(np.int32)
    large = np.minimum(large, N_BUCKETS - 1)
    return np.where(d < max_exact, d, large).astype(np.int32), band


def _bias_table(rb, lq, lk):
    bkt, band = _bucket_table(lq, lk)
    onehot = jnp.asarray(bkt)[None, :, :] == jnp.arange(N_BUCKETS, dtype=jnp.int32)[:, None, None]
    bias = jnp.sum(jnp.where(onehot[:, None], rb[:, :, None, None], 0.0), axis=0)
    return jnp.where(jnp.asarray(band)[None], bias, NEG_INF)


def _decay_tables(C):
    log_gamma = jnp.log(1.0 - 2.0 ** (-5.0 - jnp.arange(RET_HEADS, dtype=_F32)))
    idx = jnp.arange(C, dtype=_F32)
    diff = idx[:, None] - idx[None, :]
    decay_in = jnp.where((diff >= 0)[..., None],
                         jnp.exp(jnp.maximum(diff, 0.0)[..., None] * log_gamma), 0.0)
    q_dec = jnp.exp((idx + 1.0)[:, None] * log_gamma)
    k_dec = jnp.exp((C - 1.0 - idx)[:, None] * log_gamma)
    c_dec = jnp.exp(C * log_gamma)
    dec = jnp.transpose(decay_in, (2, 0, 1))
    qd = jnp.repeat(q_dec, RET_DK, axis=1)
    kd = jnp.repeat(k_dec, RET_DK, axis=1)
    return dec, qd, kd, c_dec


def _rope_tables(pos):
    half = RET_DK // 2
    inv = ROPE_BASE ** (-jnp.arange(half, dtype=_F32) * 2.0 / RET_DK)
    ang = pos.astype(_F32)[:, None] * inv[None, :]
    cos = jnp.cos(ang)
    sin = jnp.sin(ang)
    return jnp.concatenate([cos, cos], axis=1), jnp.concatenate([-sin, sin], axis=1)


def _constants(rel_bias, attn_sink, L, ls):
    cst = {}
    rb = rel_bias.astype(_F32)
    cst['bias_p'] = jnp.transpose(_bias_table(rb, WINDOW, 2 * WINDOW), (0, 2, 1))
    cst['sink'] = attn_sink.astype(_F32)
    cst['cos_p'], cst['sin_p'] = _rope_tables(jnp.arange(L))
    cst['dec_p'], cst['qd_p'], cst['kd_p'], cst['cdec_p'] = _decay_tables(min(RET_CHUNK, L))
    bias = _bias_table(rb, ls, WINDOW + ls).reshape(KV_HEADS, GQA_GROUP * ls, WINDOW + ls)
    cst['bias_hist'] = bias[:, :, :WINDOW]
    cst['bias_new'] = bias[:, :, WINDOW:]
    cst['sink_col'] = jnp.repeat(attn_sink.astype(_F32).reshape(KV_HEADS, GQA_GROUP), ls,
                                 axis=1)[..., None]
    cst['cos_s'], cst['sin_s'] = _rope_tables(PAST_LEN + jnp.arange(ls))
    cst['dec_s'], cst['qd_s'], cst['kd_s'], cst['cdec_s'] = _decay_tables(min(RET_CHUNK, ls))
    return cst


def _layer_weights(layer, norm_mix, w_in, w_branch_attn, w_branch_ret, w_out, norm_ffn,
                   w_router_group, b_router_group, w_router_expert, b_router_expert):
    D = w_in.shape[1]
    wr = jnp.concatenate([w_router_group[layer].astype(_F32), w_router_expert[layer].astype(_F32)], axis=1)
    wr = jnp.pad(wr, ((0, 0), (0, ROUTE_LANES - wr.shape[1])))
    wrh = wr.astype(_BF16)
    wrl = jnp.concatenate([wrh, (wr - wrh.astype(_F32)).astype(_BF16)], axis=1)
    br = jnp.concatenate([b_router_group[layer].astype(_F32), b_router_expert[layer].astype(_F32)])
    br = jnp.pad(br, (0, ROUTE_LANES - br.shape[0]))[None, :]
    return {
        'nmix': norm_mix[layer].astype(_F32)[None, :],
        'win': w_in[layer].astype(_BF16),
        'wba': w_branch_attn[layer].astype(_BF16),
        'wbr': w_branch_ret[layer].astype(_BF16),
        'wout': w_out[layer].astype(_BF16),
        'nffn': norm_ffn[layer].astype(_F32)[None, :],
        'wrh': wrh, 'wrl': wrl, 'br': br,
    }


def kernel(x_prompt, x_sample, cache_k, cache_v, state_ret, norm_mix, w_in, attn_sink, rel_bias,
           w_branch_attn, w_branch_ret, w_out, norm_ffn, w_router_group, b_router_group,
           w_router_expert, b_router_expert, w_gate, w_up, w_down, norm_final):
    depth = w_in.shape[0]
    assert depth == 1, "the final norm is fused into the MoE combine of the only layer"
    B, L, D = x_prompt.shape
    NB, ls, _ = x_sample.shape
    Tp, Ts = B * L, NB * ls
    nfin = norm_final.astype(_F32)[None, :]
    yp, ys = x_prompt, x_sample
    pk, pv, ps, sk, sv, ss = [], [], [], [], [], []
    for layer in range(depth):
        w = _layer_weights(layer, norm_mix, w_in, w_branch_attn, w_branch_ret, w_out, norm_ffn,
                           w_router_group, b_router_group, w_router_expert, b_router_expert)
        cst = _constants(rel_bias, attn_sink[layer], L, ls)
        x1p, routep, k1, v1, s1, cnt_p = _prompt_mixer(yp, cst, w)
        ys2 = ys.reshape(Ts, D)
        proj = _sample_inproj(ys2, w)
        attn_s, ret_s, k2, v2, s2 = _sample_core(
            proj, ls,
            cache_k[layer].reshape(NB, WINDOW, KV_W), cache_v[layer].reshape(NB, WINDOW, KV_W),
            state_ret[layer], cst)
        x1s, routes, cnt_all = _sample_post(
            ys2, attn_s, ret_s,
            proj[:, OFF_GA:OFF_GA + D], proj[:, OFF_GT:OFF_GT + D], w, cnt_p)
        n_rows = 2 * (Tp + Ts)
        assert n_rows % MOE_TM == 0
        counts = cnt_all[N_GROUPS:N_GROUPS + N_EXPERTS, 0].astype(jnp.int32)
        dest, work = _routing_tables(jnp.concatenate([routep, routes], axis=1), counts,
                                     n_rows // MOE_TM)
        xs3 = _dispatch(dest, x1p.reshape(Tp, SUB, LANE), x1s.reshape(Ts, SUB, LANE))
        yb2 = _gmm(work, xs3.reshape(n_rows * SUB, LANE), w['nffn'],
                   w_gate[layer], w_up[layer], w_down[layer])
        yb3 = yb2.reshape(n_rows, SUB, LANE)
        yp = _combine(dest, x1p, routep, nfin, yb3, 0).reshape(B, L, D)
        ys = _combine(dest, x1s, routes, nfin, yb3, Tp).reshape(NB, ls, D)
        pk.append(k1.reshape(B, WINDOW, KV_HEADS, HEAD_DIM))
        pv.append(v1.reshape(B, WINDOW, KV_HEADS, HEAD_DIM))
        ps.append(s1)
        sk.append(k2.reshape(NB, WINDOW, KV_HEADS, HEAD_DIM))
        sv.append(v2.reshape(NB, WINDOW, KV_HEADS, HEAD_DIM))
        ss.append(s2)
    return (yp, ys, jnp.stack(pk), jnp.stack(pv), jnp.stack(ps),
            jnp.stack(sk), jnp.stack(sv), jnp.stack(ss))
```

```python
import functools
import math

import jax
import jax.numpy as jnp
import numpy as np
from jax import lax
from jax.experimental import pallas as pl
from jax.experimental.pallas import tpu as pltpu

D_MODEL = 1024
HEAD_DIM = 64
KV_HEADS = 4
GQA_GROUP = 4
ATTN_HEADS = KV_HEADS * GQA_GROUP
WINDOW = 128
N_BUCKETS = 32
MAX_DISTANCE = 128
RET_HEADS = 4
RET_DK = 128
RET_DV = 256
RET_CHUNK = 128
ROPE_BASE = 10000.0
N_GROUPS = 4
EXPERTS_PER_GROUP = 8
N_EXPERTS = N_GROUPS * EXPERTS_PER_GROUP
EXPERT_FF = 512
NORM_EPS = 1e-6
NEG_INF = -1e30
PAST_LEN = 16384

ATTN_W = ATTN_HEADS * HEAD_DIM
KV_W = KV_HEADS * HEAD_DIM
RQ_W = RET_HEADS * RET_DK
RV_W = RET_HEADS * RET_DV
OFF_QA = 0
OFF_KA = OFF_QA + ATTN_W
OFF_VA = OFF_KA + KV_W
OFF_QR = OFF_VA + KV_W
OFF_KR = OFF_QR + RQ_W
OFF_VR = OFF_KR + RQ_W
OFF_GR = OFF_VR + RV_W
OFF_GA = OFF_GR + RV_W
OFF_GT = OFF_GA + D_MODEL
ROUTE_LANES = 128
ROUTE_ROWS = 40
ROUTE_OUT = 8

LANE = 128
SUB = 8
PROMPT_TM = 512
PROJ_PANEL = 256
POST_PARTS = 2
SAMPLE_GROUP = 8
SAMPLE_INPROJ_STEPS = 4
SAMPLE_ROWS = 16
MOE_TM = 1024
GMM_SUBTILE = 256
GMM_PARTS = 2
ROW_TM = 512
COMBINE_BUFS = 3
COMBINE_CHUNKS = 2
DMA_UNROLL = 8
VMEM_LIMIT = 60 * 1024 * 1024

_F32 = jnp.float32
_BF16 = jnp.bfloat16


def _const_spec(shape):
    nd = len(shape)
    return pl.BlockSpec(shape, lambda *_: (0,) * nd, pipeline_mode=pl.Buffered(1))


def _rms(x, gain):
    return x * lax.rsqrt(jnp.mean(x * x, axis=-1, keepdims=True) + NORM_EPS) * gain


def _dot(a, b):
    return jnp.dot(a, b, preferred_element_type=_F32)


def _dot_nt(a, b):
    return lax.dot_general(a, b, (((1,), (1,)), ((), ())), preferred_element_type=_F32)


def _dot_tn(a, b):
    return lax.dot_general(a, b, (((0,), (0,)), ((), ())), preferred_element_type=_F32)


def _load_rows(ref, n):
    return jnp.concatenate([ref[pl.ds(s, n, stride=SUB), :] for s in range(SUB)], axis=1)


def _store_rows(ref, val):
    n = val.shape[0]
    for s in range(SUB):
        ref[pl.ds(s, n, stride=SUB), :] = val[:, s * LANE:(s + 1) * LANE]


def _rotary(x, cosf, sinf):
    return x * cosf + pltpu.roll(x, RET_DK // 2, 1) * sinf


def _post(xs, attn_projs, rets, gates_a, gates_r, wbr, wout, nffn, wrh, wrl, br, cnt):
    parts = range(len(xs))
    ret_projs = [_dot(rets[j].astype(_BF16), wbr) for j in parts]
    x1s = []
    for j in parts:
        merged = jax.nn.sigmoid(gates_a[j]) * attn_projs[j] + jax.nn.sigmoid(gates_r[j]) * ret_projs[j]
        x1s.append(xs[j] + _dot(merged.astype(_BF16), wout))
    logit_parts = []
    for j in parts:
        xn2 = _rms(x1s[j], nffn)
        hi = xn2.astype(_BF16)
        lo = (xn2 - hi.astype(_F32)).astype(_BF16)
        t = _dot(hi, wrl)
        logit_parts.append(t[:, 0:ROUTE_LANES] + (t[:, ROUTE_LANES:] + _dot(lo, wrh)) + br)
    x1 = x1s[0] if len(x1s) == 1 else jnp.concatenate(x1s, axis=0)
    logits = logit_parts[0] if len(x1s) == 1 else jnp.concatenate(logit_parts, axis=0)
    n = logits.shape[0]
    lt = logits.T[0:ROUTE_ROWS, :]
    row = lax.broadcasted_iota(jnp.int32, (ROUTE_ROWS, n), 0)
    big = jnp.int32(1 << 20)
    neg = jnp.float32(-jnp.inf)
    gl = jnp.where(row < N_GROUPS, lt, neg)
    gmax = jnp.max(gl, axis=0, keepdims=True)
    gexp = jnp.exp(gl - gmax)
    gsum = jnp.sum(gexp, axis=0, keepdims=True)
    pg = gexp / gsum
    g_w = jnp.max(pg, axis=0, keepdims=True)
    g_idx = jnp.min(jnp.where(pg == g_w, row, big), axis=0, keepdims=True)
    e_row = row - N_GROUPS
    emask = (e_row >= 0) & (e_row < N_EXPERTS) & ((e_row >> 3) == g_idx)
    fl = jnp.where(emask, lt, neg)
    fmax = jnp.max(fl, axis=0, keepdims=True)
    fexp = jnp.exp(fl - fmax)
    fsum = jnp.sum(fexp, axis=0, keepdims=True)
    pe = jnp.where(emask, fexp / fsum, -1.0)
    p1 = jnp.max(pe, axis=0, keepdims=True)
    i1 = jnp.min(jnp.where(pe == p1, row, big), axis=0, keepdims=True)
    pe2 = jnp.where(row == i1, -1.0, pe)
    p2 = jnp.max(pe2, axis=0, keepdims=True)
    i2 = jnp.min(jnp.where(pe2 == p2, row, big), axis=0, keepdims=True)
    psum = p1 + p2
    gate1 = g_w * p1 / psum
    gate2 = g_w * p2 / psum
    oh1 = row == i1
    oh2 = row == i2
    c = jnp.where(oh1 | oh2, 1.0, 0.0)
    tt = lax.broadcasted_iota(jnp.int32, (n, n), 0)
    tc = lax.broadcasted_iota(jnp.int32, (n, n), 1)
    upper = jnp.where(tt < tc, 1.0, 0.0).astype(_BF16)
    before = _dot(c.astype(_BF16), upper) + cnt
    rank1 = jnp.sum(jnp.where(oh1, before, 0.0), axis=0, keepdims=True)
    rank2 = jnp.sum(jnp.where(oh2, before, 0.0), axis=0, keepdims=True)
    cnt = cnt + jnp.sum(c, axis=1, keepdims=True)
    r8 = lax.broadcasted_iota(jnp.int32, (ROUTE_OUT, n), 0)
    vals = [(i1 - N_GROUPS).astype(_F32), (i2 - N_GROUPS).astype(_F32), gate1, gate2, rank1, rank2]
    route = jnp.zeros((ROUTE_OUT, n), _F32)
    for k, v in enumerate(vals):
        route = jnp.where(r8 == k, v, route)
    return x1, route, cnt


def _prompt_mixer_kernel(sink_ref, cdec_ref,
                         x_ref, nmix_ref, win_ref, bias_ref, cos_ref, sin_ref, dec_ref,
                         qd_ref, kd_ref, wba_ref, wbr_ref, wout_ref, nffn_ref,
                         wrh_ref, wrl_ref, br_ref,
                         x1_ref, route_ref, knew_ref, vnew_ref, s_ref, cnt_ref,
                         qkv, proj, kctx, vctx, attn_t, ret):
    i = pl.program_id(1)
    last = pl.num_programs(1) - 1

    @pl.when((i == 0) & (pl.program_id(0) == 0))
    def _():
        cnt_ref[...] = jnp.zeros_like(cnt_ref)

    tm = x_ref.shape[1]
    n_sub = tm // WINDOW
    scale = HEAD_DIM ** -0.5

    @pl.when(i == 0)
    def _():
        s_ref[...] = jnp.zeros_like(s_ref)
        kctx[0:WINDOW, :] = jnp.zeros((WINDOW, KV_W), _BF16)
        vctx[0:WINDOW, :] = jnp.zeros((WINDOW, KV_W), _BF16)

    x = x_ref[0]
    xn = _rms(x, nmix_ref[...]).astype(_BF16)
    n_in = win_ref.shape[1]
    panel = PROJ_PANEL

    def project(c0):
        res = _dot(xn, win_ref[:, c0:c0 + panel])
        if c0 < OFF_QR:
            qkv[:, c0:c0 + panel] = res
        else:
            proj[:, c0 - OFF_QR:c0 - OFF_QR + panel] = res

    def cols(lo, width):
        return slice(lo - OFF_QR, lo - OFF_QR + width)

    for c0 in range(0, OFF_QR, panel):
        project(c0)
    later_panels = list(range(OFF_QR, n_in, panel))

    @pl.when(i == last)
    def _():
        knew_ref[0] = qkv[tm - WINDOW:tm, OFF_KA:OFF_KA + KV_W]
        vnew_ref[0] = qkv[tm - WINDOW:tm, OFF_VA:OFF_VA + KV_W]

    krow = lax.broadcasted_iota(jnp.int32, (2 * WINDOW, 1), 0)
    for c in range(n_sub):
        r0 = c * WINDOW
        kctx[WINDOW:2 * WINDOW, :] = qkv[r0:r0 + WINDOW, OFF_KA:OFF_KA + KV_W].astype(_BF16)
        vctx[WINDOW:2 * WINDOW, :] = qkv[r0:r0 + WINDOW, OFF_VA:OFF_VA + KV_W].astype(_BF16)
        if c == 0:
            pen = jnp.where((krow < WINDOW) & (i == 0), NEG_INF, 0.0).astype(_F32)
        for h in range(KV_HEADS):
            k_h = kctx[:, h * HEAD_DIM:(h + 1) * HEAD_DIM]
            v_h = vctx[:, h * HEAD_DIM:(h + 1) * HEAD_DIM]
            probs = []
            for g in range(GQA_GROUP):
                hq = h * GQA_GROUP + g
                q = (qkv[r0:r0 + WINDOW, hq * HEAD_DIM:(hq + 1) * HEAD_DIM] * scale).astype(_BF16)
                s = _dot_nt(k_h, q) + bias_ref[hq]
                if c == 0:
                    s = s + pen
                snk = sink_ref[hq]
                m = jnp.maximum(jnp.max(s, axis=0, keepdims=True), snk)
                p = jnp.exp(s - m)
                den = jnp.sum(p, axis=0, keepdims=True) + jnp.exp(snk - m)
                probs.append((p * (1.0 / den)).astype(_BF16))
            groups_left = (n_sub - c) * KV_HEADS - h
            for _ in range(-(-len(later_panels) // groups_left)):
                project(later_panels.pop(0))
            o_t = _dot_tn(v_h, jnp.concatenate(probs, axis=1))
            for g in range(GQA_GROUP):
                hq = h * GQA_GROUP + g
                attn_t[hq * HEAD_DIM:(hq + 1) * HEAD_DIM, r0:r0 + WINDOW] = (
                    o_t[:, g * WINDOW:(g + 1) * WINDOW].astype(_BF16))
        kctx[0:WINDOW, :] = kctx[WINDOW:2 * WINDOW, :]
        vctx[0:WINDOW, :] = vctx[WINDOW:2 * WINDOW, :]

    for c0 in later_panels:
        project(c0)

    for c in range(n_sub):
        r0 = c * RET_CHUNK
        cosf = cos_ref[r0:r0 + RET_CHUNK, :]
        sinf = sin_ref[r0:r0 + RET_CHUNK, :]
        for h in range(RET_HEADS):
            qc = _rotary(proj[r0:r0 + RET_CHUNK, cols(OFF_QR + h * RET_DK, RET_DK)], cosf, sinf)
            kc = _rotary(proj[r0:r0 + RET_CHUNK, cols(OFF_KR + h * RET_DK, RET_DK)], cosf, sinf) * (RET_DK ** -0.5)
            qd = qd_ref[:, h * RET_DK:(h + 1) * RET_DK]
            kd = kd_ref[:, h * RET_DK:(h + 1) * RET_DK]
            vc = proj[r0:r0 + RET_CHUNK, cols(OFF_VR + h * RET_DV, RET_DV)].astype(_BF16)
            sc = _dot_nt(qc.astype(_BF16), kc.astype(_BF16))
            s_old = s_ref[0, h]
            cross = _dot((qc * qd).astype(_BF16), s_old.astype(_BF16))
            s_ref[0, h] = s_old * cdec_ref[h] + _dot_tn((kc * kd).astype(_BF16), vc)
            o = _dot((sc * dec_ref[h]).astype(_BF16), vc) + cross
            o = o * lax.rsqrt(jnp.mean(o * o, axis=-1, keepdims=True) + NORM_EPS)
            gr = proj[r0:r0 + RET_CHUNK, cols(OFF_GR + h * RET_DV, RET_DV)]
            ret[r0:r0 + RET_CHUNK, h * RET_DV:(h + 1) * RET_DV] = o * (gr * jax.nn.sigmoid(gr))

    pn = tm // POST_PARTS
    rows = [slice(j * pn, (j + 1) * pn) for j in range(POST_PARTS)]
    x1, route, cnt = _post([x[r] for r in rows],
                           [_dot_tn(attn_t[:, r], wba_ref[...]) for r in rows],
                           [ret[r, :] for r in rows],
                           [proj[r, cols(OFF_GA, D_MODEL)] for r in rows],
                           [proj[r, cols(OFF_GT, D_MODEL)] for r in rows],
                           wbr_ref[...], wout_ref[...], nffn_ref[...],
                           wrh_ref[...], wrl_ref[...], br_ref[...], cnt_ref[:, 0:1])
    _store_rows(x1_ref, x1)
    route_ref[...] = route
    cnt_ref[...] = jnp.broadcast_to(cnt, cnt_ref.shape)


def _prompt_mixer(x, cst, w):
    B, L, D = x.shape
    tm = min(PROMPT_TM, L)
    nb = L // tm
    n_in = w['win'].shape[1]
    step = lambda b, i, *_: (b, i, 0)
    per_b = lambda b, i, *_: (b, 0, 0)

    grid_spec = pltpu.PrefetchScalarGridSpec(
        num_scalar_prefetch=2,
        grid=(B, nb),
        in_specs=[
            pl.BlockSpec((1, tm, D), step),
            _const_spec((1, D)),
            _const_spec((D, n_in)),
            _const_spec((ATTN_HEADS, 2 * WINDOW, WINDOW)),
            pl.BlockSpec((tm, RET_DK), lambda b, i, *_: (i, 0)),
            pl.BlockSpec((tm, RET_DK), lambda b, i, *_: (i, 0)),
            _const_spec((RET_HEADS, RET_CHUNK, RET_CHUNK)),
            _const_spec((RET_CHUNK, RQ_W)),
            _const_spec((RET_CHUNK, RQ_W)),
            _const_spec((ATTN_W, D)),
            _const_spec((RV_W, D)),
            _const_spec((D, D)),
            _const_spec((1, D)),
            _const_spec((D, ROUTE_LANES)),
            _const_spec((D, 2 * ROUTE_LANES)),
            _const_spec((1, ROUTE_LANES)),
        ],
        out_specs=[
            pl.BlockSpec((tm * SUB, LANE), lambda b, i, *_: (b * nb + i, 0)),
            pl.BlockSpec((ROUTE_OUT, tm), lambda b, i, *_: (0, b * nb + i)),
            pl.BlockSpec((1, WINDOW, KV_W), per_b),
            pl.BlockSpec((1, WINDOW, KV_W), per_b),
            pl.BlockSpec((1, RET_HEADS, RET_DK, RET_DV), lambda b, i, *_: (b, 0, 0, 0)),
            pl.BlockSpec((ROUTE_ROWS, LANE), lambda b, i, *_: (0, 0)),
        ],
        scratch_shapes=[
            pltpu.VMEM((tm, OFF_QR), _F32),
            pltpu.VMEM((tm, n_in - OFF_QR), _F32),
            pltpu.VMEM((2 * WINDOW, KV_W), _BF16),
            pltpu.VMEM((2 * WINDOW, KV_W), _BF16),
            pltpu.VMEM((ATTN_W, tm), _BF16),
            pltpu.VMEM((tm, RV_W), _F32),
        ],
    )
    assert D == SUB * LANE
    out_shape = [
        jax.ShapeDtypeStruct((B * L * SUB, LANE), _F32),
        jax.ShapeDtypeStruct((ROUTE_OUT, B * L), _F32),
        jax.ShapeDtypeStruct((B, WINDOW, KV_W), _F32),
        jax.ShapeDtypeStruct((B, WINDOW, KV_W), _F32),
        jax.ShapeDtypeStruct((B, RET_HEADS, RET_DK, RET_DV), _F32),
        jax.ShapeDtypeStruct((ROUTE_ROWS, LANE), _F32),
    ]
    return pl.pallas_call(
        _prompt_mixer_kernel,
        grid_spec=grid_spec,
        out_shape=out_shape,
        compiler_params=pltpu.CompilerParams(
            dimension_semantics=("arbitrary", "arbitrary"), vmem_limit_bytes=VMEM_LIMIT),
        name="prompt_mixer",
    )(cst['sink'], cst['cdec_p'],
      x, w['nmix'], w['win'], cst['bias_p'], cst['cos_p'], cst['sin_p'], cst['dec_p'],
      cst['qd_p'], cst['kd_p'], w['wba'], w['wbr'], w['wout'], w['nffn'],
      w['wrh'], w['wrl'], w['br'])


def _inproj_kernel(x_ref, nmix_ref, win_ref, o_ref):
    xn = _rms(x_ref[...], nmix_ref[...]).astype(_BF16)
    o_ref[...] = _dot(xn, win_ref[...])


def _sample_inproj(x2d, w):
    T, D = x2d.shape
    n_in = w['win'].shape[1]
    panel = n_in // SAMPLE_INPROJ_STEPS
    assert panel % LANE == 0
    return pl.pallas_call(
        _inproj_kernel,
        grid=(n_in // panel,),
        in_specs=[pl.BlockSpec((T, D), lambda j: (0, 0)),
                  pl.BlockSpec((1, D), lambda j: (0, 0)),
                  pl.BlockSpec((D, panel), lambda j: (0, j))],
        out_specs=pl.BlockSpec((T, panel), lambda j: (0, j)),
        out_shape=jax.ShapeDtypeStruct((T, n_in), _F32),
        compiler_params=pltpu.CompilerParams(
            dimension_semantics=("arbitrary",), vmem_limit_bytes=VMEM_LIMIT),
        name="sample_inproj",
    )(x2d, w['nmix'], w['win'])


def _sample_core_kernel(cdec_ref, proj_ref, ck_ref, cv_ref, st_ref, bh_ref, bn_ref, snk_ref,
                        cos_ref, sin_ref, dec_ref, qd_ref, kd_ref,
                        attn_ref, ret_ref, nk_ref, nv_ref, ns_ref):
    G = ck_ref.shape[0]
    ls = proj_ref.shape[0] // G
    per = SAMPLE_ROWS // ls
    scale = HEAD_DIM ** -0.5
    cosf = cos_ref[...]
    sinf = sin_ref[...]
    heads = [slice(h * HEAD_DIM, (h + 1) * HEAD_DIM) for h in range(KV_HEADS)]

    def body(j, carry):
        group = pl.ds(pl.multiple_of(j * SAMPLE_ROWS, SAMPLE_ROWS), SAMPLE_ROWS)
        rows = proj_ref[group, :]
        seqs = []
        for s in range(per):
            b = j * per + s
            row = rows[s * ls:(s + 1) * ls]
            k_new = row[:, OFF_KA:OFF_KA + KV_W]
            v_new = row[:, OFF_VA:OFF_VA + KV_W]
            ck = ck_ref[b]
            cv = cv_ref[b]
            nk_ref[b, 0:WINDOW - ls, :] = ck[ls:WINDOW, :]
            nk_ref[b, WINDOW - ls:WINDOW, :] = k_new
            nv_ref[b, 0:WINDOW - ls, :] = cv[ls:WINDOW, :]
            nv_ref[b, WINDOW - ls:WINDOW, :] = v_new
            seqs.append(dict(b=b, row=row, ckb=ck.astype(_BF16), cvb=cv.astype(_BF16),
                             knb=k_new.astype(_BF16), vnb=v_new.astype(_BF16)))
        for q in seqs:
            row = q['row']
            q['s1'], q['s2'] = [], []
            for h in range(KV_HEADS):
                q4 = (jnp.concatenate(
                    [row[:, (h * GQA_GROUP + g) * HEAD_DIM:(h * GQA_GROUP + g + 1) * HEAD_DIM]
                     for g in range(GQA_GROUP)], axis=0) * scale).astype(_BF16)
                q['s1'].append(_dot_nt(q4, q['ckb'][:, heads[h]]))
                q['s2'].append(_dot_nt(q4, q['knb'][:, heads[h]]))
        for q in seqs:
            row, b = q['row'], q['b']
            q['sc'], q['cross'], q['vc'] = [], [], []
            for h in range(RET_HEADS):
                qrot = _rotary(row[:, OFF_QR + h * RET_DK:OFF_QR + (h + 1) * RET_DK], cosf, sinf)
                krot = _rotary(row[:, OFF_KR + h * RET_DK:OFF_KR + (h + 1) * RET_DK], cosf, sinf) * (RET_DK ** -0.5)
                vc = row[:, OFF_VR + h * RET_DV:OFF_VR + (h + 1) * RET_DV].astype(_BF16)
                qd = qd_ref[:, h * RET_DK:(h + 1) * RET_DK]
                kd = kd_ref[:, h * RET_DK:(h + 1) * RET_DK]
                s_old = st_ref[b, h]
                q['sc'].append(_dot_nt(qrot.astype(_BF16), krot.astype(_BF16)))
                q['cross'].append(_dot((qrot * qd).astype(_BF16), s_old.astype(_BF16)))
                ns_ref[b, h] = s_old * cdec_ref[h] + _dot_tn((krot * kd).astype(_BF16), vc)
                q['vc'].append(vc)
        for q in seqs:
            q['p1'], q['p2'] = [], []
            for h in range(KV_HEADS):
                s1 = q['s1'][h] + bh_ref[h]
                s2 = q['s2'][h] + bn_ref[h]
                snk = snk_ref[h]
                m = jnp.maximum(jnp.maximum(jnp.max(s1, axis=-1, keepdims=True),
                                            jnp.max(s2, axis=-1, keepdims=True)), snk)
                p1 = jnp.exp(s1 - m)
                p2 = jnp.exp(s2 - m)
                den = (jnp.sum(p1, axis=-1, keepdims=True) + jnp.sum(p2, axis=-1, keepdims=True)
                       + jnp.exp(snk - m))
                r = 1.0 / den
                q['p1'].append((p1 * r).astype(_BF16))
                q['p2'].append((p2 * r).astype(_BF16))
            q['scb'] = [(q['sc'][h] * dec_ref[h]).astype(_BF16) for h in range(RET_HEADS)]
        for q in seqs:
            q['out'] = [_dot(q['p1'][h], q['cvb'][:, heads[h]]) + _dot(q['p2'][h], q['vnb'][:, heads[h]])
                        for h in range(KV_HEADS)]
            q['ret'] = [_dot(q['scb'][h], q['vc'][h]) + q['cross'][h] for h in range(RET_HEADS)]
        for h in range(KV_HEADS):
            for g in range(GQA_GROUP):
                hq = h * GQA_GROUP + g
                attn_ref[group, hq * HEAD_DIM:(hq + 1) * HEAD_DIM] = jnp.concatenate(
                    [q['out'][h][g * ls:(g + 1) * ls] for q in seqs], axis=0)
        for h in range(RET_HEADS):
            o = jnp.concatenate([q['ret'][h] for q in seqs], axis=0)
            o = o * lax.rsqrt(jnp.mean(o * o, axis=-1, keepdims=True) + NORM_EPS)
            gr = rows[:, OFF_GR + h * RET_DV:OFF_GR + (h + 1) * RET_DV]
            ret_ref[group, h * RET_DV:(h + 1) * RET_DV] = o * (gr * jax.nn.sigmoid(gr))
        return carry

    lax.fori_loop(0, G // per, body, 0)


def _sample_core(proj2, ls, ck, cv, st, cst):
    n_in = proj2.shape[1]
    NB = proj2.shape[0] // ls
    G = min(SAMPLE_GROUP, NB)
    assert SAMPLE_ROWS % ls == 0 and G % (SAMPLE_ROWS // ls) == 0 and NB % G == 0
    row_blk = lambda i, *_: (i, 0)
    blk3 = lambda i, *_: (i, 0, 0)
    blk4 = lambda i, *_: (i, 0, 0, 0)
    c2 = lambda i, *_: (0, 0)
    c3 = lambda i, *_: (0, 0, 0)
    ql = GQA_GROUP * ls
    grid_spec = pltpu.PrefetchScalarGridSpec(
        num_scalar_prefetch=1,
        grid=(NB // G,),
        in_specs=[
            pl.BlockSpec((G * ls, n_in), row_blk),
            pl.BlockSpec((G, WINDOW, KV_W), blk3),
            pl.BlockSpec((G, WINDOW, KV_W), blk3),
            pl.BlockSpec((G, RET_HEADS, RET_DK, RET_DV), blk4),
            pl.BlockSpec((KV_HEADS, ql, WINDOW), c3),
            pl.BlockSpec((KV_HEADS, ql, ls), c3),
            pl.BlockSpec((KV_HEADS, ql, 1), c3),
            pl.BlockSpec((ls, RET_DK), c2),
            pl.BlockSpec((ls, RET_DK), c2),
            pl.BlockSpec((RET_HEADS, ls, ls), c3),
            pl.BlockSpec((ls, RQ_W), c2),
            pl.BlockSpec((ls, RQ_W), c2),
        ],
        out_specs=[
            pl.BlockSpec((G * ls, ATTN_W), row_blk),
            pl.BlockSpec((G * ls, RV_W), row_blk),
            pl.BlockSpec((G, WINDOW, KV_W), blk3),
            pl.BlockSpec((G, WINDOW, KV_W), blk3),
            pl.BlockSpec((G, RET_HEADS, RET_DK, RET_DV), blk4),
        ],
    )
    out_shape = [
        jax.ShapeDtypeStruct((NB * ls, ATTN_W), _F32),
        jax.ShapeDtypeStruct((NB * ls, RV_W), _F32),
        jax.ShapeDtypeStruct((NB, WINDOW, KV_W), _F32),
        jax.ShapeDtypeStruct((NB, WINDOW, KV_W), _F32),
        jax.ShapeDtypeStruct((NB, RET_HEADS, RET_DK, RET_DV), _F32),
    ]
    return pl.pallas_call(
        _sample_core_kernel,
        grid_spec=grid_spec,
        out_shape=out_shape,
        compiler_params=pltpu.CompilerParams(
            dimension_semantics=("arbitrary",), vmem_limit_bytes=VMEM_LIMIT),
        name="sample_core",
    )(cst['cdec_s'], proj2, ck, cv, st, cst['bias_hist'], cst['bias_new'], cst['sink_col'],
      cst['cos_s'], cst['sin_s'], cst['dec_s'], cst['qd_s'], cst['kd_s'])


def _sample_post_kernel(x_ref, attn_ref, ret_ref, ga_ref, gt_ref, wba_ref, wbr_ref, wout_ref,
                        nffn_ref, wrh_ref, wrl_ref, br_ref, cnt0_ref, x1_ref, route_ref, cnt_ref):
    x1, route, cnt = _post([x_ref[...]], [_dot(attn_ref[...].astype(_BF16), wba_ref[...])],
                           [ret_ref[...]], [ga_ref[...]], [gt_ref[...]],
                           wbr_ref[...], wout_ref[...], nffn_ref[...],
                           wrh_ref[...], wrl_ref[...], br_ref[...], cnt0_ref[:, 0:1])
    _store_rows(x1_ref, x1)
    route_ref[...] = route
    cnt_ref[...] = jnp.broadcast_to(cnt, cnt_ref.shape)


def _sample_post(x2d, attn, ret, ga, gt, w, cnt0):
    T, D = x2d.shape
    full = lambda s: pl.BlockSpec(s, lambda i: (0,) * len(s))
    return pl.pallas_call(
        _sample_post_kernel,
        grid=(1,),
        in_specs=[full((T, D)), full((T, ATTN_W)), full((T, RV_W)), full((T, D)), full((T, D)),
                  full((ATTN_W, D)), full((RV_W, D)), full((D, D)), full((1, D)),
                  full((D, ROUTE_LANES)), full((D, 2 * ROUTE_LANES)), full((1, ROUTE_LANES)),
                  full((ROUTE_ROWS, LANE))],
        out_specs=[full((T * SUB, LANE)), full((ROUTE_OUT, T)), full((ROUTE_ROWS, LANE))],
        out_shape=[jax.ShapeDtypeStruct((T * SUB, LANE), _F32),
                   jax.ShapeDtypeStruct((ROUTE_OUT, T), _F32),
                   jax.ShapeDtypeStruct((ROUTE_ROWS, LANE), _F32)],
        compiler_params=pltpu.CompilerParams(
            dimension_semantics=("arbitrary",), vmem_limit_bytes=VMEM_LIMIT),
        name="sample_post",
    )(x2d, attn, ret, ga, gt, w['wba'], w['wbr'], w['wout'], w['nffn'], w['wrh'], w['wrl'], w['br'],
      cnt0)


def _dispatch_kernel(dest_ref, xp_ref, xq_ref, xs_ref, sem, *, p_steps):
    i = pl.program_id(0)
    n_tok = pl.num_programs(0) * ROW_TM

    def copy_tile(src):
        def tile_copy(r, d):
            return pltpu.make_async_copy(src.at[r], xs_ref.at[d], sem)

        def start(r, c):
            t = i * ROW_TM + r
            tile_copy(r, dest_ref[t]).start(priority=0)
            tile_copy(r, dest_ref[n_tok + t]).start(priority=1)
            return c

        lax.fori_loop(0, ROW_TM, start, 0, unroll=DMA_UNROLL)

        def wait(r, c):
            tile_copy(0, 0).wait()
            tile_copy(0, 0).wait()
            return c

        lax.fori_loop(0, ROW_TM, wait, 0, unroll=DMA_UNROLL)

    @pl.when(i < p_steps)
    def _():
        copy_tile(xp_ref)

    @pl.when(i >= p_steps)
    def _():
        copy_tile(xq_ref)


def _dispatch(dest, xp3, xq3):
    Tp, Tq = xp3.shape[0], xq3.shape[0]
    assert Tp % ROW_TM == 0 and Tq % ROW_TM == 0
    p_steps = Tp // ROW_TM
    grid_spec = pltpu.PrefetchScalarGridSpec(
        num_scalar_prefetch=1,
        grid=((Tp + Tq) // ROW_TM,),
        in_specs=[
            pl.BlockSpec((ROW_TM, SUB, LANE), lambda i, *_: (jnp.minimum(i, p_steps - 1), 0, 0)),
            pl.BlockSpec((ROW_TM, SUB, LANE), lambda i, *_: (jnp.maximum(i - p_steps, 0), 0, 0)),
        ],
        out_specs=pl.BlockSpec(memory_space=pl.ANY),
        scratch_shapes=[pltpu.SemaphoreType.DMA],
    )
    return pl.pallas_call(
        functools.partial(_dispatch_kernel, p_steps=p_steps),
        grid_spec=grid_spec,
        out_shape=jax.ShapeDtypeStruct((2 * (Tp + Tq), SUB, LANE), _F32),
        compiler_params=pltpu.CompilerParams(dimension_semantics=("arbitrary",)),
        name="moe_dispatch",
    )(dest, xp3, xq3)


def _gmm_kernel(tile_ref, exp_ref, lo_ref, hi_ref, chg_ref,
                x_ref, nffn_ref, wg_ref, wu_ref, wd_ref, y_ref, wg_s, wu_s, wd_s):
    m = pl.program_id(0)
    tm = x_ref.shape[0] // SUB

    @pl.when(chg_ref[m] == 1)
    def _():
        wg_s[...] = wg_ref[0].astype(_BF16)
        wu_s[...] = wu_ref[0].astype(_BF16)
        wd_s[...] = wd_ref[0].astype(_BF16)

    lo = lo_ref[m]
    hi = hi_ref[m]
    hn = GMM_SUBTILE // GMM_PARTS

    def subtile(row0):
        base = tile_ref[m] * tm + row0
        parts = [pl.ds((row0 + j * hn) * SUB, hn * SUB) for j in range(GMM_PARTS)]
        xs = []
        for j in range(GMM_PARTS):
            rows = base + j * hn + lax.broadcasted_iota(jnp.int32, (hn, 1), 0)
            mine = (rows >= lo) & (rows < hi)
            xn = _rms(_load_rows(x_ref.at[parts[j]], hn), nffn_ref[...])
            xs.append(jnp.where(mine, xn, 0.0).astype(_BF16))
        gate_up = [(_dot(x, wg_s[...]), _dot(x, wu_s[...])) for x in xs]
        ys = [_dot(((a * jax.nn.sigmoid(a)) * u).astype(_BF16), wd_s[...]) for a, u in gate_up]
        first = lo <= base

        @pl.when(first)
        def _():
            for j in range(GMM_PARTS):
                _store_rows(y_ref.at[parts[j]], ys[j])

        @pl.when(jnp.logical_not(first))
        def _():
            for j in range(GMM_PARTS):
                _store_rows(y_ref.at[parts[j]], _load_rows(y_ref.at[parts[j]], hn) + ys[j])

    tile_base = tile_ref[m] * tm
    whole = (lo <= tile_base) & (hi >= tile_base + tm)

    @pl.when(whole)
    def _():
        n_parts = tm // hn
        parts = [pl.ds(j * hn * SUB, hn * SUB) for j in range(n_parts)]
        def gate_up(j):
            x = _rms(_load_rows(x_ref.at[parts[j]], hn), nffn_ref[...]).astype(_BF16)
            return _dot(x, wg_s[...]), _dot(x, wu_s[...])

        nxt = gate_up(0)
        for j in range(n_parts):
            a, u = nxt
            if j + 1 < n_parts:
                nxt = gate_up(j + 1)
            _store_rows(y_ref.at[parts[j]],
                        _dot(((a * jax.nn.sigmoid(a)) * u).astype(_BF16), wd_s[...]))

    for row0 in range(0, tm, GMM_SUBTILE):
        base = tile_base + row0
        pl.when(jnp.logical_not(whole) & (hi > base) & (lo < base + GMM_SUBTILE) & (hi > lo))(
            functools.partial(subtile, row0))


def _gmm(work, xs2, nffn, wg, wu, wd):
    A = xs2.shape[0] // SUB
    E, D, F = wg.shape
    n_work = work[0].shape[0]
    grid_spec = pltpu.PrefetchScalarGridSpec(
        num_scalar_prefetch=5,
        grid=(n_work,),
        in_specs=[
            pl.BlockSpec((MOE_TM * SUB, LANE), lambda m, t, e, *_: (t[m], 0)),
            pl.BlockSpec((1, D), lambda m, t, e, *_: (0, 0)),
            pl.BlockSpec((1, D, F), lambda m, t, e, *_: (e[m], 0, 0)),
            pl.BlockSpec((1, D, F), lambda m, t, e, *_: (e[m], 0, 0)),
            pl.BlockSpec((1, F, D), lambda m, t, e, *_: (e[m], 0, 0)),
        ],
        out_specs=pl.BlockSpec((MOE_TM * SUB, LANE), lambda m, t, e, *_: (t[m], 0)),
        scratch_shapes=[pltpu.VMEM((D, F), _BF16), pltpu.VMEM((D, F), _BF16),
                        pltpu.VMEM((F, D), _BF16)],
    )
    return pl.pallas_call(
        _gmm_kernel,
        grid_spec=grid_spec,
        out_shape=jax.ShapeDtypeStruct((A * SUB, LANE), _F32),
        compiler_params=pltpu.CompilerParams(
            dimension_semantics=("arbitrary",), vmem_limit_bytes=VMEM_LIMIT),
        name="moe_gmm",
    )(*work, xs2, nffn, wg, wu, wd)


def _combine_kernel(dest_ref, x1_ref, route_ref, nfin_ref, yb_ref, o_ref, buf, sems, *, tok0, n_tok):
    i = pl.program_id(0)
    n_steps = pl.num_programs(0)
    tm = o_ref.shape[0]

    def tile_copy(d, slot, k, r):
        rows = pl.ds(pl.multiple_of(r * SUB, SUB), SUB)
        return pltpu.make_async_copy(yb_ref.at[d], buf.at[slot, k, rows], sems.at[slot])

    def issue(step, slot):
        base = tok0 + step * tm

        def start(r, c):
            t = base + r
            tile_copy(dest_ref[t], slot, 0, r).start(priority=0)
            tile_copy(dest_ref[n_tok + t], slot, 1, r).start(priority=1)
            return c

        lax.fori_loop(0, tm, start, 0, unroll=DMA_UNROLL)

    @pl.when(i == 0)
    def _():
        issue(0, 0)

    @pl.when((i == 0) & (n_steps > 1))
    def _():
        issue(1, 1)

    slot = i % COMBINE_BUFS
    ahead = COMBINE_BUFS - 1

    def wait(r, c):
        tile_copy(0, slot, 0, 0).wait()
        tile_copy(0, slot, 1, 0).wait()
        return c

    lax.fori_loop(0, tm, wait, 0, unroll=DMA_UNROLL)

    def combine_rows(prefetch):
        rt = jnp.concatenate([route_ref[...], jnp.zeros((LANE - ROUTE_OUT, tm), _F32)], axis=0).T
        cn = tm // COMBINE_CHUNKS
        for ch in range(COMBINE_CHUNKS):
            if prefetch:
                base = tok0 + (i + ahead) * tm
                nslot = (i + ahead) % COMBINE_BUFS
                for r in range(ch * cn, (ch + 1) * cn):
                    tile_copy(dest_ref[base + r], nslot, 0, r).start(priority=0)
                    tile_copy(dest_ref[n_tok + base + r], nslot, 1, r).start(priority=1)
            rows = pl.ds(ch * cn * SUB, cn * SUB)
            g0 = rt[ch * cn:(ch + 1) * cn, 2:3]
            g1 = rt[ch * cn:(ch + 1) * cn, 3:4]
            y0 = _load_rows(buf.at[slot, 0, rows], cn)
            y1 = _load_rows(buf.at[slot, 1, rows], cn)
            y = _load_rows(x1_ref.at[rows], cn) + (y0 * g0 + y1 * g1)
            o_ref[ch * cn:(ch + 1) * cn, :] = _rms(y, nfin_ref[...])

    @pl.when(i + ahead < n_steps)
    def _():
        combine_rows(True)

    @pl.when(i + ahead >= n_steps)
    def _():
        combine_rows(False)


def _combine(dest, x1_2, route, nfin, yb3, tok0):
    T = x1_2.shape[0] // SUB
    D = SUB * LANE
    tm = min(ROW_TM, T)
    grid_spec = pltpu.PrefetchScalarGridSpec(
        num_scalar_prefetch=1,
        grid=(T // tm,),
        in_specs=[
            pl.BlockSpec((tm * SUB, LANE), lambda i, *_: (i, 0)),
            pl.BlockSpec((ROUTE_OUT, tm), lambda i, *_: (0, i)),
            pl.BlockSpec((1, D), lambda i, *_: (0, 0)),
            pl.BlockSpec(memory_space=pl.ANY),
        ],
        out_specs=pl.BlockSpec((tm, D), lambda i, *_: (i, 0)),
        scratch_shapes=[pltpu.VMEM((COMBINE_BUFS, 2, tm * SUB, LANE), _F32),
                        pltpu.SemaphoreType.DMA((COMBINE_BUFS,))],
    )
    return pl.pallas_call(
        functools.partial(_combine_kernel, tok0=tok0, n_tok=dest.shape[0] // 2),
        grid_spec=grid_spec,
        out_shape=jax.ShapeDtypeStruct((T, D), _F32),
        compiler_params=pltpu.CompilerParams(
            dimension_semantics=("arbitrary",), vmem_limit_bytes=VMEM_LIMIT),
        name="moe_combine",
    )(dest, x1_2, route, nfin, yb3)


def _routing_tables(route, counts, n_tiles):
    experts = route[0:2].astype(jnp.int32)
    ranks = route[4:6].astype(jnp.int32)
    A = experts.size
    ids = jnp.arange(N_EXPERTS, dtype=jnp.int32)
    ends = jnp.cumsum(counts)
    starts = ends - counts
    dest = ranks + jnp.sum(jnp.where(experts[..., None] == ids, starts, 0), axis=-1)
    dest = dest.reshape(-1)
    tile_starts = jnp.arange(n_tiles, dtype=jnp.int32) * MOE_TM
    pos_t = jnp.arange(n_tiles, dtype=jnp.int32) + jnp.sum(starts[None, :] < tile_starts[:, None], axis=1)
    pos_e = ids + jnp.sum(tile_starts[None, :] <= starts[:, None], axis=1)
    slots = jnp.arange(n_tiles + N_EXPERTS, dtype=jnp.int32)
    pts = (jnp.sum(jnp.where(pos_t[None, :] == slots[:, None], tile_starts[None, :], 0), axis=1)
           + jnp.sum(jnp.where(pos_e[None, :] == slots[:, None], starts[None, :], 0), axis=1))
    lo = pts.astype(jnp.int32)
    hi = jnp.concatenate([lo[1:], jnp.array([A], jnp.int32)])
    tile = jnp.minimum(lo // MOE_TM, n_tiles - 1).astype(jnp.int32)
    expert = jnp.minimum(jnp.sum(ends[None, :] <= lo[:, None], axis=1), N_EXPERTS - 1).astype(jnp.int32)
    chg = jnp.concatenate([jnp.ones((1,), jnp.int32), (expert[1:] != expert[:-1]).astype(jnp.int32)])
    return dest.astype(jnp.int32), (tile, expert, lo, hi, chg)


def _bucket_table(lq, lk):
    dist = np.arange(lq)[:, None] + WINDOW - np.arange(lk)[None, :]
    band = (dist >= 0) & (dist < WINDOW)
    d = np.clip(dist, 0, WINDOW - 1)
    max_exact = N_BUCKETS // 2
    d_f = np.maximum(d, 1).astype(np.float32)
    large = max_exact + (np.log(d_f / max_exact) / math.log(MAX_DISTANCE / max_exact)
                         * (N_BUCKETS - max_exact)).astype(np.int32)
    large = np.minimum(large, N_BUCKETS - 1)
    return np.where(d < max_exact, d, large).astype(np.int32), band


def _bias_table(rb, lq, lk):
    bkt, band = _bucket_table(lq, lk)
    onehot = jnp.asarray(bkt)[None, :, :] == jnp.arange(N_BUCKETS, dtype=jnp.int32)[:, None, None]
    bias = jnp.sum(jnp.where(onehot[:, None], rb[:, :, None, None], 0.0), axis=0)
    return jnp.where(jnp.asarray(band)[None], bias, NEG_INF)


def _decay_tables(C):
    log_gamma = jnp.log(1.0 - 2.0 ** (-5.0 - jnp.arange(RET_HEADS, dtype=_F32)))
    idx = jnp.arange(C, dtype=_F32)
    diff = idx[:, None] - idx[None, :]
    decay_in = jnp.where((diff >= 0)[..., None],
                         jnp.exp(jnp.maximum(diff, 0.0)[..., None] * log_gamma), 0.0)
    q_dec = jnp.exp((idx + 1.0)[:, None] * log_gamma)
    k_dec = jnp.exp((C - 1.0 - idx)[:, None] * log_gamma)
    c_dec = jnp.exp(C * log_gamma)
    dec = jnp.transpose(decay_in, (2, 0, 1))
    qd = jnp.repeat(q_dec, RET_DK, axis=1)
    kd = jnp.repeat(k_dec, RET_DK, axis=1)
    return dec, qd, kd, c_dec


def _rope_tables(pos):
    half = RET_DK // 2
    inv = ROPE_BASE ** (-jnp.arange(half, dtype=_F32) * 2.0 / RET_DK)
    ang = pos.astype(_F32)[:, None] * inv[None, :]
    cos = jnp.cos(ang)
    sin = jnp.sin(ang)
    return jnp.concatenate([cos, cos], axis=1), jnp.concatenate([-sin, sin], axis=1)


def _constants(rel_bias, attn_sink, L, ls):
    cst = {}
    rb = rel_bias.astype(_F32)
    cst['bias_p'] = jnp.transpose(_bias_table(rb, WINDOW, 2 * WINDOW), (0, 2, 1))
    cst['sink'] = attn_sink.astype(_F32)
    cst['cos_p'], cst['sin_p'] = _rope_tables(jnp.arange(L))
    cst['dec_p'], cst['qd_p'], cst['kd_p'], cst['cdec_p'] = _decay_tables(min(RET_CHUNK, L))
    bias = _bias_table(rb, ls, WINDOW + ls).reshape(KV_HEADS, GQA_GROUP * ls, WINDOW + ls)
    cst['bias_hist'] = bias[:, :, :WINDOW]
    cst['bias_new'] = bias[:, :, WINDOW:]
    cst['sink_col'] = jnp.repeat(attn_sink.astype(_F32).reshape(KV_HEADS, GQA_GROUP), ls,
                                 axis=1)[..., None]
    cst['cos_s'], cst['sin_s'] = _rope_tables(PAST_LEN + jnp.arange(ls))
    cst['dec_s'], cst['qd_s'], cst['kd_s'], cst['cdec_s'] = _decay_tables(min(RET_CHUNK, ls))
    return cst


def _layer_weights(layer, norm_mix, w_in, w_branch_attn, w_branch_ret, w_out, norm_ffn,
                   w_router_group, b_router_group, w_router_expert, b_router_expert):
    D = w_in.shape[1]
    wr = jnp.concatenate([w_router_group[layer].astype(_F32), w_router_expert[layer].astype(_F32)], axis=1)
    wr = jnp.pad(wr, ((0, 0), (0, ROUTE_LANES - wr.shape[1])))
    wrh = wr.astype(_BF16)
    wrl = jnp.concatenate([wrh, (wr - wrh.astype(_F32)).astype(_BF16)], axis=1)
    br = jnp.concatenate([b_router_group[layer].astype(_F32), b_router_expert[layer].astype(_F32)])
    br = jnp.pad(br, (0, ROUTE_LANES - br.shape[0]))[None, :]
    return {
        'nmix': norm_mix[layer].astype(_F32)[None, :],
        'win': w_in[layer].astype(_BF16),
        'wba': w_branch_attn[layer].astype(_BF16),
        'wbr': w_branch_ret[layer].astype(_BF16),
        'wout': w_out[layer].astype(_BF16),
        'nffn': norm_ffn[layer].astype(_F32)[None, :],
        'wrh': wrh, 'wrl': wrl, 'br': br,
    }


def kernel(x_prompt, x_sample, cache_k, cache_v, state_ret, norm_mix, w_in, attn_sink, rel_bias,
           w_branch_attn, w_branch_ret, w_out, norm_ffn, w_router_group, b_router_group,
           w_router_expert, b_router_expert, w_gate, w_up, w_down, norm_final):
    depth = w_in.shape[0]
    assert depth == 1, "the final norm is fused into the MoE combine of the only layer"
    B, L, D = x_prompt.shape
    NB, ls, _ = x_sample.shape
    Tp, Ts = B * L, NB * ls
    nfin = norm_final.astype(_F32)[None, :]
    yp, ys = x_prompt, x_sample
    pk, pv, ps, sk, sv, ss = [], [], [], [], [], []
    for layer in range(depth):
        w = _layer_weights(layer, norm_mix, w_in, w_branch_attn, w_branch_ret, w_out, norm_ffn,
                           w_router_group, b_router_group, w_router_expert, b_router_expert)
        cst = _constants(rel_bias, attn_sink[layer], L, ls)
        x1p, routep, k1, v1, s1, cnt_p = _prompt_mixer(yp, cst, w)
        ys2 = ys.reshape(Ts, D)
        proj = _sample_inproj(ys2, w)
        attn_s, ret_s, k2, v2, s2 = _sample_core(
            proj, ls,
            cache_k[layer].reshape(NB, WINDOW, KV_W), cache_v[layer].reshape(NB, WINDOW, KV_W),
            state_ret[layer], cst)
        x1s, routes, cnt_all = _sample_post(
            ys2, attn_s, ret_s,
            proj[:, OFF_GA:OFF_GA + D], proj[:, OFF_GT:OFF_GT + D], w, cnt_p)
        n_rows = 2 * (Tp + Ts)
        assert n_rows % MOE_TM == 0
        counts = cnt_all[N_GROUPS:N_GROUPS + N_EXPERTS, 0].astype(jnp.int32)
        dest, work = _routing_tables(jnp.concatenate([routep, routes], axis=1), counts,
                                     n_rows // MOE_TM)
        xs3 = _dispatch(dest, x1p.reshape(Tp, SUB, LANE), x1s.reshape(Ts, SUB, LANE))
        yb2 = _gmm(work, xs3.reshape(n_rows * SUB, LANE), w['nffn'],
                   w_gate[layer], w_up[layer], w_down[layer])
        yb3 = yb2.reshape(n_rows, SUB, LANE)
        yp = _combine(dest, x1p, routep, nfin, yb3, 0).reshape(B, L, D)
        ys = _combine(dest, x1s, routes, nfin, yb3, Tp).reshape(NB, ls, D)
        pk.append(k1.reshape(B, WINDOW, KV_HEADS, HEAD_DIM))
        pv.append(v1.reshape(B, WINDOW, KV_HEADS, HEAD_DIM))
        ps.append(s1)
        sk.append(k2.reshape(NB, WINDOW, KV_HEADS, HEAD_DIM))
        sv.append(v2.reshape(NB, WINDOW, KV_HEADS, HEAD_DIM))
        ss.append(s2)
    return (yp, ys, jnp.stack(pk), jnp.stack(pv), jnp.stack(ps),
            jnp.stack(sk), jnp.stack(sv), jnp.stack(ss))
```
